```python
import math, functools
import jax, jax.numpy as jnp
from jax import lax
import numpy as np

D_MODEL = 2048
BATCH = 1
SEQ = 8192
DEPTH = 2
DEC_BATCH = 16
DEC_SEQ = 16
PAST_LEN = 2048

CHUNK = 64
Q_BLOCK = 128
D_RWKV = 1024
D_DIFF = D_MODEL - D_RWKV
RWKV_HEAD = 64
N_RWKV_HEADS = D_RWKV // RWKV_HEAD
DIFF_HEAD = 64
N_DIFF_HEADS = D_DIFF // (2 * DIFF_HEAD)
DECAY_LORA = 96
AAA_LORA = 96
GATE_LORA = 256
N_RWKV_COLS = 3 * D_RWKV + DECAY_LORA + AAA_LORA + GATE_LORA
N_DIFF_COLS = 3 * D_DIFF
N_IN_COLS = N_RWKV_COLS + N_DIFF_COLS
RWKV_SPLITS = (D_RWKV, 2 * D_RWKV, 3 * D_RWKV, 3 * D_RWKV + DECAY_LORA, 3 * D_RWKV + DECAY_LORA + AAA_LORA)
D_FF = 5632
CONV_W = 3
N_BUCKETS = 32
MAX_DISTANCE = 128
NORM_EPS = 1e-6
GN_EPS = 64e-5
SUBLN_EPS = 1e-5

kernel_name = 'hybrid_rwkv7_diffattn_stream_step'


def rms_norm(x, g, eps=NORM_EPS):
    xf = x.astype(jnp.float32)
    y = xf * lax.rsqrt(jnp.mean(xf * xf, axis=-1, keepdims=True) + eps)
    return (y * g.astype(jnp.float32)).astype(x.dtype)


def t5_bucket(rel):
    nb = N_BUCKETS // 2
    max_exact = nb // 2
    n = jnp.abs(rel)
    nf = jnp.maximum(n, 1).astype(jnp.float32)
    large = max_exact + (jnp.log(nf / max_exact) / math.log(MAX_DISTANCE / max_exact) * (nb - max_exact)).astype(jnp.int32)
    large = jnp.minimum(large, nb - 1)
    return jnp.where(rel > 0, nb, 0) + jnp.where(n < max_exact, n, large)


def rel_bias(q_pos, k_pos, table):
    bucket = t5_bucket(k_pos[None, :] - q_pos[:, None])
    return jnp.transpose(table.astype(jnp.float32)[bucket], (2, 0, 1))


def diff_attn_core(q, k, v, bias, mask, lam):
    s = jnp.einsum('bqhmd,bkhmd->bhmqk', q, k).astype(jnp.float32) * DIFF_HEAD ** -0.5
    s = s + bias[None, :, None]
    if mask is not None:
        s = jnp.where(mask, s, -jnp.inf)
    p = jax.nn.softmax(s, axis=-1)
    p = p[:, :, 0] - lam * p[:, :, 1]
    return jnp.einsum('bhqk,bkhe->bqhe', p.astype(v.dtype), v)


def diff_attn_prompt(q, k, v, lam, table):
    B, T = q.shape[:2]
    n_blk = T // Q_BLOCK
    k = k.reshape(B, T, N_DIFF_HEADS, 2, DIFF_HEAD)
    k_pos = jnp.arange(T)
    q_blocks = jnp.swapaxes(q.reshape(B, n_blk, Q_BLOCK, N_DIFF_HEADS, 2, DIFF_HEAD), 0, 1)

    def one_block(args):
        i, qb = args
        q_pos = i * Q_BLOCK + jnp.arange(Q_BLOCK)
        mask = (k_pos[None, :] // CHUNK) <= (q_pos[:, None] // CHUNK)
        return diff_attn_core(qb, k, v, rel_bias(q_pos, k_pos, table), mask, lam)

    out = lax.map(one_block, (jnp.arange(n_blk), q_blocks))
    return jnp.swapaxes(out, 0, 1).reshape(B, T, N_DIFF_HEADS, 2 * DIFF_HEAD)


def diff_attn_sample(q, k, v, lam, k_cache, v_cache, table):
    B, T = q.shape[:2]
    P = k_cache.shape[1]
    k_all = jnp.concatenate([k_cache.astype(k.dtype), k], axis=1).reshape(B, P + T, N_DIFF_HEADS, 2, DIFF_HEAD)
    v_all = jnp.concatenate([v_cache.astype(v.dtype), v], axis=1)
    q_pos = P + jnp.arange(T)
    k_pos = jnp.arange(P + T)
    return diff_attn_core(q, k_all, v_all, rel_bias(q_pos, k_pos, table), None, lam)


def wkv_scan(r, w, k, v, kk, a, S0):
    def step(S, inp):
        rt, wt, kt, vt, kkt, at = inp
        sa = jnp.einsum('bhvk,bhk->bhv', S, -kkt)
        S = S * wt[:, :, None, :] + sa[..., None] * (kkt * at)[:, :, None, :] + vt[..., None] * kt[:, :, None, :]
        return S, jnp.einsum('bhvk,bhk->bhv', S, rt)

    xs = tuple(jnp.moveaxis(t, 1, 0) for t in (r, w, k, v, kk, a))
    S, y = lax.scan(step, S0.astype(jnp.float32), xs)
    return jnp.moveaxis(y, 0, 1), S


def rwkv_group(p, shift0, wkv0, mu, w0, w2, a0, a2, g2, k_k, k_a, r_k, ln_w, ln_b):
    f32 = jnp.float32
    B, T, _ = p.shape
    prev = jnp.concatenate([shift0.astype(p.dtype), p[:, :-1]], axis=1)
    xm = p + (prev - p) * mu
    r, k, v, xw, xa, xg = jnp.split(xm, RWKV_SPLITS, axis=-1)
    w_log = -jax.nn.softplus(-(w0 + jnp.tanh(xw) @ w2).astype(f32)) - 0.5
    decay = jnp.exp(-jnp.exp(w_log))
    a = jax.nn.sigmoid((a0 + xa @ a2).astype(f32))
    g = jax.nn.sigmoid(xg) @ g2
    hs = (B, T, N_RWKV_HEADS, RWKV_HEAD)
    r, k, v, decay, a = (t.astype(f32).reshape(hs) for t in (r, k, v, decay, a))
    kk = k * k_k.astype(f32).reshape(N_RWKV_HEADS, RWKV_HEAD)
    kk = kk * lax.rsqrt(jnp.maximum(jnp.sum(kk * kk, axis=-1, keepdims=True), 1e-24))
    k = k * (1.0 + (a - 1.0) * k_a.astype(f32).reshape(N_RWKV_HEADS, RWKV_HEAD))
    y, S = wkv_scan(r, decay, k, v, kk, a, wkv0)
    y_mu = jnp.mean(y, axis=-1, keepdims=True)
    y_var = jnp.mean(jnp.square(y - y_mu), axis=-1, keepdims=True)
    y = ((y - y_mu) * lax.rsqrt(y_var + GN_EPS)).reshape(B, T, D_RWKV) * ln_w.astype(f32) + ln_b.astype(f32)
    bonus = jnp.sum(r * k * r_k.astype(f32), axis=-1, keepdims=True) * v
    y = (y + bonus.reshape(B, T, D_RWKV)) * g.astype(f32)
    return y.astype(p.dtype), S, p[:, -1:]


def conv_ffn(h, conv0, w_up, conv_w, conv_b, w_down):
    u = h @ w_up
    ug, uv = jnp.split(u, 2, axis=-1)
    T = ug.shape[1]
    ext = jnp.concatenate([conv0.astype(ug.dtype), ug], axis=1)
    z = conv_b + sum(ext[:, j:j + T] * conv_w[j] for j in range(CONV_W))
    y = (jax.nn.gelu(z, approximate=False) * uv) @ w_down
    return y, ext[:, T:]


def hybrid_layer(x, c, l, lp, shift0, wkv0, conv0, attend):
    B, T = x.shape[:2]
    mod = jax.nn.silu(c) @ lp['w_ada'] + lp['b_ada']
    sh1, sc1, gt1, sh2, sc2, gt2 = jnp.split(mod[:, None, :], 6, axis=-1)
    h = rms_norm(x, lp['g_mix']) * (1.0 + sc1) + sh1
    p = h @ lp['w_in']
    y_r, wkv, shift = rwkv_group(p[..., :N_RWKV_COLS], shift0, wkv0, lp['mu'], lp['w0'], lp['w2'], lp['a0'], lp['a2'], lp['g2'], lp['k_k'], lp['k_a'], lp['r_k'], lp['ln_w'], lp['ln_b'])
    pd = p[..., N_RWKV_COLS:]
    q = pd[..., :D_DIFF].reshape(B, T, N_DIFF_HEADS, 2, DIFF_HEAD)
    k = pd[..., D_DIFF:2 * D_DIFF].reshape(B, T, N_DIFF_HEADS, 2 * DIFF_HEAD)
    v = pd[..., 2 * D_DIFF:].reshape(B, T, N_DIFF_HEADS, 2 * DIFF_HEAD)
    lam_init = 0.8 - 0.6 * math.exp(-0.3 * l)
    f32 = jnp.float32
    lam = (jnp.exp(jnp.sum(lp['lq1'].astype(f32) * lp['lk1'].astype(f32)))
           - jnp.exp(jnp.sum(lp['lq2'].astype(f32) * lp['lk2'].astype(f32))) + lam_init)
    o = attend(q, k, v, lam)
    y_d = (rms_norm(o, lp['subln'], SUBLN_EPS) * (1.0 - lam_init)).reshape(B, T, D_DIFF)
    mixed = jnp.concatenate([y_r.astype(x.dtype), y_d.astype(x.dtype)], axis=-1) @ lp['w_out']
    x = x + gt1 * mixed
    h2 = rms_norm(x, lp['g_ffn']) * (1.0 + sc2) + sh2
    f, conv = conv_ffn(h2, conv0, lp['ffn_up'], lp['conv_w'], lp['conv_b'], lp['ffn_down'])
    x = x + gt2 * f
    return x, k, v, wkv, shift, conv


def setup_inputs(seed: int = 0) -> dict:
    key = jax.random.key(seed)
    ks = jax.random.split(key, 40)

    def nrm(i, shape, scale):
        return scale * jax.random.normal(ks[i], shape, jnp.float32)

    d_scale = D_MODEL ** -0.5
    kv_shape = (DEPTH, DEC_BATCH, PAST_LEN, N_DIFF_HEADS, 2 * DIFF_HEAD)
    return {
        'x_prompt': nrm(0, (BATCH, SEQ, D_MODEL), 1.0),
        'x_sample': nrm(1, (DEC_BATCH, DEC_SEQ, D_MODEL), 1.0),
        'c_prompt': nrm(2, (BATCH, D_MODEL), 1.0),
        'c_sample': nrm(3, (DEC_BATCH, D_MODEL), 1.0),
        'cache_k': nrm(4, kv_shape, 1.0),
        'cache_v': nrm(5, kv_shape, 1.0),
        'state_wkv': nrm(6, (DEPTH, DEC_BATCH, N_RWKV_HEADS, RWKV_HEAD, RWKV_HEAD), 1.0),
        'state_shift': nrm(7, (DEPTH, DEC_BATCH, 1, N_RWKV_COLS), 1.0),
        'state_conv': nrm(8, (DEPTH, DEC_BATCH, CONV_W - 1, D_FF), 1.0),
        'w_ada': nrm(9, (DEPTH, D_MODEL, 6 * D_MODEL), 0.5 * d_scale),
        'b_ada': nrm(10, (DEPTH, 6 * D_MODEL), 0.02),
        'g_mix': 1.0 + nrm(11, (DEPTH, D_MODEL), 0.02),
        'g_ffn': 1.0 + nrm(12, (DEPTH, D_MODEL), 0.02),
        'w_in': nrm(13, (DEPTH, D_MODEL, N_IN_COLS), d_scale),
        'w_out': nrm(14, (DEPTH, D_MODEL, D_MODEL), d_scale),
        'rwkv_mu': jax.random.uniform(ks[15], (DEPTH, N_RWKV_COLS), jnp.float32),
        'rwkv_w0': jnp.linspace(-6.0, -1.0, D_RWKV, dtype=jnp.float32)[None, :] + nrm(16, (DEPTH, D_RWKV), 0.1),
        'rwkv_w2': nrm(17, (DEPTH, DECAY_LORA, D_RWKV), 0.5 * DECAY_LORA ** -0.5),
        'rwkv_a0': nrm(18, (DEPTH, D_RWKV), 0.1),
        'rwkv_a2': nrm(19, (DEPTH, AAA_LORA, D_RWKV), 0.5 * AAA_LORA ** -0.5),
        'rwkv_g2': nrm(20, (DEPTH, GATE_LORA, D_RWKV), GATE_LORA ** -0.5),
        'rwkv_kk': 0.85 + nrm(21, (DEPTH, D_RWKV), 0.02),
        'rwkv_ka': 1.0 + nrm(22, (DEPTH, D_RWKV), 0.02),
        'rwkv_rk': nrm(23, (DEPTH, N_RWKV_HEADS, RWKV_HEAD), 0.1),
        'rwkv_ln_w': 1.0 + nrm(24, (DEPTH, D_RWKV), 0.02),
        'rwkv_ln_b': nrm(25, (DEPTH, D_RWKV), 0.02),
        'diff_lq1': nrm(26, (DEPTH, DIFF_HEAD), 0.1),
        'diff_lk1': nrm(27, (DEPTH, DIFF_HEAD), 0.1),
        'diff_lq2': nrm(28, (DEPTH, DIFF_HEAD), 0.1),
        'diff_lk2': nrm(29, (DEPTH, DIFF_HEAD), 0.1),
        'diff_subln': 1.0 + nrm(30, (DEPTH, 2 * DIFF_HEAD), 0.02),
        'rel_table': nrm(31, (N_BUCKETS, N_DIFF_HEADS), 0.5),
        'ffn_up': nrm(32, (DEPTH, D_MODEL, 2 * D_FF), d_scale),
        'ffn_conv_w': nrm(33, (DEPTH, CONV_W, D_FF), CONV_W ** -0.5),
        'ffn_conv_b': nrm(34, (DEPTH, D_FF), 0.02),
        'ffn_down': nrm(35, (DEPTH, D_FF, D_MODEL), D_FF ** -0.5),
        'g_final': 1.0 + nrm(36, (D_MODEL,), 0.02),
    }


def reference(x_prompt, x_sample, c_prompt, c_sample, cache_k, cache_v, state_wkv, state_shift, state_conv,
              w_ada, b_ada, g_mix, g_ffn, w_in, w_out, rwkv_mu, rwkv_w0, rwkv_w2, rwkv_a0, rwkv_a2, rwkv_g2,
              rwkv_kk, rwkv_ka, rwkv_rk, rwkv_ln_w, rwkv_ln_b, diff_lq1, diff_lk1, diff_lq2, diff_lk2, diff_subln,
              rel_table, ffn_up, ffn_conv_w, ffn_conv_b, ffn_down, g_final):
    def layer_params(l):
        return {'w_ada': w_ada[l], 'b_ada': b_ada[l], 'g_mix': g_mix[l], 'g_ffn': g_ffn[l],
                'w_in': w_in[l], 'w_out': w_out[l], 'mu': rwkv_mu[l], 'w0': rwkv_w0[l], 'w2': rwkv_w2[l],
                'a0': rwkv_a0[l], 'a2': rwkv_a2[l], 'g2': rwkv_g2[l], 'k_k': rwkv_kk[l], 'k_a': rwkv_ka[l],
                'r_k': rwkv_rk[l], 'ln_w': rwkv_ln_w[l], 'ln_b': rwkv_ln_b[l], 'lq1': diff_lq1[l],
                'lk1': diff_lk1[l], 'lq2': diff_lq2[l], 'lk2': diff_lk2[l], 'subln': diff_subln[l],
                'ffn_up': ffn_up[l], 'conv_w': ffn_conv_w[l], 'conv_b': ffn_conv_b[l], 'ffn_down': ffn_down[l]}

    B = x_prompt.shape[0]
    xp, xs = x_prompt, x_sample
    kp, vp, wp, sp, cp = [], [], [], [], []
    ksm, vsm, wsm, ssm, csm = [], [], [], [], []
    for l in range(DEPTH):
        lp = layer_params(l)
        shift0 = jnp.zeros((B, 1, N_RWKV_COLS), x_prompt.dtype)
        wkv0 = jnp.zeros((B, N_RWKV_HEADS, RWKV_HEAD, RWKV_HEAD), jnp.float32)
        conv0 = jnp.zeros((B, CONV_W - 1, D_FF), x_prompt.dtype)
        attend_p = functools.partial(diff_attn_prompt, table=rel_table)
        xp, k_l, v_l, w_l, s_l, c_l = hybrid_layer(xp, c_prompt, l, lp, shift0, wkv0, conv0, attend_p)
        kp.append(k_l); vp.append(v_l); wp.append(w_l); sp.append(s_l); cp.append(c_l)
        attend_s = functools.partial(diff_attn_sample, k_cache=cache_k[l], v_cache=cache_v[l], table=rel_table)
        xs, k_l, v_l, w_l, s_l, c_l = hybrid_layer(xs, c_sample, l, lp, state_shift[l], state_wkv[l], state_conv[l], attend_s)
        ksm.append(k_l); vsm.append(v_l); wsm.append(w_l); ssm.append(s_l); csm.append(c_l)

    y_prompt = rms_norm(xp, g_final)
    y_sample = rms_norm(xs, g_final)
    k_prompt, v_prompt = jnp.stack(kp), jnp.stack(vp)
    wkv_prompt, shift_prompt, conv_prompt = jnp.stack(wp), jnp.stack(sp), jnp.stack(cp)
    k_sample, v_sample = jnp.stack(ksm), jnp.stack(vsm)
    wkv_sample, shift_sample, conv_sample = jnp.stack(wsm), jnp.stack(ssm), jnp.stack(csm)
    return (y_prompt, y_sample, k_prompt, v_prompt, wkv_prompt, shift_prompt, conv_prompt,
            k_sample, v_sample, wkv_sample, shift_sample, conv_sample)
```

```python
import functools
import math

import jax
import jax.numpy as jnp
from jax import lax
from jax.experimental import pallas as pl
from jax.experimental.pallas import tpu as pltpu

F32 = jnp.float32
BF16 = jnp.bfloat16

CHUNK = 64
RWKV_HEAD = 64
DIFF_HEAD = 64
N_BUCKETS = 32
MAX_DISTANCE = 128
NORM_EPS = 1e-6
GN_EPS = 64e-5
SUBLN_EPS = 1e-5
DECAY_LORA = 96
AAA_LORA = 96
GATE_LORA = 256
LORA_PAD = 128
CONV_W = 3

LANES = 128
SUBLANES = 8
VMEM_LIMIT_BYTES = 56 * 1024 * 1024
MASK_VALUE = -1e30

_NN = (((1,), (0,)), ((), ()))
_NT = (((1,), (1,)), ((), ()))
_TN = (((0,), (0,)), ((), ()))


def _mm(a, b, dims=_NN):
    return lax.dot_general(a, b, dims, preferred_element_type=F32)


def _hi_lo(x):
    h = x.astype(BF16)
    return h, (x - h.astype(F32)).astype(BF16)


def _mm3(a, b, dims=_NN):
    ah, al = _hi_lo(a)
    bh, bl = _hi_lo(b)
    return _mm(ah, bh, dims) + (_mm(ah, bl, dims) + _mm(al, bh, dims))


def _mm_sel(sel, x, dims=_NN, sel_first=True):
    h = x.astype(BF16)
    r = x - h.astype(F32)
    m = r.astype(BF16)
    l = (r - m.astype(F32)).astype(BF16)
    if sel_first:
        return _mm(sel, h, dims) + (_mm(sel, m, dims) + _mm(sel, l, dims))
    return _mm(h, sel, dims) + (_mm(m, sel, dims) + _mm(l, sel, dims))


def _cparams(sem):
    return pltpu.CompilerParams(dimension_semantics=sem, vmem_limit_bytes=VMEM_LIMIT_BYTES)


def _rms(x, g, eps):
    return x * lax.rsqrt(jnp.mean(x * x, axis=-1, keepdims=True) + eps) * g


def _mod_kernel(c_ref, w_ref, b_ref, o_ref):
    c = c_ref[...]
    s = (c * jax.nn.sigmoid(c)).astype(BF16)
    o_ref[...] = _mm(s, w_ref[...].astype(BF16)) + b_ref[...]


def _adaln_mod(c_rows, w_ada, b_ada, tn=1024):
    depth, d, n = w_ada.shape
    rows = c_rows.shape[0]
    return pl.pallas_call(
        _mod_kernel,
        grid=(depth, n // tn),
        in_specs=[
            pl.BlockSpec((rows, d), lambda l, j: (0, 0)),
            pl.BlockSpec((None, d, tn), lambda l, j: (l, 0, j)),
            pl.BlockSpec((None, 1, tn), lambda l, j: (l, 0, j)),
        ],
        out_specs=pl.BlockSpec((None, rows, tn), lambda l, j: (l, 0, j)),
        out_shape=jax.ShapeDtypeStruct((depth, rows, n), F32),
        compiler_params=_cparams(("arbitrary", "arbitrary")),
        name="adaln_mod",
    )(c_rows, w_ada, b_ada.reshape(depth, 1, n))


def _bias_kernel(tab_ref, bucket_ref, o_ref, *, far_bucket):
    h = pl.program_id(0)
    bucket = bucket_ref[...]
    far = tab_ref[far_bucket, h]
    acc = jnp.full(bucket.shape, MASK_VALUE, F32)
    for b in range(N_BUCKETS):
        acc = jnp.where(bucket == b, tab_ref[b, h] - far, acc)
    o_ref[...] = acc


def _bias_lookup(table, bucket, far_bucket):
    n_heads = table.shape[1]
    r, c = bucket.shape
    return pl.pallas_call(
        functools.partial(_bias_kernel, far_bucket=far_bucket),
        grid=(n_heads,),
        in_specs=[
            pl.BlockSpec(memory_space=pltpu.SMEM),
            pl.BlockSpec((r, c), lambda h: (0, 0)),
        ],
        out_specs=pl.BlockSpec((None, r, c), lambda h: (h, 0, 0)),
        out_shape=jax.ShapeDtypeStruct((n_heads, r, c), F32),
        compiler_params=_cparams(("arbitrary",)),
        name="bias_lookup",
    )(table, bucket)


def _t5_bucket(rel):
    nb = N_BUCKETS // 2
    max_exact = nb // 2
    n = jnp.abs(rel)
    nf = jnp.maximum(n, 1).astype(F32)
    large = max_exact + (jnp.log(nf / max_exact) / math.log(MAX_DISTANCE / max_exact) * (nb - max_exact)).astype(jnp.int32)
    large = jnp.minimum(large, nb - 1)
    return jnp.where(rel > 0, nb, 0) + jnp.where(n < max_exact, n, large)


FAR_BUCKET = N_BUCKETS // 2 - 1


def _inproj_kernel(x_ref, sh_ref, sc_ref, g_ref, w_ref, p_ref, q_ref, k_ref, v_ref, h_scr, *, n_p, n_q):
    j = pl.program_id(1)

    @pl.when(j == 0)
    def _():
        xn = _rms(x_ref[...], g_ref[...], NORM_EPS)
        h_scr[...] = (xn * (1.0 + sc_ref[...]) + sh_ref[...]).astype(BF16)

    r = _mm(h_scr[...], w_ref[...])

    @pl.when(j < n_p)
    def _():
        p_ref[...] = r

    @pl.when((j >= n_p) & (j < n_p + n_q))
    def _():
        q_ref[...] = (r * (DIFF_HEAD ** -0.5)).astype(BF16)

    @pl.when((j >= n_p + n_q) & (j < n_p + 2 * n_q))
    def _():
        k_ref[...] = r

    @pl.when(j >= n_p + 2 * n_q)
    def _():
        v_ref[...] = r


def _mod_spec(mod, tm, d):
    if mod.shape[0] == 1:
        return pl.BlockSpec((1, d), lambda i, j: (0, 0))
    return pl.BlockSpec((tm, d), lambda i, j: (i, 0))


def _inproj(x, sh, sc, g, w, n_rwkv_pad, d_diff, tm, tn=512):
    m, d = x.shape
    n = w.shape[1]
    n_p, n_q = n_rwkv_pad // tn, d_diff // tn
    clip = lambda j, lo: jnp.clip(j - lo, 0, n_q - 1)
    return pl.pallas_call(
        functools.partial(_inproj_kernel, n_p=n_p, n_q=n_q),
        grid=(m // tm, n // tn),
        in_specs=[
            pl.BlockSpec((tm, d), lambda i, j: (i, 0)),
            _mod_spec(sh, tm, d),
            _mod_spec(sc, tm, d),
            pl.BlockSpec((1, d), lambda i, j: (0, 0)),
            pl.BlockSpec((d, tn), lambda i, j: (0, j)),
        ],
        out_specs=[
            pl.BlockSpec((tm, tn), lambda i, j: (i, jnp.minimum(j, n_p - 1))),
            pl.BlockSpec((tm, tn), lambda i, j: (i, clip(j, n_p))),
            pl.BlockSpec((tm, tn), lambda i, j: (i, clip(j, n_p + n_q))),
            pl.BlockSpec((tm, tn), lambda i, j: (i, clip(j, n_p + 2 * n_q))),
        ],
        out_shape=[
            jax.ShapeDtypeStruct((m, n_rwkv_pad), F32),
            jax.ShapeDtypeStruct((m, d_diff), BF16),
            jax.ShapeDtypeStruct((m, d_diff), F32),
            jax.ShapeDtypeStruct((m, d_diff), F32),
        ],
        scratch_shapes=[pltpu.VMEM((tm, d), BF16)],
        compiler_params=_cparams(("arbitrary", "arbitrary")),
        name="inproj",
    )(x, sh, sc, g, w)


def _wkv_kernel(pr_ref, pk_ref, pv_ref, pw_ref, pa_ref, pg_ref,
                mur_ref, muk_ref, muv_ref, muw_ref, mua_ref, mug_ref,
                s0r_ref, s0k_ref, s0v_ref, s0w_ref, s0a_ref, s0g_ref,
                w2_ref, a2_ref, g2_ref,
                w0_ref, a0_ref, kkp_ref, ka_ref, rk_ref, lnw_ref, lnb_ref,
                st0_ref,
                y_ref, st_ref,
                sbd, cr, ck, cv, cw, ca, cg, *, C, n_chunks):
    chunk = pl.program_id(2)
    hd = RWKV_HEAD
    zero_blk = jnp.zeros((hd, hd), F32)

    @pl.when(chunk == 0)
    def _():
        for c_ref, s_ref in ((cr, s0r_ref), (ck, s0k_ref), (cv, s0v_ref), (cw, s0w_ref), (ca, s0a_ref), (cg, s0g_ref)):
            c_ref[0:1, :] = s_ref[...]
        top = jnp.concatenate([st0_ref[0], zero_blk], axis=1)
        bot = jnp.concatenate([zero_blk, st0_ref[1]], axis=1)
        sbd[...] = jnp.concatenate([top, bot], axis=0)

    def tshift(x_ref, mu_ref, c_ref):
        x = x_ref[...]
        row = lax.broadcasted_iota(jnp.int32, x.shape, 0)
        prev = jnp.where(row == 0, c_ref[0:1, :], pltpu.roll(x, 1, 0))
        c_ref[0:1, :] = x[C - 1:C, :]
        return x + (prev - x) * mu_ref[...]

    xr = tshift(pr_ref, mur_ref, cr)
    xk = tshift(pk_ref, muk_ref, ck)
    xv = tshift(pv_ref, muv_ref, cv)
    xw = tshift(pw_ref, muw_ref, cw)
    xa = tshift(pa_ref, mua_ref, ca)
    xg = tshift(pg_ref, mug_ref, cg)

    lane = lax.broadcasted_iota(jnp.int32, (1, LANES), 1)
    head0 = lane < hd
    li = lax.broadcasted_iota(jnp.int32, (LANES, LANES), 0)
    lj = lax.broadcasted_iota(jnp.int32, (LANES, LANES), 1)
    same_head = (li < hd) == (lj < hd)
    seg = jnp.where(same_head, 1.0, 0.0).astype(BF16)

    def segsum(x):
        return _mm_sel(seg, x, _NN, sel_first=False)

    wlin = w0_ref[...] + _mm(jnp.tanh(xw).astype(BF16), w2_ref[...])
    z = -wlin
    w_log = -(jnp.maximum(z, 0.0) + jnp.log1p(jnp.exp(-jnp.abs(z)))) - 0.5
    ne = -jnp.exp(w_log)
    a = jax.nn.sigmoid(a0_ref[...] + _mm(xa.astype(BF16), a2_ref[...]))
    g = _mm(jax.nn.sigmoid(xg).astype(BF16), g2_ref[...])
    kk = xk * kkp_ref[...]
    kk = kk * lax.rsqrt(jnp.maximum(segsum(kk * kk), 1e-24))
    k2 = xk * (1.0 + (a - 1.0) * ka_ref[...])

    ti = lax.broadcasted_iota(jnp.int32, (C, C), 0)
    tj = lax.broadcasted_iota(jnp.int32, (C, C), 1)
    tri = jnp.where(tj <= ti, 1.0, 0.0).astype(BF16)
    cum = _mm_sel(tri, ne)
    gam = jnp.exp(cum)
    gprev = jnp.exp(cum - ne)
    ginv = jnp.exp(-cum)
    dte = jnp.exp(cum[C - 1:C, :] - cum)

    ag = -kk * gprev
    rg = xr * gam
    kka = kk * a
    bs = kka * ginv
    ks = k2 * ginv

    def stack_masked(x):
        return jnp.concatenate([jnp.where(head0, x, 0.0), jnp.where(head0, 0.0, x)], axis=0)

    def stack_twice(x):
        return jnp.concatenate([x, x], axis=0)

    la, lr = stack_masked(ag), stack_masked(rg)
    rb, rk2 = stack_twice(bs), stack_twice(ks)
    vst = stack_masked(xv)

    R = 2 * C
    ri = lax.broadcasted_iota(jnp.int32, (R, R), 0)
    ci = lax.broadcasted_iota(jnp.int32, (R, R), 1)
    same_blk = (ri < C) == (ci < C)
    tr = jnp.where(ri < C, ri, ri - C)
    tc = jnp.where(ci < C, ci, ci - C)
    strict = same_blk & (tc < tr)
    incl = same_blk & (tc <= tr)

    n_ab = jnp.where(strict, _mm3(la, rb, _NT), 0.0)
    m_ak = jnp.where(strict, _mm3(la, rk2, _NT), 0.0)
    m_rb = jnp.where(incl, _mm3(lr, rb, _NT), 0.0)
    m_rk = jnp.where(incl, _mm3(lr, rk2, _NT), 0.0)

    s_old = sbd[...]
    x_u = _mm3(la, s_old, _NT) + _mm3(m_ak, vst)
    pw = n_ab
    n_steps = max(1, int(math.log2(C)))
    for step in range(n_steps):
        x_u = x_u + _mm3(pw, x_u)
        if step + 1 < n_steps:
            pw = _mm3(pw, pw)
    y_st = _mm3(lr, s_old, _NT) + _mm3(m_rb, x_u) + _mm3(m_rk, vst)
    y = y_st[:C, :] + y_st[C:, :]
    u = x_u[:C, :] + x_u[C:, :]

    upd = _mm3(u, kka * dte, _TN) + _mm3(xv, k2 * dte, _TN)
    sbd[...] = s_old * gam[C - 1:C, :] + jnp.where(same_head, upd, 0.0)

    inv_hd = 1.0 / hd
    y_mu = segsum(y) * inv_hd
    yc = y - y_mu
    y_var = segsum(yc * yc) * inv_hd
    yn = yc * lax.rsqrt(y_var + GN_EPS) * lnw_ref[...] + lnb_ref[...]
    bonus = segsum(xr * k2 * rk_ref[...]) * xv
    y_ref[...] = ((yn + bonus) * g).astype(BF16)

    @pl.when(chunk == n_chunks - 1)
    def _():
        s_new = sbd[...]
        st_ref[0] = s_new[0:hd, 0:hd]
        st_ref[1] = s_new[hd:2 * hd, hd:2 * hd]


def _wkv(p, mu, shift0, w2, a2, g2, w0, a0, kkp, ka, rk, lnw, lnb, state0, n_seq, seq_len, C):
    m, n_pad = p.shape
    d_rwkv = w0.shape[1]
    n_heads = d_rwkv // RWKV_HEAD
    n_pairs = n_heads // 2
    n_chunks = seq_len // C
    nb = d_rwkv // LANES
    blk_w = 3 * nb
    blk_a = blk_w + 1
    blk_g = (blk_a + 1) // 2

    row = lambda s, h, c: s * n_chunks + c
    seg_specs = [
        pl.BlockSpec((C, LANES), lambda s, h, c: (row(s, h, c), h)),
        pl.BlockSpec((C, LANES), lambda s, h, c: (row(s, h, c), nb + h)),
        pl.BlockSpec((C, LANES), lambda s, h, c: (row(s, h, c), 2 * nb + h)),
        pl.BlockSpec((C, LANES), lambda s, h, c: (row(s, h, c), blk_w)),
        pl.BlockSpec((C, LANES), lambda s, h, c: (row(s, h, c), blk_a)),
        pl.BlockSpec((C, GATE_LORA), lambda s, h, c: (row(s, h, c), blk_g)),
    ]
    mu_specs = [
        pl.BlockSpec((1, LANES), lambda s, h, c: (0, h)),
        pl.BlockSpec((1, LANES), lambda s, h, c: (0, nb + h)),
        pl.BlockSpec((1, LANES), lambda s, h, c: (0, 2 * nb + h)),
        pl.BlockSpec((1, LANES), lambda s, h, c: (0, blk_w)),
        pl.BlockSpec((1, LANES), lambda s, h, c: (0, blk_a)),
        pl.BlockSpec((1, GATE_LORA), lambda s, h, c: (0, blk_g)),
    ]
    s0_specs = [
        pl.BlockSpec((None, 1, LANES), lambda s, h, c: (s, 0, h)),
        pl.BlockSpec((None, 1, LANES), lambda s, h, c: (s, 0, nb + h)),
        pl.BlockSpec((None, 1, LANES), lambda s, h, c: (s, 0, 2 * nb + h)),
        pl.BlockSpec((None, 1, LANES), lambda s, h, c: (s, 0, blk_w)),
        pl.BlockSpec((None, 1, LANES), lambda s, h, c: (s, 0, blk_a)),
        pl.BlockSpec((None, 1, GATE_LORA), lambda s, h, c: (s, 0, blk_g)),
    ]
    lora_specs = [
        pl.BlockSpec((LORA_PAD, LANES), lambda s, h, c: (0, h)),
        pl.BlockSpec((LORA_PAD, LANES), lambda s, h, c: (0, h)),
        pl.BlockSpec((GATE_LORA, LANES), lambda s, h, c: (0, h)),
    ]
    vec_spec = pl.BlockSpec((1, LANES), lambda s, h, c: (0, h))
    state_spec = pl.BlockSpec((None, 2, RWKV_HEAD, RWKV_HEAD), lambda s, h, c: (s, h, 0, 0))
    y, st = pl.pallas_call(
        functools.partial(_wkv_kernel, C=C, n_chunks=n_chunks),
        grid=(n_seq, n_pairs, n_chunks),
        in_specs=seg_specs + mu_specs + s0_specs + lora_specs + [vec_spec] * 7 + [state_spec],
        out_specs=[
            pl.BlockSpec((C, LANES), lambda s, h, c: (row(s, h, c), h)),
            state_spec,
        ],
        out_shape=[
            jax.ShapeDtypeStruct((m, d_rwkv), BF16),
            jax.ShapeDtypeStruct((n_seq, n_heads, RWKV_HEAD, RWKV_HEAD), F32),
        ],
        scratch_shapes=[pltpu.VMEM((LANES, LANES), F32)]
        + [pltpu.VMEM((SUBLANES, LANES), F32)] * 5 + [pltpu.VMEM((SUBLANES, GATE_LORA), F32)],
        compiler_params=_cparams(("arbitrary", "arbitrary", "arbitrary")),
        name="wkv",
    )(p, p, p, p, p, p, mu, mu, mu, mu, mu, mu,
      shift0, shift0, shift0, shift0, shift0, shift0,
      w2, a2, g2, w0, a0, kkp, ka, rk, lnw, lnb, state0)
    return y, st


def _lambda(lqk_ref, lam_init):
    t = lqk_ref[...]
    s1 = jnp.sum(t[0:1, :] * t[1:2, :], axis=-1, keepdims=True)
    s2 = jnp.sum(t[2:3, :] * t[3:4, :], axis=-1, keepdims=True)
    return jnp.exp(s1) - jnp.exp(s2) + lam_init


def _split_maps(q):
    lane = lax.broadcasted_iota(jnp.int32, q.shape, 1)
    zero = jnp.zeros_like(q)
    return jnp.where(lane < DIFF_HEAD, q, zero), jnp.where(lane < DIFF_HEAD, zero, q)


def _subln(o, subln, lam_init):
    return _rms(o, subln, SUBLN_EPS) * (1.0 - lam_init)


def _attn_kernel(q_ref, k_ref, v_ref, bias_ref, lqk_ref, subln_ref, o_ref,
                 kb, vb, m1, l1, a1, m2, l2, a2, *, TB, n_cast, lam_init):
    i = pl.program_id(1)

    @pl.when(i == 0)
    def _():
        def cast(c, carry):
            off = pl.multiple_of(c * TB, TB)
            kb[pl.ds(off, TB), :] = k_ref[pl.ds(off, TB), :].astype(BF16)
            vb[pl.ds(off, TB), :] = v_ref[pl.ds(off, TB), :].astype(BF16)
            return carry
        lax.fori_loop(0, n_cast, cast, 0)

    qa, qb = _split_maps(q_ref[...])
    for m_r, l_r, a_r in ((m1, l1, a1), (m2, l2, a2)):
        m_r[...] = jnp.full(m_r.shape, MASK_VALUE, F32)
        l_r[...] = jnp.zeros(l_r.shape, F32)
        a_r[...] = jnp.zeros(a_r.shape, F32)

    def tile(j, bias):
        off = pl.multiple_of(j * TB, TB)
        kt = kb[pl.ds(off, TB), :]
        vt = vb[pl.ds(off, TB), :]
        for qx, m_r, l_r, a_r in ((qa, m1, l1, a1), (qb, m2, l2, a2)):
            s = _mm(qx, kt, _NT)
            if bias is not None:
                s = s + bias
            m_prev = m_r[...]
            m_new = jnp.maximum(m_prev, jnp.max(s, axis=-1, keepdims=True))
            alpha = jnp.exp(m_prev - m_new)
            p = jnp.exp(s - m_new)
            l_r[...] = alpha * l_r[...] + jnp.sum(p, axis=-1, keepdims=True)
            a_r[...] = alpha * a_r[...] + _mm(p.astype(BF16), vt)
            m_r[...] = m_new

    def far(j, carry):
        tile(j, None)
        return carry
    lax.fori_loop(0, i - 1, far, 0)
    tile(jnp.maximum(i - 1, 0), bias_ref[1] + jnp.where(i == 0, MASK_VALUE, 0.0))
    tile(i, bias_ref[0])

    lam = _lambda(lqk_ref, lam_init)
    o = a1[...] / l1[...] - lam * (a2[...] / l2[...])
    o_ref[...] = _subln(o, subln_ref[...], lam_init).astype(BF16)


def _attn_prompt(q, k, v, bias, lqk, subln, lam_init, TB):
    m, d_diff = q.shape
    n_heads = d_diff // LANES
    n_blk = m // TB
    return pl.pallas_call(
        functools.partial(_attn_kernel, TB=TB, n_cast=n_blk, lam_init=lam_init),
        grid=(n_heads, n_blk),
        in_specs=[
            pl.BlockSpec((TB, LANES), lambda h, i: (i, h)),
            pl.BlockSpec((m, LANES), lambda h, i: (0, h)),
            pl.BlockSpec((m, LANES), lambda h, i: (0, h)),
            pl.BlockSpec((None, 2, TB, TB), lambda h, i: (h, 0, 0, 0)),
            pl.BlockSpec((4, DIFF_HEAD), lambda h, i: (0, 0)),
            pl.BlockSpec((1, LANES), lambda h, i: (0, 0)),
        ],
        out_specs=pl.BlockSpec((TB, LANES), lambda h, i: (i, h)),
        out_shape=jax.ShapeDtypeStruct((m, d_diff), BF16),
        scratch_shapes=[pltpu.VMEM((m, LANES), BF16), pltpu.VMEM((m, LANES), BF16)]
        + [pltpu.VMEM((TB, 1), F32), pltpu.VMEM((TB, 1), F32), pltpu.VMEM((TB, LANES), F32)] * 2,
        compiler_params=_cparams(("arbitrary", "arbitrary")),
        name="attn_prompt",
    )(q, k, v, bias, lqk, subln)


def _attn_sample_kernel(q_ref, kn_ref, vn_ref, kc_ref, vc_ref, bc_ref, bn_ref, lqk_ref, subln_ref, o_ref,
                        *, T, lam_init):
    qa, qb = _split_maps(q_ref[...])
    qs = jnp.concatenate([qa, qb], axis=0)
    sc = _mm(qs, kc_ref[...].astype(BF16), _NT) + jnp.concatenate([bc_ref[...]] * 2, axis=0)
    sn = _mm(qs, kn_ref[...].astype(BF16), _NT) + jnp.concatenate([bn_ref[...]] * 2, axis=0)
    mx = jnp.maximum(jnp.max(sc, axis=-1, keepdims=True), jnp.max(sn, axis=-1, keepdims=True))
    pc = jnp.exp(sc - mx)
    pn = jnp.exp(sn - mx)
    den = jnp.sum(pc, axis=-1, keepdims=True) + jnp.sum(pn, axis=-1, keepdims=True)
    acc = _mm(pc.astype(BF16), vc_ref[...].astype(BF16)) + _mm(pn.astype(BF16), vn_ref[...].astype(BF16))
    on = acc / den
    lam = _lambda(lqk_ref, lam_init)
    o = on[:T, :] - lam * on[T:, :]
    o_ref[...] = _subln(o, subln_ref[...], lam_init).astype(BF16)


def _attn_sample(q, k, v, k_cache, v_cache, bias_c, bias_n, lqk, subln, lam_init, n_seq, T):
    m, d_diff = q.shape
    n_heads = d_diff // LANES
    past = k_cache.shape[1]
    return pl.pallas_call(
        functools.partial(_attn_sample_kernel, T=T, lam_init=lam_init),
        grid=(n_seq, n_heads),
        in_specs=[
            pl.BlockSpec((T, LANES), lambda b, h: (b, h)),
            pl.BlockSpec((T, LANES), lambda b, h: (b, h)),
            pl.BlockSpec((T, LANES), lambda b, h: (b, h)),
            pl.BlockSpec((None, past, LANES), lambda b, h: (b, 0, h)),
            pl.BlockSpec((None, past, LANES), lambda b, h: (b, 0, h)),
            pl.BlockSpec((None, T, past), lambda b, h: (h, 0, 0)),
            pl.BlockSpec((None, T, T), lambda b, h: (h, 0, 0)),
            pl.BlockSpec((4, DIFF_HEAD), lambda b, h: (0, 0)),
            pl.BlockSpec((1, LANES), lambda b, h: (0, 0)),
        ],
        out_specs=pl.BlockSpec((T, LANES), lambda b, h: (b, h)),
        out_shape=jax.ShapeDtypeStruct((m, d_diff), BF16),
        compiler_params=_cparams(("arbitrary", "arbitrary")),
        name="attn_sample",
    )(q, k, v, k_cache, v_cache, bias_c, bias_n, lqk, subln)


def _outproj_kernel(x_ref, yr_ref, yd_ref, gt_ref, wt_ref, wb_ref, o_ref):
    mixed = _mm(yr_ref[...], wt_ref[...]) + _mm(yd_ref[...], wb_ref[...])
    o_ref[...] = x_ref[...] + gt_ref[...] * mixed


def _outproj(x, yr, yd, gt, w_out, tm, tn=512):
    m, d = x.shape
    dr = yr.shape[1]
    dd = yd.shape[1]
    gt_spec = (pl.BlockSpec((1, tn), lambda i, j: (0, j)) if gt.shape[0] == 1
               else pl.BlockSpec((tm, tn), lambda i, j: (i, j)))
    return pl.pallas_call(
        _outproj_kernel,
        grid=(m // tm, d // tn),
        in_specs=[
            pl.BlockSpec((tm, tn), lambda i, j: (i, j)),
            pl.BlockSpec((tm, dr), lambda i, j: (i, 0)),
            pl.BlockSpec((tm, dd), lambda i, j: (i, 0)),
            gt_spec,
            pl.BlockSpec((dr, tn), lambda i, j: (0, j)),
            pl.BlockSpec((dd, tn), lambda i, j: (dr // dd, j)),
        ],
        out_specs=pl.BlockSpec((tm, tn), lambda i, j: (i, j)),
        out_shape=jax.ShapeDtypeStruct((m, d), F32),
        compiler_params=_cparams(("arbitrary", "arbitrary")),
        name="outproj",
    )(x, yr, yd, gt, w_out, w_out)


def _shift_rows(x, prev, n):
    if n % SUBLANES == 0:
        return jnp.concatenate([prev, x[:x.shape[0] - n, :]], axis=0)
    rolled = pltpu.roll(x, n, 0)
    row = lax.broadcasted_iota(jnp.int32, x.shape, 0)
    out = rolled
    for r in range(n):
        out = jnp.where(row == r, prev[r:r + 1, :], out)
    return out


def _ffn_kernel(x_ref, sh_ref, sc_ref, gt_ref, g_ref, wg_ref, wv_ref, cw_ref, cb_ref, c0_ref, wd_ref, gf_ref,
                o_ref, co_ref, h_scr, acc_scr, carry, *, B, n_j, final_norm):
    i = pl.program_id(0)
    j = pl.program_id(1)
    tm = x_ref.shape[0]

    @pl.when(j == 0)
    def _():
        xn = _rms(x_ref[...], g_ref[...], NORM_EPS)
        h_scr[...] = (xn * (1.0 + sc_ref[...]) + sh_ref[...]).astype(BF16)
        acc_scr[...] = jnp.zeros(acc_scr.shape, F32)

    @pl.when(i == 0)
    def _():
        carry[j, 0:2 * B, :] = c0_ref[...]

    h = h_scr[...]
    ug = _mm(h, wg_ref[...])
    uv = _mm(h, wv_ref[...])
    prev = carry[j, 0:2 * B, :]
    s1 = _shift_rows(ug, prev[B:2 * B, :], B)
    s2 = _shift_rows(ug, prev, 2 * B)
    cw = cw_ref[...]
    z = cb_ref[...] + s2 * cw[0:1, :] + s1 * cw[1:2, :] + ug * cw[2:3, :]
    act = 0.5 * z * (1.0 + lax.erf(z * (2.0 ** -0.5)))
    acc_scr[...] += _mm((act * uv).astype(BF16), wd_ref[...])
    last2 = ug[tm - 2 * B:tm, :]
    carry[j, 0:2 * B, :] = last2
    tf = last2.shape[1]
    co_ref[:, pl.ds(pl.multiple_of(j * tf, tf), tf)] = last2

    @pl.when(j == n_j - 1)
    def _():
        x2 = x_ref[...] + gt_ref[...] * acc_scr[...]
        if final_norm:
            x2 = _rms(x2, gf_ref[...], NORM_EPS)
        o_ref[...] = x2


def _ffn(x, sh, sc, gt, g, w_up, conv_w, conv_b, conv0, w_down, g_final, B, tm, final_norm, tf=512):
    m, d = x.shape
    d_ff = w_down.shape[0]
    n_j = d_ff // tf
    rows_c = max(SUBLANES, 2 * B)
    return pl.pallas_call(
        functools.partial(_ffn_kernel, B=B, n_j=n_j, final_norm=final_norm),
        grid=(m // tm, n_j),
        in_specs=[
            pl.BlockSpec((tm, d), lambda i, j: (i, 0)),
            _mod_spec(sh, tm, d),
            _mod_spec(sc, tm, d),
            _mod_spec(gt, tm, d),
            pl.BlockSpec((1, d), lambda i, j: (0, 0)),
            pl.BlockSpec((d, tf), lambda i, j: (0, j)),
            pl.BlockSpec((d, tf), lambda i, j: (0, n_j + j)),
            pl.BlockSpec((CONV_W, tf), lambda i, j: (0, j)),
            pl.BlockSpec((1, tf), lambda i, j: (0, j)),
            pl.BlockSpec((2 * B, tf), lambda i, j: (0, j)),
            pl.BlockSpec((tf, d), lambda i, j: (j, 0)),
            pl.BlockSpec((1, d), lambda i, j: (0, 0)),
        ],
        out_specs=[
            pl.BlockSpec((tm, d), lambda i, j: (i, 0)),
            pl.BlockSpec((2 * B, d_ff), lambda i, j: (0, 0)),
        ],
        out_shape=[
            jax.ShapeDtypeStruct((m, d), F32),
            jax.ShapeDtypeStruct((2 * B, d_ff), F32),
        ],
        scratch_shapes=[pltpu.VMEM((tm, d), BF16), pltpu.VMEM((tm, d), F32),
                        pltpu.VMEM((n_j, rows_c, tf), F32)],
        compiler_params=_cparams(("arbitrary", "arbitrary")),
        name="conv_ffn",
    )(x, sh, sc, gt, g, w_up, w_up, conv_w, conv_b, conv0, w_down, g_final)


def _pad_cols(a, n):
    return jnp.pad(a, [(0, 0)] * (a.ndim - 1) + [(0, n)])


def _regroup_rwkv_cols(a, d_rwkv):
    o = 3 * d_rwkv
    return jnp.concatenate([
        a[..., :o],
        _pad_cols(a[..., o:o + DECAY_LORA], LORA_PAD - DECAY_LORA),
        _pad_cols(a[..., o + DECAY_LORA:o + DECAY_LORA + AAA_LORA], LORA_PAD - AAA_LORA),
        a[..., o + DECAY_LORA + AAA_LORA:],
    ], axis=-1)


def _ungroup_rwkv_cols(a, d_rwkv):
    o = 3 * d_rwkv
    return jnp.concatenate([
        a[..., :o],
        a[..., o:o + DECAY_LORA],
        a[..., o + LORA_PAD:o + LORA_PAD + AAA_LORA],
        a[..., o + 2 * LORA_PAD:],
    ], axis=-1)


def kernel(x_prompt, x_sample, c_prompt, c_sample, cache_k, cache_v, state_wkv, state_shift, state_conv, w_ada, b_ada, g_mix, g_ffn, w_in, w_out, rwkv_mu, rwkv_w0, rwkv_w2, rwkv_a0, rwkv_a2, rwkv_g2, rwkv_kk, rwkv_ka, rwkv_rk, rwkv_ln_w, rwkv_ln_b, diff_lq1, diff_lk1, diff_lq2, diff_lk2, diff_subln, rel_table, ffn_up, ffn_conv_w, ffn_conv_b, ffn_down, g_final):
    depth, d_model, _ = w_in.shape
    bp, seq, _ = x_prompt.shape
    bs, dseq, _ = x_sample.shape
    past = cache_k.shape[2]
    d_rwkv = rwkv_w0.shape[1]
    d_diff = d_model - d_rwkv
    n_dheads = d_diff // (2 * DIFF_HEAD)
    n_rheads = d_rwkv // RWKV_HEAD
    n_rwkv_cols = rwkv_mu.shape[1]
    n_rwkv_pad = n_rwkv_cols + 2 * LORA_PAD - DECAY_LORA - AAA_LORA
    d_ff = ffn_down.shape[1]
    assert bp == 1, "prompt path handles one sequence"
    TB = min(256, seq)
    tm_p = min(512, seq)
    m_s = bs * dseq

    w_in_b = jnp.concatenate([_regroup_rwkv_cols(w_in[..., :n_rwkv_cols], d_rwkv), w_in[..., n_rwkv_cols:]],
                             axis=-1).astype(BF16)
    mu_p = _regroup_rwkv_cols(rwkv_mu, d_rwkv)
    w2_b = jnp.pad(rwkv_w2, ((0, 0), (0, LORA_PAD - DECAY_LORA), (0, 0))).astype(BF16)
    a2_b = jnp.pad(rwkv_a2, ((0, 0), (0, LORA_PAD - AAA_LORA), (0, 0))).astype(BF16)
    g2_b = rwkv_g2.astype(BF16)
    w_out_b = w_out.astype(BF16)
    up_b = ffn_up.astype(BF16)
    down_b = ffn_down.astype(BF16)
    rk_flat = rwkv_rk.reshape(depth, d_rwkv)
    lqk = jnp.stack([diff_lq1, diff_lk1, diff_lq2, diff_lk2], axis=1)

    n_c = bp + bs
    n_c_pad = -(-n_c // SUBLANES) * SUBLANES
    c_rows = jnp.pad(jnp.concatenate([c_prompt, c_sample], axis=0), ((0, n_c_pad - n_c), (0, 0)))
    mod = _adaln_mod(c_rows, w_ada, b_ada).reshape(depth, n_c_pad, 6, d_model)

    ql = jnp.arange(TB)
    rel_diag = ql[None, :] - ql[:, None]
    allowed = (ql[None, :] // CHUNK) <= (ql[:, None] // CHUNK)
    bucket_p = jnp.concatenate([jnp.where(allowed, _t5_bucket(rel_diag), -1), _t5_bucket(rel_diag - TB)], axis=0)
    bias_p = _bias_lookup(rel_table, bucket_p.astype(jnp.int32), FAR_BUCKET).reshape(n_dheads, 2, TB, TB)
    q_pos = past + jnp.arange(dseq)
    bias_c = _bias_lookup(rel_table, _t5_bucket(jnp.arange(past)[None, :] - q_pos[:, None]).astype(jnp.int32), FAR_BUCKET)
    bias_n = _bias_lookup(rel_table, _t5_bucket(q_pos[None, :] - q_pos[:, None]).astype(jnp.int32), FAR_BUCKET)

    xp = x_prompt.reshape(seq, d_model)
    xs = x_sample.reshape(m_s, d_model)
    zero_shift = jnp.zeros((1, 1, n_rwkv_pad), F32)
    zero_state = jnp.zeros((1, n_rheads, RWKV_HEAD, RWKV_HEAD), F32)
    zero_conv = jnp.zeros((2, d_ff), F32)
    shift_s_in = _regroup_rwkv_cols(state_shift, d_rwkv)
    outs_p = {k: [] for k in ("k", "v", "wkv", "shift", "conv")}
    outs_s = {k: [] for k in ("k", "v", "wkv", "shift", "conv")}

    def time_major(a):
        return a.reshape(bs, dseq, -1).swapaxes(0, 1).reshape(m_s, -1)

    def batch_major(a):
        return a.reshape(dseq, bs, -1).swapaxes(0, 1).reshape(m_s, -1)

    for l in range(depth):
        lam_init = 0.8 - 0.6 * math.exp(-0.3 * l)
        last = l == depth - 1
        wkv_args = (w2_b[l], a2_b[l], g2_b[l], rwkv_w0[l][None], rwkv_a0[l][None], rwkv_kk[l][None],
                    rwkv_ka[l][None], rk_flat[l][None], rwkv_ln_w[l][None], rwkv_ln_b[l][None])
        g_mix_l, g_ffn_l, gf = g_mix[l][None], g_ffn[l][None], g_final[None]
        subln = diff_subln[l][None]

        mp = mod[l, 0:bp]
        sh1, sc1, gt1, sh2, sc2, gt2 = (mp[:, t] for t in range(6))
        p, q, k, v = _inproj(xp, sh1, sc1, g_mix_l, w_in_b[l], n_rwkv_pad, d_diff, tm_p)
        y_r, wkv = _wkv(p, mu_p[l][None], zero_shift, *wkv_args, zero_state, 1, seq, min(64, seq))
        y_d = _attn_prompt(q, k, v, bias_p, lqk[l], subln, lam_init, TB)
        x1 = _outproj(xp, y_r, y_d, gt1, w_out_b[l], tm_p)
        xp, conv = _ffn(x1, sh2, sc2, gt2, g_ffn_l, up_b[l], ffn_conv_w[l], ffn_conv_b[l][None], zero_conv,
                        down_b[l], gf, 1, tm_p, last)
        outs_p["k"].append(k.reshape(bp, seq, n_dheads, 2 * DIFF_HEAD))
        outs_p["v"].append(v.reshape(bp, seq, n_dheads, 2 * DIFF_HEAD))
        outs_p["wkv"].append(wkv)
        outs_p["shift"].append(_ungroup_rwkv_cols(p[seq - 1:seq], d_rwkv).reshape(bp, 1, n_rwkv_cols))
        outs_p["conv"].append(conv.reshape(bp, CONV_W - 1, d_ff))

        ms = mod[l, bp:bp + bs]
        rows_bm = jnp.repeat(ms, dseq, axis=0)
        rows_tm = jnp.tile(ms, (dseq, 1, 1))
        p, q, k, v = _inproj(xs, rows_bm[:, 0], rows_bm[:, 1], g_mix_l, w_in_b[l], n_rwkv_pad, d_diff, m_s)
        y_r, wkv = _wkv(p, mu_p[l][None], shift_s_in[l], *wkv_args, state_wkv[l], bs, dseq, dseq)
        y_d = _attn_sample(q, k, v, cache_k[l].reshape(bs, past, d_diff), cache_v[l].reshape(bs, past, d_diff),
                           bias_c, bias_n, lqk[l], subln, lam_init, bs, dseq)
        x1 = _outproj(xs, y_r, y_d, rows_bm[:, 2], w_out_b[l], m_s)
        conv0 = state_conv[l].swapaxes(0, 1).reshape((CONV_W - 1) * bs, d_ff)
        x2, conv = _ffn(time_major(x1), rows_tm[:, 3], rows_tm[:, 4], rows_tm[:, 5], g_ffn_l, up_b[l],
                        ffn_conv_w[l], ffn_conv_b[l][None], conv0, down_b[l], gf, bs, m_s, last)
        xs = batch_major(x2)
        outs_s["k"].append(k.reshape(bs, dseq, n_dheads, 2 * DIFF_HEAD))
        outs_s["v"].append(v.reshape(bs, dseq, n_dheads, 2 * DIFF_HEAD))
        outs_s["wkv"].append(wkv)
        p_last = p.reshape(bs, dseq, n_rwkv_pad)[:, dseq - 1:dseq]
        outs_s["shift"].append(_ungroup_rwkv_cols(p_last, d_rwkv))
        outs_s["conv"].append(conv.reshape(CONV_W - 1, bs, d_ff).swapaxes(0, 1))

    st = lambda xs_: jnp.stack(xs_)
    return (xp.reshape(bp, seq, d_model), xs.reshape(bs, dseq, d_model),
            st(outs_p["k"]), st(outs_p["v"]), st(outs_p["wkv"]), st(outs_p["shift"]), st(outs_p["conv"]),
            st(outs_s["k"]), st(outs_s["v"]), st(outs_s["wkv"]), st(outs_s["shift"]), st(outs_s["conv"]))
```

```python
import functools
import math

import jax
import jax.numpy as jnp
from jax import lax
from jax.experimental import pallas as pl
from jax.experimental.pallas import tpu as pltpu

F32 = jnp.float32
BF16 = jnp.bfloat16

CHUNK = 64
RWKV_HEAD = 64
DIFF_HEAD = 64
N_BUCKETS = 32
MAX_DISTANCE = 128
NORM_EPS = 1e-6
GN_EPS = 64e-5
SUBLN_EPS = 1e-5
DECAY_LORA = 96
AAA_LORA = 96
GATE_LORA = 256
LORA_PAD = 128
CONV_W = 3

LANES = 128
SUBLANES = 8
VMEM_LIMIT_BYTES = 56 * 1024 * 1024
MASK_VALUE = -1e30
WKV_PAIRS_PER_STEP = 4

_NN = (((1,), (0,)), ((), ()))
_NT = (((1,), (1,)), ((), ()))
_TN = (((0,), (0,)), ((), ()))


def _mm(a, b, dims=_NN):
    return lax.dot_general(a, b, dims, preferred_element_type=F32)


def _hi_lo(x):
    h = x.astype(BF16)
    return h, (x - h.astype(F32)).astype(BF16)


def _mm3(a, b, dims=_NN):
    ah, al = _hi_lo(a)
    bh, bl = _hi_lo(b)
    (ca,), (cb,) = dims[0]
    k = a.shape[ca]
    if (ca == 0 or k % LANES == 0) and (cb == 0 or k % LANES == 0):
        return _mm(jnp.concatenate([ah, al, ah], axis=ca), jnp.concatenate([bh, bh, bl], axis=cb), dims)
    return _mm(ah, bh, dims) + (_mm(al, bh, dims) + _mm(ah, bl, dims))


def _sel_rows(sel2, x):
    xh, xl = _hi_lo(x)
    return _mm(sel2, jnp.concatenate([xh, xl], axis=0))


def _sel_lanes(x, sel2):
    xh, xl = _hi_lo(x)
    return _mm(jnp.concatenate([xh, xl], axis=1), sel2)


def _cparams(sem):
    return pltpu.CompilerParams(dimension_semantics=sem, vmem_limit_bytes=VMEM_LIMIT_BYTES)


def _rms(x, g, eps):
    return x * lax.rsqrt(jnp.mean(x * x, axis=-1, keepdims=True) + eps) * g


def _mod_kernel(c_ref, w_ref, b_ref, o_ref):
    c = c_ref[...]
    s = (c * jax.nn.sigmoid(c)).astype(BF16)
    o_ref[...] = _mm(s, w_ref[...].astype(BF16)) + b_ref[...]


def _adaln_mod(c_rows, w_ada, b_ada, tn=1024):
    depth, d, n = w_ada.shape
    rows = c_rows.shape[0]
    return pl.pallas_call(
        _mod_kernel,
        grid=(depth, n // tn),
        in_specs=[
            pl.BlockSpec((rows, d), lambda l, j: (0, 0)),
            pl.BlockSpec((None, d, tn), lambda l, j: (l, 0, j)),
            pl.BlockSpec((None, 1, tn), lambda l, j: (l, 0, j)),
        ],
        out_specs=pl.BlockSpec((None, rows, tn), lambda l, j: (l, 0, j)),
        out_shape=jax.ShapeDtypeStruct((depth, rows, n), F32),
        compiler_params=_cparams(("arbitrary", "arbitrary")),
        name="adaln_mod",
    )(c_rows, w_ada, b_ada.reshape(depth, 1, n))


def _bias_kernel(tab_ref, bucket_ref, o_ref, *, far_bucket):
    h = pl.program_id(0)
    bucket = bucket_ref[...]
    far = tab_ref[far_bucket, h]
    acc = jnp.full(bucket.shape, MASK_VALUE, F32)
    for b in range(N_BUCKETS):
        acc = jnp.where(bucket == b, tab_ref[b, h] - far, acc)
    o_ref[...] = acc


def _bias_lookup(table, bucket, far_bucket):
    n_heads = table.shape[1]
    r, c = bucket.shape
    return pl.pallas_call(
        functools.partial(_bias_kernel, far_bucket=far_bucket),
        grid=(n_heads,),
        in_specs=[
            pl.BlockSpec(memory_space=pltpu.SMEM),
            pl.BlockSpec((r, c), lambda h: (0, 0)),
        ],
        out_specs=pl.BlockSpec((None, r, c), lambda h: (h, 0, 0)),
        out_shape=jax.ShapeDtypeStruct((n_heads, r, c), F32),
        compiler_params=_cparams(("arbitrary",)),
        name="bias_lookup",
    )(table, bucket)


def _t5_bucket(rel):
    nb = N_BUCKETS // 2
    max_exact = nb // 2
    n = jnp.abs(rel)
    nf = jnp.maximum(n, 1).astype(F32)
    large = max_exact + (jnp.log(nf / max_exact) / math.log(MAX_DISTANCE / max_exact) * (nb - max_exact)).astype(jnp.int32)
    large = jnp.minimum(large, nb - 1)
    return jnp.where(rel > 0, nb, 0) + jnp.where(n < max_exact, n, large)


FAR_BUCKET = N_BUCKETS // 2 - 1


def _inproj_kernel(x_ref, sh_ref, sc_ref, g_ref, w_ref, p_ref, q_ref, k_ref, v_ref, h_scr, *, n_p, n_q):
    j = pl.program_id(1)

    @pl.when(j == 0)
    def _():
        xn = _rms(x_ref[...], g_ref[...], NORM_EPS)
        h_scr[...] = (xn * (1.0 + sc_ref[...]) + sh_ref[...]).astype(BF16)

    r = _mm(h_scr[...], w_ref[...])

    @pl.when(j < n_p)
    def _():
        p_ref[...] = r

    @pl.when((j >= n_p) & (j < n_p + n_q))
    def _():
        q_ref[...] = (r * (DIFF_HEAD ** -0.5)).astype(BF16)

    @pl.when((j >= n_p + n_q) & (j < n_p + 2 * n_q))
    def _():
        k_ref[...] = r

    @pl.when(j >= n_p + 2 * n_q)
    def _():
        v_ref[...] = r


def _mod_spec(mod, tm, d):
    if mod.shape[0] == 1:
        return pl.BlockSpec((1, d), lambda i, j: (0, 0))
    return pl.BlockSpec((tm, d), lambda i, j: (i, 0))


def _inproj(x, sh, sc, g, w, n_rwkv_pad, d_diff, tm, tn=512):
    m, d = x.shape
    n = w.shape[1]
    n_p, n_q = n_rwkv_pad // tn, d_diff // tn
    clip = lambda j, lo: jnp.clip(j - lo, 0, n_q - 1)
    return pl.pallas_call(
        functools.partial(_inproj_kernel, n_p=n_p, n_q=n_q),
        grid=(m // tm, n // tn),
        in_specs=[
            pl.BlockSpec((tm, d), lambda i, j: (i, 0)),
            _mod_spec(sh, tm, d),
            _mod_spec(sc, tm, d),
            pl.BlockSpec((1, d), lambda i, j: (0, 0)),
            pl.BlockSpec((d, tn), lambda i, j: (0, j)),
        ],
        out_specs=[
            pl.BlockSpec((tm, tn), lambda i, j: (i, jnp.minimum(j, n_p - 1))),
            pl.BlockSpec((tm, tn), lambda i, j: (i, clip(j, n_p))),
            pl.BlockSpec((tm, tn), lambda i, j: (i, clip(j, n_p + n_q))),
            pl.BlockSpec((tm, tn), lambda i, j: (i, clip(j, n_p + 2 * n_q))),
        ],
        out_shape=[
            jax.ShapeDtypeStruct((m, n_rwkv_pad), F32),
            jax.ShapeDtypeStruct((m, d_diff), BF16),
            jax.ShapeDtypeStruct((m, d_diff), F32),
            jax.ShapeDtypeStruct((m, d_diff), F32),
        ],
        scratch_shapes=[pltpu.VMEM((tm, d), BF16)],
        compiler_params=_cparams(("arbitrary", "arbitrary")),
        name="inproj",
    )(x, sh, sc, g, w)


def _wkv_kernel(pr_ref, pk_ref, pv_ref, pw_ref, pa_ref, pg_ref,
                mur_ref, muk_ref, muv_ref, muw_ref, mua_ref, mug_ref,
                s0r_ref, s0k_ref, s0v_ref, s0w_ref, s0a_ref, s0g_ref,
                w2_ref, a2_ref, g2_ref,
                w0_ref, a0_ref, kkp_ref, ka_ref, rk_ref, lnw_ref, lnb_ref,
                st0_ref,
                y_ref, st_ref,
                sbd, cr, ck, cv, cw, ca, cg, *, C, n_chunks, n_pairs):
    chunk = pl.program_id(2)
    hd = RWKV_HEAD
    zero_blk = jnp.zeros((hd, hd), F32)

    @pl.when(chunk == 0)
    def _():
        for c_ref, s_ref in ((cr, s0r_ref), (ck, s0k_ref), (cv, s0v_ref), (cw, s0w_ref), (ca, s0a_ref), (cg, s0g_ref)):
            c_ref[0:1, :] = s_ref[...]
        for pi in range(n_pairs):
            top = jnp.concatenate([st0_ref[2 * pi], zero_blk], axis=1)
            bot = jnp.concatenate([zero_blk, st0_ref[2 * pi + 1]], axis=1)
            sbd[pi] = jnp.concatenate([top, bot], axis=0)

    def tshift(x_ref, mu_ref, c_ref):
        x = x_ref[...]
        row = lax.broadcasted_iota(jnp.int32, x.shape, 0)
        prev = jnp.where(row == 0, c_ref[0:1, :], pltpu.roll(x, 1, 0))
        c_ref[0:1, :] = x[C - 1:C, :]
        return x + (prev - x) * mu_ref[...]

    xr = tshift(pr_ref, mur_ref, cr)
    xk = tshift(pk_ref, muk_ref, ck)
    xv = tshift(pv_ref, muv_ref, cv)
    xw = tshift(pw_ref, muw_ref, cw)
    xa = tshift(pa_ref, mua_ref, ca)
    xg = tshift(pg_ref, mug_ref, cg)

    lane = lax.broadcasted_iota(jnp.int32, (1, LANES), 1)
    head0 = lane < hd
    li = lax.broadcasted_iota(jnp.int32, (LANES, LANES), 0)
    lj = lax.broadcasted_iota(jnp.int32, (LANES, LANES), 1)
    same_head = (li < hd) == (lj < hd)
    li2 = lax.broadcasted_iota(jnp.int32, (2 * LANES, LANES), 0)
    lj2 = lax.broadcasted_iota(jnp.int32, (2 * LANES, LANES), 1)
    li2 = jnp.where(li2 < LANES, li2, li2 - LANES)
    seg2 = jnp.where((li2 < hd) == (lj2 < hd), 1.0, 0.0).astype(BF16)

    def segsum(x):
        return _sel_lanes(x, seg2)

    wlin = w0_ref[...] + _mm(jnp.tanh(xw).astype(BF16), w2_ref[...])
    z = -wlin
    w_log = -(jnp.maximum(z, 0.0) + jnp.log1p(jnp.exp(-jnp.abs(z)))) - 0.5
    ne = -jnp.exp(w_log)
    a = jax.nn.sigmoid(a0_ref[...] + _mm(xa.astype(BF16), a2_ref[...]))
    g = _mm(jax.nn.sigmoid(xg).astype(BF16), g2_ref[...])
    kk_raw = xk * kkp_ref[...]
    k2 = xk * (1.0 + (a - 1.0) * ka_ref[...])
    rk_prod = xr * k2 * rk_ref[...]

    ti = lax.broadcasted_iota(jnp.int32, (C, 2 * C), 0)
    tj = lax.broadcasted_iota(jnp.int32, (C, 2 * C), 1)
    tj = jnp.where(tj < C, tj, tj - C)
    tri2 = jnp.where(tj <= ti, 1.0, 0.0).astype(BF16)
    cum = _sel_rows(tri2, ne)
    gam = jnp.exp(cum)
    gprev = jnp.exp(cum - ne)
    ginv = jnp.exp(-cum)
    dte = jnp.exp(cum[C - 1:C, :] - cum)
    rg_all = xr * gam
    ks_all = k2 * ginv
    kd_all = k2 * dte
    lnw = lnw_ref[...]
    lnb = lnb_ref[...]

    def stack_masked(x):
        return jnp.concatenate([jnp.where(head0, x, 0.0), jnp.where(head0, 0.0, x)], axis=0)

    def stack_twice(x):
        return jnp.concatenate([x, x], axis=0)

    R = 2 * C
    ri = lax.broadcasted_iota(jnp.int32, (R, R), 0)
    ci = lax.broadcasted_iota(jnp.int32, (R, R), 1)
    same_blk = (ri < C) == (ci < C)
    tr = jnp.where(ri < C, ri, ri - C)
    tc = jnp.where(ci < C, ci, ci - C)
    strict = same_blk & (tc < tr)
    incl = same_blk & (tc <= tr)
    n_steps = max(1, int(math.log2(C)))
    inv_hd = 1.0 / hd

    pairs = range(n_pairs)
    sls = [slice(pi * LANES, (pi + 1) * LANES) for pi in pairs]
    xv_p = [xv[:, sl] for sl in sls]
    sums = [segsum(jnp.concatenate([kk_raw[:, sl] * kk_raw[:, sl], rk_prod[:, sl]], axis=0)) for sl in sls]
    kk = [kk_raw[:, sl] * lax.rsqrt(jnp.maximum(sm[:C, :], 1e-24)) for sl, sm in zip(sls, sums)]
    bonus = [sm[C:, :] * v for sm, v in zip(sums, xv_p)]
    kka = [k * a[:, sl] for k, sl in zip(kk, sls)]
    lhs = [jnp.concatenate([stack_masked(-k * gprev[:, sl]), stack_masked(rg_all[:, sl])], axis=0)
           for k, sl in zip(kk, sls)]
    rhs = [jnp.concatenate([stack_twice(ka_ * ginv[:, sl]), stack_twice(ks_all[:, sl])], axis=0)
           for ka_, sl in zip(kka, sls)]
    vst = [stack_masked(v) for v in xv_p]
    sc4 = [_mm3(l_, r_, _NT) for l_, r_ in zip(lhs, rhs)]
    s_old = [sbd[pi] for pi in pairs]
    ls = [_mm3(l_, s_, _NT) for l_, s_ in zip(lhs, s_old)]
    n_ab = [jnp.where(strict, s4[:R, :R], 0.0) for s4 in sc4]
    m_ak = [jnp.where(strict, s4[:R, R:], 0.0) for s4 in sc4]
    m_r = [jnp.concatenate([jnp.where(incl, s4[R:, :R], 0.0), jnp.where(incl, s4[R:, R:], 0.0)], axis=1)
           for s4 in sc4]
    x_u = [l_[:R, :] + _mm3(mk, v) for l_, mk, v in zip(ls, m_ak, vst)]
    pw = n_ab
    for step in range(n_steps):
        if step + 1 < n_steps:
            t = [_mm3(p_, jnp.concatenate([x_, p_], axis=1)) for p_, x_ in zip(pw, x_u)]
            x_u = [x_ + t_[:, :LANES] for x_, t_ in zip(x_u, t)]
            pw = [t_[:, LANES:] for t_ in t]
        else:
            x_u = [x_ + _mm3(p_, x_) for p_, x_ in zip(pw, x_u)]
    y_st = [l_[R:, :] + _mm3(mr, jnp.concatenate([x_, v], axis=0))
            for l_, mr, x_, v in zip(ls, m_r, x_u, vst)]
    upd = [_mm3(jnp.concatenate([x_[:C, :] + x_[C:, :], v], axis=0),
                jnp.concatenate([ka_ * dte[:, sl], kd_all[:, sl]], axis=0), _TN)
           for x_, v, ka_, sl in zip(x_u, xv_p, kka, sls)]
    for pi in pairs:
        sbd[pi] = s_old[pi] * gam[C - 1:C, sls[pi]] + jnp.where(same_head, upd[pi], 0.0)

    y = [ys[:C, :] + ys[C:, :] for ys in y_st]
    yc = [y_ - segsum(y_) * inv_hd for y_ in y]
    y_var = [segsum(c_ * c_) * inv_hd for c_ in yc]
    for pi in pairs:
        sl = sls[pi]
        yn = yc[pi] * lax.rsqrt(y_var[pi] + GN_EPS) * lnw[:, sl] + lnb[:, sl]
        y_ref[:, sl] = ((yn + bonus[pi]) * g[:, sl]).astype(BF16)

    @pl.when(chunk == n_chunks - 1)
    def _():
        for pi in range(n_pairs):
            s_new = sbd[pi]
            st_ref[2 * pi] = s_new[0:hd, 0:hd]
            st_ref[2 * pi + 1] = s_new[hd:2 * hd, hd:2 * hd]


def _wkv(p, mu, shift0, w2, a2, g2, w0, a0, kkp, ka, rk, lnw, lnb, state0, n_seq, seq_len, C):
    m, n_pad = p.shape
    d_rwkv = w0.shape[1]
    n_heads = d_rwkv // RWKV_HEAD
    n_pairs = WKV_PAIRS_PER_STEP
    wd = n_pairs * LANES
    n_groups = d_rwkv // wd
    n_chunks = seq_len // C
    nb = n_groups
    blk_w = 3 * d_rwkv // LANES
    blk_a = blk_w + 1
    blk_g = (blk_a + 1) * LANES // GATE_LORA

    row = lambda s, h, c: s * n_chunks + c
    seg_specs = [
        pl.BlockSpec((C, wd), lambda s, h, c: (row(s, h, c), h)),
        pl.BlockSpec((C, wd), lambda s, h, c: (row(s, h, c), nb + h)),
        pl.BlockSpec((C, wd), lambda s, h, c: (row(s, h, c), 2 * nb + h)),
        pl.BlockSpec((C, LANES), lambda s, h, c: (row(s, h, c), blk_w)),
        pl.BlockSpec((C, LANES), lambda s, h, c: (row(s, h, c), blk_a)),
        pl.BlockSpec((C, GATE_LORA), lambda s, h, c: (row(s, h, c), blk_g)),
    ]
    mu_specs = [
        pl.BlockSpec((1, wd), lambda s, h, c: (0, h)),
        pl.BlockSpec((1, wd), lambda s, h, c: (0, nb + h)),
        pl.BlockSpec((1, wd), lambda s, h, c: (0, 2 * nb + h)),
        pl.BlockSpec((1, LANES), lambda s, h, c: (0, blk_w)),
        pl.BlockSpec((1, LANES), lambda s, h, c: (0, blk_a)),
        pl.BlockSpec((1, GATE_LORA), lambda s, h, c: (0, blk_g)),
    ]
    s0_specs = [
        pl.BlockSpec((None, 1, wd), lambda s, h, c: (s, 0, h)),
        pl.BlockSpec((None, 1, wd), lambda s, h, c: (s, 0, nb + h)),
        pl.BlockSpec((None, 1, wd), lambda s, h, c: (s, 0, 2 * nb + h)),
        pl.BlockSpec((None, 1, LANES), lambda s, h, c: (s, 0, blk_w)),
        pl.BlockSpec((None, 1, LANES), lambda s, h, c: (s, 0, blk_a)),
        pl.BlockSpec((None, 1, GATE_LORA), lambda s, h, c: (s, 0, blk_g)),
    ]
    lora_specs = [
        pl.BlockSpec((LORA_PAD, wd), lambda s, h, c: (0, h)),
        pl.BlockSpec((LORA_PAD, wd), lambda s, h, c: (0, h)),
        pl.BlockSpec((GATE_LORA, wd), lambda s, h, c: (0, h)),
    ]
    vec_spec = pl.BlockSpec((1, wd), lambda s, h, c: (0, h))
    state_spec = pl.BlockSpec((None, 2 * n_pairs, RWKV_HEAD, RWKV_HEAD), lambda s, h, c: (s, h, 0, 0))
    y, st = pl.pallas_call(
        functools.partial(_wkv_kernel, C=C, n_chunks=n_chunks, n_pairs=n_pairs),
        grid=(n_seq, n_groups, n_chunks),
        in_specs=seg_specs + mu_specs + s0_specs + lora_specs + [vec_spec] * 7 + [state_spec],
        out_specs=[
            pl.BlockSpec((C, wd), lambda s, h, c: (row(s, h, c), h)),
            state_spec,
        ],
        out_shape=[
            jax.ShapeDtypeStruct((m, d_rwkv), BF16),
            jax.ShapeDtypeStruct((n_seq, n_heads, RWKV_HEAD, RWKV_HEAD), F32),
        ],
        scratch_shapes=[pltpu.VMEM((n_pairs, LANES, LANES), F32)]
        + [pltpu.VMEM((SUBLANES, wd), F32)] * 3 + [pltpu.VMEM((SUBLANES, LANES), F32)] * 2
        + [pltpu.VMEM((SUBLANES, GATE_LORA), F32)],
        compiler_params=_cparams(("arbitrary", "arbitrary", "arbitrary")),
        name="wkv",
    )(p, p, p, p, p, p, mu, mu, mu, mu, mu, mu,
      shift0, shift0, shift0, shift0, shift0, shift0,
      w2, a2, g2, w0, a0, kkp, ka, rk, lnw, lnb, state0)
    return y, st


def _lambda(lqk_ref, lam_init):
    t = lqk_ref[...]
    s1 = jnp.sum(t[0:1, :] * t[1:2, :], axis=-1, keepdims=True)
    s2 = jnp.sum(t[2:3, :] * t[3:4, :], axis=-1, keepdims=True)
    return jnp.exp(s1) - jnp.exp(s2) + lam_init


def _split_maps(q):
    lane = lax.broadcasted_iota(jnp.int32, q.shape, 1)
    zero = jnp.zeros_like(q)
    return jnp.where(lane < DIFF_HEAD, q, zero), jnp.where(lane < DIFF_HEAD, zero, q)


def _subln(o, subln, lam_init):
    return _rms(o, subln, SUBLN_EPS) * (1.0 - lam_init)


FAR_TILES_LOG2 = 2


def _attn_kernel(q_ref, k_ref, v_ref, bias_ref, lqk_ref, subln_ref, o_ref,
                 kb, vb, m1, a1, m2, a2, *, TB, n_cast, lam_init):
    i = pl.program_id(1)

    @pl.when(i == 0)
    def _():
        ones = jnp.ones((TB, LANES), BF16)

        def cast(c, carry):
            off = pl.multiple_of(c * TB, TB)
            kb[pl.ds(off, TB), :] = k_ref[pl.ds(off, TB), :].astype(BF16)
            vb[pl.ds(off, TB), 0:LANES] = v_ref[pl.ds(off, TB), :].astype(BF16)
            vb[pl.ds(off, TB), LANES:2 * LANES] = ones
            return carry
        lax.fori_loop(0, n_cast, cast, 0)

    qa, qb = _split_maps(q_ref[...])
    for m_r, a_r in ((m1, a1), (m2, a2)):
        m_r[...] = jnp.full(m_r.shape, MASK_VALUE, F32)
        a_r[...] = jnp.zeros(a_r.shape, F32)

    def tile(off, width, bias):
        kt = kb[pl.ds(off, width), :]
        vt = vb[pl.ds(off, width), :]
        for qx, m_r, a_r in ((qa, m1, a1), (qb, m2, a2)):
            s = _mm(qx, kt, _NT)
            if bias is not None:
                s = s + bias
            m_prev = m_r[...]
            m_new = jnp.maximum(m_prev, jnp.max(s, axis=-1, keepdims=True))
            alpha = jnp.exp(m_prev - m_new)
            p = jnp.exp(s - m_new).astype(BF16)
            a_r[...] = alpha * a_r[...] + _mm(p, vt)
            m_r[...] = m_new

    n_far = jnp.maximum(i - 1, 0)
    big = TB << FAR_TILES_LOG2
    n_big = lax.shift_right_logical(n_far, FAR_TILES_LOG2)
    n_small = n_far - lax.shift_left(n_big, FAR_TILES_LOG2)

    def far_big(j, carry):
        tile(pl.multiple_of(j * big, big), big, None)
        return carry
    lax.fori_loop(0, n_big, far_big, 0)

    def far_small(j, carry):
        tile(pl.multiple_of((lax.shift_left(n_big, FAR_TILES_LOG2) + j) * TB, TB), TB, None)
        return carry
    lax.fori_loop(0, n_small, far_small, 0)

    bias = jnp.where(i == 0, bias_ref[1], bias_ref[0])
    tile(pl.multiple_of(n_far * TB, TB), 2 * TB, bias)

    lam = _lambda(lqk_ref, lam_init)
    o1 = a1[:, 0:LANES] / a1[:, LANES:2 * LANES]
    o2 = a2[:, 0:LANES] / a2[:, LANES:2 * LANES]
    o_ref[...] = _subln(o1 - lam * o2, subln_ref[...], lam_init).astype(BF16)


def _attn_prompt(q, k, v, bias, lqk, subln, lam_init, TB):
    m, d_diff = q.shape
    n_heads = d_diff // LANES
    n_blk = m // TB
    return pl.pallas_call(
        functools.partial(_attn_kernel, TB=TB, n_cast=n_blk, lam_init=lam_init),
        grid=(n_heads, n_blk),
        in_specs=[
            pl.BlockSpec((TB, LANES), lambda h, i: (i, h)),
            pl.BlockSpec((m, LANES), lambda h, i: (0, h)),
            pl.BlockSpec((m, LANES), lambda h, i: (0, h)),
            pl.BlockSpec((None, 2, TB, 2 * TB), lambda h, i: (h, 0, 0, 0)),
            pl.BlockSpec((4, DIFF_HEAD), lambda h, i: (0, 0)),
            pl.BlockSpec((1, LANES), lambda h, i: (0, 0)),
        ],
        out_specs=pl.BlockSpec((TB, LANES), lambda h, i: (i, h)),
        out_shape=jax.ShapeDtypeStruct((m, d_diff), BF16),
        scratch_shapes=[pltpu.VMEM((m, LANES), BF16), pltpu.VMEM((m, 2 * LANES), BF16)]
        + [pltpu.VMEM((TB, 1), F32), pltpu.VMEM((TB, 2 * LANES), F32)] * 2,
        compiler_params=_cparams(("arbitrary", "arbitrary")),
        name="attn_prompt",
    )(q, k, v, bias, lqk, subln)


def _attn_sample_kernel(q_ref, kn_ref, vn_ref, kc_ref, vc_ref, bc_ref, bn_ref, lqk_ref, subln_ref, o_ref,
                        *, T, lam_init):
    qa, qb = _split_maps(q_ref[...])
    qs = jnp.concatenate([qa, qb], axis=0)
    sc = _mm(qs, kc_ref[...].astype(BF16), _NT) + jnp.concatenate([bc_ref[...]] * 2, axis=0)
    sn = _mm(qs, kn_ref[...].astype(BF16), _NT) + jnp.concatenate([bn_ref[...]] * 2, axis=0)
    mx = jnp.maximum(jnp.max(sc, axis=-1, keepdims=True), jnp.max(sn, axis=-1, keepdims=True))
    pc = jnp.exp(sc - mx)
    pn = jnp.exp(sn - mx)
    den = jnp.sum(pc, axis=-1, keepdims=True) + jnp.sum(pn, axis=-1, keepdims=True)
    acc = _mm(pc.astype(BF16), vc_ref[...].astype(BF16)) + _mm(pn.astype(BF16), vn_ref[...].astype(BF16))
    on = acc / den
    lam = _lambda(lqk_ref, lam_init)
    o = on[:T, :] - lam * on[T:, :]
    o_ref[...] = _subln(o, subln_ref[...], lam_init).astype(BF16)


def _attn_sample(q, k, v, k_cache, v_cache, bias_c, bias_n, lqk, subln, lam_init, n_seq, T):
    m, d_diff = q.shape
    n_heads = d_diff // LANES
    past = k_cache.shape[1]
    return pl.pallas_call(
        functools.partial(_attn_sample_kernel, T=T, lam_init=lam_init),
        grid=(n_seq, n_heads),
        in_specs=[
            pl.BlockSpec((T, LANES), lambda b, h: (b, h)),
            pl.BlockSpec((T, LANES), lambda b, h: (b, h)),
            pl.BlockSpec((T, LANES), lambda b, h: (b, h)),
            pl.BlockSpec((None, past, LANES), lambda b, h: (b, 0, h)),
            pl.BlockSpec((None, past, LANES), lambda b, h: (b, 0, h)),
            pl.BlockSpec((None, T, past), lambda b, h: (h, 0, 0)),
            pl.BlockSpec((None, T, T), lambda b, h: (h, 0, 0)),
            pl.BlockSpec((4, DIFF_HEAD), lambda b, h: (0, 0)),
            pl.BlockSpec((1, LANES), lambda b, h: (0, 0)),
        ],
        out_specs=pl.BlockSpec((T, LANES), lambda b, h: (b, h)),
        out_shape=jax.ShapeDtypeStruct((m, d_diff), BF16),
        compiler_params=_cparams(("arbitrary", "arbitrary")),
        name="attn_sample",
    )(q, k, v, k_cache, v_cache, bias_c, bias_n, lqk, subln)


def _outproj_kernel(x_ref, yr_ref, yd_ref, gt_ref, wt_ref, wb_ref, o_ref):
    mixed = _mm(yr_ref[...], wt_ref[...]) + _mm(yd_ref[...], wb_ref[...])
    o_ref[...] = x_ref[...] + gt_ref[...] * mixed


def _outproj(x, yr, yd, gt, w_out, tm, tn=512):
    m, d = x.shape
    dr = yr.shape[1]
    dd = yd.shape[1]
    gt_spec = (pl.BlockSpec((1, tn), lambda i, j: (0, j)) if gt.shape[0] == 1
               else pl.BlockSpec((tm, tn), lambda i, j: (i, j)))
    return pl.pallas_call(
        _outproj_kernel,
        grid=(m // tm, d // tn),
        in_specs=[
            pl.BlockSpec((tm, tn), lambda i, j: (i, j)),
            pl.BlockSpec((tm, dr), lambda i, j: (i, 0)),
            pl.BlockSpec((tm, dd), lambda i, j: (i, 0)),
            gt_spec,
            pl.BlockSpec((dr, tn), lambda i, j: (0, j)),
            pl.BlockSpec((dd, tn), lambda i, j: (dr // dd, j)),
        ],
        out_specs=pl.BlockSpec((tm, tn), lambda i, j: (i, j)),
        out_shape=jax.ShapeDtypeStruct((m, d), F32),
        compiler_params=_cparams(("arbitrary", "arbitrary")),
        name="outproj",
    )(x, yr, yd, gt, w_out, w_out)


def _shift_rows(x, prev, n):
    if n % SUBLANES == 0:
        return jnp.concatenate([prev, x[:x.shape[0] - n, :]], axis=0)
    rolled = pltpu.roll(x, n, 0)
    row = lax.broadcasted_iota(jnp.int32, x.shape, 0)
    out = rolled
    for r in range(n):
        out = jnp.where(row == r, prev[r:r + 1, :], out)
    return out


def _ffn_kernel(x_ref, sh_ref, sc_ref, gt_ref, g_ref, wg_ref, wv_ref, cw_ref, cb_ref, c0_ref, wd_ref, gf_ref,
                o_ref, co_ref, h_scr, acc_scr, carry, *, B, n_j, final_norm):
    i = pl.program_id(0)
    j = pl.program_id(1)
    tm = x_ref.shape[0]

    @pl.when(j == 0)
    def _():
        xn = _rms(x_ref[...], g_ref[...], NORM_EPS)
        h_scr[...] = (xn * (1.0 + sc_ref[...]) + sh_ref[...]).astype(BF16)
        acc_scr[...] = jnp.zeros(acc_scr.shape, F32)

    @pl.when(i == 0)
    def _():
        carry[j, 0:2 * B, :] = c0_ref[...]

    h = h_scr[...]
    ug = _mm(h, wg_ref[...])
    uv = _mm(h, wv_ref[...])
    prev = carry[j, 0:2 * B, :]
    s1 = _shift_rows(ug, prev[B:2 * B, :], B)
    s2 = _shift_rows(ug, prev, 2 * B)
    cw = cw_ref[...]
    z = cb_ref[...] + s2 * cw[0:1, :] + s1 * cw[1:2, :] + ug * cw[2:3, :]
    act = 0.5 * z * (1.0 + lax.erf(z * (2.0 ** -0.5)))
    acc_scr[...] += _mm((act * uv).astype(BF16), wd_ref[...])
    last2 = ug[tm - 2 * B:tm, :]
    carry[j, 0:2 * B, :] = last2
    tf = last2.shape[1]
    co_ref[:, pl.ds(pl.multiple_of(j * tf, tf), tf)] = last2

    @pl.when(j == n_j - 1)
    def _():
        x2 = x_ref[...] + gt_ref[...] * acc_scr[...]
        if final_norm:
            x2 = _rms(x2, gf_ref[...], NORM_EPS)
        o_ref[...] = x2


def _ffn(x, sh, sc, gt, g, w_up, conv_w, conv_b, conv0, w_down, g_final, B, tm, final_norm, tf=512):
    m, d = x.shape
    d_ff = w_down.shape[0]
    n_j = d_ff // tf
    rows_c = max(SUBLANES, 2 * B)
    return pl.pallas_call(
        functools.partial(_ffn_kernel, B=B, n_j=n_j, final_norm=final_norm),
        grid=(m // tm, n_j),
        in_specs=[
            pl.BlockSpec((tm, d), lambda i, j: (i, 0)),
            _mod_spec(sh, tm, d),
            _mod_spec(sc, tm, d),
            _mod_spec(gt, tm, d),
            pl.BlockSpec((1, d), lambda i, j: (0, 0)),
            pl.BlockSpec((d, tf), lambda i, j: (0, j)),
            pl.BlockSpec((d, tf), lambda i, j: (0, n_j + j)),
            pl.BlockSpec((CONV_W, tf), lambda i, j: (0, j)),
            pl.BlockSpec((1, tf), lambda i, j: (0, j)),
            pl.BlockSpec((2 * B, tf), lambda i, j: (0, j)),
            pl.BlockSpec((tf, d), lambda i, j: (j, 0)),
            pl.BlockSpec((1, d), lambda i, j: (0, 0)),
        ],
        out_specs=[
            pl.BlockSpec((tm, d), lambda i, j: (i, 0)),
            pl.BlockSpec((2 * B, d_ff), lambda i, j: (0, 0)),
        ],
        out_shape=[
            jax.ShapeDtypeStruct((m, d), F32),
            jax.ShapeDtypeStruct((2 * B, d_ff), F32),
        ],
        scratch_shapes=[pltpu.VMEM((tm, d), BF16), pltpu.VMEM((tm, d), F32),
                        pltpu.VMEM((n_j, rows_c, tf), F32)],
        compiler_params=_cparams(("arbitrary", "arbitrary")),
        name="conv_ffn",
    )(x, sh, sc, gt, g, w_up, w_up, conv_w, conv_b, conv0, w_down, g_final)


def _pad_cols(a, n):
    return jnp.pad(a, [(0, 0)] * (a.ndim - 1) + [(0, n)])


def _regroup_rwkv_cols(a, d_rwkv):
    o = 3 * d_rwkv
    return jnp.concatenate([
        a[..., :o],
        _pad_cols(a[..., o:o + DECAY_LORA], LORA_PAD - DECAY_LORA),
        _pad_cols(a[..., o + DECAY_LORA:o + DECAY_LORA + AAA_LORA], LORA_PAD - AAA_LORA),
        a[..., o + DECAY_LORA + AAA_LORA:],
    ], axis=-1)


def _ungroup_rwkv_cols(a, d_rwkv):
    o = 3 * d_rwkv
    return jnp.concatenate([
        a[..., :o],
        a[..., o:o + DECAY_LORA],
        a[..., o + LORA_PAD:o + LORA_PAD + AAA_LORA],
        a[..., o + 2 * LORA_PAD:],
    ], axis=-1)


def kernel(x_prompt, x_sample, c_prompt, c_sample, cache_k, cache_v, state_wkv, state_shift, state_conv, w_ada, b_ada, g_mix, g_ffn, w_in, w_out, rwkv_mu, rwkv_w0, rwkv_w2, rwkv_a0, rwkv_a2, rwkv_g2, rwkv_kk, rwkv_ka, rwkv_rk, rwkv_ln_w, rwkv_ln_b, diff_lq1, diff_lk1, diff_lq2, diff_lk2, diff_subln, rel_table, ffn_up, ffn_conv_w, ffn_conv_b, ffn_down, g_final):
    depth, d_model, _ = w_in.shape
    bp, seq, _ = x_prompt.shape
    bs, dseq, _ = x_sample.shape
    past = cache_k.shape[2]
    d_rwkv = rwkv_w0.shape[1]
    d_diff = d_model - d_rwkv
    n_dheads = d_diff // (2 * DIFF_HEAD)
    n_rheads = d_rwkv // RWKV_HEAD
    n_rwkv_cols = rwkv_mu.shape[1]
    n_rwkv_pad = n_rwkv_cols + 2 * LORA_PAD - DECAY_LORA - AAA_LORA
    d_ff = ffn_down.shape[1]
    assert bp == 1, "prompt path handles one sequence"
    TB = min(256, seq // 2)
    assert seq % TB == 0 and TB % CHUNK == 0 and TB >= MAX_DISTANCE
    tm_p = min(512, seq)
    m_s = bs * dseq

    w_in_b = jnp.concatenate([_regroup_rwkv_cols(w_in[..., :n_rwkv_cols], d_rwkv), w_in[..., n_rwkv_cols:]],
                             axis=-1).astype(BF16)
    mu_p = _regroup_rwkv_cols(rwkv_mu, d_rwkv)
    w2_b = jnp.pad(rwkv_w2, ((0, 0), (0, LORA_PAD - DECAY_LORA), (0, 0))).astype(BF16)
    a2_b = jnp.pad(rwkv_a2, ((0, 0), (0, LORA_PAD - AAA_LORA), (0, 0))).astype(BF16)
    g2_b = rwkv_g2.astype(BF16)
    w_out_b = w_out.astype(BF16)
    up_b = ffn_up.astype(BF16)
    down_b = ffn_down.astype(BF16)
    rk_flat = rwkv_rk.reshape(depth, d_rwkv)
    lqk = jnp.stack([diff_lq1, diff_lk1, diff_lq2, diff_lk2], axis=1)

    n_c = bp + bs
    n_c_pad = -(-n_c // SUBLANES) * SUBLANES
    c_rows = jnp.pad(jnp.concatenate([c_prompt, c_sample], axis=0), ((0, n_c_pad - n_c), (0, 0)))
    mod = _adaln_mod(c_rows, w_ada, b_ada).reshape(depth, n_c_pad, 6, d_model)

    ql = jnp.arange(TB)
    rel_diag = ql[None, :] - ql[:, None]
    allowed = (ql[None, :] // CHUNK) <= (ql[:, None] // CHUNK)
    bucket_diag = jnp.where(allowed, _t5_bucket(rel_diag), -1)
    bucket_p = jnp.concatenate([
        jnp.concatenate([_t5_bucket(rel_diag - TB), bucket_diag], axis=1),
        jnp.concatenate([bucket_diag, jnp.full((TB, TB), -1)], axis=1),
    ], axis=0)
    bias_p = _bias_lookup(rel_table, bucket_p.astype(jnp.int32), FAR_BUCKET).reshape(n_dheads, 2, TB, 2 * TB)
    q_pos = past + jnp.arange(dseq)
    bias_c = _bias_lookup(rel_table, _t5_bucket(jnp.arange(past)[None, :] - q_pos[:, None]).astype(jnp.int32), FAR_BUCKET)
    bias_n = _bias_lookup(rel_table, _t5_bucket(q_pos[None, :] - q_pos[:, None]).astype(jnp.int32), FAR_BUCKET)

    xp = x_prompt.reshape(seq, d_model)
    xs = x_sample.reshape(m_s, d_model)
    zero_shift = jnp.zeros((1, 1, n_rwkv_pad), F32)
    zero_state = jnp.zeros((1, n_rheads, RWKV_HEAD, RWKV_HEAD), F32)
    zero_conv = jnp.zeros((2, d_ff), F32)
    shift_s_in = _regroup_rwkv_cols(state_shift, d_rwkv)
    outs_p = {k: [] for k in ("k", "v", "wkv", "shift", "conv")}
    outs_s = {k: [] for k in ("k", "v", "wkv", "shift", "conv")}

    def time_major(a):
        return a.reshape(bs, dseq, -1).swapaxes(0, 1).reshape(m_s, -1)

    def batch_major(a):
        return a.reshape(dseq, bs, -1).swapaxes(0, 1).reshape(m_s, -1)

    for l in range(depth):
        lam_init = 0.8 - 0.6 * math.exp(-0.3 * l)
        last = l == depth - 1
        wkv_args = (w2_b[l], a2_b[l], g2_b[l], rwkv_w0[l][None], rwkv_a0[l][None], rwkv_kk[l][None],
                    rwkv_ka[l][None], rk_flat[l][None], rwkv_ln_w[l][None], rwkv_ln_b[l][None])
        g_mix_l, g_ffn_l, gf = g_mix[l][None], g_ffn[l][None], g_final[None]
        subln = diff_subln[l][None]

        mp = mod[l, 0:bp]
        sh1, sc1, gt1, sh2, sc2, gt2 = (mp[:, t] for t in range(6))
        p, q, k, v = _inproj(xp, sh1, sc1, g_mix_l, w_in_b[l], n_rwkv_pad, d_diff, tm_p)
        y_r, wkv = _wkv(p, mu_p[l][None], zero_shift, *wkv_args, zero_state, 1, seq, min(64, seq))
        y_d = _attn_prompt(q, k, v, bias_p, lqk[l], subln, lam_init, TB)
        x1 = _outproj(xp, y_r, y_d, gt1, w_out_b[l], tm_p)
        xp, conv = _ffn(x1, sh2, sc2, gt2, g_ffn_l, up_b[l], ffn_conv_w[l], ffn_conv_b[l][None], zero_conv,
                        down_b[l], gf, 1, tm_p, last)
        outs_p["k"].append(k.reshape(bp, seq, n_dheads, 2 * DIFF_HEAD))
        outs_p["v"].append(v.reshape(bp, seq, n_dheads, 2 * DIFF_HEAD))
        outs_p["wkv"].append(wkv)
        outs_p["shift"].append(_ungroup_rwkv_cols(p[seq - 1:seq], d_rwkv).reshape(bp, 1, n_rwkv_cols))
        outs_p["conv"].append(conv.reshape(bp, CONV_W - 1, d_ff))

        ms = mod[l, bp:bp + bs]
        rows_bm = jnp.repeat(ms, dseq, axis=0)
        rows_tm = jnp.tile(ms, (dseq, 1, 1))
        p, q, k, v = _inproj(xs, rows_bm[:, 0], rows_bm[:, 1], g_mix_l, w_in_b[l], n_rwkv_pad, d_diff, m_s)
        y_r, wkv = _wkv(p, mu_p[l][None], shift_s_in[l], *wkv_args, state_wkv[l], bs, dseq, dseq)
        y_d = _attn_sample(q, k, v, cache_k[l].reshape(bs, past, d_diff), cache_v[l].reshape(bs, past, d_diff),
                           bias_c, bias_n, lqk[l], subln, lam_init, bs, dseq)
        x1 = _outproj(xs, y_r, y_d, rows_bm[:, 2], w_out_b[l], m_s)
        conv0 = state_conv[l].swapaxes(0, 1).reshape((CONV_W - 1) * bs, d_ff)
        x2, conv = _ffn(time_major(x1), rows_tm[:, 3], rows_tm[:, 4], rows_tm[:, 5], g_ffn_l, up_b[l],
                        ffn_conv_w[l], ffn_conv_b[l][None], conv0, down_b[l], gf, bs, m_s, last)
        xs = batch_major(x2)
        outs_s["k"].append(k.reshape(bs, dseq, n_dheads, 2 * DIFF_HEAD))
        outs_s["v"].append(v.reshape(bs, dseq, n_dheads, 2 * DIFF_HEAD))
        outs_s["wkv"].append(wkv)
        p_last = p.reshape(bs, dseq, n_rwkv_pad)[:, dseq - 1:dseq]
        outs_s["shift"].append(_ungroup_rwkv_cols(p_last, d_rwkv))
        outs_s["conv"].append(conv.reshape(CONV_W - 1, bs, d_ff).swapaxes(0, 1))

    st = lambda xs_: jnp.stack(xs_)
    return (xp.reshape(bp, seq, d_model), xs.reshape(bs, dseq, d_model),
            st(outs_p["k"]), st(outs_p["v"]), st(outs_p["wkv"]), st(outs_p["shift"]), st(outs_p["conv"]),
            st(outs_s["k"]), st(outs_s["v"]), st(outs_s["wkv"]), st(outs_s["shift"]), st(outs_s["conv"]))
```

```python
import functools
import math

import jax
import jax.numpy as jnp
from jax import lax
from jax.experimental import pallas as pl
from jax.experimental.pallas import tpu as pltpu

F32 = jnp.float32
BF16 = jnp.bfloat16

CHUNK = 64
RWKV_HEAD = 64
DIFF_HEAD = 64
N_BUCKETS = 32
MAX_DISTANCE = 128
NORM_EPS = 1e-6
GN_EPS = 64e-5
SUBLN_EPS = 1e-5
DECAY_LORA = 96
AAA_LORA = 96
GATE_LORA = 256
LORA_PAD = 128
CONV_W = 3

LANES = 128
SUBLANES = 8
VMEM_LIMIT_BYTES = 56 * 1024 * 1024
MASK_VALUE = -1e30
WKV_PAIRS_PER_STEP = 4

_NN = (((1,), (0,)), ((), ()))
_NT = (((1,), (1,)), ((), ()))
_TN = (((0,), (0,)), ((), ()))


def _mm(a, b, dims=_NN):
    return lax.dot_general(a, b, dims, preferred_element_type=F32)


def _hi_lo(x):
    h = x.astype(BF16)
    return h, (x - h.astype(F32)).astype(BF16)


def _mm3(a, b, dims=_NN):
    ah, al = _hi_lo(a)
    bh, bl = _hi_lo(b)
    (ca,), (cb,) = dims[0]
    k = a.shape[ca]
    if (ca == 0 or k % LANES == 0) and (cb == 0 or k % LANES == 0):
        return _mm(jnp.concatenate([ah, al, ah], axis=ca), jnp.concatenate([bh, bh, bl], axis=cb), dims)
    return _mm(ah, bh, dims) + (_mm(al, bh, dims) + _mm(ah, bl, dims))


def _sel_rows(sel2, x):
    xh, xl = _hi_lo(x)
    return _mm(sel2, jnp.concatenate([xh, xl], axis=0))


def _sel_lanes(x, sel2):
    xh, xl = _hi_lo(x)
    return _mm(jnp.concatenate([xh, xl], axis=1), sel2)


def _cparams(sem):
    return pltpu.CompilerParams(dimension_semantics=sem, vmem_limit_bytes=VMEM_LIMIT_BYTES)


def _rms(x, g, eps):
    return x * lax.rsqrt(jnp.mean(x * x, axis=-1, keepdims=True) + eps) * g


def _cast_kernel(x_ref, o_ref):
    o_ref[...] = x_ref[...].astype(BF16)


def _cast_bf16(w, tr=512, tc=1024):
    depth, rows, cols = w.shape
    spec = pl.BlockSpec((None, tr, tc), lambda l, i, j: (l, i, j))
    return pl.pallas_call(
        _cast_kernel,
        grid=(depth, rows // tr, cols // tc),
        in_specs=[spec],
        out_specs=spec,
        out_shape=jax.ShapeDtypeStruct(w.shape, BF16),
        compiler_params=_cparams(("arbitrary", "arbitrary", "arbitrary")),
        name="cast_bf16",
    )(w)


def _mod_kernel(c_ref, w_ref, b_ref, o_ref):
    c = c_ref[...]
    s = (c * jax.nn.sigmoid(c)).astype(BF16)
    o_ref[...] = _mm(s, w_ref[...].astype(BF16)) + b_ref[...]


def _adaln_mod(c_rows, w_ada, b_ada, tn=1024):
    depth, d, n = w_ada.shape
    rows = c_rows.shape[0]
    return pl.pallas_call(
        _mod_kernel,
        grid=(depth, n // tn),
        in_specs=[
            pl.BlockSpec((rows, d), lambda l, j: (0, 0)),
            pl.BlockSpec((None, d, tn), lambda l, j: (l, 0, j)),
            pl.BlockSpec((None, 1, tn), lambda l, j: (l, 0, j)),
        ],
        out_specs=pl.BlockSpec((None, rows, tn), lambda l, j: (l, 0, j)),
        out_shape=jax.ShapeDtypeStruct((depth, rows, n), F32),
        compiler_params=_cparams(("arbitrary", "arbitrary")),
        name="adaln_mod",
    )(c_rows, w_ada, b_ada.reshape(depth, 1, n))


def _bias_kernel(tab_ref, bucket_ref, o_ref, *, far_bucket):
    h = pl.program_id(0)
    bucket = bucket_ref[...]
    far = tab_ref[far_bucket, h]
    acc = jnp.full(bucket.shape, MASK_VALUE, F32)
    for b in range(N_BUCKETS):
        acc = jnp.where(bucket == b, tab_ref[b, h] - far, acc)
    o_ref[...] = acc


def _bias_lookup(table, bucket, far_bucket):
    n_heads = table.shape[1]
    r, c = bucket.shape
    return pl.pallas_call(
        functools.partial(_bias_kernel, far_bucket=far_bucket),
        grid=(n_heads,),
        in_specs=[
            pl.BlockSpec(memory_space=pltpu.SMEM),
            pl.BlockSpec((r, c), lambda h: (0, 0)),
        ],
        out_specs=pl.BlockSpec((None, r, c), lambda h: (h, 0, 0)),
        out_shape=jax.ShapeDtypeStruct((n_heads, r, c), F32),
        compiler_params=_cparams(("arbitrary",)),
        name="bias_lookup",
    )(table, bucket)


def _t5_bucket(rel):
    nb = N_BUCKETS // 2
    max_exact = nb // 2
    n = jnp.abs(rel)
    nf = jnp.maximum(n, 1).astype(F32)
    large = max_exact + (jnp.log(nf / max_exact) / math.log(MAX_DISTANCE / max_exact) * (nb - max_exact)).astype(jnp.int32)
    large = jnp.minimum(large, nb - 1)
    return jnp.where(rel > 0, nb, 0) + jnp.where(n < max_exact, n, large)


FAR_BUCKET = N_BUCKETS // 2 - 1


def _inproj_kernel(x_ref, sh_ref, sc_ref, g_ref, w_ref, *rest, n_p, n_q):
    p_ref, q_ref, k_ref, v_ref, h_scr = rest[-5:]
    j = pl.program_id(1)

    @pl.when(j == 0)
    def _():
        xn = _rms(x_ref[...], g_ref[...], NORM_EPS)
        h_scr[...] = (xn * (1.0 + sc_ref[...]) + sh_ref[...]).astype(BF16)

    r = _mm(h_scr[...], w_ref[...])

    @pl.when(j < n_p)
    def _():
        p_ref[...] = r

    @pl.when((j >= n_p) & (j < n_p + n_q))
    def _():
        q_ref[...] = (r * (DIFF_HEAD ** -0.5)).astype(BF16)

    @pl.when((j >= n_p + n_q) & (j < n_p + 2 * n_q))
    def _():
        k_ref[...] = r

    @pl.when(j >= n_p + 2 * n_q)
    def _():
        v_ref[...] = r


def _mod_spec(mod, tm, d):
    if mod.shape[0] == 1:
        return pl.BlockSpec((1, d), lambda i, j: (0, 0))
    return pl.BlockSpec((tm, d), lambda i, j: (i, 0))


def _inproj(x, sh, sc, g_all, w_all, l, kv_all, n_rwkv_pad, d_diff, tm, tn=512):
    m, d = x.shape
    depth, _, n = w_all.shape
    n_p, n_q = n_rwkv_pad // tn, d_diff // tn
    clip = lambda j, lo: jnp.clip(j - lo, 0, n_q - 1)
    in_specs = [
        pl.BlockSpec((tm, d), lambda i, j: (i, 0)),
        _mod_spec(sh, tm, d),
        _mod_spec(sc, tm, d),
        pl.BlockSpec((None, 1, d), lambda i, j: (l, 0, 0)),
        pl.BlockSpec((None, d, tn), lambda i, j: (l, 0, j)),
        pl.BlockSpec(memory_space=pl.ANY),
        pl.BlockSpec(memory_space=pl.ANY),
    ]
    args = [x, sh, sc, g_all, w_all, kv_all[0], kv_all[1]]
    aliases = {5: 2, 6: 3}
    return pl.pallas_call(
        functools.partial(_inproj_kernel, n_p=n_p, n_q=n_q),
        grid=(m // tm, n // tn),
        in_specs=in_specs,
        out_specs=[
            pl.BlockSpec((tm, tn), lambda i, j: (i, jnp.minimum(j, n_p - 1))),
            pl.BlockSpec((tm, tn), lambda i, j: (i, clip(j, n_p))),
            pl.BlockSpec((None, tm, tn), lambda i, j: (l, i, clip(j, n_p + n_q))),
            pl.BlockSpec((None, tm, tn), lambda i, j: (l, i, clip(j, n_p + 2 * n_q))),
        ],
        out_shape=[
            jax.ShapeDtypeStruct((m, n_rwkv_pad), F32),
            jax.ShapeDtypeStruct((m, d_diff), BF16),
            jax.ShapeDtypeStruct((depth, m, d_diff), F32),
            jax.ShapeDtypeStruct((depth, m, d_diff), F32),
        ],
        scratch_shapes=[pltpu.VMEM((tm, d), BF16)],
        input_output_aliases=aliases,
        compiler_params=_cparams(("arbitrary", "arbitrary")),
        name="inproj",
    )(*args)


def _wkv_kernel(pr_ref, pk_ref, pv_ref, pw_ref, pa_ref, pg_ref,
                mur_ref, muk_ref, muv_ref, muw_ref, mua_ref, mug_ref,
                s0r_ref, s0k_ref, s0v_ref, s0w_ref, s0a_ref, s0g_ref,
                w2_ref, a2_ref, g2_ref,
                w0_ref, a0_ref, kkp_ref, ka_ref, rk_ref, lnw_ref, lnb_ref,
                st0_ref,
                y_ref, st_ref,
                sbd, cr, ck, cv, cw, ca, cg, *, C, n_chunks, n_pairs):
    chunk = pl.program_id(2)
    hd = RWKV_HEAD
    zero_blk = jnp.zeros((hd, hd), F32)

    @pl.when(chunk == 0)
    def _():
        for c_ref, s_ref in ((cr, s0r_ref), (ck, s0k_ref), (cv, s0v_ref), (cw, s0w_ref), (ca, s0a_ref), (cg, s0g_ref)):
            c_ref[0:1, :] = s_ref[...]
        for pi in range(n_pairs):
            top = jnp.concatenate([st0_ref[2 * pi], zero_blk], axis=1)
            bot = jnp.concatenate([zero_blk, st0_ref[2 * pi + 1]], axis=1)
            sbd[pi] = jnp.concatenate([top, bot], axis=0)

    def tshift(x_ref, mu_ref, c_ref):
        x = x_ref[...]
        row = lax.broadcasted_iota(jnp.int32, x.shape, 0)
        prev = jnp.where(row == 0, c_ref[0:1, :], pltpu.roll(x, 1, 0))
        c_ref[0:1, :] = x[C - 1:C, :]
        return x + (prev - x) * mu_ref[...]

    xr = tshift(pr_ref, mur_ref, cr)
    xk = tshift(pk_ref, muk_ref, ck)
    xv = tshift(pv_ref, muv_ref, cv)
    xw = tshift(pw_ref, muw_ref, cw)
    xa = tshift(pa_ref, mua_ref, ca)
    xg = tshift(pg_ref, mug_ref, cg)

    lane = lax.broadcasted_iota(jnp.int32, (1, LANES), 1)
    head0 = lane < hd
    li = lax.broadcasted_iota(jnp.int32, (LANES, LANES), 0)
    lj = lax.broadcasted_iota(jnp.int32, (LANES, LANES), 1)
    same_head = (li < hd) == (lj < hd)
    li2 = lax.broadcasted_iota(jnp.int32, (2 * LANES, LANES), 0)
    lj2 = lax.broadcasted_iota(jnp.int32, (2 * LANES, LANES), 1)
    li2 = jnp.where(li2 < LANES, li2, li2 - LANES)
    seg2 = jnp.where((li2 < hd) == (lj2 < hd), 1.0, 0.0).astype(BF16)

    def segsum(x):
        return _sel_lanes(x, seg2)

    wlin = w0_ref[...] + _mm(jnp.tanh(xw).astype(BF16), w2_ref[...])
    z = -wlin
    w_log = -(jnp.maximum(z, 0.0) + jnp.log1p(jnp.exp(-jnp.abs(z)))) - 0.5
    ne = -jnp.exp(w_log)
    a = jax.nn.sigmoid(a0_ref[...] + _mm(xa.astype(BF16), a2_ref[...]))
    g = _mm(jax.nn.sigmoid(xg).astype(BF16), g2_ref[...])
    kk_raw = xk * kkp_ref[...]
    k2 = xk * (1.0 + (a - 1.0) * ka_ref[...])
    rk_prod = xr * k2 * rk_ref[...]

    ti = lax.broadcasted_iota(jnp.int32, (C, 2 * C), 0)
    tj = lax.broadcasted_iota(jnp.int32, (C, 2 * C), 1)
    tj = jnp.where(tj < C, tj, tj - C)
    tri2 = jnp.where(tj <= ti, 1.0, 0.0).astype(BF16)
    cum = _sel_rows(tri2, ne)
    gam = jnp.exp(cum)
    gprev = jnp.exp(cum - ne)
    ginv = jnp.exp(-cum)
    dte = jnp.exp(cum[C - 1:C, :] - cum)
    rg_all = xr * gam
    ks_all = k2 * ginv
    kd_all = k2 * dte
    lnw = lnw_ref[...]
    lnb = lnb_ref[...]

    def stack_masked(x):
        return jnp.concatenate([jnp.where(head0, x, 0.0), jnp.where(head0, 0.0, x)], axis=0)

    def stack_twice(x):
        return jnp.concatenate([x, x], axis=0)

    R = 2 * C
    ri = lax.broadcasted_iota(jnp.int32, (R, R), 0)
    ci = lax.broadcasted_iota(jnp.int32, (R, R), 1)
    same_blk = (ri < C) == (ci < C)
    tr = jnp.where(ri < C, ri, ri - C)
    tc = jnp.where(ci < C, ci, ci - C)
    strict = same_blk & (tc < tr)
    incl = same_blk & (tc <= tr)
    n_steps = max(1, int(math.log2(C)))
    inv_hd = 1.0 / hd

    pairs = range(n_pairs)
    sls = [slice(pi * LANES, (pi + 1) * LANES) for pi in pairs]
    xv_p = [xv[:, sl] for sl in sls]
    sums = [segsum(jnp.concatenate([kk_raw[:, sl] * kk_raw[:, sl], rk_prod[:, sl]], axis=0)) for sl in sls]
    kk = [kk_raw[:, sl] * lax.rsqrt(jnp.maximum(sm[:C, :], 1e-24)) for sl, sm in zip(sls, sums)]
    bonus = [sm[C:, :] * v for sm, v in zip(sums, xv_p)]
    kka = [k * a[:, sl] for k, sl in zip(kk, sls)]
    lhs = [jnp.concatenate([stack_masked(-k * gprev[:, sl]), stack_masked(rg_all[:, sl])], axis=0)
           for k, sl in zip(kk, sls)]
    rhs = [jnp.concatenate([stack_twice(ka_ * ginv[:, sl]), stack_twice(ks_all[:, sl])], axis=0)
           for ka_, sl in zip(kka, sls)]
    vst = [stack_masked(v) for v in xv_p]
    sc4 = [_mm3(l_, r_, _NT) for l_, r_ in zip(lhs, rhs)]
    s_old = [sbd[pi] for pi in pairs]
    ls = [_mm3(l_, s_, _NT) for l_, s_ in zip(lhs, s_old)]
    n_ab = [jnp.where(strict, s4[:R, :R], 0.0) for s4 in sc4]
    m_ak = [jnp.where(strict, s4[:R, R:], 0.0) for s4 in sc4]
    m_r = [jnp.concatenate([jnp.where(incl, s4[R:, :R], 0.0), jnp.where(incl, s4[R:, R:], 0.0)], axis=1)
           for s4 in sc4]
    x_u = [l_[:R, :] + _mm3(mk, v) for l_, mk, v in zip(ls, m_ak, vst)]
    pw = n_ab
    for step in range(n_steps):
        if step + 1 < n_steps:
            t = [_mm3(p_, jnp.concatenate([x_, p_], axis=1)) for p_, x_ in zip(pw, x_u)]
            x_u = [x_ + t_[:, :LANES] for x_, t_ in zip(x_u, t)]
            pw = [t_[:, LANES:] for t_ in t]
        else:
            x_u = [x_ + _mm3(p_, x_) for p_, x_ in zip(pw, x_u)]
    y_st = [l_[R:, :] + _mm3(mr, jnp.concatenate([x_, v], axis=0))
            for l_, mr, x_, v in zip(ls, m_r, x_u, vst)]
    upd = [_mm3(jnp.concatenate([x_[:C, :] + x_[C:, :], v], axis=0),
                jnp.concatenate([ka_ * dte[:, sl], kd_all[:, sl]], axis=0), _TN)
           for x_, v, ka_, sl in zip(x_u, xv_p, kka, sls)]
    for pi in pairs:
        sbd[pi] = s_old[pi] * gam[C - 1:C, sls[pi]] + jnp.where(same_head, upd[pi], 0.0)

    y = [ys[:C, :] + ys[C:, :] for ys in y_st]
    yc = [y_ - segsum(y_) * inv_hd for y_ in y]
    y_var = [segsum(c_ * c_) * inv_hd for c_ in yc]
    for pi in pairs:
        sl = sls[pi]
        yn = yc[pi] * lax.rsqrt(y_var[pi] + GN_EPS) * lnw[:, sl] + lnb[:, sl]
        y_ref[:, sl] = ((yn + bonus[pi]) * g[:, sl]).astype(BF16)

    @pl.when(chunk == n_chunks - 1)
    def _():
        for pi in range(n_pairs):
            s_new = sbd[pi]
            st_ref[2 * pi] = s_new[0:hd, 0:hd]
            st_ref[2 * pi + 1] = s_new[hd:2 * hd, hd:2 * hd]


def _wkv(p, l, mu, shift0, w2, a2, g2, w0, a0, kkp, ka, rk, lnw, lnb, state0, n_seq, seq_len, C):
    m, n_pad = p.shape
    d_rwkv = w0.shape[-1]
    n_heads = d_rwkv // RWKV_HEAD
    n_pairs = WKV_PAIRS_PER_STEP
    wd = n_pairs * LANES
    n_groups = d_rwkv // wd
    n_chunks = seq_len // C
    nb = n_groups
    blk_w = 3 * d_rwkv // LANES
    blk_a = blk_w + 1
    blk_g = (blk_a + 1) * LANES // GATE_LORA

    row = lambda s, h, c: s * n_chunks + c
    seg_specs = [
        pl.BlockSpec((C, wd), lambda s, h, c: (row(s, h, c), h)),
        pl.BlockSpec((C, wd), lambda s, h, c: (row(s, h, c), nb + h)),
        pl.BlockSpec((C, wd), lambda s, h, c: (row(s, h, c), 2 * nb + h)),
        pl.BlockSpec((C, LANES), lambda s, h, c: (row(s, h, c), blk_w)),
        pl.BlockSpec((C, LANES), lambda s, h, c: (row(s, h, c), blk_a)),
        pl.BlockSpec((C, GATE_LORA), lambda s, h, c: (row(s, h, c), blk_g)),
    ]
    mu_specs = [
        pl.BlockSpec((None, 1, wd), lambda s, h, c: (l, 0, h)),
        pl.BlockSpec((None, 1, wd), lambda s, h, c: (l, 0, nb + h)),
        pl.BlockSpec((None, 1, wd), lambda s, h, c: (l, 0, 2 * nb + h)),
        pl.BlockSpec((None, 1, LANES), lambda s, h, c: (l, 0, blk_w)),
        pl.BlockSpec((None, 1, LANES), lambda s, h, c: (l, 0, blk_a)),
        pl.BlockSpec((None, 1, GATE_LORA), lambda s, h, c: (l, 0, blk_g)),
    ]
    s0_specs = [
        pl.BlockSpec((None, 1, wd), lambda s, h, c: (s, 0, h)),
        pl.BlockSpec((None, 1, wd), lambda s, h, c: (s, 0, nb + h)),
        pl.BlockSpec((None, 1, wd), lambda s, h, c: (s, 0, 2 * nb + h)),
        pl.BlockSpec((None, 1, LANES), lambda s, h, c: (s, 0, blk_w)),
        pl.BlockSpec((None, 1, LANES), lambda s, h, c: (s, 0, blk_a)),
        pl.BlockSpec((None, 1, GATE_LORA), lambda s, h, c: (s, 0, blk_g)),
    ]
    lora_specs = [
        pl.BlockSpec((None, LORA_PAD, wd), lambda s, h, c: (l, 0, h)),
        pl.BlockSpec((None, LORA_PAD, wd), lambda s, h, c: (l, 0, h)),
        pl.BlockSpec((None, GATE_LORA, wd), lambda s, h, c: (l, 0, h)),
    ]
    vec_spec = pl.BlockSpec((None, 1, wd), lambda s, h, c: (l, 0, h))
    state_spec = pl.BlockSpec((None, 2 * n_pairs, RWKV_HEAD, RWKV_HEAD), lambda s, h, c: (s, h, 0, 0))
    y, st = pl.pallas_call(
        functools.partial(_wkv_kernel, C=C, n_chunks=n_chunks, n_pairs=n_pairs),
        grid=(n_seq, n_groups, n_chunks),
        in_specs=seg_specs + mu_specs + s0_specs + lora_specs + [vec_spec] * 7 + [state_spec],
        out_specs=[
            pl.BlockSpec((C, wd), lambda s, h, c: (row(s, h, c), h)),
            state_spec,
        ],
        out_shape=[
            jax.ShapeDtypeStruct((m, d_rwkv), BF16),
            jax.ShapeDtypeStruct((n_seq, n_heads, RWKV_HEAD, RWKV_HEAD), F32),
        ],
        scratch_shapes=[pltpu.VMEM((n_pairs, LANES, LANES), F32)]
        + [pltpu.VMEM((SUBLANES, wd), F32)] * 3 + [pltpu.VMEM((SUBLANES, LANES), F32)] * 2
        + [pltpu.VMEM((SUBLANES, GATE_LORA), F32)],
        compiler_params=_cparams(("arbitrary", "arbitrary", "arbitrary")),
        name="wkv",
    )(p, p, p, p, p, p, mu, mu, mu, mu, mu, mu,
      shift0, shift0, shift0, shift0, shift0, shift0,
      w2, a2, g2, w0, a0, kkp, ka, rk, lnw, lnb, state0)
    return y, st


def _lambda(lqk_ref, lam_init):
    t = lqk_ref[...]
    s1 = jnp.sum(t[0:1, :] * t[1:2, :], axis=-1, keepdims=True)
    s2 = jnp.sum(t[2:3, :] * t[3:4, :], axis=-1, keepdims=True)
    return jnp.exp(s1) - jnp.exp(s2) + lam_init


def _split_maps(q):
    lane = lax.broadcasted_iota(jnp.int32, q.shape, 1)
    zero = jnp.zeros_like(q)
    return jnp.where(lane < DIFF_HEAD, q, zero), jnp.where(lane < DIFF_HEAD, zero, q)


def _subln(o, subln, lam_init):
    return _rms(o, subln, SUBLN_EPS) * (1.0 - lam_init)


FAR_TILES = 4


def _attn_kernel(q_ref, k_ref, v_ref, bias_ref, lqk_ref, subln_ref, o_ref,
                 kb, vb, m_s, a_s, *, TB, n_cast, lam_init):
    i = pl.program_id(1)

    @pl.when(i == 0)
    def _():
        ones = jnp.ones((TB, LANES), BF16)

        def cast(c, carry):
            off = pl.multiple_of(c * TB, TB)
            kb[pl.ds(off, TB), :] = k_ref[pl.ds(off, TB), :].astype(BF16)
            vb[pl.ds(off, TB), 0:LANES] = v_ref[pl.ds(off, TB), :].astype(BF16)
            vb[pl.ds(off, TB), LANES:2 * LANES] = ones
            return carry
        lax.fori_loop(0, n_cast, cast, 0)

    qmaps = _split_maps(q_ref[...])
    m_s[...] = jnp.full(m_s.shape, MASK_VALUE, F32)
    a_s[...] = jnp.zeros(a_s.shape, F32)

    def tile(off, width, bias):
        kt = kb[pl.ds(off, width), :]
        vt = vb[pl.ds(off, width), :]
        for mp in range(2):
            s = _mm(qmaps[mp], kt, _NT)
            if bias is not None:
                s = s + bias
            m_prev = m_s[mp]
            m_new = jnp.maximum(m_prev, jnp.max(s, axis=-1, keepdims=True))
            alpha = jnp.exp(m_prev - m_new)
            p = jnp.exp(s - m_new).astype(BF16)
            a_s[mp] = alpha * a_s[mp] + _mm(p, vt)
            m_s[mp] = m_new

    n_far = jnp.maximum(i - 1, 0)
    big = TB * FAR_TILES
    n_big = n_far // FAR_TILES
    n_small = n_far - n_big * FAR_TILES

    def far_big(j, carry):
        tile(pl.multiple_of(j * big, big), big, None)
        return carry
    lax.fori_loop(0, n_big, far_big, 0)

    def far_small(j, carry):
        tile(pl.multiple_of((n_big * FAR_TILES + j) * TB, TB), TB, None)
        return carry
    lax.fori_loop(0, n_small, far_small, 0)

    tile(pl.multiple_of(n_far * TB, TB), 2 * TB, jnp.where(i == 0, bias_ref[1], bias_ref[0]))

    lam = _lambda(lqk_ref, lam_init)
    on = [a_s[mp][:, 0:LANES] / a_s[mp][:, LANES:2 * LANES] for mp in range(2)]
    o_ref[...] = _subln(on[0] - lam * on[1], subln_ref[...], lam_init).astype(BF16)


def _attn_prompt(q, k_all, v_all, l, bias, lqk, subln, lam_init, TB):
    m, d_diff = q.shape
    n_heads = d_diff // LANES
    n_blk = m // TB
    return pl.pallas_call(
        functools.partial(_attn_kernel, TB=TB, n_cast=n_blk, lam_init=lam_init),
        grid=(n_heads, n_blk),
        in_specs=[
            pl.BlockSpec((TB, LANES), lambda h, i: (i, h)),
            pl.BlockSpec((None, m, LANES), lambda h, i: (l, 0, h)),
            pl.BlockSpec((None, m, LANES), lambda h, i: (l, 0, h)),
            pl.BlockSpec((None, 2, TB, 2 * TB), lambda h, i: (h, 0, 0, 0)),
            pl.BlockSpec((None, 4, DIFF_HEAD), lambda h, i: (l, 0, 0)),
            pl.BlockSpec((None, 1, LANES), lambda h, i: (l, 0, 0)),
        ],
        out_specs=pl.BlockSpec((TB, LANES), lambda h, i: (i, h)),
        out_shape=jax.ShapeDtypeStruct((m, d_diff), BF16),
        scratch_shapes=[pltpu.VMEM((m, LANES), BF16), pltpu.VMEM((m, 2 * LANES), BF16),
                        pltpu.VMEM((2, TB, 1), F32), pltpu.VMEM((2, TB, 2 * LANES), F32)],
        compiler_params=_cparams(("arbitrary", "arbitrary")),
        name="attn_prompt",
    )(q, k_all, v_all, bias, lqk, subln)


def _attn_sample_kernel(q_ref, kn_ref, vn_ref, kc_ref, vc_ref, bc_ref, bn_ref, lqk_ref, subln_ref, o_ref,
                        *, T, lam_init):
    qa, qb = _split_maps(q_ref[...])
    qs = jnp.concatenate([qa, qb], axis=0)
    sc = _mm(qs, kc_ref[...].astype(BF16), _NT) + jnp.concatenate([bc_ref[...]] * 2, axis=0)
    sn = _mm(qs, kn_ref[...].astype(BF16), _NT) + jnp.concatenate([bn_ref[...]] * 2, axis=0)
    mx = jnp.maximum(jnp.max(sc, axis=-1, keepdims=True), jnp.max(sn, axis=-1, keepdims=True))
    pc = jnp.exp(sc - mx)
    pn = jnp.exp(sn - mx)
    den = jnp.sum(pc, axis=-1, keepdims=True) + jnp.sum(pn, axis=-1, keepdims=True)
    acc = _mm(pc.astype(BF16), vc_ref[...].astype(BF16)) + _mm(pn.astype(BF16), vn_ref[...].astype(BF16))
    on = acc / den
    lam = _lambda(lqk_ref, lam_init)
    o = on[:T, :] - lam * on[T:, :]
    o_ref[...] = _subln(o, subln_ref[...], lam_init).astype(BF16)


def _attn_sample(q, k_all, v_all, l, k_cache, v_cache, bias_c, bias_n, lqk, subln, lam_init, n_seq, T):
    m, d_diff = q.shape
    n_heads = d_diff // LANES
    past = k_cache.shape[2]
    return pl.pallas_call(
        functools.partial(_attn_sample_kernel, T=T, lam_init=lam_init),
        grid=(n_seq, n_heads),
        in_specs=[
            pl.BlockSpec((T, LANES), lambda b, h: (b, h)),
            pl.BlockSpec((None, T, LANES), lambda b, h: (l, b, h)),
            pl.BlockSpec((None, T, LANES), lambda b, h: (l, b, h)),
            pl.BlockSpec((None, None, past, LANES), lambda b, h: (l, b, 0, h)),
            pl.BlockSpec((None, None, past, LANES), lambda b, h: (l, b, 0, h)),
            pl.BlockSpec((None, T, past), lambda b, h: (h, 0, 0)),
            pl.BlockSpec((None, T, T), lambda b, h: (h, 0, 0)),
            pl.BlockSpec((None, 4, DIFF_HEAD), lambda b, h: (l, 0, 0)),
            pl.BlockSpec((None, 1, LANES), lambda b, h: (l, 0, 0)),
        ],
        out_specs=pl.BlockSpec((T, LANES), lambda b, h: (b, h)),
        out_shape=jax.ShapeDtypeStruct((m, d_diff), BF16),
        compiler_params=_cparams(("arbitrary", "arbitrary")),
        name="attn_sample",
    )(q, k_all, v_all, k_cache, v_cache, bias_c, bias_n, lqk, subln)


def _outproj_kernel(x_ref, yr_ref, yd_ref, gt_ref, wt_ref, wb_ref, o_ref):
    mixed = _mm(yr_ref[...], wt_ref[...]) + _mm(yd_ref[...], wb_ref[...])
    o_ref[...] = x_ref[...] + gt_ref[...] * mixed


def _outproj(x, yr, yd, gt, w_out_all, l, tm, tn=512):
    m, d = x.shape
    dr = yr.shape[1]
    dd = yd.shape[1]
    gt_spec = (pl.BlockSpec((1, tn), lambda i, j: (0, j)) if gt.shape[0] == 1
               else pl.BlockSpec((tm, tn), lambda i, j: (i, j)))
    return pl.pallas_call(
        _outproj_kernel,
        grid=(m // tm, d // tn),
        in_specs=[
            pl.BlockSpec((tm, tn), lambda i, j: (i, j)),
            pl.BlockSpec((tm, dr), lambda i, j: (i, 0)),
            pl.BlockSpec((tm, dd), lambda i, j: (i, 0)),
            gt_spec,
            pl.BlockSpec((None, dr, tn), lambda i, j: (l, 0, j)),
            pl.BlockSpec((None, dd, tn), lambda i, j: (l, dr // dd, j)),
        ],
        out_specs=pl.BlockSpec((tm, tn), lambda i, j: (i, j)),
        out_shape=jax.ShapeDtypeStruct((m, d), F32),
        compiler_params=_cparams(("arbitrary", "arbitrary")),
        name="outproj",
    )(x, yr, yd, gt, w_out_all, w_out_all)


def _shift_rows(x, prev, n):
    if n % SUBLANES == 0:
        return jnp.concatenate([prev, x[:x.shape[0] - n, :]], axis=0)
    rolled = pltpu.roll(x, n, 0)
    row = lax.broadcasted_iota(jnp.int32, x.shape, 0)
    out = rolled
    for r in range(n):
        out = jnp.where(row == r, prev[r:r + 1, :], out)
    return out


def _ffn_kernel(x_ref, sh_ref, sc_ref, gt_ref, g_ref, wg_ref, wv_ref, cw_ref, cb_ref, c0_ref, wd_ref, gf_ref,
                o_ref, co_ref, h_scr, acc_scr, carry, *, B, n_j, final_norm):
    i = pl.program_id(0)
    j = pl.program_id(1)
    tm = x_ref.shape[0]

    @pl.when(j == 0)
    def _():
        xn = _rms(x_ref[...], g_ref[...], NORM_EPS)
        h_scr[...] = (xn * (1.0 + sc_ref[...]) + sh_ref[...]).astype(BF16)
        acc_scr[...] = jnp.zeros(acc_scr.shape, F32)

    @pl.when(i == 0)
    def _():
        carry[j, 0:2 * B, :] = c0_ref[...]

    h = h_scr[...]
    ug = _mm(h, wg_ref[...])
    uv = _mm(h, wv_ref[...])
    prev = carry[j, 0:2 * B, :]
    s1 = _shift_rows(ug, prev[B:2 * B, :], B)
    s2 = _shift_rows(ug, prev, 2 * B)
    cw = cw_ref[...]
    z = cb_ref[...] + s2 * cw[0:1, :] + s1 * cw[1:2, :] + ug * cw[2:3, :]
    act = 0.5 * z * (1.0 + lax.erf(z * (2.0 ** -0.5)))
    acc_scr[...] += _mm((act * uv).astype(BF16), wd_ref[...])
    last2 = ug[tm - 2 * B:tm, :]
    carry[j, 0:2 * B, :] = last2
    tf = last2.shape[1]
    co_ref[:, pl.ds(pl.multiple_of(j * tf, tf), tf)] = last2

    @pl.when(j == n_j - 1)
    def _():
        x2 = x_ref[...] + gt_ref[...] * acc_scr[...]
        if final_norm:
            x2 = _rms(x2, gf_ref[...], NORM_EPS)
        o_ref[...] = x2


def _ffn(x, sh, sc, gt, g_all, w_up_all, conv_w_all, conv_b_all, conv0, w_down_all, g_final, l, B, tm,
         final_norm, tf=512):
    m, d = x.shape
    d_ff = w_down_all.shape[1]
    n_j = d_ff // tf
    rows_c = max(SUBLANES, 2 * B)
    return pl.pallas_call(
        functools.partial(_ffn_kernel, B=B, n_j=n_j, final_norm=final_norm),
        grid=(m // tm, n_j),
        in_specs=[
            pl.BlockSpec((tm, d), lambda i, j: (i, 0)),
            _mod_spec(sh, tm, d),
            _mod_spec(sc, tm, d),
            _mod_spec(gt, tm, d),
            pl.BlockSpec((None, 1, d), lambda i, j: (l, 0, 0)),
            pl.BlockSpec((None, d, tf), lambda i, j: (l, 0, j)),
            pl.BlockSpec((None, d, tf), lambda i, j: (l, 0, n_j + j)),
            pl.BlockSpec((None, CONV_W, tf), lambda i, j: (l, 0, j)),
            pl.BlockSpec((None, 1, tf), lambda i, j: (l, 0, j)),
            pl.BlockSpec((2 * B, tf), lambda i, j: (0, j)),
            pl.BlockSpec((None, tf, d), lambda i, j: (l, j, 0)),
            pl.BlockSpec((1, d), lambda i, j: (0, 0)),
        ],
        out_specs=[
            pl.BlockSpec((tm, d), lambda i, j: (i, 0)),
            pl.BlockSpec((2 * B, d_ff), lambda i, j: (0, 0)),
        ],
        out_shape=[
            jax.ShapeDtypeStruct((m, d), F32),
            jax.ShapeDtypeStruct((2 * B, d_ff), F32),
        ],
        scratch_shapes=[pltpu.VMEM((tm, d), BF16), pltpu.VMEM((tm, d), F32),
                        pltpu.VMEM((n_j, rows_c, tf), F32)],
        compiler_params=_cparams(("arbitrary", "arbitrary")),
        name="conv_ffn",
    )(x, sh, sc, gt, g_all, w_up_all, w_up_all, conv_w_all, conv_b_all, conv0, w_down_all, g_final)


def _pad_cols(a, n):
    return jnp.pad(a, [(0, 0)] * (a.ndim - 1) + [(0, n)])


def _regroup_rwkv_cols(a, d_rwkv):
    o = 3 * d_rwkv
    return jnp.concatenate([
        a[..., :o],
        _pad_cols(a[..., o:o + DECAY_LORA], LORA_PAD - DECAY_LORA),
        _pad_cols(a[..., o + DECAY_LORA:o + DECAY_LORA + AAA_LORA], LORA_PAD - AAA_LORA),
        a[..., o + DECAY_LORA + AAA_LORA:],
    ], axis=-1)


def _ungroup_rwkv_cols(a, d_rwkv):
    o = 3 * d_rwkv
    return jnp.concatenate([
        a[..., :o],
        a[..., o:o + DECAY_LORA],
        a[..., o + LORA_PAD:o + LORA_PAD + AAA_LORA],
        a[..., o + 2 * LORA_PAD:],
    ], axis=-1)


def kernel(x_prompt, x_sample, c_prompt, c_sample, cache_k, cache_v, state_wkv, state_shift, state_conv, w_ada, b_ada, g_mix, g_ffn, w_in, w_out, rwkv_mu, rwkv_w0, rwkv_w2, rwkv_a0, rwkv_a2, rwkv_g2, rwkv_kk, rwkv_ka, rwkv_rk, rwkv_ln_w, rwkv_ln_b, diff_lq1, diff_lk1, diff_lq2, diff_lk2, diff_subln, rel_table, ffn_up, ffn_conv_w, ffn_conv_b, ffn_down, g_final):
    depth, d_model, _ = w_in.shape
    bp, seq, _ = x_prompt.shape
    bs, dseq, _ = x_sample.shape
    past = cache_k.shape[2]
    d_rwkv = rwkv_w0.shape[1]
    d_diff = d_model - d_rwkv
    n_dheads = d_diff // (2 * DIFF_HEAD)
    n_rheads = d_rwkv // RWKV_HEAD
    n_rwkv_cols = rwkv_mu.shape[1]
    n_rwkv_pad = n_rwkv_cols + 2 * LORA_PAD - DECAY_LORA - AAA_LORA
    d_ff = ffn_down.shape[1]
    assert bp == 1, "prompt path handles one sequence"
    TB = min(256, seq // 2)
    assert seq % (2 * TB) == 0 and TB % CHUNK == 0 and TB >= MAX_DISTANCE
    tm_p = min(512, seq)
    m_s = bs * dseq

    w_in_b = jnp.concatenate([_regroup_rwkv_cols(w_in[..., :n_rwkv_cols], d_rwkv), w_in[..., n_rwkv_cols:]],
                             axis=-1).astype(BF16)
    mu_p = _regroup_rwkv_cols(rwkv_mu, d_rwkv)[:, None]
    w2_b = jnp.pad(rwkv_w2, ((0, 0), (0, LORA_PAD - DECAY_LORA), (0, 0))).astype(BF16)
    a2_b = jnp.pad(rwkv_a2, ((0, 0), (0, LORA_PAD - AAA_LORA), (0, 0))).astype(BF16)
    g2_b = rwkv_g2.astype(BF16)
    w_out_b = _cast_bf16(w_out)
    up_b = _cast_bf16(ffn_up)
    down_b = _cast_bf16(ffn_down)
    vec = lambda a: a.reshape(depth, 1, -1)
    wkv_args = (w2_b, a2_b, g2_b, vec(rwkv_w0), vec(rwkv_a0), vec(rwkv_kk), vec(rwkv_ka), vec(rwkv_rk),
                vec(rwkv_ln_w), vec(rwkv_ln_b))
    g_mix_v, g_ffn_v, subln_v, conv_b_v = vec(g_mix), vec(g_ffn), vec(diff_subln), vec(ffn_conv_b)
    gf = g_final[None]
    lqk = jnp.stack([diff_lq1, diff_lk1, diff_lq2, diff_lk2], axis=1)

    n_c = bp + bs
    n_c_pad = -(-n_c // SUBLANES) * SUBLANES
    c_rows = jnp.pad(jnp.concatenate([c_prompt, c_sample], axis=0), ((0, n_c_pad - n_c), (0, 0)))
    mod = _adaln_mod(c_rows, w_ada, b_ada).reshape(depth, n_c_pad, 6, d_model)

    ql = jnp.arange(TB)
    rel_diag = ql[None, :] - ql[:, None]
    allowed = (ql[None, :] // CHUNK) <= (ql[:, None] // CHUNK)
    bucket_diag = jnp.where(allowed, _t5_bucket(rel_diag), -1)
    bucket_prev = _t5_bucket(rel_diag - TB)
    bucket_p = jnp.concatenate([
        jnp.concatenate([bucket_prev, bucket_diag], axis=1),
        jnp.concatenate([bucket_diag, jnp.full((TB, TB), -1)], axis=1),
    ], axis=0)
    bias_p = _bias_lookup(rel_table, bucket_p.astype(jnp.int32), FAR_BUCKET).reshape(n_dheads, 2, TB, 2 * TB)
    q_pos = past + jnp.arange(dseq)
    bias_c = _bias_lookup(rel_table, _t5_bucket(jnp.arange(past)[None, :] - q_pos[:, None]).astype(jnp.int32), FAR_BUCKET)
    bias_n = _bias_lookup(rel_table, _t5_bucket(q_pos[None, :] - q_pos[:, None]).astype(jnp.int32), FAR_BUCKET)

    xp = x_prompt.reshape(seq, d_model)
    xs = x_sample.reshape(m_s, d_model)
    zero_shift = jnp.zeros((1, 1, n_rwkv_pad), F32)
    zero_state = jnp.zeros((1, n_rheads, RWKV_HEAD, RWKV_HEAD), F32)
    zero_conv = jnp.zeros((2, d_ff), F32)
    shift_s_in = _regroup_rwkv_cols(state_shift, d_rwkv)
    outs_p = {k: [] for k in ("wkv", "shift", "conv")}
    outs_s = {k: [] for k in ("wkv", "shift", "conv")}
    kv_p = (jnp.zeros((depth, seq, d_diff), F32), jnp.zeros((depth, seq, d_diff), F32))
    kv_s = (jnp.zeros((depth, m_s, d_diff), F32), jnp.zeros((depth, m_s, d_diff), F32))
    cache_k2 = cache_k.reshape(depth, bs, past, d_diff)
    cache_v2 = cache_v.reshape(depth, bs, past, d_diff)

    def time_major(a):
        return a.reshape(bs, dseq, -1).swapaxes(0, 1).reshape(m_s, -1)

    def batch_major(a):
        return a.reshape(dseq, bs, -1).swapaxes(0, 1).reshape(m_s, -1)

    for l in range(depth):
        lam_init = 0.8 - 0.6 * math.exp(-0.3 * l)
        last = l == depth - 1

        mp = mod[l, 0:bp]
        sh1, sc1, gt1, sh2, sc2, gt2 = (mp[:, t] for t in range(6))
        p, q, *kv_p = _inproj(xp, sh1, sc1, g_mix_v, w_in_b, l, kv_p, n_rwkv_pad, d_diff, tm_p)
        y_r, wkv = _wkv(p, l, mu_p, zero_shift, *wkv_args, zero_state, 1, seq, min(64, seq))
        y_d = _attn_prompt(q, kv_p[0], kv_p[1], l, bias_p, lqk, subln_v, lam_init, TB)
        x1 = _outproj(xp, y_r, y_d, gt1, w_out_b, l, tm_p)
        xp, conv = _ffn(x1, sh2, sc2, gt2, g_ffn_v, up_b, ffn_conv_w, conv_b_v, zero_conv, down_b, gf, l,
                        1, tm_p, last)
        outs_p["wkv"].append(wkv)
        outs_p["shift"].append(_ungroup_rwkv_cols(p[seq - 1:seq], d_rwkv).reshape(bp, 1, n_rwkv_cols))
        outs_p["conv"].append(conv.reshape(bp, CONV_W - 1, d_ff))

        ms = mod[l, bp:bp + bs]
        rows_bm = jnp.repeat(ms, dseq, axis=0)
        rows_tm = jnp.tile(ms, (dseq, 1, 1))
        p, q, *kv_s = _inproj(xs, rows_bm[:, 0], rows_bm[:, 1], g_mix_v, w_in_b, l, kv_s, n_rwkv_pad, d_diff, m_s)
        y_r, wkv = _wkv(p, l, mu_p, shift_s_in[l], *wkv_args, state_wkv[l], bs, dseq, dseq)
        y_d = _attn_sample(q, kv_s[0], kv_s[1], l, cache_k2, cache_v2, bias_c, bias_n, lqk, subln_v, lam_init,
                           bs, dseq)
        x1 = _outproj(xs, y_r, y_d, rows_bm[:, 2], w_out_b, l, m_s)
        conv0 = state_conv[l].swapaxes(0, 1).reshape((CONV_W - 1) * bs, d_ff)
        x2, conv = _ffn(time_major(x1), rows_tm[:, 3], rows_tm[:, 4], rows_tm[:, 5], g_ffn_v, up_b, ffn_conv_w,
                        conv_b_v, conv0, down_b, gf, l, bs, m_s, last)
        xs = batch_major(x2)
        outs_s["wkv"].append(wkv)
        p_last = p.reshape(bs, dseq, n_rwkv_pad)[:, dseq - 1:dseq]
        outs_s["shift"].append(_ungroup_rwkv_cols(p_last, d_rwkv))
        outs_s["conv"].append(conv.reshape(CONV_W - 1, bs, d_ff).swapaxes(0, 1))

    st = lambda xs_: jnp.stack(xs_)
    head_shape = (n_dheads, 2 * DIFF_HEAD)
    return (xp.reshape(bp, seq, d_model), xs.reshape(bs, dseq, d_model),
            kv_p[0].reshape(depth, bp, seq, *head_shape), kv_p[1].reshape(depth, bp, seq, *head_shape),
            st(outs_p["wkv"]), st(outs_p["shift"]), st(outs_p["conv"]),
            kv_s[0].reshape(depth, bs, dseq, *head_shape), kv_s[1].reshape(depth, bs, dseq, *head_shape),
            st(outs_s["wkv"]), st(outs_s["shift"]), st(outs_s["conv"]))
```

```python
import functools
import math

import jax
import jax.numpy as jnp
from jax import lax
from jax.experimental import pallas as pl
from jax.experimental.pallas import tpu as pltpu

F32 = jnp.float32
BF16 = jnp.bfloat16

CHUNK = 64
RWKV_HEAD = 64
DIFF_HEAD = 64
N_BUCKETS = 32
MAX_DISTANCE = 128
NORM_EPS = 1e-6
GN_EPS = 64e-5
SUBLN_EPS = 1e-5
DECAY_LORA = 96
AAA_LORA = 96
GATE_LORA = 256
LORA_PAD = 128
CONV_W = 3

LANES = 128
SUBLANES = 8
VMEM_LIMIT_BYTES = 56 * 1024 * 1024
MASK_VALUE = -1e30
WKV_PAIRS_PER_STEP = 4

_NN = (((1,), (0,)), ((), ()))
_NT = (((1,), (1,)), ((), ()))
_TN = (((0,), (0,)), ((), ()))


def _mm(a, b, dims=_NN):
    return lax.dot_general(a, b, dims, preferred_element_type=F32)


def _hi_lo(x):
    h = x.astype(BF16)
    return h, (x - h.astype(F32)).astype(BF16)


def _mm3(a, b, dims=_NN):
    ah, al = _hi_lo(a)
    bh, bl = _hi_lo(b)
    (ca,), (cb,) = dims[0]
    k = a.shape[ca]
    if (ca == 0 or k % LANES == 0) and (cb == 0 or k % LANES == 0):
        return _mm(jnp.concatenate([ah, al, ah], axis=ca), jnp.concatenate([bh, bh, bl], axis=cb), dims)
    return _mm(ah, bh, dims) + (_mm(al, bh, dims) + _mm(ah, bl, dims))


def _sel_rows(sel2, x):
    xh, xl = _hi_lo(x)
    return _mm(sel2, jnp.concatenate([xh, xl], axis=0))


def _sel_lanes(x, sel2):
    xh, xl = _hi_lo(x)
    return _mm(jnp.concatenate([xh, xl], axis=1), sel2)


def _cparams(sem):
    return pltpu.CompilerParams(dimension_semantics=sem, vmem_limit_bytes=VMEM_LIMIT_BYTES)


def _rms(x, g, eps):
    return x * lax.rsqrt(jnp.mean(x * x, axis=-1, keepdims=True) + eps) * g


def _cast_kernel(x_ref, o_ref):
    o_ref[...] = x_ref[...].astype(BF16)


def _cast_bf16(w, tr=512, tc=1024):
    depth, rows, cols = w.shape
    spec = pl.BlockSpec((None, tr, tc), lambda l, i, j: (l, i, j))
    return pl.pallas_call(
        _cast_kernel,
        grid=(depth, rows // tr, cols // tc),
        in_specs=[spec],
        out_specs=spec,
        out_shape=jax.ShapeDtypeStruct(w.shape, BF16),
        compiler_params=_cparams(("arbitrary", "arbitrary", "arbitrary")),
        name="cast_bf16",
    )(w)


def _mod_kernel(c_ref, w_ref, b_ref, o_ref):
    c = c_ref[...]
    s = (c * jax.nn.sigmoid(c)).astype(BF16)
    o_ref[...] = _mm(s, w_ref[...].astype(BF16)) + b_ref[...]


def _adaln_mod(c_rows, w_ada, b_ada, tn=1024):
    depth, d, n = w_ada.shape
    rows = c_rows.shape[0]
    return pl.pallas_call(
        _mod_kernel,
        grid=(depth, n // tn),
        in_specs=[
            pl.BlockSpec((rows, d), lambda l, j: (0, 0)),
            pl.BlockSpec((None, d, tn), lambda l, j: (l, 0, j)),
            pl.BlockSpec((None, 1, tn), lambda l, j: (l, 0, j)),
        ],
        out_specs=pl.BlockSpec((None, rows, tn), lambda l, j: (l, 0, j)),
        out_shape=jax.ShapeDtypeStruct((depth, rows, n), F32),
        compiler_params=_cparams(("arbitrary", "arbitrary")),
        name="adaln_mod",
    )(c_rows, w_ada, b_ada.reshape(depth, 1, n))


def _bias_kernel(tab_ref, bucket_ref, o_ref, *, far_bucket):
    h = pl.program_id(0)
    bucket = bucket_ref[...]
    far = tab_ref[far_bucket, h]
    acc = jnp.full(bucket.shape, MASK_VALUE, F32)
    for b in range(N_BUCKETS):
        acc = jnp.where(bucket == b, tab_ref[b, h] - far, acc)
    o_ref[...] = acc


def _bias_lookup(table, bucket, far_bucket):
    n_heads = table.shape[1]
    r, c = bucket.shape
    return pl.pallas_call(
        functools.partial(_bias_kernel, far_bucket=far_bucket),
        grid=(n_heads,),
        in_specs=[
            pl.BlockSpec(memory_space=pltpu.SMEM),
            pl.BlockSpec((r, c), lambda h: (0, 0)),
        ],
        out_specs=pl.BlockSpec((None, r, c), lambda h: (h, 0, 0)),
        out_shape=jax.ShapeDtypeStruct((n_heads, r, c), F32),
        compiler_params=_cparams(("arbitrary",)),
        name="bias_lookup",
    )(table, bucket)


def _t5_bucket(rel):
    nb = N_BUCKETS // 2
    max_exact = nb // 2
    n = jnp.abs(rel)
    nf = jnp.maximum(n, 1).astype(F32)
    large = max_exact + (jnp.log(nf / max_exact) / math.log(MAX_DISTANCE / max_exact) * (nb - max_exact)).astype(jnp.int32)
    large = jnp.minimum(large, nb - 1)
    return jnp.where(rel > 0, nb, 0) + jnp.where(n < max_exact, n, large)


FAR_BUCKET = N_BUCKETS // 2 - 1


def _inproj_kernel(x_ref, sh_ref, sc_ref, g_ref, w_ref, *rest, n_p, n_q):
    p_ref, q_ref, k_ref, v_ref, h_scr = rest[-5:]
    j = pl.program_id(1)

    @pl.when(j == 0)
    def _():
        xn = _rms(x_ref[...], g_ref[...], NORM_EPS)
        h_scr[...] = (xn * (1.0 + sc_ref[...]) + sh_ref[...]).astype(BF16)

    @pl.when(j < n_p)
    def _():
        p_ref[...] = _mm(h_scr[...], w_ref[...])

    @pl.when((j >= n_p) & (j < n_p + n_q))
    def _():
        q_ref[...] = (_mm(h_scr[...], w_ref[...]) * (DIFF_HEAD ** -0.5)).astype(BF16)

    @pl.when((j >= n_p + n_q) & (j < n_p + 2 * n_q))
    def _():
        k_ref[...] = _mm(h_scr[...], w_ref[...])

    @pl.when(j >= n_p + 2 * n_q)
    def _():
        v_ref[...] = _mm(h_scr[...], w_ref[...])


def _mod_spec(mod, tm, d):
    if mod.shape[0] == 1:
        return pl.BlockSpec((1, d), lambda i, j: (0, 0))
    return pl.BlockSpec((tm, d), lambda i, j: (i, 0))


def _inproj(x, sh, sc, g_all, w_all, l, kv_all, n_rwkv_pad, d_diff, tm, tn=512):
    m, d = x.shape
    depth, _, n = w_all.shape
    n_p, n_q = n_rwkv_pad // tn, d_diff // tn
    clip = lambda j, lo: jnp.clip(j - lo, 0, n_q - 1)
    in_specs = [
        pl.BlockSpec((tm, d), lambda i, j: (i, 0)),
        _mod_spec(sh, tm, d),
        _mod_spec(sc, tm, d),
        pl.BlockSpec((None, 1, d), lambda i, j: (l, 0, 0)),
        pl.BlockSpec((None, d, tn), lambda i, j: (l, 0, j)),
        pl.BlockSpec(memory_space=pl.ANY),
        pl.BlockSpec(memory_space=pl.ANY),
    ]
    args = [x, sh, sc, g_all, w_all, kv_all[0], kv_all[1]]
    aliases = {5: 2, 6: 3}
    return pl.pallas_call(
        functools.partial(_inproj_kernel, n_p=n_p, n_q=n_q),
        grid=(m // tm, n // tn),
        in_specs=in_specs,
        out_specs=[
            pl.BlockSpec((tm, tn), lambda i, j: (i, jnp.minimum(j, n_p - 1))),
            pl.BlockSpec((tm, tn), lambda i, j: (i, clip(j, n_p))),
            pl.BlockSpec((None, tm, tn), lambda i, j: (l, i, clip(j, n_p + n_q))),
            pl.BlockSpec((None, tm, tn), lambda i, j: (l, i, clip(j, n_p + 2 * n_q))),
        ],
        out_shape=[
            jax.ShapeDtypeStruct((m, n_rwkv_pad), F32),
            jax.ShapeDtypeStruct((m, d_diff), BF16),
            jax.ShapeDtypeStruct((depth, m, d_diff), F32),
            jax.ShapeDtypeStruct((depth, m, d_diff), F32),
        ],
        scratch_shapes=[pltpu.VMEM((tm, d), BF16)],
        input_output_aliases=aliases,
        compiler_params=_cparams(("arbitrary", "arbitrary")),
        name="inproj",
    )(*args)


def _wkv_kernel(pr_ref, pk_ref, pv_ref, pw_ref, pa_ref, pg_ref,
                mur_ref, muk_ref, muv_ref, muw_ref, mua_ref, mug_ref,
                s0r_ref, s0k_ref, s0v_ref, s0w_ref, s0a_ref, s0g_ref,
                w2_ref, a2_ref, g2_ref,
                w0_ref, a0_ref, kkp_ref, ka_ref, rk_ref, lnw_ref, lnb_ref,
                st0_ref,
                y_ref, st_ref,
                sbd, cr, ck, cv, cw, ca, cg, *, C, n_chunks, n_pairs):
    chunk = pl.program_id(2)
    hd = RWKV_HEAD
    zero_blk = jnp.zeros((hd, hd), F32)

    @pl.when(chunk == 0)
    def _():
        for c_ref, s_ref in ((cr, s0r_ref), (ck, s0k_ref), (cv, s0v_ref), (cw, s0w_ref), (ca, s0a_ref), (cg, s0g_ref)):
            c_ref[0:1, :] = s_ref[...]
        for pi in range(n_pairs):
            top = jnp.concatenate([st0_ref[2 * pi], zero_blk], axis=1)
            bot = jnp.concatenate([zero_blk, st0_ref[2 * pi + 1]], axis=1)
            sbd[pi] = jnp.concatenate([top, bot], axis=0)

    def tshift(x_ref, mu_ref, c_ref):
        x = x_ref[...]
        row = lax.broadcasted_iota(jnp.int32, x.shape, 0)
        prev = jnp.where(row == 0, c_ref[0:1, :], pltpu.roll(x, 1, 0))
        c_ref[0:1, :] = x[C - 1:C, :]
        return x + (prev - x) * mu_ref[...]

    xr = tshift(pr_ref, mur_ref, cr)
    xk = tshift(pk_ref, muk_ref, ck)
    xv = tshift(pv_ref, muv_ref, cv)
    xw = tshift(pw_ref, muw_ref, cw)
    xa = tshift(pa_ref, mua_ref, ca)
    xg = tshift(pg_ref, mug_ref, cg)

    lane = lax.broadcasted_iota(jnp.int32, (1, LANES), 1)
    head0 = lane < hd
    li = lax.broadcasted_iota(jnp.int32, (LANES, LANES), 0)
    lj = lax.broadcasted_iota(jnp.int32, (LANES, LANES), 1)
    same_head = (li < hd) == (lj < hd)
    li2 = lax.broadcasted_iota(jnp.int32, (2 * LANES, LANES), 0)
    lj2 = lax.broadcasted_iota(jnp.int32, (2 * LANES, LANES), 1)
    li2 = jnp.where(li2 < LANES, li2, li2 - LANES)
    seg2 = jnp.where((li2 < hd) == (lj2 < hd), 1.0, 0.0).astype(BF16)

    def segsum(x):
        return _sel_lanes(x, seg2)

    wlin = w0_ref[...] + _mm(jnp.tanh(xw).astype(BF16), w2_ref[...])
    z = -wlin
    w_log = -(jnp.maximum(z, 0.0) + jnp.log1p(jnp.exp(-jnp.abs(z)))) - 0.5
    ne = -jnp.exp(w_log)
    a = jax.nn.sigmoid(a0_ref[...] + _mm(xa.astype(BF16), a2_ref[...]))
    g = _mm(jax.nn.sigmoid(xg).astype(BF16), g2_ref[...])
    kk_raw = xk * kkp_ref[...]
    k2 = xk * (1.0 + (a - 1.0) * ka_ref[...])
    rk_prod = xr * k2 * rk_ref[...]

    ti = lax.broadcasted_iota(jnp.int32, (C, 2 * C), 0)
    tj = lax.broadcasted_iota(jnp.int32, (C, 2 * C), 1)
    tj = jnp.where(tj < C, tj, tj - C)
    tri2 = jnp.where(tj <= ti, 1.0, 0.0).astype(BF16)
    cum = _sel_rows(tri2, ne)
    gam = jnp.exp(cum)
    gprev = jnp.exp(cum - ne)
    ginv = jnp.exp(-cum)
    dte = jnp.exp(cum[C - 1:C, :] - cum)
    rg_all = xr * gam
    ks_all = k2 * ginv
    kd_all = k2 * dte
    lnw = lnw_ref[...]
    lnb = lnb_ref[...]

    def stack_masked(x):
        return jnp.concatenate([jnp.where(head0, x, 0.0), jnp.where(head0, 0.0, x)], axis=0)

    def stack_twice(x):
        return jnp.concatenate([x, x], axis=0)

    R = 2 * C
    ri = lax.broadcasted_iota(jnp.int32, (R, R), 0)
    ci = lax.broadcasted_iota(jnp.int32, (R, R), 1)
    same_blk = (ri < C) == (ci < C)
    tr = jnp.where(ri < C, ri, ri - C)
    tc = jnp.where(ci < C, ci, ci - C)
    strict = same_blk & (tc < tr)
    incl = same_blk & (tc <= tr)
    n_steps = max(1, int(math.log2(C)))
    inv_hd = 1.0 / hd

    pairs = range(n_pairs)
    sls = [slice(pi * LANES, (pi + 1) * LANES) for pi in pairs]
    xv_p = [xv[:, sl] for sl in sls]
    sums = [segsum(jnp.concatenate([kk_raw[:, sl] * kk_raw[:, sl], rk_prod[:, sl]], axis=0)) for sl in sls]
    kk = [kk_raw[:, sl] * lax.rsqrt(jnp.maximum(sm[:C, :], 1e-24)) for sl, sm in zip(sls, sums)]
    bonus = [sm[C:, :] * v for sm, v in zip(sums, xv_p)]
    kka = [k * a[:, sl] for k, sl in zip(kk, sls)]
    lhs = [jnp.concatenate([stack_masked(-k * gprev[:, sl]), stack_masked(rg_all[:, sl])], axis=0)
           for k, sl in zip(kk, sls)]
    rhs = [jnp.concatenate([stack_twice(ka_ * ginv[:, sl]), stack_twice(ks_all[:, sl])], axis=0)
           for ka_, sl in zip(kka, sls)]
    vst = [stack_masked(v) for v in xv_p]
    sc4 = [_mm3(l_, r_, _NT) for l_, r_ in zip(lhs, rhs)]
    s_old = [sbd[pi] for pi in pairs]
    ls = [_mm3(l_, s_, _NT) for l_, s_ in zip(lhs, s_old)]
    n_ab = [jnp.where(strict, s4[:R, :R], 0.0) for s4 in sc4]
    m_ak = [jnp.where(strict, s4[:R, R:], 0.0) for s4 in sc4]
    m_r = [jnp.concatenate([jnp.where(incl, s4[R:, :R], 0.0), jnp.where(incl, s4[R:, R:], 0.0)], axis=1)
           for s4 in sc4]
    x_u = [l_[:R, :] + _mm3(mk, v) for l_, mk, v in zip(ls, m_ak, vst)]
    pw = n_ab
    for step in range(n_steps):
        if step + 1 < n_steps:
            t = [_mm3(p_, jnp.concatenate([x_, p_], axis=1)) for p_, x_ in zip(pw, x_u)]
            x_u = [x_ + t_[:, :LANES] for x_, t_ in zip(x_u, t)]
            pw = [t_[:, LANES:] for t_ in t]
        else:
            x_u = [x_ + _mm3(p_, x_) for p_, x_ in zip(pw, x_u)]
    y_st = [l_[R:, :] + _mm3(mr, jnp.concatenate([x_, v], axis=0))
            for l_, mr, x_, v in zip(ls, m_r, x_u, vst)]
    upd = [_mm3(jnp.concatenate([x_[:C, :] + x_[C:, :], v], axis=0),
                jnp.concatenate([ka_ * dte[:, sl], kd_all[:, sl]], axis=0), _TN)
           for x_, v, ka_, sl in zip(x_u, xv_p, kka, sls)]
    for pi in pairs:
        sbd[pi] = s_old[pi] * gam[C - 1:C, sls[pi]] + jnp.where(same_head, upd[pi], 0.0)

    y = [ys[:C, :] + ys[C:, :] for ys in y_st]
    yc = [y_ - segsum(y_) * inv_hd for y_ in y]
    y_var = [segsum(c_ * c_) * inv_hd for c_ in yc]
    for pi in pairs:
        sl = sls[pi]
        yn = yc[pi] * lax.rsqrt(y_var[pi] + GN_EPS) * lnw[:, sl] + lnb[:, sl]
        y_ref[:, sl] = ((yn + bonus[pi]) * g[:, sl]).astype(BF16)

    @pl.when(chunk == n_chunks - 1)
    def _():
        for pi in range(n_pairs):
            s_new = sbd[pi]
            st_ref[2 * pi] = s_new[0:hd, 0:hd]
            st_ref[2 * pi + 1] = s_new[hd:2 * hd, hd:2 * hd]


def _wkv(p, l, mu, shift0, w2, a2, g2, w0, a0, kkp, ka, rk, lnw, lnb, state0, n_seq, seq_len, C):
    m, n_pad = p.shape
    d_rwkv = w0.shape[-1]
    n_heads = d_rwkv // RWKV_HEAD
    n_pairs = WKV_PAIRS_PER_STEP
    wd = n_pairs * LANES
    n_groups = d_rwkv // wd
    n_chunks = seq_len // C
    nb = n_groups
    blk_w = 3 * d_rwkv // LANES
    blk_a = blk_w + 1
    blk_g = (blk_a + 1) * LANES // GATE_LORA

    row = lambda s, h, c: s * n_chunks + c
    seg_specs = [
        pl.BlockSpec((C, wd), lambda s, h, c: (row(s, h, c), h)),
        pl.BlockSpec((C, wd), lambda s, h, c: (row(s, h, c), nb + h)),
        pl.BlockSpec((C, wd), lambda s, h, c: (row(s, h, c), 2 * nb + h)),
        pl.BlockSpec((C, LANES), lambda s, h, c: (row(s, h, c), blk_w)),
        pl.BlockSpec((C, LANES), lambda s, h, c: (row(s, h, c), blk_a)),
        pl.BlockSpec((C, GATE_LORA), lambda s, h, c: (row(s, h, c), blk_g)),
    ]
    mu_specs = [
        pl.BlockSpec((None, 1, wd), lambda s, h, c: (l, 0, h)),
        pl.BlockSpec((None, 1, wd), lambda s, h, c: (l, 0, nb + h)),
        pl.BlockSpec((None, 1, wd), lambda s, h, c: (l, 0, 2 * nb + h)),
        pl.BlockSpec((None, 1, LANES), lambda s, h, c: (l, 0, blk_w)),
        pl.BlockSpec((None, 1, LANES), lambda s, h, c: (l, 0, blk_a)),
        pl.BlockSpec((None, 1, GATE_LORA), lambda s, h, c: (l, 0, blk_g)),
    ]
    s0_specs = [
        pl.BlockSpec((None, 1, wd), lambda s, h, c: (s, 0, h)),
        pl.BlockSpec((None, 1, wd), lambda s, h, c: (s, 0, nb + h)),
        pl.BlockSpec((None, 1, wd), lambda s, h, c: (s, 0, 2 * nb + h)),
        pl.BlockSpec((None, 1, LANES), lambda s, h, c: (s, 0, blk_w)),
        pl.BlockSpec((None, 1, LANES), lambda s, h, c: (s, 0, blk_a)),
        pl.BlockSpec((None, 1, GATE_LORA), lambda s, h, c: (s, 0, blk_g)),
    ]
    lora_specs = [
        pl.BlockSpec((None, LORA_PAD, wd), lambda s, h, c: (l, 0, h)),
        pl.BlockSpec((None, LORA_PAD, wd), lambda s, h, c: (l, 0, h)),
        pl.BlockSpec((None, GATE_LORA, wd), lambda s, h, c: (l, 0, h)),
    ]
    vec_spec = pl.BlockSpec((None, 1, wd), lambda s, h, c: (l, 0, h))
    state_spec = pl.BlockSpec((None, 2 * n_pairs, RWKV_HEAD, RWKV_HEAD), lambda s, h, c: (s, h, 0, 0))
    y, st = pl.pallas_call(
        functools.partial(_wkv_kernel, C=C, n_chunks=n_chunks, n_pairs=n_pairs),
        grid=(n_seq, n_groups, n_chunks),
        in_specs=seg_specs + mu_specs + s0_specs + lora_specs + [vec_spec] * 7 + [state_spec],
        out_specs=[
            pl.BlockSpec((C, wd), lambda s, h, c: (row(s, h, c), h)),
            state_spec,
        ],
        out_shape=[
            jax.ShapeDtypeStruct((m, d_rwkv), BF16),
            jax.ShapeDtypeStruct((n_seq, n_heads, RWKV_HEAD, RWKV_HEAD), F32),
        ],
        scratch_shapes=[pltpu.VMEM((n_pairs, LANES, LANES), F32)]
        + [pltpu.VMEM((SUBLANES, wd), F32)] * 3 + [pltpu.VMEM((SUBLANES, LANES), F32)] * 2
        + [pltpu.VMEM((SUBLANES, GATE_LORA), F32)],
        compiler_params=_cparams(("arbitrary", "arbitrary", "arbitrary")),
        name="wkv",
    )(p, p, p, p, p, p, mu, mu, mu, mu, mu, mu,
      shift0, shift0, shift0, shift0, shift0, shift0,
      w2, a2, g2, w0, a0, kkp, ka, rk, lnw, lnb, state0)
    return y, st


def _lambda(lqk_ref, lam_init):
    t = lqk_ref[...]
    s1 = jnp.sum(t[0:1, :] * t[1:2, :], axis=-1, keepdims=True)
    s2 = jnp.sum(t[2:3, :] * t[3:4, :], axis=-1, keepdims=True)
    return jnp.exp(s1) - jnp.exp(s2) + lam_init


def _split_maps(q):
    lane = lax.broadcasted_iota(jnp.int32, q.shape, 1)
    zero = jnp.zeros_like(q)
    return jnp.where(lane < DIFF_HEAD, q, zero), jnp.where(lane < DIFF_HEAD, zero, q)


def _subln(o, subln, lam_init):
    return _rms(o, subln, SUBLN_EPS) * (1.0 - lam_init)


FAR_TILES = 4


def _attn_kernel(q_ref, k_ref, v_ref, bias_ref, lqk_ref, subln_ref, o_ref,
                 kb, vb, m_s, a_s, *, TB, n_cast, lam_init):
    i = pl.program_id(1)

    @pl.when(i == 0)
    def _():
        ones = jnp.ones((TB, LANES), BF16)

        def cast(c, carry):
            off = pl.multiple_of(c * TB, TB)
            kb[pl.ds(off, TB), :] = k_ref[pl.ds(off, TB), :].astype(BF16)
            vb[pl.ds(off, TB), 0:LANES] = v_ref[pl.ds(off, TB), :].astype(BF16)
            vb[pl.ds(off, TB), LANES:2 * LANES] = ones
            return carry
        lax.fori_loop(0, n_cast, cast, 0)

    qmaps = _split_maps(q_ref[...])
    m_s[...] = jnp.full(m_s.shape, MASK_VALUE, F32)
    a_s[...] = jnp.zeros(a_s.shape, F32)

    def tile(off, width, bias):
        kt = kb[pl.ds(off, width), :]
        vt = vb[pl.ds(off, width), :]
        for mp in range(2):
            s = _mm(qmaps[mp], kt, _NT)
            if bias is not None:
                s = s + bias
            m_prev = m_s[mp]
            m_new = jnp.maximum(m_prev, jnp.max(s, axis=-1, keepdims=True))
            alpha = jnp.exp(m_prev - m_new)
            p = jnp.exp(s - m_new).astype(BF16)
            a_s[mp] = alpha * a_s[mp] + _mm(p, vt)
            m_s[mp] = m_new

    n_far = jnp.maximum(i - 1, 0)
    big = TB * FAR_TILES
    n_big = n_far // FAR_TILES
    n_small = n_far - n_big * FAR_TILES

    def far_big(j, carry):
        tile(pl.multiple_of(j * big, big), big, None)
        return carry
    lax.fori_loop(0, n_big, far_big, 0)

    def far_small(j, carry):
        tile(pl.multiple_of((n_big * FAR_TILES + j) * TB, TB), TB, None)
        return carry
    lax.fori_loop(0, n_small, far_small, 0)

    tile(pl.multiple_of(n_far * TB, TB), 2 * TB, jnp.where(i == 0, bias_ref[1], bias_ref[0]))

    lam = _lambda(lqk_ref, lam_init)
    on = [a_s[mp][:, 0:LANES] / a_s[mp][:, LANES:2 * LANES] for mp in range(2)]
    o_ref[...] = _subln(on[0] - lam * on[1], subln_ref[...], lam_init).astype(BF16)


def _attn_prompt(q, k_all, v_all, l, bias, lqk, subln, lam_init, TB):
    m, d_diff = q.shape
    n_heads = d_diff // LANES
    n_blk = m // TB
    return pl.pallas_call(
        functools.partial(_attn_kernel, TB=TB, n_cast=n_blk, lam_init=lam_init),
        grid=(n_heads, n_blk),
        in_specs=[
            pl.BlockSpec((TB, LANES), lambda h, i: (i, h)),
            pl.BlockSpec((None, m, LANES), lambda h, i: (l, 0, h)),
            pl.BlockSpec((None, m, LANES), lambda h, i: (l, 0, h)),
            pl.BlockSpec((None, 2, TB, 2 * TB), lambda h, i: (h, 0, 0, 0)),
            pl.BlockSpec((None, 4, DIFF_HEAD), lambda h, i: (l, 0, 0)),
            pl.BlockSpec((None, 1, LANES), lambda h, i: (l, 0, 0)),
        ],
        out_specs=pl.BlockSpec((TB, LANES), lambda h, i: (i, h)),
        out_shape=jax.ShapeDtypeStruct((m, d_diff), BF16),
        scratch_shapes=[pltpu.VMEM((m, LANES), BF16), pltpu.VMEM((m, 2 * LANES), BF16),
                        pltpu.VMEM((2, TB, 1), F32), pltpu.VMEM((2, TB, 2 * LANES), F32)],
        compiler_params=_cparams(("arbitrary", "arbitrary")),
        name="attn_prompt",
    )(q, k_all, v_all, bias, lqk, subln)


def _attn_sample_kernel(q_ref, kn_ref, vn_ref, kc_ref, vc_ref, bc_ref, bn_ref, lqk_ref, subln_ref, o_ref,
                        *, T, n_heads, lam_init):
    lam = _lambda(lqk_ref, lam_init)
    past = kc_ref.shape[0]
    for h in range(n_heads):
        hs = slice(h * LANES, (h + 1) * LANES)
        qa, qb = _split_maps(q_ref[:, hs])
        qs = jnp.concatenate([qa, qb], axis=0)
        kc = kc_ref[:, h, :].astype(BF16)
        vc = vc_ref[:, h, :].astype(BF16)
        sc = _mm(qs, kc, _NT) + jnp.concatenate([bc_ref[h]] * 2, axis=0)
        sn = _mm(qs, kn_ref[:, hs].astype(BF16), _NT) + jnp.concatenate([bn_ref[h]] * 2, axis=0)
        mx = jnp.maximum(jnp.max(sc, axis=-1, keepdims=True), jnp.max(sn, axis=-1, keepdims=True))
        pc = jnp.exp(sc - mx)
        pn = jnp.exp(sn - mx)
        den = jnp.sum(pc, axis=-1, keepdims=True) + jnp.sum(pn, axis=-1, keepdims=True)
        acc = _mm(pc.astype(BF16), vc) + _mm(pn.astype(BF16), vn_ref[:, hs].astype(BF16))
        on = acc / den
        o = on[:T, :] - lam * on[T:, :]
        o_ref[:, hs] = _subln(o, subln_ref[...], lam_init).astype(BF16)


def _attn_sample(q, k_all, v_all, l, k_cache, v_cache, bias_c, bias_n, lqk, subln, lam_init, n_seq, T):
    m, d_diff = q.shape
    n_heads = d_diff // LANES
    past = k_cache.shape[2]
    cache_spec = pl.BlockSpec((None, None, past, n_heads, LANES), lambda b: (l, b, 0, 0, 0))
    return pl.pallas_call(
        functools.partial(_attn_sample_kernel, T=T, n_heads=n_heads, lam_init=lam_init),
        grid=(n_seq,),
        in_specs=[
            pl.BlockSpec((T, d_diff), lambda b: (b, 0)),
            pl.BlockSpec((None, T, d_diff), lambda b: (l, b, 0)),
            pl.BlockSpec((None, T, d_diff), lambda b: (l, b, 0)),
            cache_spec,
            cache_spec,
            pl.BlockSpec((n_heads, T, past), lambda b: (0, 0, 0)),
            pl.BlockSpec((n_heads, T, T), lambda b: (0, 0, 0)),
            pl.BlockSpec((None, 4, DIFF_HEAD), lambda b: (l, 0, 0)),
            pl.BlockSpec((None, 1, LANES), lambda b: (l, 0, 0)),
        ],
        out_specs=pl.BlockSpec((T, d_diff), lambda b: (b, 0)),
        out_shape=jax.ShapeDtypeStruct((m, d_diff), BF16),
        compiler_params=_cparams(("arbitrary",)),
        name="attn_sample",
    )(q, k_all, v_all, k_cache, v_cache, bias_c, bias_n, lqk, subln)


def _outproj_kernel(x_ref, yr_ref, yd_ref, gt_ref, wt_ref, wb_ref, o_ref):
    mixed = _mm(yr_ref[...], wt_ref[...]) + _mm(yd_ref[...], wb_ref[...])
    o_ref[...] = x_ref[...] + gt_ref[...] * mixed


def _outproj(x, yr, yd, gt, w_out_all, l, tm):
    m, d = x.shape
    dr = yr.shape[1]
    dd = yd.shape[1]
    gt_spec = (pl.BlockSpec((1, d), lambda i: (0, 0)) if gt.shape[0] == 1
               else pl.BlockSpec((tm, d), lambda i: (i, 0)))
    return pl.pallas_call(
        _outproj_kernel,
        grid=(m // tm,),
        in_specs=[
            pl.BlockSpec((tm, d), lambda i: (i, 0)),
            pl.BlockSpec((tm, dr), lambda i: (i, 0)),
            pl.BlockSpec((tm, dd), lambda i: (i, 0)),
            gt_spec,
            pl.BlockSpec((None, dr, d), lambda i: (l, 0, 0)),
            pl.BlockSpec((None, dd, d), lambda i: (l, dr // dd, 0)),
        ],
        out_specs=pl.BlockSpec((tm, d), lambda i: (i, 0)),
        out_shape=jax.ShapeDtypeStruct((m, d), F32),
        compiler_params=_cparams(("arbitrary",)),
        name="outproj",
    )(x, yr, yd, gt, w_out_all, w_out_all)


def _shift_rows(x, prev, n):
    if n % SUBLANES == 0:
        return jnp.concatenate([prev, x[:x.shape[0] - n, :]], axis=0)
    rolled = pltpu.roll(x, n, 0)
    row = lax.broadcasted_iota(jnp.int32, x.shape, 0)
    out = rolled
    for r in range(n):
        out = jnp.where(row == r, prev[r:r + 1, :], out)
    return out


def _ffn_kernel(x_ref, sh_ref, sc_ref, gt_ref, g_ref, wg_ref, wv_ref, cw_ref, cb_ref, c0_ref, wd_ref, gf_ref,
                o_ref, co_ref, h_scr, acc_scr, carry, *, B, n_j, final_norm):
    i = pl.program_id(0)
    j = pl.program_id(1)
    tm = x_ref.shape[0]

    @pl.when(j == 0)
    def _():
        xn = _rms(x_ref[...], g_ref[...], NORM_EPS)
        h_scr[...] = (xn * (1.0 + sc_ref[...]) + sh_ref[...]).astype(BF16)
        acc_scr[...] = jnp.zeros(acc_scr.shape, F32)

    @pl.when(i == 0)
    def _():
        carry[j, 0:2 * B, :] = c0_ref[...]

    h = h_scr[...]
    ug = _mm(h, wg_ref[...])
    uv = _mm(h, wv_ref[...])
    prev = carry[j, 0:2 * B, :]
    s1 = _shift_rows(ug, prev[B:2 * B, :], B)
    s2 = _shift_rows(ug, prev, 2 * B)
    cw = cw_ref[...]
    z = cb_ref[...] + s2 * cw[0:1, :] + s1 * cw[1:2, :] + ug * cw[2:3, :]
    act = 0.5 * z * (1.0 + lax.erf(z * (2.0 ** -0.5)))
    acc_scr[...] += _mm((act * uv).astype(BF16), wd_ref[...])
    last2 = ug[tm - 2 * B:tm, :]
    carry[j, 0:2 * B, :] = last2
    tf = last2.shape[1]
    co_ref[:, pl.ds(pl.multiple_of(j * tf, tf), tf)] = last2

    @pl.when(j == n_j - 1)
    def _():
        x2 = x_ref[...] + gt_ref[...] * acc_scr[...]
        if final_norm:
            x2 = _rms(x2, gf_ref[...], NORM_EPS)
        o_ref[...] = x2


def _ffn(x, sh, sc, gt, g_all, w_up_all, conv_w_all, conv_b_all, conv0, w_down_all, g_final, l, B, tm,
         final_norm, tf=512):
    m, d = x.shape
    d_ff = w_down_all.shape[1]
    n_j = d_ff // tf
    rows_c = max(SUBLANES, 2 * B)
    return pl.pallas_call(
        functools.partial(_ffn_kernel, B=B, n_j=n_j, final_norm=final_norm),
        grid=(m // tm, n_j),
        in_specs=[
            pl.BlockSpec((tm, d), lambda i, j: (i, 0)),
            _mod_spec(sh, tm, d),
            _mod_spec(sc, tm, d),
            _mod_spec(gt, tm, d),
            pl.BlockSpec((None, 1, d), lambda i, j: (l, 0, 0)),
            pl.BlockSpec((None, d, tf), lambda i, j: (l, 0, j)),
            pl.BlockSpec((None, d, tf), lambda i, j: (l, 0, n_j + j)),
            pl.BlockSpec((None, CONV_W, tf), lambda i, j: (l, 0, j)),
            pl.BlockSpec((None, 1, tf), lambda i, j: (l, 0, j)),
            pl.BlockSpec((2 * B, tf), lambda i, j: (0, j)),
            pl.BlockSpec((None, tf, d), lambda i, j: (l, j, 0)),
            pl.BlockSpec((1, d), lambda i, j: (0, 0)),
        ],
        out_specs=[
            pl.BlockSpec((tm, d), lambda i, j: (i, 0)),
            pl.BlockSpec((2 * B, d_ff), lambda i, j: (0, 0)),
        ],
        out_shape=[
            jax.ShapeDtypeStruct((m, d), F32),
            jax.ShapeDtypeStruct((2 * B, d_ff), F32),
        ],
        scratch_shapes=[pltpu.VMEM((tm, d), BF16), pltpu.VMEM((tm, d), F32),
                        pltpu.VMEM((n_j, rows_c, tf), F32)],
        compiler_params=_cparams(("arbitrary", "arbitrary")),
        name="conv_ffn",
    )(x, sh, sc, gt, g_all, w_up_all, w_up_all, conv_w_all, conv_b_all, conv0, w_down_all, g_final)


def _pad_cols(a, n):
    return jnp.pad(a, [(0, 0)] * (a.ndim - 1) + [(0, n)])


def _regroup_rwkv_cols(a, d_rwkv):
    o = 3 * d_rwkv
    return jnp.concatenate([
        a[..., :o],
        _pad_cols(a[..., o:o + DECAY_LORA], LORA_PAD - DECAY_LORA),
        _pad_cols(a[..., o + DECAY_LORA:o + DECAY_LORA + AAA_LORA], LORA_PAD - AAA_LORA),
        a[..., o + DECAY_LORA + AAA_LORA:],
    ], axis=-1)


def _ungroup_rwkv_cols(a, d_rwkv):
    o = 3 * d_rwkv
    return jnp.concatenate([
        a[..., :o],
        a[..., o:o + DECAY_LORA],
        a[..., o + LORA_PAD:o + LORA_PAD + AAA_LORA],
        a[..., o + 2 * LORA_PAD:],
    ], axis=-1)


def kernel(x_prompt, x_sample, c_prompt, c_sample, cache_k, cache_v, state_wkv, state_shift, state_conv, w_ada, b_ada, g_mix, g_ffn, w_in, w_out, rwkv_mu, rwkv_w0, rwkv_w2, rwkv_a0, rwkv_a2, rwkv_g2, rwkv_kk, rwkv_ka, rwkv_rk, rwkv_ln_w, rwkv_ln_b, diff_lq1, diff_lk1, diff_lq2, diff_lk2, diff_subln, rel_table, ffn_up, ffn_conv_w, ffn_conv_b, ffn_down, g_final):
    depth, d_model, _ = w_in.shape
    bp, seq, _ = x_prompt.shape
    bs, dseq, _ = x_sample.shape
    past = cache_k.shape[2]
    d_rwkv = rwkv_w0.shape[1]
    d_diff = d_model - d_rwkv
    n_dheads = d_diff // (2 * DIFF_HEAD)
    n_rheads = d_rwkv // RWKV_HEAD
    n_rwkv_cols = rwkv_mu.shape[1]
    n_rwkv_pad = n_rwkv_cols + 2 * LORA_PAD - DECAY_LORA - AAA_LORA
    d_ff = ffn_down.shape[1]
    assert bp == 1, "prompt path handles one sequence"
    TB = min(256, seq // 2)
    assert seq % (2 * TB) == 0 and TB % CHUNK == 0 and TB >= MAX_DISTANCE
    tm_p = min(512, seq)
    m_s = bs * dseq

    w_in_b = jnp.concatenate([_regroup_rwkv_cols(w_in[..., :n_rwkv_cols], d_rwkv), w_in[..., n_rwkv_cols:]],
                             axis=-1).astype(BF16)
    mu_p = _regroup_rwkv_cols(rwkv_mu, d_rwkv)[:, None]
    w2_b = jnp.pad(rwkv_w2, ((0, 0), (0, LORA_PAD - DECAY_LORA), (0, 0))).astype(BF16)
    a2_b = jnp.pad(rwkv_a2, ((0, 0), (0, LORA_PAD - AAA_LORA), (0, 0))).astype(BF16)
    g2_b = rwkv_g2.astype(BF16)
    w_out_b = _cast_bf16(w_out)
    up_b = _cast_bf16(ffn_up)
    down_b = _cast_bf16(ffn_down)
    vec = lambda a: a.reshape(depth, 1, -1)
    wkv_args = (w2_b, a2_b, g2_b, vec(rwkv_w0), vec(rwkv_a0), vec(rwkv_kk), vec(rwkv_ka), vec(rwkv_rk),
                vec(rwkv_ln_w), vec(rwkv_ln_b))
    g_mix_v, g_ffn_v, subln_v, conv_b_v = vec(g_mix), vec(g_ffn), vec(diff_subln), vec(ffn_conv_b)
    gf = g_final[None]
    lqk = jnp.stack([diff_lq1, diff_lk1, diff_lq2, diff_lk2], axis=1)

    n_c = bp + bs
    n_c_pad = -(-n_c // SUBLANES) * SUBLANES
    c_rows = jnp.pad(jnp.concatenate([c_prompt, c_sample], axis=0), ((0, n_c_pad - n_c), (0, 0)))
    mod = _adaln_mod(c_rows, w_ada, b_ada).reshape(depth, n_c_pad, 6, d_model)

    ql = jnp.arange(TB)
    rel_diag = ql[None, :] - ql[:, None]
    allowed = (ql[None, :] // CHUNK) <= (ql[:, None] // CHUNK)
    bucket_diag = jnp.where(allowed, _t5_bucket(rel_diag), -1)
    bucket_prev = _t5_bucket(rel_diag - TB)
    bucket_p = jnp.concatenate([
        jnp.concatenate([bucket_prev, bucket_diag], axis=1),
        jnp.concatenate([bucket_diag, jnp.full((TB, TB), -1)], axis=1),
    ], axis=0)
    bias_p = _bias_lookup(rel_table, bucket_p.astype(jnp.int32), FAR_BUCKET).reshape(n_dheads, 2, TB, 2 * TB)
    q_pos = past + jnp.arange(dseq)
    bias_c = _bias_lookup(rel_table, _t5_bucket(jnp.arange(past)[None, :] - q_pos[:, None]).astype(jnp.int32), FAR_BUCKET)
    bias_n = _bias_lookup(rel_table, _t5_bucket(q_pos[None, :] - q_pos[:, None]).astype(jnp.int32), FAR_BUCKET)

    xp = x_prompt.reshape(seq, d_model)
    xs = x_sample.reshape(m_s, d_model)
    zero_shift = jnp.zeros((1, 1, n_rwkv_pad), F32)
    zero_state = jnp.zeros((1, n_rheads, RWKV_HEAD, RWKV_HEAD), F32)
    zero_conv = jnp.zeros((2, d_ff), F32)
    shift_s_in = _regroup_rwkv_cols(state_shift, d_rwkv)
    outs_p = {k: [] for k in ("wkv", "shift", "conv")}
    outs_s = {k: [] for k in ("wkv", "shift", "conv")}
    kv_p = (jnp.zeros((depth, seq, d_diff), F32), jnp.zeros((depth, seq, d_diff), F32))
    kv_s = (jnp.zeros((depth, m_s, d_diff), F32), jnp.zeros((depth, m_s, d_diff), F32))

    def time_major(a):
        return a.reshape(bs, dseq, -1).swapaxes(0, 1).reshape(m_s, -1)

    def batch_major(a):
        return a.reshape(dseq, bs, -1).swapaxes(0, 1).reshape(m_s, -1)

    for l in range(depth):
        lam_init = 0.8 - 0.6 * math.exp(-0.3 * l)
        last = l == depth - 1

        mp = mod[l, 0:bp]
        sh1, sc1, gt1, sh2, sc2, gt2 = (mp[:, t] for t in range(6))
        p, q, *kv_p = _inproj(xp, sh1, sc1, g_mix_v, w_in_b, l, kv_p, n_rwkv_pad, d_diff, tm_p)
        y_r, wkv = _wkv(p, l, mu_p, zero_shift, *wkv_args, zero_state, 1, seq, min(64, seq))
        y_d = _attn_prompt(q, kv_p[0], kv_p[1], l, bias_p, lqk, subln_v, lam_init, TB)
        x1 = _outproj(xp, y_r, y_d, gt1, w_out_b, l, tm_p)
        xp, conv = _ffn(x1, sh2, sc2, gt2, g_ffn_v, up_b, ffn_conv_w, conv_b_v, zero_conv, down_b, gf, l,
                        1, tm_p, last)
        outs_p["wkv"].append(wkv)
        outs_p["shift"].append(_ungroup_rwkv_cols(p[seq - 1:seq], d_rwkv).reshape(bp, 1, n_rwkv_cols))
        outs_p["conv"].append(conv.reshape(bp, CONV_W - 1, d_ff))

        ms = mod[l, bp:bp + bs]
        rows_bm = jnp.repeat(ms, dseq, axis=0)
        rows_tm = jnp.tile(ms, (dseq, 1, 1))
        p, q, *kv_s = _inproj(xs, rows_bm[:, 0], rows_bm[:, 1], g_mix_v, w_in_b, l, kv_s, n_rwkv_pad, d_diff, m_s)
        y_r, wkv = _wkv(p, l, mu_p, shift_s_in[l], *wkv_args, state_wkv[l], bs, dseq, dseq)
        y_d = _attn_sample(q, kv_s[0], kv_s[1], l, cache_k, cache_v, bias_c, bias_n, lqk, subln_v, lam_init,
                           bs, dseq)
        x1 = _outproj(xs, y_r, y_d, rows_bm[:, 2], w_out_b, l, m_s)
        conv0 = state_conv[l].swapaxes(0, 1).reshape((CONV_W - 1) * bs, d_ff)
        x2, conv = _ffn(time_major(x1), rows_tm[:, 3], rows_tm[:, 4], rows_tm[:, 5], g_ffn_v, up_b, ffn_conv_w,
                        conv_b_v, conv0, down_b, gf, l, bs, m_s, last)
        xs = batch_major(x2)
        outs_s["wkv"].append(wkv)
        p_last = p.reshape(bs, dseq, n_rwkv_pad)[:, dseq - 1:dseq]
        outs_s["shift"].append(_ungroup_rwkv_cols(p_last, d_rwkv))
        outs_s["conv"].append(conv.reshape(CONV_W - 1, bs, d_ff).swapaxes(0, 1))

    st = lambda xs_: jnp.stack(xs_)
    head_shape = (n_dheads, 2 * DIFF_HEAD)
    return (xp.reshape(bp, seq, d_model), xs.reshape(bs, dseq, d_model),
            kv_p[0].reshape(depth, bp, seq, *head_shape), kv_p[1].reshape(depth, bp, seq, *head_shape),
            st(outs_p["wkv"]), st(outs_p["shift"]), st(outs_p["conv"]),
            kv_s[0].reshape(depth, bs, dseq, *head_shape), kv_s[1].reshape(depth, bs, dseq, *head_shape),
            st(outs_s["wkv"]), st(outs_s["shift"]), st(outs_s["conv"]))
```

```python
import functools
import math

import jax
import jax.numpy as jnp
from jax import lax
from jax.experimental import pallas as pl
from jax.experimental.pallas import tpu as pltpu

F32 = jnp.float32
BF16 = jnp.bfloat16

CHUNK = 64
RWKV_HEAD = 64
DIFF_HEAD = 64
N_BUCKETS = 32
MAX_DISTANCE = 128
NORM_EPS = 1e-6
GN_EPS = 64e-5
SUBLN_EPS = 1e-5
DECAY_LORA = 96
AAA_LORA = 96
GATE_LORA = 256
LORA_PAD = 128
CONV_W = 3

LANES = 128
SUBLANES = 8
VMEM_LIMIT_BYTES = 56 * 1024 * 1024
MASK_VALUE = -1e30
WKV_PAIRS_PER_STEP = 4

_NN = (((1,), (0,)), ((), ()))
_NT = (((1,), (1,)), ((), ()))
_TN = (((0,), (0,)), ((), ()))


def _mm(a, b, dims=_NN):
    return lax.dot_general(a, b, dims, preferred_element_type=F32)


def _hi_lo(x):
    h = x.astype(BF16)
    return h, (x - h.astype(F32)).astype(BF16)


def _mm3(a, b, dims=_NN):
    ah, al = _hi_lo(a)
    bh, bl = _hi_lo(b)
    (ca,), (cb,) = dims[0]
    k = a.shape[ca]
    if (ca == 0 or k % LANES == 0) and (cb == 0 or k % LANES == 0):
        return _mm(jnp.concatenate([ah, al, ah], axis=ca), jnp.concatenate([bh, bh, bl], axis=cb), dims)
    return _mm(ah, bh, dims) + (_mm(al, bh, dims) + _mm(ah, bl, dims))


def _sel_rows(sel2, x):
    xh, xl = _hi_lo(x)
    return _mm(sel2, jnp.concatenate([xh, xl], axis=0))


def _sel_lanes(x, sel2):
    xh, xl = _hi_lo(x)
    return _mm(jnp.concatenate([xh, xl], axis=1), sel2)


def _cparams(sem):
    return pltpu.CompilerParams(dimension_semantics=sem, vmem_limit_bytes=VMEM_LIMIT_BYTES)


def _rms(x, g, eps):
    return x * lax.rsqrt(jnp.mean(x * x, axis=-1, keepdims=True) + eps) * g


def _cast_kernel(x_ref, o_ref):
    o_ref[...] = x_ref[...].astype(BF16)


def _cast_bf16(w, tr=512, tc=1024):
    depth, rows, cols = w.shape
    spec = pl.BlockSpec((None, tr, tc), lambda l, i, j: (l, i, j))
    return pl.pallas_call(
        _cast_kernel,
        grid=(depth, rows // tr, cols // tc),
        in_specs=[spec],
        out_specs=spec,
        out_shape=jax.ShapeDtypeStruct(w.shape, BF16),
        compiler_params=_cparams(("arbitrary", "arbitrary", "arbitrary")),
        name="cast_bf16",
    )(w)


def _mod_kernel(c_ref, w_ref, b_ref, o_ref):
    c = c_ref[...]
    s = (c * jax.nn.sigmoid(c)).astype(BF16)
    o_ref[...] = _mm(s, w_ref[...].astype(BF16)) + b_ref[...]


def _adaln_mod(c_rows, w_ada, b_ada, tn=1024):
    depth, d, n = w_ada.shape
    rows = c_rows.shape[0]
    return pl.pallas_call(
        _mod_kernel,
        grid=(depth, n // tn),
        in_specs=[
            pl.BlockSpec((rows, d), lambda l, j: (0, 0)),
            pl.BlockSpec((None, d, tn), lambda l, j: (l, 0, j)),
            pl.BlockSpec((None, 1, tn), lambda l, j: (l, 0, j)),
        ],
        out_specs=pl.BlockSpec((None, rows, tn), lambda l, j: (l, 0, j)),
        out_shape=jax.ShapeDtypeStruct((depth, rows, n), F32),
        compiler_params=_cparams(("arbitrary", "arbitrary")),
        name="adaln_mod",
    )(c_rows, w_ada, b_ada.reshape(depth, 1, n))


def _bias_kernel(tab_ref, bucket_ref, o_ref, *, far_bucket):
    h = pl.program_id(0)
    bucket = bucket_ref[...]
    far = tab_ref[far_bucket, h]
    acc = jnp.full(bucket.shape, MASK_VALUE, F32)
    for b in range(N_BUCKETS):
        acc = jnp.where(bucket == b, tab_ref[b, h] - far, acc)
    o_ref[...] = acc


def _bias_lookup(table, bucket, far_bucket):
    n_heads = table.shape[1]
    r, c = bucket.shape
    return pl.pallas_call(
        functools.partial(_bias_kernel, far_bucket=far_bucket),
        grid=(n_heads,),
        in_specs=[
            pl.BlockSpec(memory_space=pltpu.SMEM),
            pl.BlockSpec((r, c), lambda h: (0, 0)),
        ],
        out_specs=pl.BlockSpec((None, r, c), lambda h: (h, 0, 0)),
        out_shape=jax.ShapeDtypeStruct((n_heads, r, c), F32),
        compiler_params=_cparams(("arbitrary",)),
        name="bias_lookup",
    )(table, bucket)


def _t5_bucket(rel):
    nb = N_BUCKETS // 2
    max_exact = nb // 2
    n = jnp.abs(rel)
    nf = jnp.maximum(n, 1).astype(F32)
    large = max_exact + (jnp.log(nf / max_exact) / math.log(MAX_DISTANCE / max_exact) * (nb - max_exact)).astype(jnp.int32)
    large = jnp.minimum(large, nb - 1)
    return jnp.where(rel > 0, nb, 0) + jnp.where(n < max_exact, n, large)


FAR_BUCKET = N_BUCKETS // 2 - 1


def _inproj_kernel(x_ref, sh_ref, sc_ref, g_ref, w_ref, *rest, n_p, n_q):
    p_ref, q_ref, k_ref, v_ref, h_scr = rest[-5:]
    j = pl.program_id(1)

    @pl.when(j == 0)
    def _():
        xn = _rms(x_ref[...], g_ref[...], NORM_EPS)
        h_scr[...] = (xn * (1.0 + sc_ref[...]) + sh_ref[...]).astype(BF16)

    @pl.when(j < n_p)
    def _():
        p_ref[...] = _mm(h_scr[...], w_ref[...])

    @pl.when((j >= n_p) & (j < n_p + n_q))
    def _():
        q_ref[...] = (_mm(h_scr[...], w_ref[...]) * (DIFF_HEAD ** -0.5)).astype(BF16)

    @pl.when((j >= n_p + n_q) & (j < n_p + 2 * n_q))
    def _():
        k_ref[...] = _mm(h_scr[...], w_ref[...])

    @pl.when(j >= n_p + 2 * n_q)
    def _():
        v_ref[...] = _mm(h_scr[...], w_ref[...])


def _mod_spec(mod, tm, d):
    if mod.shape[0] == 1:
        return pl.BlockSpec((1, d), lambda i, j: (0, 0))
    return pl.BlockSpec((tm, d), lambda i, j: (i, 0))


def _inproj(x, sh, sc, g_all, w_all, l, kv_all, n_rwkv_pad, d_diff, tm, tn=512):
    m, d = x.shape
    depth, _, n = w_all.shape
    n_p, n_q = n_rwkv_pad // tn, d_diff // tn
    clip = lambda j, lo: jnp.clip(j - lo, 0, n_q - 1)
    in_specs = [
        pl.BlockSpec((tm, d), lambda i, j: (i, 0)),
        _mod_spec(sh, tm, d),
        _mod_spec(sc, tm, d),
        pl.BlockSpec((None, 1, d), lambda i, j: (l, 0, 0)),
        pl.BlockSpec((None, d, tn), lambda i, j: (l, 0, j)),
        pl.BlockSpec(memory_space=pl.ANY),
        pl.BlockSpec(memory_space=pl.ANY),
    ]
    args = [x, sh, sc, g_all, w_all, kv_all[0], kv_all[1]]
    aliases = {5: 2, 6: 3}
    return pl.pallas_call(
        functools.partial(_inproj_kernel, n_p=n_p, n_q=n_q),
        grid=(m // tm, n // tn),
        in_specs=in_specs,
        out_specs=[
            pl.BlockSpec((tm, tn), lambda i, j: (i, jnp.minimum(j, n_p - 1))),
            pl.BlockSpec((tm, tn), lambda i, j: (i, clip(j, n_p))),
            pl.BlockSpec((None, tm, tn), lambda i, j: (l, i, clip(j, n_p + n_q))),
            pl.BlockSpec((None, tm, tn), lambda i, j: (l, i, clip(j, n_p + 2 * n_q))),
        ],
        out_shape=[
            jax.ShapeDtypeStruct((m, n_rwkv_pad), F32),
            jax.ShapeDtypeStruct((m, d_diff), BF16),
            jax.ShapeDtypeStruct((depth, m, d_diff), F32),
            jax.ShapeDtypeStruct((depth, m, d_diff), F32),
        ],
        scratch_shapes=[pltpu.VMEM((tm, d), BF16)],
        input_output_aliases=aliases,
        compiler_params=_cparams(("arbitrary", "arbitrary")),
        name="inproj",
    )(*args)


def _wkv_kernel(pr_ref, pk_ref, pv_ref, pw_ref, pa_ref, pg_ref,
                mur_ref, muk_ref, muv_ref, muw_ref, mua_ref, mug_ref,
                s0r_ref, s0k_ref, s0v_ref, s0w_ref, s0a_ref, s0g_ref,
                w2_ref, a2_ref, g2_ref,
                w0_ref, a0_ref, kkp_ref, ka_ref, rk_ref, lnw_ref, lnb_ref,
                st0_ref,
                y_ref, st_ref,
                sbd, cr, ck, cv, cw, ca, cg, *, C, n_chunks, n_pairs):
    chunk = pl.program_id(2)
    hd = RWKV_HEAD
    zero_blk = jnp.zeros((hd, hd), F32)

    @pl.when(chunk == 0)
    def _():
        for c_ref, s_ref in ((cr, s0r_ref), (ck, s0k_ref), (cv, s0v_ref), (cw, s0w_ref), (ca, s0a_ref), (cg, s0g_ref)):
            c_ref[0:1, :] = s_ref[...]
        for pi in range(n_pairs):
            top = jnp.concatenate([st0_ref[2 * pi], zero_blk], axis=1)
            bot = jnp.concatenate([zero_blk, st0_ref[2 * pi + 1]], axis=1)
            sbd[pi] = jnp.concatenate([top, bot], axis=0)

    def tshift(x_ref, mu_ref, c_ref):
        x = x_ref[...]
        row = lax.broadcasted_iota(jnp.int32, x.shape, 0)
        prev = jnp.where(row == 0, c_ref[0:1, :], pltpu.roll(x, 1, 0))
        c_ref[0:1, :] = x[C - 1:C, :]
        return x + (prev - x) * mu_ref[...]

    xr = tshift(pr_ref, mur_ref, cr)
    xk = tshift(pk_ref, muk_ref, ck)
    xv = tshift(pv_ref, muv_ref, cv)
    xw = tshift(pw_ref, muw_ref, cw)
    xa = tshift(pa_ref, mua_ref, ca)
    xg = tshift(pg_ref, mug_ref, cg)

    lane = lax.broadcasted_iota(jnp.int32, (1, LANES), 1)
    head0 = lane < hd
    li = lax.broadcasted_iota(jnp.int32, (LANES, LANES), 0)
    lj = lax.broadcasted_iota(jnp.int32, (LANES, LANES), 1)
    same_head = (li < hd) == (lj < hd)
    li2 = lax.broadcasted_iota(jnp.int32, (2 * LANES, LANES), 0)
    lj2 = lax.broadcasted_iota(jnp.int32, (2 * LANES, LANES), 1)
    li2 = jnp.where(li2 < LANES, li2, li2 - LANES)
    seg2 = jnp.where((li2 < hd) == (lj2 < hd), 1.0, 0.0).astype(BF16)

    def segsum(x):
        return _sel_lanes(x, seg2)

    wlin = w0_ref[...] + _mm(jnp.tanh(xw).astype(BF16), w2_ref[...])
    z = -wlin
    w_log = -(jnp.maximum(z, 0.0) + jnp.log1p(jnp.exp(-jnp.abs(z)))) - 0.5
    ne = -jnp.exp(w_log)
    a = jax.nn.sigmoid(a0_ref[...] + _mm(xa.astype(BF16), a2_ref[...]))
    g = _mm(jax.nn.sigmoid(xg).astype(BF16), g2_ref[...])
    kk_raw = xk * kkp_ref[...]
    k2 = xk * (1.0 + (a - 1.0) * ka_ref[...])
    rk_prod = xr * k2 * rk_ref[...]

    ti = lax.broadcasted_iota(jnp.int32, (C, 2 * C), 0)
    tj = lax.broadcasted_iota(jnp.int32, (C, 2 * C), 1)
    tj = jnp.where(tj < C, tj, tj - C)
    tri2 = jnp.where(tj <= ti, 1.0, 0.0).astype(BF16)
    cum = _sel_rows(tri2, ne)
    gam = jnp.exp(cum)
    gprev = jnp.exp(cum - ne)
    ginv = jnp.exp(-cum)
    dte = jnp.exp(cum[C - 1:C, :] - cum)
    rg_all = xr * gam
    ks_all = k2 * ginv
    kd_all = k2 * dte
    lnw = lnw_ref[...]
    lnb = lnb_ref[...]

    def stack_masked(x):
        return jnp.concatenate([jnp.where(head0, x, 0.0), jnp.where(head0, 0.0, x)], axis=0)

    def stack_twice(x):
        return jnp.concatenate([x, x], axis=0)

    R = 2 * C
    ri = lax.broadcasted_iota(jnp.int32, (R, R), 0)
    ci = lax.broadcasted_iota(jnp.int32, (R, R), 1)
    same_blk = (ri < C) == (ci < C)
    tr = jnp.where(ri < C, ri, ri - C)
    tc = jnp.where(ci < C, ci, ci - C)
    strict = same_blk & (tc < tr)
    incl = same_blk & (tc <= tr)
    n_steps = max(1, int(math.log2(C)))
    inv_hd = 1.0 / hd

    pairs = range(n_pairs)
    sls = [slice(pi * LANES, (pi + 1) * LANES) for pi in pairs]
    xv_p = [xv[:, sl] for sl in sls]
    sums = [segsum(jnp.concatenate([kk_raw[:, sl] * kk_raw[:, sl], rk_prod[:, sl]], axis=0)) for sl in sls]
    kk = [kk_raw[:, sl] * lax.rsqrt(jnp.maximum(sm[:C, :], 1e-24)) for sl, sm in zip(sls, sums)]
    bonus = [sm[C:, :] * v for sm, v in zip(sums, xv_p)]
    kka = [k * a[:, sl] for k, sl in zip(kk, sls)]
    lhs = [jnp.concatenate([stack_masked(-k * gprev[:, sl]), stack_masked(rg_all[:, sl])], axis=0)
           for k, sl in zip(kk, sls)]
    rhs = [jnp.concatenate([stack_twice(ka_ * ginv[:, sl]), stack_twice(ks_all[:, sl])], axis=0)
           for ka_, sl in zip(kka, sls)]
    vst = [stack_masked(v) for v in xv_p]
    sc4 = [_mm3(l_, r_, _NT) for l_, r_ in zip(lhs, rhs)]
    s_old = [sbd[pi] for pi in pairs]
    ls = [_mm3(l_, s_, _NT) for l_, s_ in zip(lhs, s_old)]
    n_ab = [jnp.where(strict, s4[:R, :R], 0.0) for s4 in sc4]
    m_ak = [jnp.where(strict, s4[:R, R:], 0.0) for s4 in sc4]
    m_r = [jnp.concatenate([jnp.where(incl, s4[R:, :R], 0.0), jnp.where(incl, s4[R:, R:], 0.0)], axis=1)
           for s4 in sc4]
    x_u = [l_[:R, :] + _mm3(mk, v) for l_, mk, v in zip(ls, m_ak, vst)]
    pw = n_ab
    for step in range(n_steps):
        if step + 1 < n_steps:
            t = [_mm3(p_, jnp.concatenate([x_, p_], axis=1)) for p_, x_ in zip(pw, x_u)]
            x_u = [x_ + t_[:, :LANES] for x_, t_ in zip(x_u, t)]
            pw = [t_[:, LANES:] for t_ in t]
        else:
            x_u = [x_ + _mm3(p_, x_) for p_, x_ in zip(pw, x_u)]
    y_st = [l_[R:, :] + _mm3(mr, jnp.concatenate([x_, v], axis=0))
            for l_, mr, x_, v in zip(ls, m_r, x_u, vst)]
    upd = [_mm3(jnp.concatenate([x_[:C, :] + x_[C:, :], v], axis=0),
                jnp.concatenate([ka_ * dte[:, sl], kd_all[:, sl]], axis=0), _TN)
           for x_, v, ka_, sl in zip(x_u, xv_p, kka, sls)]
    for pi in pairs:
        sbd[pi] = s_old[pi] * gam[C - 1:C, sls[pi]] + jnp.where(same_head, upd[pi], 0.0)

    y = [ys[:C, :] + ys[C:, :] for ys in y_st]
    yc = [y_ - segsum(y_) * inv_hd for y_ in y]
    y_var = [segsum(c_ * c_) * inv_hd for c_ in yc]
    for pi in pairs:
        sl = sls[pi]
        yn = yc[pi] * lax.rsqrt(y_var[pi] + GN_EPS) * lnw[:, sl] + lnb[:, sl]
        y_ref[:, sl] = ((yn + bonus[pi]) * g[:, sl]).astype(BF16)

    @pl.when(chunk == n_chunks - 1)
    def _():
        for pi in range(n_pairs):
            s_new = sbd[pi]
            st_ref[2 * pi] = s_new[0:hd, 0:hd]
            st_ref[2 * pi + 1] = s_new[hd:2 * hd, hd:2 * hd]


def _wkv(p, l, mu, shift0, w2, a2, g2, w0, a0, kkp, ka, rk, lnw, lnb, state0, n_seq, seq_len, C):
    m, n_pad = p.shape
    d_rwkv = w0.shape[-1]
    n_heads = d_rwkv // RWKV_HEAD
    n_pairs = WKV_PAIRS_PER_STEP
    wd = n_pairs * LANES
    n_groups = d_rwkv // wd
    n_chunks = seq_len // C
    nb = n_groups
    blk_w = 3 * d_rwkv // LANES
    blk_a = blk_w + 1
    blk_g = (blk_a + 1) * LANES // GATE_LORA

    row = lambda s, h, c: s * n_chunks + c
    seg_specs = [
        pl.BlockSpec((C, wd), lambda s, h, c: (row(s, h, c), h)),
        pl.BlockSpec((C, wd), lambda s, h, c: (row(s, h, c), nb + h)),
        pl.BlockSpec((C, wd), lambda s, h, c: (row(s, h, c), 2 * nb + h)),
        pl.BlockSpec((C, LANES), lambda s, h, c: (row(s, h, c), blk_w)),
        pl.BlockSpec((C, LANES), lambda s, h, c: (row(s, h, c), blk_a)),
        pl.BlockSpec((C, GATE_LORA), lambda s, h, c: (row(s, h, c), blk_g)),
    ]
    mu_specs = [
        pl.BlockSpec((None, 1, wd), lambda s, h, c: (l, 0, h)),
        pl.BlockSpec((None, 1, wd), lambda s, h, c: (l, 0, nb + h)),
        pl.BlockSpec((None, 1, wd), lambda s, h, c: (l, 0, 2 * nb + h)),
        pl.BlockSpec((None, 1, LANES), lambda s, h, c: (l, 0, blk_w)),
        pl.BlockSpec((None, 1, LANES), lambda s, h, c: (l, 0, blk_a)),
        pl.BlockSpec((None, 1, GATE_LORA), lambda s, h, c: (l, 0, blk_g)),
    ]
    s0_specs = [
        pl.BlockSpec((None, 1, wd), lambda s, h, c: (s, 0, h)),
        pl.BlockSpec((None, 1, wd), lambda s, h, c: (s, 0, nb + h)),
        pl.BlockSpec((None, 1, wd), lambda s, h, c: (s, 0, 2 * nb + h)),
        pl.BlockSpec((None, 1, LANES), lambda s, h, c: (s, 0, blk_w)),
        pl.BlockSpec((None, 1, LANES), lambda s, h, c: (s, 0, blk_a)),
        pl.BlockSpec((None, 1, GATE_LORA), lambda s, h, c: (s, 0, blk_g)),
    ]
    lora_specs = [
        pl.BlockSpec((None, LORA_PAD, wd), lambda s, h, c: (l, 0, h)),
        pl.BlockSpec((None, LORA_PAD, wd), lambda s, h, c: (l, 0, h)),
        pl.BlockSpec((None, GATE_LORA, wd), lambda s, h, c: (l, 0, h)),
    ]
    vec_spec = pl.BlockSpec((None, 1, wd), lambda s, h, c: (l, 0, h))
    state_spec = pl.BlockSpec((None, 2 * n_pairs, RWKV_HEAD, RWKV_HEAD), lambda s, h, c: (s, h, 0, 0))
    y, st = pl.pallas_call(
        functools.partial(_wkv_kernel, C=C, n_chunks=n_chunks, n_pairs=n_pairs),
        grid=(n_seq, n_groups, n_chunks),
        in_specs=seg_specs + mu_specs + s0_specs + lora_specs + [vec_spec] * 7 + [state_spec],
        out_specs=[
            pl.BlockSpec((C, wd), lambda s, h, c: (row(s, h, c), h)),
            state_spec,
        ],
        out_shape=[
            jax.ShapeDtypeStruct((m, d_rwkv), BF16),
            jax.ShapeDtypeStruct((n_seq, n_heads, RWKV_HEAD, RWKV_HEAD), F32),
        ],
        scratch_shapes=[pltpu.VMEM((n_pairs, LANES, LANES), F32)]
        + [pltpu.VMEM((SUBLANES, wd), F32)] * 3 + [pltpu.VMEM((SUBLANES, LANES), F32)] * 2
        + [pltpu.VMEM((SUBLANES, GATE_LORA), F32)],
        compiler_params=_cparams(("arbitrary", "arbitrary", "arbitrary")),
        name="wkv",
    )(p, p, p, p, p, p, mu, mu, mu, mu, mu, mu,
      shift0, shift0, shift0, shift0, shift0, shift0,
      w2, a2, g2, w0, a0, kkp, ka, rk, lnw, lnb, state0)
    return y, st


def _lambda(lqk_ref, lam_init):
    t = lqk_ref[...]
    s1 = jnp.sum(t[0:1, :] * t[1:2, :], axis=-1, keepdims=True)
    s2 = jnp.sum(t[2:3, :] * t[3:4, :], axis=-1, keepdims=True)
    return jnp.exp(s1) - jnp.exp(s2) + lam_init


def _split_maps(q):
    lane = lax.broadcasted_iota(jnp.int32, q.shape, 1)
    zero = jnp.zeros_like(q)
    return jnp.where(lane < DIFF_HEAD, q, zero), jnp.where(lane < DIFF_HEAD, zero, q)


def _subln(o, subln, lam_init):
    return _rms(o, subln, SUBLN_EPS) * (1.0 - lam_init)


FAR_TILES = 4


def _attn_kernel(q_ref, k_ref, v_ref, bias_ref, lqk_ref, subln_ref, o_ref,
                 kb, vt, m_s, a_s, *, TB, n_cast, lam_init):
    g = pl.program_id(1)

    @pl.when(g == 0)
    def _():
        vt[LANES:2 * LANES, :] = jnp.ones((LANES, vt.shape[1]), BF16)

        def cast(c, carry):
            off = pl.multiple_of(c * TB, TB)
            kb[pl.ds(off, TB), :] = k_ref[pl.ds(off, TB), :].astype(BF16)
            vt[0:LANES, pl.ds(off, TB)] = v_ref[pl.ds(off, TB), :].T.astype(BF16)
            return carry
        lax.fori_loop(0, n_cast, cast, 0)

    qs = [jnp.concatenate(_split_maps(q_ref[c * TB:(c + 1) * TB, :]), axis=0) for c in range(2)]
    m_s[...] = jnp.full(m_s.shape, MASK_VALUE, F32)
    a_s[...] = jnp.zeros(a_s.shape, F32)

    def scores(c, off, width):
        return _mm(kb[pl.ds(off, width), :], qs[c], _NT)

    def update(c, s, off, width):
        m_prev = m_s[c]
        m_new = jnp.maximum(m_prev, jnp.max(s, axis=0, keepdims=True))
        alpha = jnp.exp(m_prev - m_new)
        p = jnp.exp(s - m_new).astype(BF16)
        a_s[c] = alpha * a_s[c] + _mm(vt[:, pl.ds(off, width)], p)
        m_s[c] = m_new

    def far_both(off, width):
        s = [scores(c, off, width) for c in range(2)]
        for c in range(2):
            update(c, s[c], off, width)

    n_common = jnp.maximum(2 * g - 1, 0)
    big = TB * FAR_TILES
    n_big = n_common // FAR_TILES
    n_small = n_common - n_big * FAR_TILES

    def far_big(j, carry):
        far_both(pl.multiple_of(j * big, big), big)
        return carry
    lax.fori_loop(0, n_big, far_big, 0)

    def far_small(j, carry):
        far_both(pl.multiple_of((n_big * FAR_TILES + j) * TB, TB), TB)
        return carry
    lax.fori_loop(0, n_small, far_small, 0)

    off_a = pl.multiple_of(n_common * TB, TB)
    s_a = scores(0, off_a, 2 * TB) + jnp.where(g == 0, bias_ref[1], bias_ref[0])
    off_b = pl.multiple_of(2 * g * TB, TB)
    s_b = scores(1, off_b, 2 * TB) + bias_ref[0]
    update(0, s_a, off_a, 2 * TB)

    @pl.when(g > 0)
    def _():
        update(1, scores(1, off_a, TB), off_a, TB)
    update(1, s_b, off_b, 2 * TB)

    lam = _lambda(lqk_ref, lam_init)
    for c in range(2):
        acc = a_s[c]
        on = acc[0:LANES, :] / acc[LANES:LANES + 1, :]
        o = (on[:, :TB] - lam * on[:, TB:]).T
        o_ref[c * TB:(c + 1) * TB, :] = _subln(o, subln_ref[...], lam_init).astype(BF16)


def _attn_prompt(q, k_all, v_all, l, bias, lqk, subln, lam_init, TB):
    m, d_diff = q.shape
    n_heads = d_diff // LANES
    n_blk = m // TB
    return pl.pallas_call(
        functools.partial(_attn_kernel, TB=TB, n_cast=n_blk, lam_init=lam_init),
        grid=(n_heads, n_blk // 2),
        in_specs=[
            pl.BlockSpec((2 * TB, LANES), lambda h, g: (g, h)),
            pl.BlockSpec((None, m, LANES), lambda h, g: (l, 0, h)),
            pl.BlockSpec((None, m, LANES), lambda h, g: (l, 0, h)),
            pl.BlockSpec((None, 2, 2 * TB, 2 * TB), lambda h, g: (h, 0, 0, 0)),
            pl.BlockSpec((None, 4, DIFF_HEAD), lambda h, g: (l, 0, 0)),
            pl.BlockSpec((None, 1, LANES), lambda h, g: (l, 0, 0)),
        ],
        out_specs=pl.BlockSpec((2 * TB, LANES), lambda h, g: (g, h)),
        out_shape=jax.ShapeDtypeStruct((m, d_diff), BF16),
        scratch_shapes=[pltpu.VMEM((m, LANES), BF16), pltpu.VMEM((2 * LANES, m), BF16),
                        pltpu.VMEM((2, 1, 2 * TB), F32), pltpu.VMEM((2, 2 * LANES, 2 * TB), F32)],
        compiler_params=_cparams(("arbitrary", "arbitrary")),
        name="attn_prompt",
    )(q, k_all, v_all, bias, lqk, subln)


SAMPLE_FAR_FRAMES = 384


def _attn_sample_kernel(q_ref, kn_ref, vn_ref, kc_ref, vc_ref, bias_ref, lqk_ref, subln_ref, o_ref,
                        kb, acc, *, T, n_heads, near, lam_init):
    past = kc_ref.shape[0]
    n_cols = 2 * T * n_heads
    fc = SAMPLE_FAR_FRAMES
    n_far = (past - near) // fc
    rows = fc * n_heads
    qs = jnp.concatenate(
        [m_ for h in range(n_heads) for m_ in _split_maps(q_ref[:, h * LANES:(h + 1) * LANES])], axis=0)
    sub = lax.broadcasted_iota(jnp.int32, (n_heads, n_cols), 0)
    col = lax.broadcasted_iota(jnp.int32, (n_heads, n_cols), 1)
    valid = (col >= sub * (2 * T)) & (col < (sub + 1) * (2 * T))

    def flat(x3):
        return x3.reshape(x3.shape[0] * n_heads, LANES)

    def scores(k_rows):
        s = _mm(k_rows, qs, _NT)
        return s.reshape(s.shape[0] // n_heads, n_heads, n_cols)

    near_k = jnp.concatenate([flat(kc_ref[past - near:past]), flat(kn_ref[...])], axis=0).astype(BF16)
    near_v = jnp.concatenate([flat(vc_ref[past - near:past]), flat(vn_ref[...])], axis=0).astype(BF16)
    s_near = scores(near_k) + bias_ref[...].reshape(near + T, n_heads, n_cols)

    def pass1(c, m3):
        f0 = pl.multiple_of(c * fc, fc)
        k_rows = flat(kc_ref[pl.ds(f0, fc)]).astype(BF16)
        kb[pl.ds(pl.multiple_of(c * rows, rows), rows), :] = k_rows
        return jnp.maximum(m3, jnp.max(scores(k_rows), axis=0))
    m3 = lax.fori_loop(0, n_far, pass1, jnp.max(s_near, axis=0))

    ones = jnp.ones((rows, LANES), BF16)

    def weights(s3):
        p = jnp.where(valid, jnp.exp(s3 - m3), 0.0)
        return p.reshape(p.shape[0] * n_heads, n_cols).astype(BF16)

    acc[...] = _mm(weights(s_near), jnp.concatenate([near_v, ones[:near_v.shape[0]]], axis=1), _TN)

    def pass2(c, carry):
        f0 = pl.multiple_of(c * fc, fc)
        p = weights(scores(kb[pl.ds(pl.multiple_of(c * rows, rows), rows), :]))
        v_aug = jnp.concatenate([flat(vc_ref[pl.ds(f0, fc)]).astype(BF16), ones], axis=1)
        acc[...] += _mm(p, v_aug, _TN)
        return carry
    lax.fori_loop(0, n_far, pass2, 0)

    lam = _lambda(lqk_ref, lam_init)
    for h in range(n_heads):
        a = acc[h * 2 * T:(h + 1) * 2 * T, :]
        on = a[:, 0:LANES] / a[:, LANES:2 * LANES]
        o = on[:T, :] - lam * on[T:, :]
        o_ref[:, h * LANES:(h + 1) * LANES] = _subln(o, subln_ref[...], lam_init).astype(BF16)


def _attn_sample(q, kn_all, vn_all, l, k_cache, v_cache, bias_t, lqk, subln, lam_init, n_seq, T, near):
    m, d_diff = q.shape
    n_heads = d_diff // LANES
    past = k_cache.shape[2]
    assert (past - near) % SAMPLE_FAR_FRAMES == 0
    cache_spec = pl.BlockSpec((None, None, past, n_heads, LANES), lambda b: (l, b, 0, 0, 0))
    new_spec = pl.BlockSpec((None, T, n_heads, LANES), lambda b: (l, b, 0, 0))
    return pl.pallas_call(
        functools.partial(_attn_sample_kernel, T=T, n_heads=n_heads, near=near, lam_init=lam_init),
        grid=(n_seq,),
        in_specs=[
            pl.BlockSpec((T, d_diff), lambda b: (b, 0)),
            new_spec,
            new_spec,
            cache_spec,
            cache_spec,
            pl.BlockSpec(bias_t.shape, lambda b: (0, 0)),
            pl.BlockSpec((None, 4, DIFF_HEAD), lambda b: (l, 0, 0)),
            pl.BlockSpec((None, 1, LANES), lambda b: (l, 0, 0)),
        ],
        out_specs=pl.BlockSpec((T, d_diff), lambda b: (b, 0)),
        out_shape=jax.ShapeDtypeStruct((m, d_diff), BF16),
        scratch_shapes=[pltpu.VMEM(((past - near) * n_heads, LANES), BF16),
                        pltpu.VMEM((2 * T * n_heads, 2 * LANES), F32)],
        compiler_params=_cparams(("arbitrary",)),
        name="attn_sample",
    )(q, kn_all, vn_all, k_cache, v_cache, bias_t, lqk, subln)


def _outproj_kernel(x_ref, yr_ref, yd_ref, gt_ref, wt_ref, wb_ref, o_ref):
    mixed = _mm(yr_ref[...], wt_ref[...]) + _mm(yd_ref[...], wb_ref[...])
    o_ref[...] = x_ref[...] + gt_ref[...] * mixed


def _outproj(x, yr, yd, gt, w_out_all, l, tm):
    m, d = x.shape
    dr = yr.shape[1]
    dd = yd.shape[1]
    gt_spec = (pl.BlockSpec((1, d), lambda i: (0, 0)) if gt.shape[0] == 1
               else pl.BlockSpec((tm, d), lambda i: (i, 0)))
    return pl.pallas_call(
        _outproj_kernel,
        grid=(m // tm,),
        in_specs=[
            pl.BlockSpec((tm, d), lambda i: (i, 0)),
            pl.BlockSpec((tm, dr), lambda i: (i, 0)),
            pl.BlockSpec((tm, dd), lambda i: (i, 0)),
            gt_spec,
            pl.BlockSpec((None, dr, d), lambda i: (l, 0, 0)),
            pl.BlockSpec((None, dd, d), lambda i: (l, dr // dd, 0)),
        ],
        out_specs=pl.BlockSpec((tm, d), lambda i: (i, 0)),
        out_shape=jax.ShapeDtypeStruct((m, d), F32),
        compiler_params=_cparams(("arbitrary",)),
        name="outproj",
    )(x, yr, yd, gt, w_out_all, w_out_all)


def _shift_rows(x, prev, n):
    if n % SUBLANES == 0:
        return jnp.concatenate([prev, x[:x.shape[0] - n, :]], axis=0)
    rolled = pltpu.roll(x, n, 0)
    row = lax.broadcasted_iota(jnp.int32, x.shape, 0)
    out = rolled
    for r in range(n):
        out = jnp.where(row == r, prev[r:r + 1, :], out)
    return out


def _ffn_kernel(x_ref, sh_ref, sc_ref, gt_ref, g_ref, wg_ref, wv_ref, cw_ref, cb_ref, c0_ref, wd_ref, gf_ref,
                o_ref, co_ref, h_scr, acc_scr, carry, *, B, n_j, final_norm):
    i = pl.program_id(0)
    j = pl.program_id(1)
    tm = x_ref.shape[0]

    @pl.when(j == 0)
    def _():
        xn = _rms(x_ref[...], g_ref[...], NORM_EPS)
        h_scr[...] = (xn * (1.0 + sc_ref[...]) + sh_ref[...]).astype(BF16)
        acc_scr[...] = jnp.zeros(acc_scr.shape, F32)

    @pl.when(i == 0)
    def _():
        carry[j, 0:2 * B, :] = c0_ref[...]

    h = h_scr[...]
    ug = _mm(h, wg_ref[...])
    uv = _mm(h, wv_ref[...])
    prev = carry[j, 0:2 * B, :]
    s1 = _shift_rows(ug, prev[B:2 * B, :], B)
    s2 = _shift_rows(ug, prev, 2 * B)
    cw = cw_ref[...]
    z = cb_ref[...] + s2 * cw[0:1, :] + s1 * cw[1:2, :] + ug * cw[2:3, :]
    act = 0.5 * z * (1.0 + lax.erf(z * (2.0 ** -0.5)))
    acc_scr[...] += _mm((act * uv).astype(BF16), wd_ref[...])
    last2 = ug[tm - 2 * B:tm, :]
    carry[j, 0:2 * B, :] = last2
    tf = last2.shape[1]
    co_ref[:, pl.ds(pl.multiple_of(j * tf, tf), tf)] = last2

    @pl.when(j == n_j - 1)
    def _():
        x2 = x_ref[...] + gt_ref[...] * acc_scr[...]
        if final_norm:
            x2 = _rms(x2, gf_ref[...], NORM_EPS)
        o_ref[...] = x2


def _ffn(x, sh, sc, gt, g_all, w_up_all, conv_w_all, conv_b_all, conv0, w_down_all, g_final, l, B, tm,
         final_norm, tf=512):
    m, d = x.shape
    d_ff = w_down_all.shape[1]
    n_j = d_ff // tf
    rows_c = max(SUBLANES, 2 * B)
    return pl.pallas_call(
        functools.partial(_ffn_kernel, B=B, n_j=n_j, final_norm=final_norm),
        grid=(m // tm, n_j),
        in_specs=[
            pl.BlockSpec((tm, d), lambda i, j: (i, 0)),
            _mod_spec(sh, tm, d),
            _mod_spec(sc, tm, d),
            _mod_spec(gt, tm, d),
            pl.BlockSpec((None, 1, d), lambda i, j: (l, 0, 0)),
            pl.BlockSpec((None, d, tf), lambda i, j: (l, 0, j)),
            pl.BlockSpec((None, d, tf), lambda i, j: (l, 0, n_j + j)),
            pl.BlockSpec((None, CONV_W, tf), lambda i, j: (l, 0, j)),
            pl.BlockSpec((None, 1, tf), lambda i, j: (l, 0, j)),
            pl.BlockSpec((2 * B, tf), lambda i, j: (0, j)),
            pl.BlockSpec((None, tf, d), lambda i, j: (l, j, 0)),
            pl.BlockSpec((1, d), lambda i, j: (0, 0)),
        ],
        out_specs=[
            pl.BlockSpec((tm, d), lambda i, j: (i, 0)),
            pl.BlockSpec((2 * B, d_ff), lambda i, j: (0, 0)),
        ],
        out_shape=[
            jax.ShapeDtypeStruct((m, d), F32),
            jax.ShapeDtypeStruct((2 * B, d_ff), F32),
        ],
        scratch_shapes=[pltpu.VMEM((tm, d), BF16), pltpu.VMEM((tm, d), F32),
                        pltpu.VMEM((n_j, rows_c, tf), F32)],
        compiler_params=_cparams(("arbitrary", "arbitrary")),
        name="conv_ffn",
    )(x, sh, sc, gt, g_all, w_up_all, w_up_all, conv_w_all, conv_b_all, conv0, w_down_all, g_final)


def _pad_cols(a, n):
    return jnp.pad(a, [(0, 0)] * (a.ndim - 1) + [(0, n)])


def _regroup_rwkv_cols(a, d_rwkv):
    o = 3 * d_rwkv
    return jnp.concatenate([
        a[..., :o],
        _pad_cols(a[..., o:o + DECAY_LORA], LORA_PAD - DECAY_LORA),
        _pad_cols(a[..., o + DECAY_LORA:o + DECAY_LORA + AAA_LORA], LORA_PAD - AAA_LORA),
        a[..., o + DECAY_LORA + AAA_LORA:],
    ], axis=-1)


def _ungroup_rwkv_cols(a, d_rwkv):
    o = 3 * d_rwkv
    return jnp.concatenate([
        a[..., :o],
        a[..., o:o + DECAY_LORA],
        a[..., o + LORA_PAD:o + LORA_PAD + AAA_LORA],
        a[..., o + 2 * LORA_PAD:],
    ], axis=-1)


def kernel(x_prompt, x_sample, c_prompt, c_sample, cache_k, cache_v, state_wkv, state_shift, state_conv, w_ada, b_ada, g_mix, g_ffn, w_in, w_out, rwkv_mu, rwkv_w0, rwkv_w2, rwkv_a0, rwkv_a2, rwkv_g2, rwkv_kk, rwkv_ka, rwkv_rk, rwkv_ln_w, rwkv_ln_b, diff_lq1, diff_lk1, diff_lq2, diff_lk2, diff_subln, rel_table, ffn_up, ffn_conv_w, ffn_conv_b, ffn_down, g_final):
    depth, d_model, _ = w_in.shape
    bp, seq, _ = x_prompt.shape
    bs, dseq, _ = x_sample.shape
    past = cache_k.shape[2]
    d_rwkv = rwkv_w0.shape[1]
    d_diff = d_model - d_rwkv
    n_dheads = d_diff // (2 * DIFF_HEAD)
    n_rheads = d_rwkv // RWKV_HEAD
    n_rwkv_cols = rwkv_mu.shape[1]
    n_rwkv_pad = n_rwkv_cols + 2 * LORA_PAD - DECAY_LORA - AAA_LORA
    d_ff = ffn_down.shape[1]
    assert bp == 1, "prompt path handles one sequence"
    TB = min(256, seq // 2)
    assert seq % (2 * TB) == 0 and TB % CHUNK == 0 and TB >= MAX_DISTANCE
    tm_p = min(512, seq)
    tm_in = min(1024, seq)
    m_s = bs * dseq

    w_in_b = jnp.concatenate([_regroup_rwkv_cols(w_in[..., :n_rwkv_cols], d_rwkv), w_in[..., n_rwkv_cols:]],
                             axis=-1).astype(BF16)
    mu_p = _regroup_rwkv_cols(rwkv_mu, d_rwkv)[:, None]
    w2_b = jnp.pad(rwkv_w2, ((0, 0), (0, LORA_PAD - DECAY_LORA), (0, 0))).astype(BF16)
    a2_b = jnp.pad(rwkv_a2, ((0, 0), (0, LORA_PAD - AAA_LORA), (0, 0))).astype(BF16)
    g2_b = rwkv_g2.astype(BF16)
    w_out_b = _cast_bf16(w_out)
    up_b = _cast_bf16(ffn_up)
    down_b = _cast_bf16(ffn_down)
    vec = lambda a: a.reshape(depth, 1, -1)
    wkv_args = (w2_b, a2_b, g2_b, vec(rwkv_w0), vec(rwkv_a0), vec(rwkv_kk), vec(rwkv_ka), vec(rwkv_rk),
                vec(rwkv_ln_w), vec(rwkv_ln_b))
    g_mix_v, g_ffn_v, subln_v, conv_b_v = vec(g_mix), vec(g_ffn), vec(diff_subln), vec(ffn_conv_b)
    gf = g_final[None]
    lqk = jnp.stack([diff_lq1, diff_lk1, diff_lq2, diff_lk2], axis=1)

    n_c = bp + bs
    n_c_pad = -(-n_c // SUBLANES) * SUBLANES
    c_rows = jnp.pad(jnp.concatenate([c_prompt, c_sample], axis=0), ((0, n_c_pad - n_c), (0, 0)))
    mod = _adaln_mod(c_rows, w_ada, b_ada).reshape(depth, n_c_pad, 6, d_model)

    ql = jnp.arange(TB)
    rel_diag = ql[None, :] - ql[:, None]
    allowed = (ql[None, :] // CHUNK) <= (ql[:, None] // CHUNK)
    bucket_diag = jnp.where(allowed, _t5_bucket(rel_diag), -1)
    bucket_prev = _t5_bucket(rel_diag - TB)
    windows = [jnp.concatenate([bucket_prev, bucket_diag], axis=1),
               jnp.concatenate([bucket_diag, jnp.full((TB, TB), -1)], axis=1)]
    bucket_p = jnp.concatenate([jnp.tile(w.T, (1, 2)) for w in windows], axis=0)
    bias_p = _bias_lookup(rel_table, bucket_p.astype(jnp.int32), FAR_BUCKET).reshape(n_dheads, 2, 2 * TB, 2 * TB)
    q_pos = past + jnp.arange(dseq)
    assert past >= MAX_DISTANCE
    k_pos = past - MAX_DISTANCE + jnp.arange(MAX_DISTANCE + dseq)
    bias_s = _bias_lookup(rel_table, _t5_bucket(k_pos[None, :] - q_pos[:, None]).astype(jnp.int32), FAR_BUCKET)
    bias_t = jnp.broadcast_to(bias_s.transpose(2, 0, 1)[:, None, :, None, :],
                              (MAX_DISTANCE + dseq, n_dheads, n_dheads, 2, dseq))
    bias_t = bias_t.reshape((MAX_DISTANCE + dseq) * n_dheads, n_dheads * 2 * dseq)

    xp = x_prompt.reshape(seq, d_model)
    xs = x_sample.reshape(m_s, d_model)
    zero_shift = jnp.zeros((1, 1, n_rwkv_pad), F32)
    zero_state = jnp.zeros((1, n_rheads, RWKV_HEAD, RWKV_HEAD), F32)
    zero_conv = jnp.zeros((2, d_ff), F32)
    shift_s_in = _regroup_rwkv_cols(state_shift, d_rwkv)
    outs_p = {k: [] for k in ("wkv", "shift", "conv")}
    outs_s = {k: [] for k in ("wkv", "shift", "conv")}
    kv_p = (jnp.zeros((depth, seq, d_diff), F32), jnp.zeros((depth, seq, d_diff), F32))
    kv_s = (jnp.zeros((depth, m_s, d_diff), F32), jnp.zeros((depth, m_s, d_diff), F32))

    def time_major(a):
        return a.reshape(bs, dseq, -1).swapaxes(0, 1).reshape(m_s, -1)

    def batch_major(a):
        return a.reshape(dseq, bs, -1).swapaxes(0, 1).reshape(m_s, -1)

    for l in range(depth):
        lam_init = 0.8 - 0.6 * math.exp(-0.3 * l)
        last = l == depth - 1

        mp = mod[l, 0:bp]
        sh1, sc1, gt1, sh2, sc2, gt2 = (mp[:, t] for t in range(6))
        p, q, *kv_p = _inproj(xp, sh1, sc1, g_mix_v, w_in_b, l, kv_p, n_rwkv_pad, d_diff, tm_in)
        y_r, wkv = _wkv(p, l, mu_p, zero_shift, *wkv_args, zero_state, 1, seq, min(64, seq))
        y_d = _attn_prompt(q, kv_p[0], kv_p[1], l, bias_p, lqk, subln_v, lam_init, TB)
        x1 = _outproj(xp, y_r, y_d, gt1, w_out_b, l, tm_p)
        xp, conv = _ffn(x1, sh2, sc2, gt2, g_ffn_v, up_b, ffn_conv_w, conv_b_v, zero_conv, down_b, gf, l,
                        1, tm_p, last)
        outs_p["wkv"].append(wkv)
        outs_p["shift"].append(_ungroup_rwkv_cols(p[seq - 1:seq], d_rwkv).reshape(bp, 1, n_rwkv_cols))
        outs_p["conv"].append(conv.reshape(bp, CONV_W - 1, d_ff))

        ms = mod[l, bp:bp + bs]
        rows_bm = jnp.repeat(ms, dseq, axis=0)
        rows_tm = jnp.tile(ms, (dseq, 1, 1))
        p, q, *kv_s = _inproj(xs, rows_bm[:, 0], rows_bm[:, 1], g_mix_v, w_in_b, l, kv_s, n_rwkv_pad, d_diff, m_s)
        y_r, wkv = _wkv(p, l, mu_p, shift_s_in[l], *wkv_args, state_wkv[l], bs, dseq, dseq)
        kn4, vn4 = (a.reshape(depth, m_s, n_dheads, 2 * DIFF_HEAD) for a in kv_s)
        y_d = _attn_sample(q, kn4, vn4, l, cache_k, cache_v, bias_t, lqk, subln_v, lam_init, bs, dseq,
                           MAX_DISTANCE)
        x1 = _outproj(xs, y_r, y_d, rows_bm[:, 2], w_out_b, l, m_s)
        conv0 = state_conv[l].swapaxes(0, 1).reshape((CONV_W - 1) * bs, d_ff)
        x2, conv = _ffn(time_major(x1), rows_tm[:, 3], rows_tm[:, 4], rows_tm[:, 5], g_ffn_v, up_b, ffn_conv_w,
                        conv_b_v, conv0, down_b, gf, l, bs, m_s, last)
        xs = batch_major(x2)
        outs_s["wkv"].append(wkv)
        p_last = p.reshape(bs, dseq, n_rwkv_pad)[:, dseq - 1:dseq]
        outs_s["shift"].append(_ungroup_rwkv_cols(p_last, d_rwkv))
        outs_s["conv"].append(conv.reshape(CONV_W - 1, bs, d_ff).swapaxes(0, 1))

    st = lambda xs_: jnp.stack(xs_)
    head_shape = (n_dheads, 2 * DIFF_HEAD)
    return (xp.reshape(bp, seq, d_model), xs.reshape(bs, dseq, d_model),
            kv_p[0].reshape(depth, bp, seq, *head_shape), kv_p[1].reshape(depth, bp, seq, *head_shape),
            st(outs_p["wkv"]), st(outs_p["shift"]), st(outs_p["conv"]),
            kv_s[0].reshape(depth, bs, dseq, *head_shape), kv_s[1].reshape(depth, bs, dseq, *head_shape),
            st(outs_s["wkv"]), st(outs_s["shift"]), st(outs_s["conv"]))
```

```python
import functools
import math

import jax
import jax.numpy as jnp
from jax import lax
from jax.experimental import pallas as pl
from jax.experimental.pallas import tpu as pltpu

F32 = jnp.float32
BF16 = jnp.bfloat16

CHUNK = 64
RWKV_HEAD = 64
DIFF_HEAD = 64
N_BUCKETS = 32
MAX_DISTANCE = 128
NORM_EPS = 1e-6
GN_EPS = 64e-5
SUBLN_EPS = 1e-5
DECAY_LORA = 96
AAA_LORA = 96
GATE_LORA = 256
LORA_PAD = 128
CONV_W = 3

LANES = 128
SUBLANES = 8
VMEM_LIMIT_BYTES = 56 * 1024 * 1024
MASK_VALUE = -1e30
WKV_PAIRS_PER_STEP = 8

_NN = (((1,), (0,)), ((), ()))
_NT = (((1,), (1,)), ((), ()))
_TN = (((0,), (0,)), ((), ()))


def _mm(a, b, dims=_NN):
    return lax.dot_general(a, b, dims, preferred_element_type=F32)


def _hi_lo(x):
    h = x.astype(BF16)
    return h, (x - h.astype(F32)).astype(BF16)


def _mm3(a, b, dims=_NN):
    ah, al = _hi_lo(a)
    bh, bl = _hi_lo(b)
    (ca,), (cb,) = dims[0]
    k = a.shape[ca]
    if (ca == 0 or k % LANES == 0) and (cb == 0 or k % LANES == 0):
        return _mm(jnp.concatenate([ah, al, ah], axis=ca), jnp.concatenate([bh, bh, bl], axis=cb), dims)
    return _mm(ah, bh, dims) + (_mm(al, bh, dims) + _mm(ah, bl, dims))


def _sel_rows(sel2, x):
    xh, xl = _hi_lo(x)
    return _mm(sel2, jnp.concatenate([xh, xl], axis=0))


def _sel_lanes(x, sel2):
    xh, xl = _hi_lo(x)
    return _mm(jnp.concatenate([xh, xl], axis=1), sel2)


def _cparams(sem):
    return pltpu.CompilerParams(dimension_semantics=sem, vmem_limit_bytes=VMEM_LIMIT_BYTES)


def _rms(x, g, eps):
    return x * lax.rsqrt(jnp.mean(x * x, axis=-1, keepdims=True) + eps) * g


def _cast_kernel(x_ref, o_ref):
    o_ref[...] = x_ref[...].astype(BF16)


def _cast_bf16(w, tr=512, tc=1024):
    depth, rows, cols = w.shape
    spec = pl.BlockSpec((None, tr, tc), lambda l, i, j: (l, i, j))
    return pl.pallas_call(
        _cast_kernel,
        grid=(depth, rows // tr, cols // tc),
        in_specs=[spec],
        out_specs=spec,
        out_shape=jax.ShapeDtypeStruct(w.shape, BF16),
        compiler_params=_cparams(("arbitrary", "arbitrary", "arbitrary")),
        name="cast_bf16",
    )(w)


def _mod_kernel(c_ref, w_ref, b_ref, o_ref):
    c = c_ref[...]
    s = (c * jax.nn.sigmoid(c)).astype(BF16)
    o_ref[...] = _mm(s, w_ref[...].astype(BF16)) + b_ref[...]


def _adaln_mod(c_rows, w_ada, b_ada, tn=1024):
    depth, d, n = w_ada.shape
    rows = c_rows.shape[0]
    return pl.pallas_call(
        _mod_kernel,
        grid=(depth, n // tn),
        in_specs=[
            pl.BlockSpec((rows, d), lambda l, j: (0, 0)),
            pl.BlockSpec((None, d, tn), lambda l, j: (l, 0, j)),
            pl.BlockSpec((None, 1, tn), lambda l, j: (l, 0, j)),
        ],
        out_specs=pl.BlockSpec((None, rows, tn), lambda l, j: (l, 0, j)),
        out_shape=jax.ShapeDtypeStruct((depth, rows, n), F32),
        compiler_params=_cparams(("arbitrary", "arbitrary")),
        name="adaln_mod",
    )(c_rows, w_ada, b_ada.reshape(depth, 1, n))


def _bias_kernel(tab_ref, bucket_ref, o_ref, *, far_bucket):
    h = pl.program_id(0)
    bucket = bucket_ref[...]
    far = tab_ref[far_bucket, h]
    acc = jnp.full(bucket.shape, MASK_VALUE, F32)
    for b in range(N_BUCKETS):
        acc = jnp.where(bucket == b, tab_ref[b, h] - far, acc)
    o_ref[...] = acc


def _bias_lookup(table, bucket, far_bucket):
    n_heads = table.shape[1]
    r, c = bucket.shape
    return pl.pallas_call(
        functools.partial(_bias_kernel, far_bucket=far_bucket),
        grid=(n_heads,),
        in_specs=[
            pl.BlockSpec(memory_space=pltpu.SMEM),
            pl.BlockSpec((r, c), lambda h: (0, 0)),
        ],
        out_specs=pl.BlockSpec((None, r, c), lambda h: (h, 0, 0)),
        out_shape=jax.ShapeDtypeStruct((n_heads, r, c), F32),
        compiler_params=_cparams(("arbitrary",)),
        name="bias_lookup",
    )(table, bucket)


def _t5_bucket(rel):
    nb = N_BUCKETS // 2
    max_exact = nb // 2
    n = jnp.abs(rel)
    nf = jnp.maximum(n, 1).astype(F32)
    large = max_exact + (jnp.log(nf / max_exact) / math.log(MAX_DISTANCE / max_exact) * (nb - max_exact)).astype(jnp.int32)
    large = jnp.minimum(large, nb - 1)
    return jnp.where(rel > 0, nb, 0) + jnp.where(n < max_exact, n, large)


FAR_BUCKET = N_BUCKETS // 2 - 1


def _inproj_kernel(x_ref, sh_ref, sc_ref, g_ref, w_ref, *rest, n_p, n_q):
    p_ref, q_ref, k_ref, v_ref, h_scr = rest[-5:]
    j = pl.program_id(1)

    @pl.when(j == 0)
    def _():
        xn = _rms(x_ref[...], g_ref[...], NORM_EPS)
        h_scr[...] = (xn * (1.0 + sc_ref[...]) + sh_ref[...]).astype(BF16)

    @pl.when(j < n_p)
    def _():
        p_ref[...] = _mm(h_scr[...], w_ref[...])

    @pl.when((j >= n_p) & (j < n_p + n_q))
    def _():
        q_ref[...] = (_mm(h_scr[...], w_ref[...]) * (DIFF_HEAD ** -0.5)).astype(BF16)

    @pl.when((j >= n_p + n_q) & (j < n_p + 2 * n_q))
    def _():
        k_ref[...] = _mm(h_scr[...], w_ref[...])

    @pl.when(j >= n_p + 2 * n_q)
    def _():
        v_ref[...] = _mm(h_scr[...], w_ref[...])


def _mod_spec(mod, tm, d):
    if mod.shape[0] == 1:
        return pl.BlockSpec((1, d), lambda i, j: (0, 0))
    return pl.BlockSpec((tm, d), lambda i, j: (i, 0))


def _inproj(x, sh, sc, g_all, w_all, l, kv_all, n_rwkv_pad, d_diff, tm, tn=512):
    m, d = x.shape
    depth, _, n = w_all.shape
    n_p, n_q = n_rwkv_pad // tn, d_diff // tn
    clip = lambda j, lo: jnp.clip(j - lo, 0, n_q - 1)
    in_specs = [
        pl.BlockSpec((tm, d), lambda i, j: (i, 0)),
        _mod_spec(sh, tm, d),
        _mod_spec(sc, tm, d),
        pl.BlockSpec((None, 1, d), lambda i, j: (l, 0, 0)),
        pl.BlockSpec((None, d, tn), lambda i, j: (l, 0, j)),
        pl.BlockSpec(memory_space=pl.ANY),
        pl.BlockSpec(memory_space=pl.ANY),
    ]
    args = [x, sh, sc, g_all, w_all, kv_all[0], kv_all[1]]
    aliases = {5: 2, 6: 3}
    return pl.pallas_call(
        functools.partial(_inproj_kernel, n_p=n_p, n_q=n_q),
        grid=(m // tm, n // tn),
        in_specs=in_specs,
        out_specs=[
            pl.BlockSpec((tm, tn), lambda i, j: (i, jnp.minimum(j, n_p - 1))),
            pl.BlockSpec((tm, tn), lambda i, j: (i, clip(j, n_p))),
            pl.BlockSpec((None, tm, tn), lambda i, j: (l, i, clip(j, n_p + n_q))),
            pl.BlockSpec((None, tm, tn), lambda i, j: (l, i, clip(j, n_p + 2 * n_q))),
        ],
        out_shape=[
            jax.ShapeDtypeStruct((m, n_rwkv_pad), F32),
            jax.ShapeDtypeStruct((m, d_diff), BF16),
            jax.ShapeDtypeStruct((depth, m, d_diff), F32),
            jax.ShapeDtypeStruct((depth, m, d_diff), F32),
        ],
        scratch_shapes=[pltpu.VMEM((tm, d), BF16)],
        input_output_aliases=aliases,
        compiler_params=_cparams(("arbitrary", "arbitrary")),
        name="inproj",
    )(*args)


def _wkv_kernel(pr_ref, pk_ref, pv_ref, pw_ref, pa_ref, pg_ref,
                mur_ref, muk_ref, muv_ref, muw_ref, mua_ref, mug_ref,
                s0r_ref, s0k_ref, s0v_ref, s0w_ref, s0a_ref, s0g_ref,
                w2_ref, a2_ref, g2_ref,
                w0_ref, a0_ref, kkp_ref, ka_ref, rk_ref, lnw_ref, lnb_ref,
                st0_ref,
                y_ref, st_ref,
                sbd, cr, ck, cv, cw, ca, cg, *, C, n_chunks, n_pairs):
    chunk = pl.program_id(2)
    hd = RWKV_HEAD
    zero_blk = jnp.zeros((hd, hd), F32)

    @pl.when(chunk == 0)
    def _():
        for c_ref, s_ref in ((cr, s0r_ref), (ck, s0k_ref), (cv, s0v_ref), (cw, s0w_ref), (ca, s0a_ref), (cg, s0g_ref)):
            c_ref[0:1, :] = s_ref[...]
        for pi in range(n_pairs):
            top = jnp.concatenate([st0_ref[2 * pi], zero_blk], axis=1)
            bot = jnp.concatenate([zero_blk, st0_ref[2 * pi + 1]], axis=1)
            sbd[pi] = jnp.concatenate([top, bot], axis=0)

    def tshift(x_ref, mu_ref, c_ref):
        x = x_ref[...]
        row = lax.broadcasted_iota(jnp.int32, x.shape, 0)
        prev = jnp.where(row == 0, c_ref[0:1, :], pltpu.roll(x, 1, 0))
        c_ref[0:1, :] = x[C - 1:C, :]
        return x + (prev - x) * mu_ref[...]

    xr = tshift(pr_ref, mur_ref, cr)
    xk = tshift(pk_ref, muk_ref, ck)
    xv = tshift(pv_ref, muv_ref, cv)
    xw = tshift(pw_ref, muw_ref, cw)
    xa = tshift(pa_ref, mua_ref, ca)
    xg = tshift(pg_ref, mug_ref, cg)

    lane = lax.broadcasted_iota(jnp.int32, (1, LANES), 1)
    head0 = lane < hd
    li = lax.broadcasted_iota(jnp.int32, (LANES, LANES), 0)
    lj = lax.broadcasted_iota(jnp.int32, (LANES, LANES), 1)
    same_head = (li < hd) == (lj < hd)
    li2 = lax.broadcasted_iota(jnp.int32, (2 * LANES, LANES), 0)
    lj2 = lax.broadcasted_iota(jnp.int32, (2 * LANES, LANES), 1)
    li2 = jnp.where(li2 < LANES, li2, li2 - LANES)
    seg2 = jnp.where((li2 < hd) == (lj2 < hd), 1.0, 0.0).astype(BF16)

    def segsum(x):
        return _sel_lanes(x, seg2)

    wlin = w0_ref[...] + _mm(jnp.tanh(xw).astype(BF16), w2_ref[...])
    z = -wlin
    w_log = -(jnp.maximum(z, 0.0) + jnp.log1p(jnp.exp(-jnp.abs(z)))) - 0.5
    ne = -jnp.exp(w_log)
    a = jax.nn.sigmoid(a0_ref[...] + _mm(xa.astype(BF16), a2_ref[...]))
    g = _mm(jax.nn.sigmoid(xg).astype(BF16), g2_ref[...])
    kk_raw = xk * kkp_ref[...]
    k2 = xk * (1.0 + (a - 1.0) * ka_ref[...])
    rk_prod = xr * k2 * rk_ref[...]

    ti = lax.broadcasted_iota(jnp.int32, (C, 2 * C), 0)
    tj = lax.broadcasted_iota(jnp.int32, (C, 2 * C), 1)
    tj = jnp.where(tj < C, tj, tj - C)
    tri2 = jnp.where(tj <= ti, 1.0, 0.0).astype(BF16)
    cum = _sel_rows(tri2, ne)
    gam = jnp.exp(cum)
    gprev = jnp.exp(cum - ne)
    ginv = jnp.exp(-cum)
    dte = jnp.exp(cum[C - 1:C, :] - cum)
    rg_all = xr * gam
    ks_all = k2 * ginv
    kd_all = k2 * dte
    lnw = lnw_ref[...]
    lnb = lnb_ref[...]

    def stack_masked(x):
        return jnp.concatenate([jnp.where(head0, x, 0.0), jnp.where(head0, 0.0, x)], axis=0)

    def stack_twice(x):
        return jnp.concatenate([x, x], axis=0)

    R = 2 * C
    ri = lax.broadcasted_iota(jnp.int32, (R, R), 0)
    ci = lax.broadcasted_iota(jnp.int32, (R, R), 1)
    same_blk = (ri < C) == (ci < C)
    tr = jnp.where(ri < C, ri, ri - C)
    tc = jnp.where(ci < C, ci, ci - C)
    strict = same_blk & (tc < tr)
    incl = same_blk & (tc <= tr)
    n_steps = max(1, int(math.log2(C)))
    inv_hd = 1.0 / hd

    pairs = range(n_pairs)
    sls = [slice(pi * LANES, (pi + 1) * LANES) for pi in pairs]
    xv_p = [xv[:, sl] for sl in sls]
    sums = [segsum(jnp.concatenate([kk_raw[:, sl] * kk_raw[:, sl], rk_prod[:, sl]], axis=0)) for sl in sls]
    kk = [kk_raw[:, sl] * lax.rsqrt(jnp.maximum(sm[:C, :], 1e-24)) for sl, sm in zip(sls, sums)]
    bonus = [sm[C:, :] * v for sm, v in zip(sums, xv_p)]
    kka = [k * a[:, sl] for k, sl in zip(kk, sls)]
    la = [stack_masked(-k * gprev[:, sl]) for k, sl in zip(kk, sls)]
    lr = [stack_masked(rg_all[:, sl]).astype(BF16) for sl in sls]
    rhs = [jnp.concatenate([stack_twice(ka_ * ginv[:, sl]), stack_twice(ks_all[:, sl])], axis=0)
           for ka_, sl in zip(kka, sls)]
    vst = [stack_masked(v) for v in xv_p]
    s_old = [sbd[pi] for pi in pairs]
    sc_a = [_mm3(l_, r_, _NT) for l_, r_ in zip(la, rhs)]
    sc_r = [_mm(l_, r_.astype(BF16), _NT) for l_, r_ in zip(lr, rhs)]
    ls_a = [_mm3(l_, s_, _NT) for l_, s_ in zip(la, s_old)]
    ls_r = [_mm(l_, s_.astype(BF16), _NT) for l_, s_ in zip(lr, s_old)]
    n_ab = [jnp.where(strict, s4[:, :R], 0.0) for s4 in sc_a]
    m_ak = [jnp.where(strict, s4[:, R:], 0.0) for s4 in sc_a]
    m_r = [jnp.concatenate([jnp.where(incl, s4[:, :R], 0.0), jnp.where(incl, s4[:, R:], 0.0)],
                           axis=1).astype(BF16) for s4 in sc_r]
    x_u = [l_ + _mm3(mk, v) for l_, mk, v in zip(ls_a, m_ak, vst)]
    pw = n_ab
    for step in range(n_steps):
        if step + 1 < n_steps:
            t = [_mm3(p_, jnp.concatenate([x_, p_], axis=1)) for p_, x_ in zip(pw, x_u)]
            x_u = [x_ + t_[:, :LANES] for x_, t_ in zip(x_u, t)]
            pw = [t_[:, LANES:] for t_ in t]
        else:
            x_u = [x_ + _mm3(p_, x_) for p_, x_ in zip(pw, x_u)]
    y_st = [l_ + _mm(mr, jnp.concatenate([x_, v], axis=0).astype(BF16))
            for l_, mr, x_, v in zip(ls_r, m_r, x_u, vst)]
    upd = [_mm3(jnp.concatenate([x_[:C, :] + x_[C:, :], v], axis=0),
                jnp.concatenate([ka_ * dte[:, sl], kd_all[:, sl]], axis=0), _TN)
           for x_, v, ka_, sl in zip(x_u, xv_p, kka, sls)]
    for pi in pairs:
        sbd[pi] = s_old[pi] * gam[C - 1:C, sls[pi]] + jnp.where(same_head, upd[pi], 0.0)

    y = [ys[:C, :] + ys[C:, :] for ys in y_st]
    yc = [y_ - segsum(y_) * inv_hd for y_ in y]
    y_var = [segsum(c_ * c_) * inv_hd for c_ in yc]
    for pi in pairs:
        sl = sls[pi]
        yn = yc[pi] * lax.rsqrt(y_var[pi] + GN_EPS) * lnw[:, sl] + lnb[:, sl]
        y_ref[:, sl] = ((yn + bonus[pi]) * g[:, sl]).astype(BF16)

    @pl.when(chunk == n_chunks - 1)
    def _():
        for pi in range(n_pairs):
            s_new = sbd[pi]
            st_ref[2 * pi] = s_new[0:hd, 0:hd]
            st_ref[2 * pi + 1] = s_new[hd:2 * hd, hd:2 * hd]


def _wkv(p, l, mu, shift0, w2, a2, g2, w0, a0, kkp, ka, rk, lnw, lnb, state0, n_seq, seq_len, C):
    m, n_pad = p.shape
    d_rwkv = w0.shape[-1]
    n_heads = d_rwkv // RWKV_HEAD
    n_pairs = WKV_PAIRS_PER_STEP
    wd = n_pairs * LANES
    n_groups = d_rwkv // wd
    n_chunks = seq_len // C
    nb = n_groups
    blk_w = 3 * d_rwkv // LANES
    blk_a = blk_w + 1
    blk_g = (blk_a + 1) * LANES // GATE_LORA

    row = lambda s, h, c: s * n_chunks + c
    seg_specs = [
        pl.BlockSpec((C, wd), lambda s, h, c: (row(s, h, c), h)),
        pl.BlockSpec((C, wd), lambda s, h, c: (row(s, h, c), nb + h)),
        pl.BlockSpec((C, wd), lambda s, h, c: (row(s, h, c), 2 * nb + h)),
        pl.BlockSpec((C, LANES), lambda s, h, c: (row(s, h, c), blk_w)),
        pl.BlockSpec((C, LANES), lambda s, h, c: (row(s, h, c), blk_a)),
        pl.BlockSpec((C, GATE_LORA), lambda s, h, c: (row(s, h, c), blk_g)),
    ]
    mu_specs = [
        pl.BlockSpec((None, 1, wd), lambda s, h, c: (l, 0, h)),
        pl.BlockSpec((None, 1, wd), lambda s, h, c: (l, 0, nb + h)),
        pl.BlockSpec((None, 1, wd), lambda s, h, c: (l, 0, 2 * nb + h)),
        pl.BlockSpec((None, 1, LANES), lambda s, h, c: (l, 0, blk_w)),
        pl.BlockSpec((None, 1, LANES), lambda s, h, c: (l, 0, blk_a)),
        pl.BlockSpec((None, 1, GATE_LORA), lambda s, h, c: (l, 0, blk_g)),
    ]
    s0_specs = [
        pl.BlockSpec((None, 1, wd), lambda s, h, c: (s, 0, h)),
        pl.BlockSpec((None, 1, wd), lambda s, h, c: (s, 0, nb + h)),
        pl.BlockSpec((None, 1, wd), lambda s, h, c: (s, 0, 2 * nb + h)),
        pl.BlockSpec((None, 1, LANES), lambda s, h, c: (s, 0, blk_w)),
        pl.BlockSpec((None, 1, LANES), lambda s, h, c: (s, 0, blk_a)),
        pl.BlockSpec((None, 1, GATE_LORA), lambda s, h, c: (s, 0, blk_g)),
    ]
    lora_specs = [
        pl.BlockSpec((None, LORA_PAD, wd), lambda s, h, c: (l, 0, h)),
        pl.BlockSpec((None, LORA_PAD, wd), lambda s, h, c: (l, 0, h)),
        pl.BlockSpec((None, GATE_LORA, wd), lambda s, h, c: (l, 0, h)),
    ]
    vec_spec = pl.BlockSpec((None, 1, wd), lambda s, h, c: (l, 0, h))
    state_spec = pl.BlockSpec((None, 2 * n_pairs, RWKV_HEAD, RWKV_HEAD), lambda s, h, c: (s, h, 0, 0))
    y, st = pl.pallas_call(
        functools.partial(_wkv_kernel, C=C, n_chunks=n_chunks, n_pairs=n_pairs),
        grid=(n_seq, n_groups, n_chunks),
        in_specs=seg_specs + mu_specs + s0_specs + lora_specs + [vec_spec] * 7 + [state_spec],
        out_specs=[
            pl.BlockSpec((C, wd), lambda s, h, c: (row(s, h, c), h)),
            state_spec,
        ],
        out_shape=[
            jax.ShapeDtypeStruct((m, d_rwkv), BF16),
            jax.ShapeDtypeStruct((n_seq, n_heads, RWKV_HEAD, RWKV_HEAD), F32),
        ],
        scratch_shapes=[pltpu.VMEM((n_pairs, LANES, LANES), F32)]
        + [pltpu.VMEM((SUBLANES, wd), F32)] * 3 + [pltpu.VMEM((SUBLANES, LANES), F32)] * 2
        + [pltpu.VMEM((SUBLANES, GATE_LORA), F32)],
        compiler_params=_cparams(("arbitrary", "arbitrary", "arbitrary")),
        name="wkv",
    )(p, p, p, p, p, p, mu, mu, mu, mu, mu, mu,
      shift0, shift0, shift0, shift0, shift0, shift0,
      w2, a2, g2, w0, a0, kkp, ka, rk, lnw, lnb, state0)
    return y, st


def _lambda(lqk_ref, lam_init):
    t = lqk_ref[...]
    s1 = jnp.sum(t[0:1, :] * t[1:2, :], axis=-1, keepdims=True)
    s2 = jnp.sum(t[2:3, :] * t[3:4, :], axis=-1, keepdims=True)
    return jnp.exp(s1) - jnp.exp(s2) + lam_init


def _split_maps(q):
    lane = lax.broadcasted_iota(jnp.int32, q.shape, 1)
    zero = jnp.zeros_like(q)
    return jnp.where(lane < DIFF_HEAD, q, zero), jnp.where(lane < DIFF_HEAD, zero, q)


def _subln(o, subln, lam_init):
    return _rms(o, subln, SUBLN_EPS) * (1.0 - lam_init)


FAR_STEP_TILES = (8, 4, 1)


def _attn_kernel(q_ref, k_ref, v_ref, bias_ref, lqk_ref, subln_ref, o_ref,
                 kb, vt, m_s, a_s, *, TB, n_cast, lam_init):
    g = pl.program_id(1)

    @pl.when(g == 0)
    def _():
        vt[LANES:2 * LANES, :] = jnp.ones((LANES, vt.shape[1]), BF16)

        def cast(c, carry):
            off = pl.multiple_of(c * TB, TB)
            kb[pl.ds(off, TB), :] = k_ref[pl.ds(off, TB), :].astype(BF16)
            vt[0:LANES, pl.ds(off, TB)] = v_ref[pl.ds(off, TB), :].T.astype(BF16)
            return carry
        lax.fori_loop(0, n_cast, cast, 0)

    qs = [jnp.concatenate(_split_maps(q_ref[c * TB:(c + 1) * TB, :]), axis=0) for c in range(2)]
    m_s[...] = jnp.full(m_s.shape, MASK_VALUE, F32)
    a_s[...] = jnp.zeros(a_s.shape, F32)

    def scores(c, off, width):
        return _mm(kb[pl.ds(off, width), :], qs[c], _NT)

    def update(c, s, off, width):
        m_prev = m_s[c]
        m_new = jnp.maximum(m_prev, jnp.max(s, axis=0, keepdims=True))
        alpha = jnp.exp(m_prev - m_new)
        p = jnp.exp(s - m_new).astype(BF16)
        a_s[c] = alpha * a_s[c] + _mm(vt[:, pl.ds(off, width)], p)
        m_s[c] = m_new

    def far_both(off, width):
        s = [scores(c, off, width) for c in range(2)]
        for c in range(2):
            update(c, s[c], off, width)

    n_common = jnp.maximum(2 * g - 1, 0)
    done = 0
    for step_tiles in FAR_STEP_TILES:
        width = TB * step_tiles
        n_steps = (n_common - done) // step_tiles

        def far_step(j, carry, done=done, width=width):
            far_both(pl.multiple_of(done * TB + j * width, TB), width)
            return carry
        lax.fori_loop(0, n_steps, far_step, 0)
        done = done + n_steps * step_tiles

    off_a = pl.multiple_of(n_common * TB, TB)
    s_a = scores(0, off_a, 2 * TB) + jnp.where(g == 0, bias_ref[1], bias_ref[0])
    off_b = pl.multiple_of(2 * g * TB, TB)
    s_b = scores(1, off_b, 2 * TB) + bias_ref[0]
    update(0, s_a, off_a, 2 * TB)

    @pl.when(g > 0)
    def _():
        update(1, scores(1, off_a, TB), off_a, TB)
    update(1, s_b, off_b, 2 * TB)

    lam = _lambda(lqk_ref, lam_init)
    for c in range(2):
        acc = a_s[c]
        on = acc[0:LANES, :] / acc[LANES:LANES + 1, :]
        o = (on[:, :TB] - lam * on[:, TB:]).T
        o_ref[c * TB:(c + 1) * TB, :] = _subln(o, subln_ref[...], lam_init).astype(BF16)


def _attn_prompt(q, k_all, v_all, l, bias, lqk, subln, lam_init, TB):
    m, d_diff = q.shape
    n_heads = d_diff // LANES
    n_blk = m // TB
    return pl.pallas_call(
        functools.partial(_attn_kernel, TB=TB, n_cast=n_blk, lam_init=lam_init),
        grid=(n_heads, n_blk // 2),
        in_specs=[
            pl.BlockSpec((2 * TB, LANES), lambda h, g: (g, h)),
            pl.BlockSpec((None, m, LANES), lambda h, g: (l, 0, h)),
            pl.BlockSpec((None, m, LANES), lambda h, g: (l, 0, h)),
            pl.BlockSpec((None, 2, 2 * TB, 2 * TB), lambda h, g: (h, 0, 0, 0)),
            pl.BlockSpec((None, 4, DIFF_HEAD), lambda h, g: (l, 0, 0)),
            pl.BlockSpec((None, 1, LANES), lambda h, g: (l, 0, 0)),
        ],
        out_specs=pl.BlockSpec((2 * TB, LANES), lambda h, g: (g, h)),
        out_shape=jax.ShapeDtypeStruct((m, d_diff), BF16),
        scratch_shapes=[pltpu.VMEM((m, LANES), BF16), pltpu.VMEM((2 * LANES, m), BF16),
                        pltpu.VMEM((2, 1, 2 * TB), F32), pltpu.VMEM((2, 2 * LANES, 2 * TB), F32)],
        compiler_params=_cparams(("arbitrary", "arbitrary")),
        name="attn_prompt",
    )(q, k_all, v_all, bias, lqk, subln)


SAMPLE_FAR_FRAMES = 384


def _attn_sample_kernel(q_ref, kn_ref, vn_ref, kc_ref, vc_ref, bias_ref, lqk_ref, subln_ref, o_ref,
                        kb, acc, *, T, n_heads, near, lam_init):
    past = kc_ref.shape[0]
    n_cols = 2 * T * n_heads
    fc = SAMPLE_FAR_FRAMES
    n_far = (past - near) // fc
    rows = fc * n_heads
    qs = jnp.concatenate(
        [m_ for h in range(n_heads) for m_ in _split_maps(q_ref[:, h * LANES:(h + 1) * LANES])], axis=0)
    sub = lax.broadcasted_iota(jnp.int32, (n_heads, n_cols), 0)
    col = lax.broadcasted_iota(jnp.int32, (n_heads, n_cols), 1)
    valid = (col >= sub * (2 * T)) & (col < (sub + 1) * (2 * T))

    def flat(x3):
        return x3.reshape(x3.shape[0] * n_heads, LANES)

    def scores(k_rows):
        s = _mm(k_rows, qs, _NT)
        return s.reshape(s.shape[0] // n_heads, n_heads, n_cols)

    near_k = jnp.concatenate([flat(kc_ref[past - near:past]), flat(kn_ref[...])], axis=0).astype(BF16)
    near_v = jnp.concatenate([flat(vc_ref[past - near:past]), flat(vn_ref[...])], axis=0).astype(BF16)
    s_near = scores(near_k) + bias_ref[...].reshape(near + T, n_heads, n_cols)

    def pass1(c, m3):
        f0 = pl.multiple_of(c * fc, fc)
        k_rows = flat(kc_ref[pl.ds(f0, fc)]).astype(BF16)
        kb[pl.ds(pl.multiple_of(c * rows, rows), rows), :] = k_rows
        return jnp.maximum(m3, jnp.max(scores(k_rows), axis=0))
    m3 = lax.fori_loop(0, n_far, pass1, jnp.max(s_near, axis=0))

    ones = jnp.ones((rows, LANES), BF16)

    def weights(s3):
        p = jnp.where(valid, jnp.exp(s3 - m3), 0.0)
        return p.reshape(p.shape[0] * n_heads, n_cols).astype(BF16)

    acc[...] = _mm(weights(s_near), jnp.concatenate([near_v, ones[:near_v.shape[0]]], axis=1), _TN)

    def pass2(c, carry):
        f0 = pl.multiple_of(c * fc, fc)
        p = weights(scores(kb[pl.ds(pl.multiple_of(c * rows, rows), rows), :]))
        v_aug = jnp.concatenate([flat(vc_ref[pl.ds(f0, fc)]).astype(BF16), ones], axis=1)
        acc[...] += _mm(p, v_aug, _TN)
        return carry
    lax.fori_loop(0, n_far, pass2, 0)

    lam = _lambda(lqk_ref, lam_init)
    for h in range(n_heads):
        a = acc[h * 2 * T:(h + 1) * 2 * T, :]
        on = a[:, 0:LANES] / a[:, LANES:2 * LANES]
        o = on[:T, :] - lam * on[T:, :]
        o_ref[:, h * LANES:(h + 1) * LANES] = _subln(o, subln_ref[...], lam_init).astype(BF16)


def _attn_sample(q, kn_all, vn_all, l, k_cache, v_cache, bias_t, lqk, subln, lam_init, n_seq, T, near):
    m, d_diff = q.shape
    n_heads = d_diff // LANES
    past = k_cache.shape[2]
    assert (past - near) % SAMPLE_FAR_FRAMES == 0
    cache_spec = pl.BlockSpec((None, None, past, n_heads, LANES), lambda b: (l, b, 0, 0, 0))
    new_spec = pl.BlockSpec((None, T, n_heads, LANES), lambda b: (l, b, 0, 0))
    return pl.pallas_call(
        functools.partial(_attn_sample_kernel, T=T, n_heads=n_heads, near=near, lam_init=lam_init),
        grid=(n_seq,),
        in_specs=[
            pl.BlockSpec((T, d_diff), lambda b: (b, 0)),
            new_spec,
            new_spec,
            cache_spec,
            cache_spec,
            pl.BlockSpec(bias_t.shape, lambda b: (0, 0)),
            pl.BlockSpec((None, 4, DIFF_HEAD), lambda b: (l, 0, 0)),
            pl.BlockSpec((None, 1, LANES), lambda b: (l, 0, 0)),
        ],
        out_specs=pl.BlockSpec((T, d_diff), lambda b: (b, 0)),
        out_shape=jax.ShapeDtypeStruct((m, d_diff), BF16),
        scratch_shapes=[pltpu.VMEM(((past - near) * n_heads, LANES), BF16),
                        pltpu.VMEM((2 * T * n_heads, 2 * LANES), F32)],
        compiler_params=_cparams(("arbitrary",)),
        name="attn_sample",
    )(q, kn_all, vn_all, k_cache, v_cache, bias_t, lqk, subln)


def _outproj_kernel(x_ref, yr_ref, yd_ref, gt_ref, wt_ref, wb_ref, o_ref):
    mixed = _mm(yr_ref[...], wt_ref[...]) + _mm(yd_ref[...], wb_ref[...])
    o_ref[...] = x_ref[...] + gt_ref[...] * mixed


def _outproj(x, yr, yd, gt, w_out_all, l, tm):
    m, d = x.shape
    dr = yr.shape[1]
    dd = yd.shape[1]
    gt_spec = (pl.BlockSpec((1, d), lambda i: (0, 0)) if gt.shape[0] == 1
               else pl.BlockSpec((tm, d), lambda i: (i, 0)))
    return pl.pallas_call(
        _outproj_kernel,
        grid=(m // tm,),
        in_specs=[
            pl.BlockSpec((tm, d), lambda i: (i, 0)),
            pl.BlockSpec((tm, dr), lambda i: (i, 0)),
            pl.BlockSpec((tm, dd), lambda i: (i, 0)),
            gt_spec,
            pl.BlockSpec((None, dr, d), lambda i: (l, 0, 0)),
            pl.BlockSpec((None, dd, d), lambda i: (l, dr // dd, 0)),
        ],
        out_specs=pl.BlockSpec((tm, d), lambda i: (i, 0)),
        out_shape=jax.ShapeDtypeStruct((m, d), F32),
        compiler_params=_cparams(("arbitrary",)),
        name="outproj",
    )(x, yr, yd, gt, w_out_all, w_out_all)


def _shift_rows(x, prev, n):
    if n % SUBLANES == 0:
        return jnp.concatenate([prev, x[:x.shape[0] - n, :]], axis=0)
    rolled = pltpu.roll(x, n, 0)
    row = lax.broadcasted_iota(jnp.int32, x.shape, 0)
    out = rolled
    for r in range(n):
        out = jnp.where(row == r, prev[r:r + 1, :], out)
    return out


def _ffn_kernel(x_ref, sh_ref, sc_ref, gt_ref, g_ref, wg_ref, wv_ref, cw_ref, cb_ref, c0_ref, wd_ref, gf_ref,
                o_ref, co_ref, h_scr, acc_scr, carry, *, B, n_j, final_norm):
    i = pl.program_id(0)
    j = pl.program_id(1)
    tm = x_ref.shape[0]

    @pl.when(j == 0)
    def _():
        xn = _rms(x_ref[...], g_ref[...], NORM_EPS)
        h_scr[...] = (xn * (1.0 + sc_ref[...]) + sh_ref[...]).astype(BF16)
        acc_scr[...] = jnp.zeros(acc_scr.shape, F32)

    @pl.when(i == 0)
    def _():
        carry[j, 0:2 * B, :] = c0_ref[...]

    h = h_scr[...]
    ug = _mm(h, wg_ref[...])
    uv = _mm(h, wv_ref[...])
    prev = carry[j, 0:2 * B, :]
    s1 = _shift_rows(ug, prev[B:2 * B, :], B)
    s2 = _shift_rows(ug, prev, 2 * B)
    cw = cw_ref[...]
    z = cb_ref[...] + s2 * cw[0:1, :] + s1 * cw[1:2, :] + ug * cw[2:3, :]
    act = 0.5 * z * (1.0 + lax.erf(z * (2.0 ** -0.5)))
    acc_scr[...] += _mm((act * uv).astype(BF16), wd_ref[...])
    last2 = ug[tm - 2 * B:tm, :]
    carry[j, 0:2 * B, :] = last2
    tf = last2.shape[1]
    co_ref[:, pl.ds(pl.multiple_of(j * tf, tf), tf)] = last2

    @pl.when(j == n_j - 1)
    def _():
        x2 = x_ref[...] + gt_ref[...] * acc_scr[...]
        if final_norm:
            x2 = _rms(x2, gf_ref[...], NORM_EPS)
        o_ref[...] = x2


def _ffn(x, sh, sc, gt, g_all, w_up_all, conv_w_all, conv_b_all, conv0, w_down_all, g_final, l, B, tm,
         final_norm, tf=512):
    m, d = x.shape
    d_ff = w_down_all.shape[1]
    n_j = d_ff // tf
    rows_c = max(SUBLANES, 2 * B)
    return pl.pallas_call(
        functools.partial(_ffn_kernel, B=B, n_j=n_j, final_norm=final_norm),
        grid=(m // tm, n_j),
        in_specs=[
            pl.BlockSpec((tm, d), lambda i, j: (i, 0)),
            _mod_spec(sh, tm, d),
            _mod_spec(sc, tm, d),
            _mod_spec(gt, tm, d),
            pl.BlockSpec((None, 1, d), lambda i, j: (l, 0, 0)),
            pl.BlockSpec((None, d, tf), lambda i, j: (l, 0, j)),
            pl.BlockSpec((None, d, tf), lambda i, j: (l, 0, n_j + j)),
            pl.BlockSpec((None, CONV_W, tf), lambda i, j: (l, 0, j)),
            pl.BlockSpec((None, 1, tf), lambda i, j: (l, 0, j)),
            pl.BlockSpec((2 * B, tf), lambda i, j: (0, j)),
            pl.BlockSpec((None, tf, d), lambda i, j: (l, j, 0)),
            pl.BlockSpec((1, d), lambda i, j: (0, 0)),
        ],
        out_specs=[
            pl.BlockSpec((tm, d), lambda i, j: (i, 0)),
            pl.BlockSpec((2 * B, d_ff), lambda i, j: (0, 0)),
        ],
        out_shape=[
            jax.ShapeDtypeStruct((m, d), F32),
            jax.ShapeDtypeStruct((2 * B, d_ff), F32),
        ],
        scratch_shapes=[pltpu.VMEM((tm, d), BF16), pltpu.VMEM((tm, d), F32),
                        pltpu.VMEM((n_j, rows_c, tf), F32)],
        compiler_params=_cparams(("arbitrary", "arbitrary")),
        name="conv_ffn",
    )(x, sh, sc, gt, g_all, w_up_all, w_up_all, conv_w_all, conv_b_all, conv0, w_down_all, g_final)


def _pad_cols(a, n):
    return jnp.pad(a, [(0, 0)] * (a.ndim - 1) + [(0, n)])


def _regroup_rwkv_cols(a, d_rwkv):
    o = 3 * d_rwkv
    return jnp.concatenate([
        a[..., :o],
        _pad_cols(a[..., o:o + DECAY_LORA], LORA_PAD - DECAY_LORA),
        _pad_cols(a[..., o + DECAY_LORA:o + DECAY_LORA + AAA_LORA], LORA_PAD - AAA_LORA),
        a[..., o + DECAY_LORA + AAA_LORA:],
    ], axis=-1)


def _ungroup_rwkv_cols(a, d_rwkv):
    o = 3 * d_rwkv
    return jnp.concatenate([
        a[..., :o],
        a[..., o:o + DECAY_LORA],
        a[..., o + LORA_PAD:o + LORA_PAD + AAA_LORA],
        a[..., o + 2 * LORA_PAD:],
    ], axis=-1)


def kernel(x_prompt, x_sample, c_prompt, c_sample, cache_k, cache_v, state_wkv, state_shift, state_conv, w_ada, b_ada, g_mix, g_ffn, w_in, w_out, rwkv_mu, rwkv_w0, rwkv_w2, rwkv_a0, rwkv_a2, rwkv_g2, rwkv_kk, rwkv_ka, rwkv_rk, rwkv_ln_w, rwkv_ln_b, diff_lq1, diff_lk1, diff_lq2, diff_lk2, diff_subln, rel_table, ffn_up, ffn_conv_w, ffn_conv_b, ffn_down, g_final):
    depth, d_model, _ = w_in.shape
    bp, seq, _ = x_prompt.shape
    bs, dseq, _ = x_sample.shape
    past = cache_k.shape[2]
    d_rwkv = rwkv_w0.shape[1]
    d_diff = d_model - d_rwkv
    n_dheads = d_diff // (2 * DIFF_HEAD)
    n_rheads = d_rwkv // RWKV_HEAD
    n_rwkv_cols = rwkv_mu.shape[1]
    n_rwkv_pad = n_rwkv_cols + 2 * LORA_PAD - DECAY_LORA - AAA_LORA
    d_ff = ffn_down.shape[1]
    assert bp == 1, "prompt path handles one sequence"
    TB = min(256, seq // 2)
    assert seq % (2 * TB) == 0 and TB % CHUNK == 0 and TB >= MAX_DISTANCE
    tm_p = min(512, seq)
    tm_in = min(1024, seq)
    m_s = bs * dseq

    w_in_b = jnp.concatenate([_regroup_rwkv_cols(w_in[..., :n_rwkv_cols], d_rwkv), w_in[..., n_rwkv_cols:]],
                             axis=-1).astype(BF16)
    mu_p = _regroup_rwkv_cols(rwkv_mu, d_rwkv)[:, None]
    w2_b = jnp.pad(rwkv_w2, ((0, 0), (0, LORA_PAD - DECAY_LORA), (0, 0))).astype(BF16)
    a2_b = jnp.pad(rwkv_a2, ((0, 0), (0, LORA_PAD - AAA_LORA), (0, 0))).astype(BF16)
    g2_b = rwkv_g2.astype(BF16)
    w_out_b = _cast_bf16(w_out)
    up_b = _cast_bf16(ffn_up)
    down_b = _cast_bf16(ffn_down)
    vec = lambda a: a.reshape(depth, 1, -1)
    wkv_args = (w2_b, a2_b, g2_b, vec(rwkv_w0), vec(rwkv_a0), vec(rwkv_kk), vec(rwkv_ka), vec(rwkv_rk),
                vec(rwkv_ln_w), vec(rwkv_ln_b))
    g_mix_v, g_ffn_v, subln_v, conv_b_v = vec(g_mix), vec(g_ffn), vec(diff_subln), vec(ffn_conv_b)
    gf = g_final[None]
    lqk = jnp.stack([diff_lq1, diff_lk1, diff_lq2, diff_lk2], axis=1)

    n_c = bp + bs
    n_c_pad = -(-n_c // SUBLANES) * SUBLANES
    c_rows = jnp.pad(jnp.concatenate([c_prompt, c_sample], axis=0), ((0, n_c_pad - n_c), (0, 0)))
    mod = _adaln_mod(c_rows, w_ada, b_ada).reshape(depth, n_c_pad, 6, d_model)

    ql = jnp.arange(TB)
    rel_diag = ql[None, :] - ql[:, None]
    allowed = (ql[None, :] // CHUNK) <= (ql[:, None] // CHUNK)
    bucket_diag = jnp.where(allowed, _t5_bucket(rel_diag), -1)
    bucket_prev = _t5_bucket(rel_diag - TB)
    windows = [jnp.concatenate([bucket_prev, bucket_diag], axis=1),
               jnp.concatenate([bucket_diag, jnp.full((TB, TB), -1)], axis=1)]
    bucket_p = jnp.concatenate([jnp.tile(w.T, (1, 2)) for w in windows], axis=0)
    bias_p = _bias_lookup(rel_table, bucket_p.astype(jnp.int32), FAR_BUCKET).reshape(n_dheads, 2, 2 * TB, 2 * TB)
    q_pos = past + jnp.arange(dseq)
    assert past >= MAX_DISTANCE
    k_pos = past - MAX_DISTANCE + jnp.arange(MAX_DISTANCE + dseq)
    bias_s = _bias_lookup(rel_table, _t5_bucket(k_pos[None, :] - q_pos[:, None]).astype(jnp.int32), FAR_BUCKET)
    bias_t = jnp.broadcast_to(bias_s.transpose(2, 0, 1)[:, None, :, None, :],
                              (MAX_DISTANCE + dseq, n_dheads, n_dheads, 2, dseq))
    bias_t = bias_t.reshape((MAX_DISTANCE + dseq) * n_dheads, n_dheads * 2 * dseq)

    xp = x_prompt.reshape(seq, d_model)
    xs = x_sample.reshape(m_s, d_model)
    zero_shift = jnp.zeros((1, 1, n_rwkv_pad), F32)
    zero_state = jnp.zeros((1, n_rheads, RWKV_HEAD, RWKV_HEAD), F32)
    zero_conv = jnp.zeros((2, d_ff), F32)
    shift_s_in = _regroup_rwkv_cols(state_shift, d_rwkv)
    outs_p = {k: [] for k in ("wkv", "shift", "conv")}
    outs_s = {k: [] for k in ("wkv", "shift", "conv")}
    kv_p = (jnp.zeros((depth, seq, d_diff), F32), jnp.zeros((depth, seq, d_diff), F32))
    kv_s = (jnp.zeros((depth, m_s, d_diff), F32), jnp.zeros((depth, m_s, d_diff), F32))

    def time_major(a):
        return a.reshape(bs, dseq, -1).swapaxes(0, 1).reshape(m_s, -1)

    def batch_major(a):
        return a.reshape(dseq, bs, -1).swapaxes(0, 1).reshape(m_s, -1)

    for l in range(depth):
        lam_init = 0.8 - 0.6 * math.exp(-0.3 * l)
        last = l == depth - 1

        mp = mod[l, 0:bp]
        sh1, sc1, gt1, sh2, sc2, gt2 = (mp[:, t] for t in range(6))
        p, q, *kv_p = _inproj(xp, sh1, sc1, g_mix_v, w_in_b, l, kv_p, n_rwkv_pad, d_diff, tm_in)
        y_r, wkv = _wkv(p, l, mu_p, zero_shift, *wkv_args, zero_state, 1, seq, min(64, seq))
        y_d = _attn_prompt(q, kv_p[0], kv_p[1], l, bias_p, lqk, subln_v, lam_init, TB)
        x1 = _outproj(xp, y_r, y_d, gt1, w_out_b, l, tm_p)
        xp, conv = _ffn(x1, sh2, sc2, gt2, g_ffn_v, up_b, ffn_conv_w, conv_b_v, zero_conv, down_b, gf, l,
                        1, tm_p, last)
        outs_p["wkv"].append(wkv)
        outs_p["shift"].append(_ungroup_rwkv_cols(p[seq - 1:seq], d_rwkv).reshape(bp, 1, n_rwkv_cols))
        outs_p["conv"].append(conv.reshape(bp, CONV_W - 1, d_ff))

        ms = mod[l, bp:bp + bs]
        rows_bm = jnp.repeat(ms, dseq, axis=0)
        rows_tm = jnp.tile(ms, (dseq, 1, 1))
        p, q, *kv_s = _inproj(xs, rows_bm[:, 0], rows_bm[:, 1], g_mix_v, w_in_b, l, kv_s, n_rwkv_pad, d_diff, m_s)
        y_r, wkv = _wkv(p, l, mu_p, shift_s_in[l], *wkv_args, state_wkv[l], bs, dseq, dseq)
        kn4, vn4 = (a.reshape(depth, m_s, n_dheads, 2 * DIFF_HEAD) for a in kv_s)
        y_d = _attn_sample(q, kn4, vn4, l, cache_k, cache_v, bias_t, lqk, subln_v, lam_init, bs, dseq,
                           MAX_DISTANCE)
        x1 = _outproj(xs, y_r, y_d, rows_bm[:, 2], w_out_b, l, m_s)
        conv0 = state_conv[l].swapaxes(0, 1).reshape((CONV_W - 1) * bs, d_ff)
        x2, conv = _ffn(time_major(x1), rows_tm[:, 3], rows_tm[:, 4], rows_tm[:, 5], g_ffn_v, up_b, ffn_conv_w,
                        conv_b_v, conv0, down_b, gf, l, bs, m_s, last)
        xs = batch_major(x2)
        outs_s["wkv"].append(wkv)
        p_last = p.reshape(bs, dseq, n_rwkv_pad)[:, dseq - 1:dseq]
        outs_s["shift"].append(_ungroup_rwkv_cols(p_last, d_rwkv))
        outs_s["conv"].append(conv.reshape(CONV_W - 1, bs, d_ff).swapaxes(0, 1))

    st = lambda xs_: jnp.stack(xs_)
    head_shape = (n_dheads, 2 * DIFF_HEAD)
    return (xp.reshape(bp, seq, d_model), xs.reshape(bs, dseq, d_model),
            kv_p[0].reshape(depth, bp, seq, *head_shape), kv_p[1].reshape(depth, bp, seq, *head_shape),
            st(outs_p["wkv"]), st(outs_p["shift"]), st(outs_p["conv"]),
            kv_s[0].reshape(depth, bs, dseq, *head_shape), kv_s[1].reshape(depth, bs, dseq, *head_shape),
            st(outs_s["wkv"]), st(outs_s["shift"]), st(outs_s["conv"]))
```

```python
import functools
import math

import jax
import jax.numpy as jnp
from jax import lax
from jax.experimental import pallas as pl
from jax.experimental.pallas import tpu as pltpu

F32 = jnp.float32
BF16 = jnp.bfloat16

CHUNK = 64
RWKV_HEAD = 64
DIFF_HEAD = 64
N_BUCKETS = 32
MAX_DISTANCE = 128
NORM_EPS = 1e-6
GN_EPS = 64e-5
SUBLN_EPS = 1e-5
DECAY_LORA = 96
AAA_LORA = 96
GATE_LORA = 256
LORA_PAD = 128
CONV_W = 3

LANES = 128
SUBLANES = 8
VMEM_LIMIT_BYTES = 56 * 1024 * 1024
MASK_VALUE = -1e30
WKV_PAIRS_PER_STEP = 8

_NN = (((1,), (0,)), ((), ()))
_NT = (((1,), (1,)), ((), ()))
_TN = (((0,), (0,)), ((), ()))


def _mm(a, b, dims=_NN):
    return lax.dot_general(a, b, dims, preferred_element_type=F32)


def _hi_lo(x):
    h = x.astype(BF16)
    return h, (x - h.astype(F32)).astype(BF16)


def _mm3(a, b, dims=_NN):
    ah, al = _hi_lo(a)
    bh, bl = _hi_lo(b)
    (ca,), (cb,) = dims[0]
    k = a.shape[ca]
    if (ca == 0 or k % LANES == 0) and (cb == 0 or k % LANES == 0):
        return _mm(jnp.concatenate([ah, al, ah], axis=ca), jnp.concatenate([bh, bh, bl], axis=cb), dims)
    return _mm(ah, bh, dims) + (_mm(al, bh, dims) + _mm(ah, bl, dims))


def _sel_rows(sel2, x):
    xh, xl = _hi_lo(x)
    return _mm(sel2, jnp.concatenate([xh, xl], axis=0))


def _sel_lanes(x, sel2):
    xh, xl = _hi_lo(x)
    return _mm(jnp.concatenate([xh, xl], axis=1), sel2)


def _cparams(sem):
    return pltpu.CompilerParams(dimension_semantics=sem, vmem_limit_bytes=VMEM_LIMIT_BYTES)


def _rms(x, g, eps):
    return x * lax.rsqrt(jnp.mean(x * x, axis=-1, keepdims=True) + eps) * g


def _cast_kernel(x_ref, o_ref):
    o_ref[...] = x_ref[...].astype(BF16)


def _cast_bf16(w, tr=512, tc=1024):
    depth, rows, cols = w.shape
    spec = pl.BlockSpec((None, tr, tc), lambda l, i, j: (l, i, j))
    return pl.pallas_call(
        _cast_kernel,
        grid=(depth, rows // tr, cols // tc),
        in_specs=[spec],
        out_specs=spec,
        out_shape=jax.ShapeDtypeStruct(w.shape, BF16),
        compiler_params=_cparams(("arbitrary", "arbitrary", "arbitrary")),
        name="cast_bf16",
    )(w)


def _mod_kernel(c_ref, w_ref, b_ref, o_ref):
    c = c_ref[...]
    s = (c * jax.nn.sigmoid(c)).astype(BF16)
    o_ref[...] = _mm(s, w_ref[...].astype(BF16)) + b_ref[...]


def _adaln_mod(c_rows, w_ada, b_ada, tn=1024):
    depth, d, n = w_ada.shape
    rows = c_rows.shape[0]
    return pl.pallas_call(
        _mod_kernel,
        grid=(depth, n // tn),
        in_specs=[
            pl.BlockSpec((rows, d), lambda l, j: (0, 0)),
            pl.BlockSpec((None, d, tn), lambda l, j: (l, 0, j)),
            pl.BlockSpec((None, 1, tn), lambda l, j: (l, 0, j)),
        ],
        out_specs=pl.BlockSpec((None, rows, tn), lambda l, j: (l, 0, j)),
        out_shape=jax.ShapeDtypeStruct((depth, rows, n), F32),
        compiler_params=_cparams(("arbitrary", "arbitrary")),
        name="adaln_mod",
    )(c_rows, w_ada, b_ada.reshape(depth, 1, n))


def _bias_kernel(tab_ref, bucket_ref, o_ref, *, far_bucket):
    h = pl.program_id(0)
    bucket = bucket_ref[...]
    far = tab_ref[far_bucket, h]
    acc = jnp.full(bucket.shape, MASK_VALUE, F32)
    for b in range(N_BUCKETS):
        acc = jnp.where(bucket == b, tab_ref[b, h] - far, acc)
    o_ref[...] = acc


def _bias_lookup(table, bucket, far_bucket):
    n_heads = table.shape[1]
    r, c = bucket.shape
    return pl.pallas_call(
        functools.partial(_bias_kernel, far_bucket=far_bucket),
        grid=(n_heads,),
        in_specs=[
            pl.BlockSpec(memory_space=pltpu.SMEM),
            pl.BlockSpec((r, c), lambda h: (0, 0)),
        ],
        out_specs=pl.BlockSpec((None, r, c), lambda h: (h, 0, 0)),
        out_shape=jax.ShapeDtypeStruct((n_heads, r, c), F32),
        compiler_params=_cparams(("arbitrary",)),
        name="bias_lookup",
    )(table, bucket)


def _t5_bucket(rel):
    nb = N_BUCKETS // 2
    max_exact = nb // 2
    n = jnp.abs(rel)
    nf = jnp.maximum(n, 1).astype(F32)
    large = max_exact + (jnp.log(nf / max_exact) / math.log(MAX_DISTANCE / max_exact) * (nb - max_exact)).astype(jnp.int32)
    large = jnp.minimum(large, nb - 1)
    return jnp.where(rel > 0, nb, 0) + jnp.where(n < max_exact, n, large)


FAR_BUCKET = N_BUCKETS // 2 - 1


def _inproj_kernel(x_ref, sh_ref, sc_ref, g_ref, w_ref, *rest, n_p, n_q):
    p_ref, q_ref, k_ref, v_ref, h_scr = rest[-5:]
    j = pl.program_id(1)

    @pl.when(j == 0)
    def _():
        xn = _rms(x_ref[...], g_ref[...], NORM_EPS)
        h_scr[...] = (xn * (1.0 + sc_ref[...]) + sh_ref[...]).astype(BF16)

    @pl.when(j < n_p)
    def _():
        p_ref[...] = _mm(h_scr[...], w_ref[...])

    @pl.when((j >= n_p) & (j < n_p + n_q))
    def _():
        q_ref[...] = (_mm(h_scr[...], w_ref[...]) * (DIFF_HEAD ** -0.5)).astype(BF16)

    @pl.when((j >= n_p + n_q) & (j < n_p + 2 * n_q))
    def _():
        k_ref[...] = _mm(h_scr[...], w_ref[...])

    @pl.when(j >= n_p + 2 * n_q)
    def _():
        v_ref[...] = _mm(h_scr[...], w_ref[...])


def _mod_spec(mod, tm, d):
    if mod.shape[0] == 1:
        return pl.BlockSpec((1, d), lambda i, j: (0, 0))
    return pl.BlockSpec((tm, d), lambda i, j: (i, 0))


def _inproj(x, sh, sc, g_all, w_all, l, kv_all, n_rwkv_pad, d_diff, tm, tn=512):
    m, d = x.shape
    depth, _, n = w_all.shape
    n_p, n_q = n_rwkv_pad // tn, d_diff // tn
    clip = lambda j, lo: jnp.clip(j - lo, 0, n_q - 1)
    in_specs = [
        pl.BlockSpec((tm, d), lambda i, j: (i, 0)),
        _mod_spec(sh, tm, d),
        _mod_spec(sc, tm, d),
        pl.BlockSpec((None, 1, d), lambda i, j: (l, 0, 0)),
        pl.BlockSpec((None, d, tn), lambda i, j: (l, 0, j)),
        pl.BlockSpec(memory_space=pl.ANY),
        pl.BlockSpec(memory_space=pl.ANY),
    ]
    args = [x, sh, sc, g_all, w_all, kv_all[0], kv_all[1]]
    aliases = {5: 2, 6: 3}
    return pl.pallas_call(
        functools.partial(_inproj_kernel, n_p=n_p, n_q=n_q),
        grid=(m // tm, n // tn),
        in_specs=in_specs,
        out_specs=[
            pl.BlockSpec((tm, tn), lambda i, j: (i, jnp.minimum(j, n_p - 1))),
            pl.BlockSpec((tm, tn), lambda i, j: (i, clip(j, n_p))),
            pl.BlockSpec((None, tm, tn), lambda i, j: (l, i, clip(j, n_p + n_q))),
            pl.BlockSpec((None, tm, tn), lambda i, j: (l, i, clip(j, n_p + 2 * n_q))),
        ],
        out_shape=[
            jax.ShapeDtypeStruct((m, n_rwkv_pad), F32),
            jax.ShapeDtypeStruct((m, d_diff), BF16),
            jax.ShapeDtypeStruct((depth, m, d_diff), F32),
            jax.ShapeDtypeStruct((depth, m, d_diff), F32),
        ],
        scratch_shapes=[pltpu.VMEM((tm, d), BF16)],
        input_output_aliases=aliases,
        compiler_params=_cparams(("arbitrary", "arbitrary")),
        name="inproj",
    )(*args)


def _wkv_kernel(pr_ref, pk_ref, pv_ref, pw_ref, pa_ref, pg_ref,
                mur_ref, muk_ref, muv_ref, muw_ref, mua_ref, mug_ref,
                s0r_ref, s0k_ref, s0v_ref, s0w_ref, s0a_ref, s0g_ref,
                w2_ref, a2_ref, g2_ref,
                w0_ref, a0_ref, kkp_ref, ka_ref, rk_ref, lnw_ref, lnb_ref,
                st0_ref,
                y_ref, st_ref,
                sbd, cr, ck, cv, cw, ca, cg, *, C, n_chunks, n_pairs):
    chunk = pl.program_id(2)
    hd = RWKV_HEAD
    zero_blk = jnp.zeros((hd, hd), F32)

    @pl.when(chunk == 0)
    def _():
        for c_ref, s_ref in ((cr, s0r_ref), (ck, s0k_ref), (cv, s0v_ref), (cw, s0w_ref), (ca, s0a_ref), (cg, s0g_ref)):
            c_ref[0:1, :] = s_ref[...]
        for pi in range(n_pairs):
            top = jnp.concatenate([st0_ref[2 * pi], zero_blk], axis=1)
            bot = jnp.concatenate([zero_blk, st0_ref[2 * pi + 1]], axis=1)
            sbd[pi] = jnp.concatenate([top, bot], axis=0)

    def tshift(x_ref, mu_ref, c_ref):
        x = x_ref[...]
        row = lax.broadcasted_iota(jnp.int32, x.shape, 0)
        prev = jnp.where(row == 0, c_ref[0:1, :], pltpu.roll(x, 1, 0))
        c_ref[0:1, :] = x[C - 1:C, :]
        return x + (prev - x) * mu_ref[...]

    xr = tshift(pr_ref, mur_ref, cr)
    xk = tshift(pk_ref, muk_ref, ck)
    xv = tshift(pv_ref, muv_ref, cv)
    xw = tshift(pw_ref, muw_ref, cw)
    xa = tshift(pa_ref, mua_ref, ca)
    xg = tshift(pg_ref, mug_ref, cg)

    lane = lax.broadcasted_iota(jnp.int32, (1, LANES), 1)
    head0 = lane < hd
    li = lax.broadcasted_iota(jnp.int32, (LANES, LANES), 0)
    lj = lax.broadcasted_iota(jnp.int32, (LANES, LANES), 1)
    same_head = (li < hd) == (lj < hd)
    li2 = lax.broadcasted_iota(jnp.int32, (2 * LANES, LANES), 0)
    lj2 = lax.broadcasted_iota(jnp.int32, (2 * LANES, LANES), 1)
    li2 = jnp.where(li2 < LANES, li2, li2 - LANES)
    seg2 = jnp.where((li2 < hd) == (lj2 < hd), 1.0, 0.0).astype(BF16)

    def segsum(x):
        return _sel_lanes(x, seg2)

    wlin = w0_ref[...] + _mm(jnp.tanh(xw).astype(BF16), w2_ref[...])
    z = -wlin
    w_log = -(jnp.maximum(z, 0.0) + jnp.log1p(jnp.exp(-jnp.abs(z)))) - 0.5
    ne = -jnp.exp(w_log)
    a = jax.nn.sigmoid(a0_ref[...] + _mm(xa.astype(BF16), a2_ref[...]))
    g = _mm(jax.nn.sigmoid(xg).astype(BF16), g2_ref[...])
    kk_raw = xk * kkp_ref[...]
    k2 = xk * (1.0 + (a - 1.0) * ka_ref[...])
    rk_prod = xr * k2 * rk_ref[...]

    ti = lax.broadcasted_iota(jnp.int32, (C, 2 * C), 0)
    tj = lax.broadcasted_iota(jnp.int32, (C, 2 * C), 1)
    tj = jnp.where(tj < C, tj, tj - C)
    tri2 = jnp.where(tj <= ti, 1.0, 0.0).astype(BF16)
    cum = _sel_rows(tri2, ne)
    gam = jnp.exp(cum)
    gprev = jnp.exp(cum - ne)
    ginv = jnp.exp(-cum)
    dte = jnp.exp(cum[C - 1:C, :] - cum)
    rg_all = xr * gam
    ks_all = k2 * ginv
    kd_all = k2 * dte
    lnw = lnw_ref[...]
    lnb = lnb_ref[...]

    def stack_masked(x):
        return jnp.concatenate([jnp.where(head0, x, 0.0), jnp.where(head0, 0.0, x)], axis=0)

    def stack_twice(x):
        return jnp.concatenate([x, x], axis=0)

    R = 2 * C
    ri = lax.broadcasted_iota(jnp.int32, (R, R), 0)
    ci = lax.broadcasted_iota(jnp.int32, (R, R), 1)
    same_blk = (ri < C) == (ci < C)
    tr = jnp.where(ri < C, ri, ri - C)
    tc = jnp.where(ci < C, ci, ci - C)
    strict = same_blk & (tc < tr)
    incl = same_blk & (tc <= tr)
    n_steps = max(1, int(math.log2(C)))
    inv_hd = 1.0 / hd

    pairs = range(n_pairs)
    sls = [slice(pi * LANES, (pi + 1) * LANES) for pi in pairs]
    xv_p = [xv[:, sl] for sl in sls]
    sums = [segsum(jnp.concatenate([kk_raw[:, sl] * kk_raw[:, sl], rk_prod[:, sl]], axis=0)) for sl in sls]
    kk = [kk_raw[:, sl] * lax.rsqrt(jnp.maximum(sm[:C, :], 1e-24)) for sl, sm in zip(sls, sums)]
    bonus = [sm[C:, :] * v for sm, v in zip(sums, xv_p)]
    kka = [k * a[:, sl] for k, sl in zip(kk, sls)]
    la = [stack_masked(-k * gprev[:, sl]) for k, sl in zip(kk, sls)]
    lr = [stack_masked(rg_all[:, sl]).astype(BF16) for sl in sls]
    rhs = [jnp.concatenate([stack_twice(ka_ * ginv[:, sl]), stack_twice(ks_all[:, sl])], axis=0)
           for ka_, sl in zip(kka, sls)]
    vst = [stack_masked(v) for v in xv_p]
    s_old = [sbd[pi] for pi in pairs]
    sc_a = [_mm3(l_, r_, _NT) for l_, r_ in zip(la, rhs)]
    sc_r = [_mm(l_, r_.astype(BF16), _NT) for l_, r_ in zip(lr, rhs)]
    ls_a = [_mm3(l_, s_, _NT) for l_, s_ in zip(la, s_old)]
    ls_r = [_mm(l_, s_.astype(BF16), _NT) for l_, s_ in zip(lr, s_old)]
    n_ab = [jnp.where(strict, s4[:, :R], 0.0) for s4 in sc_a]
    m_ak = [jnp.where(strict, s4[:, R:], 0.0) for s4 in sc_a]
    m_r = [jnp.concatenate([jnp.where(incl, s4[:, :R], 0.0), jnp.where(incl, s4[:, R:], 0.0)],
                           axis=1).astype(BF16) for s4 in sc_r]
    x_u = [l_ + _mm3(mk, v) for l_, mk, v in zip(ls_a, m_ak, vst)]
    pw = n_ab
    for step in range(n_steps):
        if step + 1 < n_steps:
            t = [_mm3(p_, jnp.concatenate([x_, p_], axis=1)) for p_, x_ in zip(pw, x_u)]
            x_u = [x_ + t_[:, :LANES] for x_, t_ in zip(x_u, t)]
            pw = [t_[:, LANES:] for t_ in t]
        else:
            x_u = [x_ + _mm3(p_, x_) for p_, x_ in zip(pw, x_u)]
    y_st = [l_ + _mm(mr, jnp.concatenate([x_, v], axis=0).astype(BF16))
            for l_, mr, x_, v in zip(ls_r, m_r, x_u, vst)]
    upd = [_mm3(jnp.concatenate([x_[:C, :] + x_[C:, :], v], axis=0),
                jnp.concatenate([ka_ * dte[:, sl], kd_all[:, sl]], axis=0), _TN)
           for x_, v, ka_, sl in zip(x_u, xv_p, kka, sls)]
    for pi in pairs:
        sbd[pi] = s_old[pi] * gam[C - 1:C, sls[pi]] + jnp.where(same_head, upd[pi], 0.0)

    y = [ys[:C, :] + ys[C:, :] for ys in y_st]
    yc = [y_ - segsum(y_) * inv_hd for y_ in y]
    y_var = [segsum(c_ * c_) * inv_hd for c_ in yc]
    for pi in pairs:
        sl = sls[pi]
        yn = yc[pi] * lax.rsqrt(y_var[pi] + GN_EPS) * lnw[:, sl] + lnb[:, sl]
        y_ref[:, sl] = ((yn + bonus[pi]) * g[:, sl]).astype(BF16)

    @pl.when(chunk == n_chunks - 1)
    def _():
        for pi in range(n_pairs):
            s_new = sbd[pi]
            st_ref[2 * pi] = s_new[0:hd, 0:hd]
            st_ref[2 * pi + 1] = s_new[hd:2 * hd, hd:2 * hd]


def _wkv(p, l, mu, shift0, w2, a2, g2, w0, a0, kkp, ka, rk, lnw, lnb, state0, n_seq, seq_len, C):
    m, n_pad = p.shape
    d_rwkv = w0.shape[-1]
    n_heads = d_rwkv // RWKV_HEAD
    n_pairs = WKV_PAIRS_PER_STEP
    wd = n_pairs * LANES
    n_groups = d_rwkv // wd
    n_chunks = seq_len // C
    nb = n_groups
    blk_w = 3 * d_rwkv // LANES
    blk_a = blk_w + 1
    blk_g = (blk_a + 1) * LANES // GATE_LORA

    row = lambda s, h, c: s * n_chunks + c
    seg_specs = [
        pl.BlockSpec((C, wd), lambda s, h, c: (row(s, h, c), h)),
        pl.BlockSpec((C, wd), lambda s, h, c: (row(s, h, c), nb + h)),
        pl.BlockSpec((C, wd), lambda s, h, c: (row(s, h, c), 2 * nb + h)),
        pl.BlockSpec((C, LANES), lambda s, h, c: (row(s, h, c), blk_w)),
        pl.BlockSpec((C, LANES), lambda s, h, c: (row(s, h, c), blk_a)),
        pl.BlockSpec((C, GATE_LORA), lambda s, h, c: (row(s, h, c), blk_g)),
    ]
    mu_specs = [
        pl.BlockSpec((None, 1, wd), lambda s, h, c: (l, 0, h)),
        pl.BlockSpec((None, 1, wd), lambda s, h, c: (l, 0, nb + h)),
        pl.BlockSpec((None, 1, wd), lambda s, h, c: (l, 0, 2 * nb + h)),
        pl.BlockSpec((None, 1, LANES), lambda s, h, c: (l, 0, blk_w)),
        pl.BlockSpec((None, 1, LANES), lambda s, h, c: (l, 0, blk_a)),
        pl.BlockSpec((None, 1, GATE_LORA), lambda s, h, c: (l, 0, blk_g)),
    ]
    s0_specs = [
        pl.BlockSpec((None, 1, wd), lambda s, h, c: (s, 0, h)),
        pl.BlockSpec((None, 1, wd), lambda s, h, c: (s, 0, nb + h)),
        pl.BlockSpec((None, 1, wd), lambda s, h, c: (s, 0, 2 * nb + h)),
        pl.BlockSpec((None, 1, LANES), lambda s, h, c: (s, 0, blk_w)),
        pl.BlockSpec((None, 1, LANES), lambda s, h, c: (s, 0, blk_a)),
        pl.BlockSpec((None, 1, GATE_LORA), lambda s, h, c: (s, 0, blk_g)),
    ]
    lora_specs = [
        pl.BlockSpec((None, LORA_PAD, wd), lambda s, h, c: (l, 0, h)),
        pl.BlockSpec((None, LORA_PAD, wd), lambda s, h, c: (l, 0, h)),
        pl.BlockSpec((None, GATE_LORA, wd), lambda s, h, c: (l, 0, h)),
    ]
    vec_spec = pl.BlockSpec((None, 1, wd), lambda s, h, c: (l, 0, h))
    state_spec = pl.BlockSpec((None, 2 * n_pairs, RWKV_HEAD, RWKV_HEAD), lambda s, h, c: (s, h, 0, 0))
    y, st = pl.pallas_call(
        functools.partial(_wkv_kernel, C=C, n_chunks=n_chunks, n_pairs=n_pairs),
        grid=(n_seq, n_groups, n_chunks),
        in_specs=seg_specs + mu_specs + s0_specs + lora_specs + [vec_spec] * 7 + [state_spec],
        out_specs=[
            pl.BlockSpec((C, wd), lambda s, h, c: (row(s, h, c), h)),
            state_spec,
        ],
        out_shape=[
            jax.ShapeDtypeStruct((m, d_rwkv), BF16),
            jax.ShapeDtypeStruct((n_seq, n_heads, RWKV_HEAD, RWKV_HEAD), F32),
        ],
        scratch_shapes=[pltpu.VMEM((n_pairs, LANES, LANES), F32)]
        + [pltpu.VMEM((SUBLANES, wd), F32)] * 3 + [pltpu.VMEM((SUBLANES, LANES), F32)] * 2
        + [pltpu.VMEM((SUBLANES, GATE_LORA), F32)],
        compiler_params=_cparams(("arbitrary", "arbitrary", "arbitrary")),
        name="wkv",
    )(p, p, p, p, p, p, mu, mu, mu, mu, mu, mu,
      shift0, shift0, shift0, shift0, shift0, shift0,
      w2, a2, g2, w0, a0, kkp, ka, rk, lnw, lnb, state0)
    return y, st


def _lambda(lqk_ref, lam_init):
    t = lqk_ref[...]
    s1 = jnp.sum(t[0:1, :] * t[1:2, :], axis=-1, keepdims=True)
    s2 = jnp.sum(t[2:3, :] * t[3:4, :], axis=-1, keepdims=True)
    return jnp.exp(s1) - jnp.exp(s2) + lam_init


def _split_maps(q):
    lane = lax.broadcasted_iota(jnp.int32, q.shape, 1)
    zero = jnp.zeros_like(q)
    return jnp.where(lane < DIFF_HEAD, q, zero), jnp.where(lane < DIFF_HEAD, zero, q)


def _subln(o, subln, lam_init):
    return _rms(o, subln, SUBLN_EPS) * (1.0 - lam_init)


FAR_STEP_TILES = (4, 1)
ATTN_TILES_PER_STEP = 4


def _attn_kernel(q_ref, k_ref, v_ref, bias_ref, lqk_ref, subln_ref, o_ref,
                 kb, vt, m_s, a_s, *, TB, n_cast, lam_init):
    g = pl.program_id(1)

    @pl.when(g == 0)
    def _():
        vt[LANES:2 * LANES, :] = jnp.ones((LANES, vt.shape[1]), BF16)

        def cast(c, carry):
            off = pl.multiple_of(c * TB, TB)
            kb[pl.ds(off, TB), :] = k_ref[pl.ds(off, TB), :].astype(BF16)
            vt[0:LANES, pl.ds(off, TB)] = v_ref[pl.ds(off, TB), :].T.astype(BF16)
            return carry
        lax.fori_loop(0, n_cast, cast, 0)

    nq = ATTN_TILES_PER_STEP
    chains = range(nq)
    qs = [jnp.concatenate(_split_maps(q_ref[c * TB:(c + 1) * TB, :]), axis=0) for c in chains]
    m_s[...] = jnp.full(m_s.shape, MASK_VALUE, F32)
    a_s[...] = jnp.zeros(a_s.shape, F32)

    def scores(c, off, width):
        return _mm(kb[pl.ds(off, width), :], qs[c], _NT)

    def update(c, s, off, width):
        m_prev = m_s[c]
        m_new = jnp.maximum(m_prev, jnp.max(s, axis=0, keepdims=True))
        alpha = jnp.exp(m_prev - m_new)
        p = jnp.exp(s - m_new).astype(BF16)
        a_s[c] = alpha * a_s[c] + _mm(vt[:, pl.ds(off, width)], p)
        m_s[c] = m_new

    def far_all(off, width):
        s = [scores(c, off, width) for c in chains]
        for c in chains:
            update(c, s[c], off, width)

    n_common = jnp.maximum(nq * g - 1, 0)
    done = 0
    for step_tiles in FAR_STEP_TILES:
        width = TB * step_tiles
        n_steps = (n_common - done) // step_tiles

        def far_step(j, carry, done=done, width=width):
            far_all(pl.multiple_of(done * TB + j * width, TB), width)
            return carry
        lax.fori_loop(0, n_steps, far_step, 0)
        done = done + n_steps * step_tiles

    def own_far(first_chain, off0):
        for k in range(nq - first_chain):
            off = pl.multiple_of(off0 + k * TB, TB)
            cs = range(first_chain + k, nq)
            s = [scores(c, off, TB) for c in cs]
            for c, s_ in zip(cs, s):
                update(c, s_, off, TB)

    @pl.when(g > 0)
    def _():
        own_far(1, n_common * TB)

    @pl.when(g == 0)
    def _():
        own_far(2, 0)

    offs = [pl.multiple_of(jnp.maximum(nq * g + c - 1, 0) * TB, TB) for c in chains]
    s_near = [scores(c, offs[c], 2 * TB) + (bias_ref[jnp.where(g == 0, 1, 0)] if c == 0 else bias_ref[0])
              for c in chains]
    for c in chains:
        update(c, s_near[c], offs[c], 2 * TB)

    lam = _lambda(lqk_ref, lam_init)
    for c in chains:
        acc = a_s[c]
        on = acc[0:LANES, :] / acc[LANES:LANES + 1, :]
        o = (on[:, :TB] - lam * on[:, TB:]).T
        o_ref[c * TB:(c + 1) * TB, :] = _subln(o, subln_ref[...], lam_init).astype(BF16)


def _attn_prompt(q, k_all, v_all, l, bias, lqk, subln, lam_init, TB):
    m, d_diff = q.shape
    n_heads = d_diff // LANES
    n_blk = m // TB
    nq = ATTN_TILES_PER_STEP
    assert n_blk % nq == 0
    return pl.pallas_call(
        functools.partial(_attn_kernel, TB=TB, n_cast=n_blk, lam_init=lam_init),
        grid=(n_heads, n_blk // nq),
        in_specs=[
            pl.BlockSpec((nq * TB, LANES), lambda h, g: (g, h)),
            pl.BlockSpec((None, m, LANES), lambda h, g: (l, 0, h)),
            pl.BlockSpec((None, m, LANES), lambda h, g: (l, 0, h)),
            pl.BlockSpec((None, 2, 2 * TB, 2 * TB), lambda h, g: (h, 0, 0, 0)),
            pl.BlockSpec((None, 4, DIFF_HEAD), lambda h, g: (l, 0, 0)),
            pl.BlockSpec((None, 1, LANES), lambda h, g: (l, 0, 0)),
        ],
        out_specs=pl.BlockSpec((nq * TB, LANES), lambda h, g: (g, h)),
        out_shape=jax.ShapeDtypeStruct((m, d_diff), BF16),
        scratch_shapes=[pltpu.VMEM((m, LANES), BF16), pltpu.VMEM((2 * LANES, m), BF16),
                        pltpu.VMEM((nq, 1, 2 * TB), F32), pltpu.VMEM((nq, 2 * LANES, 2 * TB), F32)],
        compiler_params=_cparams(("arbitrary", "arbitrary")),
        name="attn_prompt",
    )(q, k_all, v_all, bias, lqk, subln)


SAMPLE_FAR_FRAMES = 384


def _attn_sample_kernel(q_ref, kn_ref, vn_ref, kc_ref, vc_ref, bias_ref, lqk_ref, subln_ref, o_ref,
                        kb, acc, *, T, n_heads, near, lam_init):
    past = kc_ref.shape[0]
    n_cols = 2 * T * n_heads
    fc = SAMPLE_FAR_FRAMES
    n_far = (past - near) // fc
    rows = fc * n_heads
    qs = jnp.concatenate(
        [m_ for h in range(n_heads) for m_ in _split_maps(q_ref[:, h * LANES:(h + 1) * LANES])], axis=0)
    sub = lax.broadcasted_iota(jnp.int32, (n_heads, n_cols), 0)
    col = lax.broadcasted_iota(jnp.int32, (n_heads, n_cols), 1)
    valid = (col >= sub * (2 * T)) & (col < (sub + 1) * (2 * T))

    def flat(x3):
        return x3.reshape(x3.shape[0] * n_heads, LANES)

    def scores(k_rows):
        s = _mm(k_rows, qs, _NT)
        return s.reshape(s.shape[0] // n_heads, n_heads, n_cols)

    near_k = jnp.concatenate([flat(kc_ref[past - near:past]), flat(kn_ref[...])], axis=0).astype(BF16)
    near_v = jnp.concatenate([flat(vc_ref[past - near:past]), flat(vn_ref[...])], axis=0).astype(BF16)
    s_near = scores(near_k) + bias_ref[...].reshape(near + T, n_heads, n_cols)

    def pass1(c, m3):
        f0 = pl.multiple_of(c * fc, fc)
        k_rows = flat(kc_ref[pl.ds(f0, fc)]).astype(BF16)
        kb[pl.ds(pl.multiple_of(c * rows, rows), rows), :] = k_rows
        return jnp.maximum(m3, jnp.max(scores(k_rows), axis=0))
    m3 = lax.fori_loop(0, n_far, pass1, jnp.max(s_near, axis=0))

    ones = jnp.ones((rows, LANES), BF16)

    def weights(s3):
        p = jnp.where(valid, jnp.exp(s3 - m3), 0.0)
        return p.reshape(p.shape[0] * n_heads, n_cols).astype(BF16)

    acc[...] = _mm(weights(s_near), jnp.concatenate([near_v, ones[:near_v.shape[0]]], axis=1), _TN)

    def pass2(c, carry):
        f0 = pl.multiple_of(c * fc, fc)
        p = weights(scores(kb[pl.ds(pl.multiple_of(c * rows, rows), rows), :]))
        v_aug = jnp.concatenate([flat(vc_ref[pl.ds(f0, fc)]).astype(BF16), ones], axis=1)
        acc[...] += _mm(p, v_aug, _TN)
        return carry
    lax.fori_loop(0, n_far, pass2, 0)

    lam = _lambda(lqk_ref, lam_init)
    for h in range(n_heads):
        a = acc[h * 2 * T:(h + 1) * 2 * T, :]
        on = a[:, 0:LANES] / a[:, LANES:2 * LANES]
        o = on[:T, :] - lam * on[T:, :]
        o_ref[:, h * LANES:(h + 1) * LANES] = _subln(o, subln_ref[...], lam_init).astype(BF16)


def _attn_sample(q, kn_all, vn_all, l, k_cache, v_cache, bias_t, lqk, subln, lam_init, n_seq, T, near):
    m, d_diff = q.shape
    n_heads = d_diff // LANES
    past = k_cache.shape[2]
    assert (past - near) % SAMPLE_FAR_FRAMES == 0
    cache_spec = pl.BlockSpec((None, None, past, n_heads, LANES), lambda b: (l, b, 0, 0, 0))
    new_spec = pl.BlockSpec((None, T, n_heads, LANES), lambda b: (l, b, 0, 0))
    return pl.pallas_call(
        functools.partial(_attn_sample_kernel, T=T, n_heads=n_heads, near=near, lam_init=lam_init),
        grid=(n_seq,),
        in_specs=[
            pl.BlockSpec((T, d_diff), lambda b: (b, 0)),
            new_spec,
            new_spec,
            cache_spec,
            cache_spec,
            pl.BlockSpec(bias_t.shape, lambda b: (0, 0)),
            pl.BlockSpec((None, 4, DIFF_HEAD), lambda b: (l, 0, 0)),
            pl.BlockSpec((None, 1, LANES), lambda b: (l, 0, 0)),
        ],
        out_specs=pl.BlockSpec((T, d_diff), lambda b: (b, 0)),
        out_shape=jax.ShapeDtypeStruct((m, d_diff), BF16),
        scratch_shapes=[pltpu.VMEM(((past - near) * n_heads, LANES), BF16),
                        pltpu.VMEM((2 * T * n_heads, 2 * LANES), F32)],
        compiler_params=_cparams(("arbitrary",)),
        name="attn_sample",
    )(q, kn_all, vn_all, k_cache, v_cache, bias_t, lqk, subln)


def _outproj_kernel(x_ref, yr_ref, yd_ref, gt_ref, wt_ref, wb_ref, o_ref):
    mixed = _mm(yr_ref[...], wt_ref[...]) + _mm(yd_ref[...], wb_ref[...])
    o_ref[...] = x_ref[...] + gt_ref[...] * mixed


def _outproj(x, yr, yd, gt, w_out_all, l, tm):
    m, d = x.shape
    dr = yr.shape[1]
    dd = yd.shape[1]
    gt_spec = (pl.BlockSpec((1, d), lambda i: (0, 0)) if gt.shape[0] == 1
               else pl.BlockSpec((tm, d), lambda i: (i, 0)))
    return pl.pallas_call(
        _outproj_kernel,
        grid=(m // tm,),
        in_specs=[
            pl.BlockSpec((tm, d), lambda i: (i, 0)),
            pl.BlockSpec((tm, dr), lambda i: (i, 0)),
            pl.BlockSpec((tm, dd), lambda i: (i, 0)),
            gt_spec,
            pl.BlockSpec((None, dr, d), lambda i: (l, 0, 0)),
            pl.BlockSpec((None, dd, d), lambda i: (l, dr // dd, 0)),
        ],
        out_specs=pl.BlockSpec((tm, d), lambda i: (i, 0)),
        out_shape=jax.ShapeDtypeStruct((m, d), F32),
        compiler_params=_cparams(("arbitrary",)),
        name="outproj",
    )(x, yr, yd, gt, w_out_all, w_out_all)


def _shift_rows(x, prev, n):
    if n % SUBLANES == 0:
        return jnp.concatenate([prev, x[:x.shape[0] - n, :]], axis=0)
    rolled = pltpu.roll(x, n, 0)
    row = lax.broadcasted_iota(jnp.int32, x.shape, 0)
    out = rolled
    for r in range(n):
        out = jnp.where(row == r, prev[r:r + 1, :], out)
    return out


def _ffn_kernel(x_ref, sh_ref, sc_ref, gt_ref, g_ref, wg_ref, wv_ref, cw_ref, cb_ref, c0_ref, wd_ref, gf_ref,
                o_ref, co_ref, h_scr, acc_scr, carry, *, B, n_j, final_norm):
    i = pl.program_id(0)
    j = pl.program_id(1)
    tm = x_ref.shape[0]

    @pl.when(j == 0)
    def _():
        xn = _rms(x_ref[...], g_ref[...], NORM_EPS)
        h_scr[...] = (xn * (1.0 + sc_ref[...]) + sh_ref[...]).astype(BF16)
        acc_scr[...] = jnp.zeros(acc_scr.shape, F32)

    @pl.when(i == 0)
    def _():
        carry[j, 0:2 * B, :] = c0_ref[...]

    h = h_scr[...]
    ug = _mm(h, wg_ref[...])
    uv = _mm(h, wv_ref[...])
    prev = carry[j, 0:2 * B, :]
    s1 = _shift_rows(ug, prev[B:2 * B, :], B)
    s2 = _shift_rows(ug, prev, 2 * B)
    cw = cw_ref[...]
    z = cb_ref[...] + s2 * cw[0:1, :] + s1 * cw[1:2, :] + ug * cw[2:3, :]
    act = 0.5 * z * (1.0 + lax.erf(z * (2.0 ** -0.5)))
    acc_scr[...] += _mm((act * uv).astype(BF16), wd_ref[...])
    last2 = ug[tm - 2 * B:tm, :]
    carry[j, 0:2 * B, :] = last2
    tf = last2.shape[1]
    co_ref[:, pl.ds(pl.multiple_of(j * tf, tf), tf)] = last2

    @pl.when(j == n_j - 1)
    def _():
        x2 = x_ref[...] + gt_ref[...] * acc_scr[...]
        if final_norm:
            x2 = _rms(x2, gf_ref[...], NORM_EPS)
        o_ref[...] = x2


def _ffn(x, sh, sc, gt, g_all, w_up_all, conv_w_all, conv_b_all, conv0, w_down_all, g_final, l, B, tm,
         final_norm, tf=512):
    m, d = x.shape
    d_ff = w_down_all.shape[1]
    n_j = d_ff // tf
    rows_c = max(SUBLANES, 2 * B)
    return pl.pallas_call(
        functools.partial(_ffn_kernel, B=B, n_j=n_j, final_norm=final_norm),
        grid=(m // tm, n_j),
        in_specs=[
            pl.BlockSpec((tm, d), lambda i, j: (i, 0)),
            _mod_spec(sh, tm, d),
            _mod_spec(sc, tm, d),
            _mod_spec(gt, tm, d),
            pl.BlockSpec((None, 1, d), lambda i, j: (l, 0, 0)),
            pl.BlockSpec((None, d, tf), lambda i, j: (l, 0, j)),
            pl.BlockSpec((None, d, tf), lambda i, j: (l, 0, n_j + j)),
            pl.BlockSpec((None, CONV_W, tf), lambda i, j: (l, 0, j)),
            pl.BlockSpec((None, 1, tf), lambda i, j: (l, 0, j)),
            pl.BlockSpec((2 * B, tf), lambda i, j: (0, j)),
            pl.BlockSpec((None, tf, d), lambda i, j: (l, j, 0)),
            pl.BlockSpec((1, d), lambda i, j: (0, 0)),
        ],
        out_specs=[
            pl.BlockSpec((tm, d), lambda i, j: (i, 0)),
            pl.BlockSpec((2 * B, d_ff), lambda i, j: (0, 0)),
        ],
        out_shape=[
            jax.ShapeDtypeStruct((m, d), F32),
            jax.ShapeDtypeStruct((2 * B, d_ff), F32),
        ],
        scratch_shapes=[pltpu.VMEM((tm, d), BF16), pltpu.VMEM((tm, d), F32),
                        pltpu.VMEM((n_j, rows_c, tf), F32)],
        compiler_params=_cparams(("arbitrary", "arbitrary")),
        name="conv_ffn",
    )(x, sh, sc, gt, g_all, w_up_all, w_up_all, conv_w_all, conv_b_all, conv0, w_down_all, g_final)


def _pad_cols(a, n):
    return jnp.pad(a, [(0, 0)] * (a.ndim - 1) + [(0, n)])


def _regroup_rwkv_cols(a, d_rwkv):
    o = 3 * d_rwkv
    return jnp.concatenate([
        a[..., :o],
        _pad_cols(a[..., o:o + DECAY_LORA], LORA_PAD - DECAY_LORA),
        _pad_cols(a[..., o + DECAY_LORA:o + DECAY_LORA + AAA_LORA], LORA_PAD - AAA_LORA),
        a[..., o + DECAY_LORA + AAA_LORA:],
    ], axis=-1)


def _ungroup_rwkv_cols(a, d_rwkv):
    o = 3 * d_rwkv
    return jnp.concatenate([
        a[..., :o],
        a[..., o:o + DECAY_LORA],
        a[..., o + LORA_PAD:o + LORA_PAD + AAA_LORA],
        a[..., o + 2 * LORA_PAD:],
    ], axis=-1)


def kernel(x_prompt, x_sample, c_prompt, c_sample, cache_k, cache_v, state_wkv, state_shift, state_conv, w_ada, b_ada, g_mix, g_ffn, w_in, w_out, rwkv_mu, rwkv_w0, rwkv_w2, rwkv_a0, rwkv_a2, rwkv_g2, rwkv_kk, rwkv_ka, rwkv_rk, rwkv_ln_w, rwkv_ln_b, diff_lq1, diff_lk1, diff_lq2, diff_lk2, diff_subln, rel_table, ffn_up, ffn_conv_w, ffn_conv_b, ffn_down, g_final):
    depth, d_model, _ = w_in.shape
    bp, seq, _ = x_prompt.shape
    bs, dseq, _ = x_sample.shape
    past = cache_k.shape[2]
    d_rwkv = rwkv_w0.shape[1]
    d_diff = d_model - d_rwkv
    n_dheads = d_diff // (2 * DIFF_HEAD)
    n_rheads = d_rwkv // RWKV_HEAD
    n_rwkv_cols = rwkv_mu.shape[1]
    n_rwkv_pad = n_rwkv_cols + 2 * LORA_PAD - DECAY_LORA - AAA_LORA
    d_ff = ffn_down.shape[1]
    assert bp == 1, "prompt path handles one sequence"
    TB = min(256, seq // 2)
    assert seq % (2 * TB) == 0 and TB % CHUNK == 0 and TB >= MAX_DISTANCE
    tm_p = min(512, seq)
    tm_in = min(1024, seq)
    m_s = bs * dseq

    w_in_b = jnp.concatenate([_regroup_rwkv_cols(w_in[..., :n_rwkv_cols], d_rwkv), w_in[..., n_rwkv_cols:]],
                             axis=-1).astype(BF16)
    mu_p = _regroup_rwkv_cols(rwkv_mu, d_rwkv)[:, None]
    w2_b = jnp.pad(rwkv_w2, ((0, 0), (0, LORA_PAD - DECAY_LORA), (0, 0))).astype(BF16)
    a2_b = jnp.pad(rwkv_a2, ((0, 0), (0, LORA_PAD - AAA_LORA), (0, 0))).astype(BF16)
    g2_b = rwkv_g2.astype(BF16)
    w_out_b = _cast_bf16(w_out)
    up_b = _cast_bf16(ffn_up)
    down_b = _cast_bf16(ffn_down)
    vec = lambda a: a.reshape(depth, 1, -1)
    wkv_args = (w2_b, a2_b, g2_b, vec(rwkv_w0), vec(rwkv_a0), vec(rwkv_kk), vec(rwkv_ka), vec(rwkv_rk),
                vec(rwkv_ln_w), vec(rwkv_ln_b))
    g_mix_v, g_ffn_v, subln_v, conv_b_v = vec(g_mix), vec(g_ffn), vec(diff_subln), vec(ffn_conv_b)
    gf = g_final[None]
    lqk = jnp.stack([diff_lq1, diff_lk1, diff_lq2, diff_lk2], axis=1)

    n_c = bp + bs
    n_c_pad = -(-n_c // SUBLANES) * SUBLANES
    c_rows = jnp.pad(jnp.concatenate([c_prompt, c_sample], axis=0), ((0, n_c_pad - n_c), (0, 0)))
    mod = _adaln_mod(c_rows, w_ada, b_ada).reshape(depth, n_c_pad, 6, d_model)

    ql = jnp.arange(TB)
    rel_diag = ql[None, :] - ql[:, None]
    allowed = (ql[None, :] // CHUNK) <= (ql[:, None] // CHUNK)
    bucket_diag = jnp.where(allowed, _t5_bucket(rel_diag), -1)
    bucket_prev = _t5_bucket(rel_diag - TB)
    windows = [jnp.concatenate([bucket_prev, bucket_diag], axis=1),
               jnp.concatenate([bucket_diag, jnp.full((TB, TB), -1)], axis=1)]
    bucket_p = jnp.concatenate([jnp.tile(w.T, (1, 2)) for w in windows], axis=0)
    bias_p = _bias_lookup(rel_table, bucket_p.astype(jnp.int32), FAR_BUCKET).reshape(n_dheads, 2, 2 * TB, 2 * TB)
    q_pos = past + jnp.arange(dseq)
    assert past >= MAX_DISTANCE
    k_pos = past - MAX_DISTANCE + jnp.arange(MAX_DISTANCE + dseq)
    bias_s = _bias_lookup(rel_table, _t5_bucket(k_pos[None, :] - q_pos[:, None]).astype(jnp.int32), FAR_BUCKET)
    bias_t = jnp.broadcast_to(bias_s.transpose(2, 0, 1)[:, None, :, None, :],
                              (MAX_DISTANCE + dseq, n_dheads, n_dheads, 2, dseq))
    bias_t = bias_t.reshape((MAX_DISTANCE + dseq) * n_dheads, n_dheads * 2 * dseq)

    xp = x_prompt.reshape(seq, d_model)
    xs = x_sample.reshape(m_s, d_model)
    zero_shift = jnp.zeros((1, 1, n_rwkv_pad), F32)
    zero_state = jnp.zeros((1, n_rheads, RWKV_HEAD, RWKV_HEAD), F32)
    zero_conv = jnp.zeros((2, d_ff), F32)
    shift_s_in = _regroup_rwkv_cols(state_shift, d_rwkv)
    outs_p = {k: [] for k in ("wkv", "shift", "conv")}
    outs_s = {k: [] for k in ("wkv", "shift", "conv")}
    kv_p = (jnp.zeros((depth, seq, d_diff), F32), jnp.zeros((depth, seq, d_diff), F32))
    kv_s = (jnp.zeros((depth, m_s, d_diff), F32), jnp.zeros((depth, m_s, d_diff), F32))

    def time_major(a):
        return a.reshape(bs, dseq, -1).swapaxes(0, 1).reshape(m_s, -1)

    def batch_major(a):
        return a.reshape(dseq, bs, -1).swapaxes(0, 1).reshape(m_s, -1)

    for l in range(depth):
        lam_init = 0.8 - 0.6 * math.exp(-0.3 * l)
        last = l == depth - 1

        mp = mod[l, 0:bp]
        sh1, sc1, gt1, sh2, sc2, gt2 = (mp[:, t] for t in range(6))
        p, q, *kv_p = _inproj(xp, sh1, sc1, g_mix_v, w_in_b, l, kv_p, n_rwkv_pad, d_diff, tm_in)
        y_r, wkv = _wkv(p, l, mu_p, zero_shift, *wkv_args, zero_state, 1, seq, min(64, seq))
        y_d = _attn_prompt(q, kv_p[0], kv_p[1], l, bias_p, lqk, subln_v, lam_init, TB)
        x1 = _outproj(xp, y_r, y_d, gt1, w_out_b, l, tm_p)
        xp, conv = _ffn(x1, sh2, sc2, gt2, g_ffn_v, up_b, ffn_conv_w, conv_b_v, zero_conv, down_b, gf, l,
                        1, tm_p, last)
        outs_p["wkv"].append(wkv)
        outs_p["shift"].append(_ungroup_rwkv_cols(p[seq - 1:seq], d_rwkv).reshape(bp, 1, n_rwkv_cols))
        outs_p["conv"].append(conv.reshape(bp, CONV_W - 1, d_ff))

        ms = mod[l, bp:bp + bs]
        rows_bm = jnp.repeat(ms, dseq, axis=0)
        rows_tm = jnp.tile(ms, (dseq, 1, 1))
        p, q, *kv_s = _inproj(xs, rows_bm[:, 0], rows_bm[:, 1], g_mix_v, w_in_b, l, kv_s, n_rwkv_pad, d_diff, m_s)
        y_r, wkv = _wkv(p, l, mu_p, shift_s_in[l], *wkv_args, state_wkv[l], bs, dseq, dseq)
        kn4, vn4 = (a.reshape(depth, m_s, n_dheads, 2 * DIFF_HEAD) for a in kv_s)
        y_d = _attn_sample(q, kn4, vn4, l, cache_k, cache_v, bias_t, lqk, subln_v, lam_init, bs, dseq,
                           MAX_DISTANCE)
        x1 = _outproj(xs, y_r, y_d, rows_bm[:, 2], w_out_b, l, m_s)
        conv0 = state_conv[l].swapaxes(0, 1).reshape((CONV_W - 1) * bs, d_ff)
        x2, conv = _ffn(time_major(x1), rows_tm[:, 3], rows_tm[:, 4], rows_tm[:, 5], g_ffn_v, up_b, ffn_conv_w,
                        conv_b_v, conv0, down_b, gf, l, bs, m_s, last)
        xs = batch_major(x2)
        outs_s["wkv"].append(wkv)
        p_last = p.reshape(bs, dseq, n_rwkv_pad)[:, dseq - 1:dseq]
        outs_s["shift"].append(_ungroup_rwkv_cols(p_last, d_rwkv))
        outs_s["conv"].append(conv.reshape(CONV_W - 1, bs, d_ff).swapaxes(0, 1))

    st = lambda xs_: jnp.stack(xs_)
    head_shape = (n_dheads, 2 * DIFF_HEAD)
    return (xp.reshape(bp, seq, d_model), xs.reshape(bs, dseq, d_model),
            kv_p[0].reshape(depth, bp, seq, *head_shape), kv_p[1].reshape(depth, bp, seq, *head_shape),
            st(outs_p["wkv"]), st(outs_p["shift"]), st(outs_p["conv"]),
            kv_s[0].reshape(depth, bs, dseq, *head_shape), kv_s[1].reshape(depth, bs, dseq, *head_shape),
            st(outs_s["wkv"]), st(outs_s["shift"]), st(outs_s["conv"]))
```

```python
import functools
import math

import jax
import jax.numpy as jnp
from jax import lax
from jax.experimental import pallas as pl
from jax.experimental.pallas import tpu as pltpu

F32 = jnp.float32
BF16 = jnp.bfloat16

CHUNK = 64
RWKV_HEAD = 64
DIFF_HEAD = 64
N_BUCKETS = 32
MAX_DISTANCE = 128
NORM_EPS = 1e-6
GN_EPS = 64e-5
SUBLN_EPS = 1e-5
DECAY_LORA = 96
AAA_LORA = 96
GATE_LORA = 256
LORA_PAD = 128
CONV_W = 3

LANES = 128
SUBLANES = 8
BF16_ROWS = 16
VMEM_LIMIT_BYTES = 56 * 1024 * 1024
MASK_VALUE = -1e30
WKV_PAIRS_PER_STEP = 8

_NN = (((1,), (0,)), ((), ()))
_NT = (((1,), (1,)), ((), ()))
_TN = (((0,), (0,)), ((), ()))


def _mm(a, b, dims=_NN):
    return lax.dot_general(a, b, dims, preferred_element_type=F32)


def _hi_lo(x):
    h = x.astype(BF16)
    return h, (x - h.astype(F32)).astype(BF16)


def _mm3(a, b, dims=_NN):
    ah, al = _hi_lo(a)
    bh, bl = _hi_lo(b)
    (ca,), (cb,) = dims[0]
    k = a.shape[ca]
    if (ca == 0 or k % LANES == 0) and (cb == 0 or k % LANES == 0):
        return _mm(jnp.concatenate([ah, al, ah], axis=ca), jnp.concatenate([bh, bh, bl], axis=cb), dims)
    return _mm(ah, bh, dims) + (_mm(al, bh, dims) + _mm(ah, bl, dims))


def _sel_rows(sel2, x):
    xh, xl = _hi_lo(x)
    return _mm(sel2, jnp.concatenate([xh, xl], axis=0))


def _sel_lanes(x, sel2):
    xh, xl = _hi_lo(x)
    return _mm(jnp.concatenate([xh, xl], axis=1), sel2)


def _cparams(sem):
    return pltpu.CompilerParams(dimension_semantics=sem, vmem_limit_bytes=VMEM_LIMIT_BYTES)


def _rms(x, g, eps):
    return x * lax.rsqrt(jnp.mean(x * x, axis=-1, keepdims=True) + eps) * g


def _cast_kernel(x_ref, o_ref):
    o_ref[...] = x_ref[...].astype(BF16)


def _cast_bf16(w, tr=512, tc=1024):
    depth, rows, cols = w.shape
    spec = pl.BlockSpec((None, tr, tc), lambda l, i, j: (l, i, j))
    return pl.pallas_call(
        _cast_kernel,
        grid=(depth, rows // tr, cols // tc),
        in_specs=[spec],
        out_specs=spec,
        out_shape=jax.ShapeDtypeStruct(w.shape, BF16),
        compiler_params=_cparams(("arbitrary", "arbitrary", "arbitrary")),
        name="cast_bf16",
    )(w)


def _mod_kernel(c_ref, w_ref, b_ref, o_ref):
    c = c_ref[...]
    s = (c * jax.nn.sigmoid(c)).astype(BF16)
    o_ref[...] = _mm(s, w_ref[...].astype(BF16)) + b_ref[...]


def _adaln_mod(c_rows, w_ada, b_ada, tn=1024):
    depth, d, n = w_ada.shape
    rows = c_rows.shape[0]
    return pl.pallas_call(
        _mod_kernel,
        grid=(depth, n // tn),
        in_specs=[
            pl.BlockSpec((rows, d), lambda l, j: (0, 0)),
            pl.BlockSpec((None, d, tn), lambda l, j: (l, 0, j)),
            pl.BlockSpec((None, 1, tn), lambda l, j: (l, 0, j)),
        ],
        out_specs=pl.BlockSpec((None, rows, tn), lambda l, j: (l, 0, j)),
        out_shape=jax.ShapeDtypeStruct((depth, rows, n), F32),
        compiler_params=_cparams(("arbitrary", "arbitrary")),
        name="adaln_mod",
    )(c_rows, w_ada, b_ada.reshape(depth, 1, n))


def _bias_kernel(tab_ref, bucket_ref, o_ref, *, far_bucket):
    h = pl.program_id(0)
    bucket = bucket_ref[...]
    far = tab_ref[far_bucket, h]
    acc = jnp.full(bucket.shape, MASK_VALUE, F32)
    for b in range(N_BUCKETS):
        acc = jnp.where(bucket == b, tab_ref[b, h] - far, acc)
    c = bucket.shape[1]
    for k in range(o_ref.shape[1] // c):
        o_ref[:, k * c:(k + 1) * c] = acc


def _bias_lookup(table, bucket, far_bucket, col_repeats=1):
    n_heads = table.shape[1]
    r, c = bucket.shape
    return pl.pallas_call(
        functools.partial(_bias_kernel, far_bucket=far_bucket),
        grid=(n_heads,),
        in_specs=[
            pl.BlockSpec(memory_space=pltpu.SMEM),
            pl.BlockSpec((r, c), lambda h: (0, 0)),
        ],
        out_specs=pl.BlockSpec((None, r, c * col_repeats), lambda h: (h, 0, 0)),
        out_shape=jax.ShapeDtypeStruct((n_heads, r, c * col_repeats), F32),
        compiler_params=_cparams(("arbitrary",)),
        name="bias_lookup",
    )(table, bucket)


def _t5_bucket(rel):
    nb = N_BUCKETS // 2
    max_exact = nb // 2
    n = jnp.abs(rel)
    nf = jnp.maximum(n, 1).astype(F32)
    large = max_exact + (jnp.log(nf / max_exact) / math.log(MAX_DISTANCE / max_exact) * (nb - max_exact)).astype(jnp.int32)
    large = jnp.minimum(large, nb - 1)
    return jnp.where(rel > 0, nb, 0) + jnp.where(n < max_exact, n, large)


FAR_BUCKET = N_BUCKETS // 2 - 1


def _inproj_kernel(x_ref, sh_ref, sc_ref, g_ref, w_ref, *rest, n_p, n_q):
    p_ref, q_ref, k_ref, v_ref, h_scr = rest[-5:]
    j = pl.program_id(1)

    @pl.when(j == 0)
    def _():
        xn = _rms(x_ref[...], g_ref[...], NORM_EPS)
        h_scr[...] = (xn * (1.0 + sc_ref[...]) + sh_ref[...]).astype(BF16)

    @pl.when(j < n_p)
    def _():
        p_ref[...] = _mm(h_scr[...], w_ref[...])

    @pl.when((j >= n_p) & (j < n_p + n_q))
    def _():
        q_ref[...] = (_mm(h_scr[...], w_ref[...]) * (DIFF_HEAD ** -0.5)).astype(BF16)

    @pl.when((j >= n_p + n_q) & (j < n_p + 2 * n_q))
    def _():
        k_ref[...] = _mm(h_scr[...], w_ref[...])

    @pl.when(j >= n_p + 2 * n_q)
    def _():
        v_ref[...] = _mm(h_scr[...], w_ref[...])


def _mod_spec(mod, tm, d):
    if mod.shape[0] == 1:
        return pl.BlockSpec((1, d), lambda i, j: (0, 0))
    return pl.BlockSpec((tm, d), lambda i, j: (i, 0))


def _inproj(x, sh, sc, g_all, w_all, l, kv_all, n_rwkv_pad, d_diff, tm, tn=512):
    m, d = x.shape
    depth, _, n = w_all.shape
    n_p, n_q = n_rwkv_pad // tn, d_diff // tn
    clip = lambda j, lo: jnp.clip(j - lo, 0, n_q - 1)
    in_specs = [
        pl.BlockSpec((tm, d), lambda i, j: (i, 0)),
        _mod_spec(sh, tm, d),
        _mod_spec(sc, tm, d),
        pl.BlockSpec((None, 1, d), lambda i, j: (l, 0, 0)),
        pl.BlockSpec((None, d, tn), lambda i, j: (l, 0, j)),
        pl.BlockSpec(memory_space=pl.ANY),
        pl.BlockSpec(memory_space=pl.ANY),
    ]
    args = [x, sh, sc, g_all, w_all, kv_all[0], kv_all[1]]
    aliases = {5: 2, 6: 3}
    return pl.pallas_call(
        functools.partial(_inproj_kernel, n_p=n_p, n_q=n_q),
        grid=(m // tm, n // tn),
        in_specs=in_specs,
        out_specs=[
            pl.BlockSpec((tm, tn), lambda i, j: (i, jnp.minimum(j, n_p - 1))),
            pl.BlockSpec((tm, tn), lambda i, j: (i, clip(j, n_p))),
            pl.BlockSpec((None, tm, tn), lambda i, j: (l, i, clip(j, n_p + n_q))),
            pl.BlockSpec((None, tm, tn), lambda i, j: (l, i, clip(j, n_p + 2 * n_q))),
        ],
        out_shape=[
            jax.ShapeDtypeStruct((m, n_rwkv_pad), F32),
            jax.ShapeDtypeStruct((m, d_diff), BF16),
            jax.ShapeDtypeStruct((depth, m, d_diff), F32),
            jax.ShapeDtypeStruct((depth, m, d_diff), F32),
        ],
        scratch_shapes=[pltpu.VMEM((tm, d), BF16)],
        input_output_aliases=aliases,
        compiler_params=_cparams(("arbitrary", "arbitrary")),
        name="inproj",
    )(*args)


def _wkv_kernel(pr_ref, pk_ref, pv_ref, pw_ref, pa_ref, pg_ref,
                mur_ref, muk_ref, muv_ref, muw_ref, mua_ref, mug_ref,
                s0r_ref, s0k_ref, s0v_ref, s0w_ref, s0a_ref, s0g_ref,
                w2_ref, a2_ref, g2_ref,
                w0_ref, a0_ref, kkp_ref, ka_ref, rk_ref, lnw_ref, lnb_ref,
                st0_ref,
                y_ref, st_ref,
                sbd, cr, ck, cv, cw, ca, cg, *, C, n_chunks, n_pairs):
    chunk = pl.program_id(2)
    hd = RWKV_HEAD
    zero_blk = jnp.zeros((hd, hd), F32)

    @pl.when(chunk == 0)
    def _():
        for c_ref, s_ref in ((cr, s0r_ref), (ck, s0k_ref), (cv, s0v_ref), (cw, s0w_ref), (ca, s0a_ref), (cg, s0g_ref)):
            c_ref[0:1, :] = s_ref[...]
        for pi in range(n_pairs):
            top = jnp.concatenate([st0_ref[2 * pi], zero_blk], axis=1)
            bot = jnp.concatenate([zero_blk, st0_ref[2 * pi + 1]], axis=1)
            sbd[pi] = jnp.concatenate([top, bot], axis=0)

    def tshift(x_ref, mu_ref, c_ref):
        x = x_ref[...]
        row = lax.broadcasted_iota(jnp.int32, x.shape, 0)
        prev = jnp.where(row == 0, c_ref[0:1, :], pltpu.roll(x, 1, 0))
        c_ref[0:1, :] = x[C - 1:C, :]
        return x + (prev - x) * mu_ref[...]

    xr = tshift(pr_ref, mur_ref, cr)
    xk = tshift(pk_ref, muk_ref, ck)
    xv = tshift(pv_ref, muv_ref, cv)
    xw = tshift(pw_ref, muw_ref, cw)
    xa = tshift(pa_ref, mua_ref, ca)
    xg = tshift(pg_ref, mug_ref, cg)

    lane = lax.broadcasted_iota(jnp.int32, (1, LANES), 1)
    head0 = lane < hd
    li = lax.broadcasted_iota(jnp.int32, (LANES, LANES), 0)
    lj = lax.broadcasted_iota(jnp.int32, (LANES, LANES), 1)
    same_head = (li < hd) == (lj < hd)
    li2 = lax.broadcasted_iota(jnp.int32, (2 * LANES, LANES), 0)
    lj2 = lax.broadcasted_iota(jnp.int32, (2 * LANES, LANES), 1)
    li2 = jnp.where(li2 < LANES, li2, li2 - LANES)
    seg2 = jnp.where((li2 < hd) == (lj2 < hd), 1.0, 0.0).astype(BF16)

    def segsum(x):
        return _sel_lanes(x, seg2)

    wlin = w0_ref[...] + _mm(jnp.tanh(xw).astype(BF16), w2_ref[...])
    z = -wlin
    w_log = -(jnp.maximum(z, 0.0) + jnp.log1p(jnp.exp(-jnp.abs(z)))) - 0.5
    ne = -jnp.exp(w_log)
    a = jax.nn.sigmoid(a0_ref[...] + _mm(xa.astype(BF16), a2_ref[...]))
    g = _mm(jax.nn.sigmoid(xg).astype(BF16), g2_ref[...])
    kk_raw = xk * kkp_ref[...]
    k2 = xk * (1.0 + (a - 1.0) * ka_ref[...])
    rk_prod = xr * k2 * rk_ref[...]

    ti = lax.broadcasted_iota(jnp.int32, (C, 2 * C), 0)
    tj = lax.broadcasted_iota(jnp.int32, (C, 2 * C), 1)
    tj = jnp.where(tj < C, tj, tj - C)
    tri2 = jnp.where(tj <= ti, 1.0, 0.0).astype(BF16)
    cum = _sel_rows(tri2, ne)
    gam = jnp.exp(cum)
    gprev = jnp.exp(cum - ne)
    ginv = jnp.exp(-cum)
    dte = jnp.exp(cum[C - 1:C, :] - cum)
    rg_all = xr * gam
    ks_all = k2 * ginv
    kd_all = k2 * dte
    lnw = lnw_ref[...]
    lnb = lnb_ref[...]

    def stack_masked(x):
        return jnp.concatenate([jnp.where(head0, x, 0.0), jnp.where(head0, 0.0, x)], axis=0)

    def stack_twice(x):
        return jnp.concatenate([x, x], axis=0)

    R = 2 * C
    ri = lax.broadcasted_iota(jnp.int32, (R, R), 0)
    ci = lax.broadcasted_iota(jnp.int32, (R, R), 1)
    same_blk = (ri < C) == (ci < C)
    tr = jnp.where(ri < C, ri, ri - C)
    tc = jnp.where(ci < C, ci, ci - C)
    strict = same_blk & (tc < tr)
    incl = same_blk & (tc <= tr)
    n_steps = max(1, int(math.log2(C)))
    inv_hd = 1.0 / hd

    pairs = range(n_pairs)
    sls = [slice(pi * LANES, (pi + 1) * LANES) for pi in pairs]
    xv_p = [xv[:, sl] for sl in sls]
    sums = [segsum(jnp.concatenate([kk_raw[:, sl] * kk_raw[:, sl], rk_prod[:, sl]], axis=0)) for sl in sls]
    kk = [kk_raw[:, sl] * lax.rsqrt(jnp.maximum(sm[:C, :], 1e-24)) for sl, sm in zip(sls, sums)]
    bonus = [sm[C:, :] * v for sm, v in zip(sums, xv_p)]
    kka = [k * a[:, sl] for k, sl in zip(kk, sls)]
    la = [stack_masked(-k * gprev[:, sl]) for k, sl in zip(kk, sls)]
    lr = [stack_masked(rg_all[:, sl]).astype(BF16) for sl in sls]
    rhs = [jnp.concatenate([stack_twice(ka_ * ginv[:, sl]), stack_twice(ks_all[:, sl])], axis=0)
           for ka_, sl in zip(kka, sls)]
    vst = [stack_masked(v) for v in xv_p]
    s_old = [sbd[pi] for pi in pairs]
    sc_a = [_mm3(l_, r_, _NT) for l_, r_ in zip(la, rhs)]
    sc_r = [_mm(l_, r_.astype(BF16), _NT) for l_, r_ in zip(lr, rhs)]
    ls_a = [_mm3(l_, s_, _NT) for l_, s_ in zip(la, s_old)]
    ls_r = [_mm(l_, s_.astype(BF16), _NT) for l_, s_ in zip(lr, s_old)]
    n_ab = [jnp.where(strict, s4[:, :R], 0.0) for s4 in sc_a]
    m_ak = [jnp.where(strict, s4[:, R:], 0.0) for s4 in sc_a]
    m_r = [jnp.concatenate([jnp.where(incl, s4[:, :R], 0.0), jnp.where(incl, s4[:, R:], 0.0)],
                           axis=1).astype(BF16) for s4 in sc_r]
    x_u = [l_ + _mm3(mk, v) for l_, mk, v in zip(ls_a, m_ak, vst)]
    pw = n_ab
    for step in range(n_steps):
        if step + 1 < n_steps:
            t = [_mm3(p_, jnp.concatenate([x_, p_], axis=1)) for p_, x_ in zip(pw, x_u)]
            x_u = [x_ + t_[:, :LANES] for x_, t_ in zip(x_u, t)]
            pw = [t_[:, LANES:] for t_ in t]
        else:
            x_u = [x_ + _mm3(p_, x_) for p_, x_ in zip(pw, x_u)]
    y_st = [l_ + _mm(mr, jnp.concatenate([x_, v], axis=0).astype(BF16))
            for l_, mr, x_, v in zip(ls_r, m_r, x_u, vst)]
    upd = [_mm3(jnp.concatenate([x_[:C, :] + x_[C:, :], v], axis=0),
                jnp.concatenate([ka_ * dte[:, sl], kd_all[:, sl]], axis=0), _TN)
           for x_, v, ka_, sl in zip(x_u, xv_p, kka, sls)]
    for pi in pairs:
        sbd[pi] = s_old[pi] * gam[C - 1:C, sls[pi]] + jnp.where(same_head, upd[pi], 0.0)

    y = [ys[:C, :] + ys[C:, :] for ys in y_st]
    yc = [y_ - segsum(y_) * inv_hd for y_ in y]
    y_var = [segsum(c_ * c_) * inv_hd for c_ in yc]
    for pi in pairs:
        sl = sls[pi]
        yn = yc[pi] * lax.rsqrt(y_var[pi] + GN_EPS) * lnw[:, sl] + lnb[:, sl]
        y_ref[:, sl] = ((yn + bonus[pi]) * g[:, sl]).astype(BF16)

    @pl.when(chunk == n_chunks - 1)
    def _():
        for pi in range(n_pairs):
            s_new = sbd[pi]
            st_ref[2 * pi] = s_new[0:hd, 0:hd]
            st_ref[2 * pi + 1] = s_new[hd:2 * hd, hd:2 * hd]


def _wkv(p, l, mu, shift0, w2, a2, g2, w0, a0, kkp, ka, rk, lnw, lnb, state0, n_seq, seq_len, C):
    m, n_pad = p.shape
    d_rwkv = w0.shape[-1]
    n_heads = d_rwkv // RWKV_HEAD
    n_pairs = WKV_PAIRS_PER_STEP
    wd = n_pairs * LANES
    n_groups = d_rwkv // wd
    n_chunks = seq_len // C
    nb = n_groups
    blk_w = 3 * d_rwkv // LANES
    blk_a = blk_w + 1
    blk_g = (blk_a + 1) * LANES // GATE_LORA

    row = lambda s, h, c: s * n_chunks + c
    seg_specs = [
        pl.BlockSpec((C, wd), lambda s, h, c: (row(s, h, c), h)),
        pl.BlockSpec((C, wd), lambda s, h, c: (row(s, h, c), nb + h)),
        pl.BlockSpec((C, wd), lambda s, h, c: (row(s, h, c), 2 * nb + h)),
        pl.BlockSpec((C, LANES), lambda s, h, c: (row(s, h, c), blk_w)),
        pl.BlockSpec((C, LANES), lambda s, h, c: (row(s, h, c), blk_a)),
        pl.BlockSpec((C, GATE_LORA), lambda s, h, c: (row(s, h, c), blk_g)),
    ]
    mu_specs = [
        pl.BlockSpec((None, 1, wd), lambda s, h, c: (l, 0, h)),
        pl.BlockSpec((None, 1, wd), lambda s, h, c: (l, 0, nb + h)),
        pl.BlockSpec((None, 1, wd), lambda s, h, c: (l, 0, 2 * nb + h)),
        pl.BlockSpec((None, 1, LANES), lambda s, h, c: (l, 0, blk_w)),
        pl.BlockSpec((None, 1, LANES), lambda s, h, c: (l, 0, blk_a)),
        pl.BlockSpec((None, 1, GATE_LORA), lambda s, h, c: (l, 0, blk_g)),
    ]
    s0_specs = [
        pl.BlockSpec((None, 1, wd), lambda s, h, c: (s, 0, h)),
        pl.BlockSpec((None, 1, wd), lambda s, h, c: (s, 0, nb + h)),
        pl.BlockSpec((None, 1, wd), lambda s, h, c: (s, 0, 2 * nb + h)),
        pl.BlockSpec((None, 1, LANES), lambda s, h, c: (s, 0, blk_w)),
        pl.BlockSpec((None, 1, LANES), lambda s, h, c: (s, 0, blk_a)),
        pl.BlockSpec((None, 1, GATE_LORA), lambda s, h, c: (s, 0, blk_g)),
    ]
    lora_specs = [
        pl.BlockSpec((None, LORA_PAD, wd), lambda s, h, c: (l, 0, h)),
        pl.BlockSpec((None, LORA_PAD, wd), lambda s, h, c: (l, 0, h)),
        pl.BlockSpec((None, GATE_LORA, wd), lambda s, h, c: (l, 0, h)),
    ]
    vec_spec = pl.BlockSpec((None, 1, wd), lambda s, h, c: (l, 0, h))
    state_spec = pl.BlockSpec((None, 2 * n_pairs, RWKV_HEAD, RWKV_HEAD), lambda s, h, c: (s, h, 0, 0))
    y, st = pl.pallas_call(
        functools.partial(_wkv_kernel, C=C, n_chunks=n_chunks, n_pairs=n_pairs),
        grid=(n_seq, n_groups, n_chunks),
        in_specs=seg_specs + mu_specs + s0_specs + lora_specs + [vec_spec] * 7 + [state_spec],
        out_specs=[
            pl.BlockSpec((C, wd), lambda s, h, c: (row(s, h, c), h)),
            state_spec,
        ],
        out_shape=[
            jax.ShapeDtypeStruct((m, d_rwkv), BF16),
            jax.ShapeDtypeStruct((n_seq, n_heads, RWKV_HEAD, RWKV_HEAD), F32),
        ],
        scratch_shapes=[pltpu.VMEM((n_pairs, LANES, LANES), F32)]
        + [pltpu.VMEM((SUBLANES, wd), F32)] * 3 + [pltpu.VMEM((SUBLANES, LANES), F32)] * 2
        + [pltpu.VMEM((SUBLANES, GATE_LORA), F32)],
        compiler_params=_cparams(("arbitrary", "arbitrary", "arbitrary")),
        name="wkv",
    )(p, p, p, p, p, p, mu, mu, mu, mu, mu, mu,
      shift0, shift0, shift0, shift0, shift0, shift0,
      w2, a2, g2, w0, a0, kkp, ka, rk, lnw, lnb, state0)
    return y, st


def _lambda(lqk_ref, lam_init):
    t = lqk_ref[...]
    s1 = jnp.sum(t[0:1, :] * t[1:2, :], axis=-1, keepdims=True)
    s2 = jnp.sum(t[2:3, :] * t[3:4, :], axis=-1, keepdims=True)
    return jnp.exp(s1) - jnp.exp(s2) + lam_init


def _split_maps(q):
    lane = lax.broadcasted_iota(jnp.int32, q.shape, 1)
    zero = jnp.zeros_like(q)
    return jnp.where(lane < DIFF_HEAD, q, zero), jnp.where(lane < DIFF_HEAD, zero, q)


def _subln(o, subln, lam_init):
    return _rms(o, subln, SUBLN_EPS) * (1.0 - lam_init)


FAR_STEP_TILES = (4, 1)
ATTN_TILES_PER_STEP = 4


def _attn_kernel(q_ref, k_ref, v_ref, bias_ref, lqk_ref, subln_ref, o_ref,
                 kb, vt, m_s, a_s, *, TB, n_cast, lam_init):
    g = pl.program_id(1)

    @pl.when(g == 0)
    def _():
        vt[LANES:, :] = jnp.ones((vt.shape[0] - LANES, vt.shape[1]), BF16)

        def cast(c, carry):
            off = pl.multiple_of(c * TB, TB)
            kb[pl.ds(off, TB), :] = k_ref[pl.ds(off, TB), :].astype(BF16)
            vt[0:LANES, pl.ds(off, TB)] = v_ref[pl.ds(off, TB), :].T.astype(BF16)
            return carry
        lax.fori_loop(0, n_cast, cast, 0)

    nq = ATTN_TILES_PER_STEP
    chains = range(nq)
    qs = [jnp.concatenate(_split_maps(q_ref[c * TB:(c + 1) * TB, :]), axis=0) for c in chains]
    m_s[...] = jnp.full(m_s.shape, MASK_VALUE, F32)
    a_s[...] = jnp.zeros(a_s.shape, F32)

    def scores(c, off, width):
        return _mm(kb[pl.ds(off, width), :], qs[c], _NT)

    def update(c, s, off, width):
        m_prev = m_s[c]
        m_new = jnp.maximum(m_prev, jnp.max(s, axis=0, keepdims=True))
        alpha = jnp.exp(m_prev - m_new)
        p = jnp.exp(s - m_new).astype(BF16)
        a_s[c] = alpha * a_s[c] + _mm(vt[:, pl.ds(off, width)], p)
        m_s[c] = m_new

    def far_all(off, width):
        s = [scores(c, off, width) for c in chains]
        for c in chains:
            update(c, s[c], off, width)

    n_common = jnp.maximum(nq * g - 1, 0)
    done = 0
    for step_tiles in FAR_STEP_TILES:
        width = TB * step_tiles
        n_steps = (n_common - done) // step_tiles

        def far_step(j, carry, done=done, width=width):
            far_all(pl.multiple_of(done * TB + j * width, TB), width)
            return carry
        lax.fori_loop(0, n_steps, far_step, 0)
        done = done + n_steps * step_tiles

    def own_far(first_chain, off0):
        for k in range(nq - first_chain):
            off = pl.multiple_of(off0 + k * TB, TB)
            cs = range(first_chain + k, nq)
            s = [scores(c, off, TB) for c in cs]
            for c, s_ in zip(cs, s):
                update(c, s_, off, TB)

    @pl.when(g > 0)
    def _():
        own_far(1, n_common * TB)

    @pl.when(g == 0)
    def _():
        own_far(2, 0)

    offs = [pl.multiple_of(jnp.maximum(nq * g + c - 1, 0) * TB, TB) for c in chains]
    s_near = [scores(c, offs[c], 2 * TB) + (bias_ref[jnp.where(g == 0, 1, 0)] if c == 0 else bias_ref[0])
              for c in chains]
    for c in chains:
        update(c, s_near[c], offs[c], 2 * TB)

    lam = _lambda(lqk_ref, lam_init)
    for c in chains:
        acc = a_s[c]
        on = acc[0:LANES, :] / acc[LANES:LANES + 1, :]
        o = (on[:, :TB] - lam * on[:, TB:]).T
        o_ref[c * TB:(c + 1) * TB, :] = _subln(o, subln_ref[...], lam_init).astype(BF16)


def _attn_prompt(q, k_all, v_all, l, bias, lqk, subln, lam_init, TB):
    m, d_diff = q.shape
    n_heads = d_diff // LANES
    n_blk = m // TB
    nq = ATTN_TILES_PER_STEP
    assert n_blk % nq == 0
    return pl.pallas_call(
        functools.partial(_attn_kernel, TB=TB, n_cast=n_blk, lam_init=lam_init),
        grid=(n_heads, n_blk // nq),
        in_specs=[
            pl.BlockSpec((nq * TB, LANES), lambda h, g: (g, h)),
            pl.BlockSpec((None, m, LANES), lambda h, g: (l, 0, h)),
            pl.BlockSpec((None, m, LANES), lambda h, g: (l, 0, h)),
            pl.BlockSpec((None, 2, 2 * TB, 2 * TB), lambda h, g: (h, 0, 0, 0)),
            pl.BlockSpec((None, 4, DIFF_HEAD), lambda h, g: (l, 0, 0)),
            pl.BlockSpec((None, 1, LANES), lambda h, g: (l, 0, 0)),
        ],
        out_specs=pl.BlockSpec((nq * TB, LANES), lambda h, g: (g, h)),
        out_shape=jax.ShapeDtypeStruct((m, d_diff), BF16),
        scratch_shapes=[pltpu.VMEM((m, LANES), BF16), pltpu.VMEM((LANES + BF16_ROWS, m), BF16),
                        pltpu.VMEM((nq, 1, 2 * TB), F32), pltpu.VMEM((nq, LANES + BF16_ROWS, 2 * TB), F32)],
        compiler_params=_cparams(("arbitrary", "arbitrary")),
        name="attn_prompt",
    )(q, k_all, v_all, bias, lqk, subln)


SAMPLE_FAR_FRAMES = 384


def _attn_sample_kernel(q_ref, kn_ref, vn_ref, kc_ref, vc_ref, bias_ref, lqk_ref, subln_ref, o_ref,
                        kb, acc, *, T, n_heads, near, lam_init):
    past = kc_ref.shape[0]
    n_cols = 2 * T * n_heads
    fc = SAMPLE_FAR_FRAMES
    n_far = (past - near) // fc
    rows = fc * n_heads
    qs = jnp.concatenate(
        [m_ for h in range(n_heads) for m_ in _split_maps(q_ref[:, h * LANES:(h + 1) * LANES])], axis=0)
    sub = lax.broadcasted_iota(jnp.int32, (n_heads, n_cols), 0)
    col = lax.broadcasted_iota(jnp.int32, (n_heads, n_cols), 1)
    valid = (col >= sub * (2 * T)) & (col < (sub + 1) * (2 * T))

    def flat(x3):
        return x3.reshape(x3.shape[0] * n_heads, LANES)

    def scores(k_rows):
        s = _mm(k_rows, qs, _NT)
        return s.reshape(s.shape[0] // n_heads, n_heads, n_cols)

    near_k = jnp.concatenate([flat(kc_ref[past - near:past]), flat(kn_ref[...])], axis=0).astype(BF16)
    near_v = jnp.concatenate([flat(vc_ref[past - near:past]), flat(vn_ref[...])], axis=0).astype(BF16)
    s_near = scores(near_k) + bias_ref[...].reshape(near + T, n_heads, n_cols)

    def pass1(c, m3):
        f0 = pl.multiple_of(c * fc, fc)
        k_rows = flat(kc_ref[pl.ds(f0, fc)]).astype(BF16)
        kb[pl.ds(pl.multiple_of(c * rows, rows), rows), :] = k_rows
        return jnp.maximum(m3, jnp.max(scores(k_rows), axis=0))
    m3 = lax.fori_loop(0, n_far, pass1, jnp.max(s_near, axis=0))

    ones = jnp.ones((rows, LANES), BF16)

    def weights(s3):
        p = jnp.where(valid, jnp.exp(s3 - m3), 0.0)
        return p.reshape(p.shape[0] * n_heads, n_cols).astype(BF16)

    acc[...] = _mm(weights(s_near), jnp.concatenate([near_v, ones[:near_v.shape[0]]], axis=1), _TN)

    def pass2(c, carry):
        f0 = pl.multiple_of(c * fc, fc)
        p = weights(scores(kb[pl.ds(pl.multiple_of(c * rows, rows), rows), :]))
        v_aug = jnp.concatenate([flat(vc_ref[pl.ds(f0, fc)]).astype(BF16), ones], axis=1)
        acc[...] += _mm(p, v_aug, _TN)
        return carry
    lax.fori_loop(0, n_far, pass2, 0)

    lam = _lambda(lqk_ref, lam_init)
    for h in range(n_heads):
        a = acc[h * 2 * T:(h + 1) * 2 * T, :]
        on = a[:, 0:LANES] / a[:, LANES:2 * LANES]
        o = on[:T, :] - lam * on[T:, :]
        o_ref[:, h * LANES:(h + 1) * LANES] = _subln(o, subln_ref[...], lam_init).astype(BF16)


def _attn_sample(q, kn_all, vn_all, l, k_cache, v_cache, bias_t, lqk, subln, lam_init, n_seq, T, near):
    m, d_diff = q.shape
    n_heads = d_diff // LANES
    past = k_cache.shape[2]
    assert (past - near) % SAMPLE_FAR_FRAMES == 0
    cache_spec = pl.BlockSpec((None, None, past, n_heads, LANES), lambda b: (l, b, 0, 0, 0))
    new_spec = pl.BlockSpec((None, T, n_heads, LANES), lambda b: (l, b, 0, 0))
    return pl.pallas_call(
        functools.partial(_attn_sample_kernel, T=T, n_heads=n_heads, near=near, lam_init=lam_init),
        grid=(n_seq,),
        in_specs=[
            pl.BlockSpec((T, d_diff), lambda b: (b, 0)),
            new_spec,
            new_spec,
            cache_spec,
            cache_spec,
            pl.BlockSpec(bias_t.shape, lambda b: (0, 0)),
            pl.BlockSpec((None, 4, DIFF_HEAD), lambda b: (l, 0, 0)),
            pl.BlockSpec((None, 1, LANES), lambda b: (l, 0, 0)),
        ],
        out_specs=pl.BlockSpec((T, d_diff), lambda b: (b, 0)),
        out_shape=jax.ShapeDtypeStruct((m, d_diff), BF16),
        scratch_shapes=[pltpu.VMEM(((past - near) * n_heads, LANES), BF16),
                        pltpu.VMEM((2 * T * n_heads, 2 * LANES), F32)],
        compiler_params=_cparams(("arbitrary",)),
        name="attn_sample",
    )(q, kn_all, vn_all, k_cache, v_cache, bias_t, lqk, subln)


def _outproj_kernel(x_ref, yr_ref, yd_ref, gt_ref, wt_ref, wb_ref, o_ref):
    mixed = _mm(yr_ref[...], wt_ref[...]) + _mm(yd_ref[...], wb_ref[...])
    o_ref[...] = x_ref[...] + gt_ref[...] * mixed


def _outproj(x, yr, yd, gt, w_out_all, l, tm):
    m, d = x.shape
    dr = yr.shape[1]
    dd = yd.shape[1]
    gt_spec = (pl.BlockSpec((1, d), lambda i: (0, 0)) if gt.shape[0] == 1
               else pl.BlockSpec((tm, d), lambda i: (i, 0)))
    return pl.pallas_call(
        _outproj_kernel,
        grid=(m // tm,),
        in_specs=[
            pl.BlockSpec((tm, d), lambda i: (i, 0)),
            pl.BlockSpec((tm, dr), lambda i: (i, 0)),
            pl.BlockSpec((tm, dd), lambda i: (i, 0)),
            gt_spec,
            pl.BlockSpec((None, dr, d), lambda i: (l, 0, 0)),
            pl.BlockSpec((None, dd, d), lambda i: (l, dr // dd, 0)),
        ],
        out_specs=pl.BlockSpec((tm, d), lambda i: (i, 0)),
        out_shape=jax.ShapeDtypeStruct((m, d), F32),
        compiler_params=_cparams(("arbitrary",)),
        name="outproj",
    )(x, yr, yd, gt, w_out_all, w_out_all)


def _shift_rows(x, prev, n):
    if n % SUBLANES == 0:
        return jnp.concatenate([prev, x[:x.shape[0] - n, :]], axis=0)
    rolled = pltpu.roll(x, n, 0)
    row = lax.broadcasted_iota(jnp.int32, x.shape, 0)
    out = rolled
    for r in range(n):
        out = jnp.where(row == r, prev[r:r + 1, :], out)
    return out


def _ffn_kernel(x_ref, sh_ref, sc_ref, gt_ref, g_ref, wg_ref, wv_ref, cw_ref, cb_ref, c0_ref, wd_ref, gf_ref,
                o_ref, co_ref, h_scr, acc_scr, carry, *, B, n_j, final_norm):
    i = pl.program_id(0)
    j = pl.program_id(1)
    tm = x_ref.shape[0]

    @pl.when(j == 0)
    def _():
        xn = _rms(x_ref[...], g_ref[...], NORM_EPS)
        h_scr[...] = (xn * (1.0 + sc_ref[...]) + sh_ref[...]).astype(BF16)
        acc_scr[...] = jnp.zeros(acc_scr.shape, F32)

    @pl.when(i == 0)
    def _():
        carry[j, 0:2 * B, :] = c0_ref[...]

    h = h_scr[...]
    ug = _mm(h, wg_ref[...])
    uv = _mm(h, wv_ref[...])
    prev = carry[j, 0:2 * B, :]
    s1 = _shift_rows(ug, prev[B:2 * B, :], B)
    s2 = _shift_rows(ug, prev, 2 * B)
    cw = cw_ref[...]
    z = cb_ref[...] + s2 * cw[0:1, :] + s1 * cw[1:2, :] + ug * cw[2:3, :]
    act = 0.5 * z * (1.0 + lax.erf(z * (2.0 ** -0.5)))
    acc_scr[...] += _mm((act * uv).astype(BF16), wd_ref[...])
    last2 = ug[tm - 2 * B:tm, :]
    carry[j, 0:2 * B, :] = last2
    tf = last2.shape[1]
    co_ref[:, pl.ds(pl.multiple_of(j * tf, tf), tf)] = last2

    @pl.when(j == n_j - 1)
    def _():
        x2 = x_ref[...] + gt_ref[...] * acc_scr[...]
        if final_norm:
            x2 = _rms(x2, gf_ref[...], NORM_EPS)
        o_ref[...] = x2


def _ffn(x, sh, sc, gt, g_all, w_up_all, conv_w_all, conv_b_all, conv0, w_down_all, g_final, l, B, tm,
         final_norm, tf=512):
    m, d = x.shape
    d_ff = w_down_all.shape[1]
    n_j = d_ff // tf
    rows_c = max(SUBLANES, 2 * B)
    return pl.pallas_call(
        functools.partial(_ffn_kernel, B=B, n_j=n_j, final_norm=final_norm),
        grid=(m // tm, n_j),
        in_specs=[
            pl.BlockSpec((tm, d), lambda i, j: (i, 0)),
            _mod_spec(sh, tm, d),
            _mod_spec(sc, tm, d),
            _mod_spec(gt, tm, d),
            pl.BlockSpec((None, 1, d), lambda i, j: (l, 0, 0)),
            pl.BlockSpec((None, d, tf), lambda i, j: (l, 0, j)),
            pl.BlockSpec((None, d, tf), lambda i, j: (l, 0, n_j + j)),
            pl.BlockSpec((None, CONV_W, tf), lambda i, j: (l, 0, j)),
            pl.BlockSpec((None, 1, tf), lambda i, j: (l, 0, j)),
            pl.BlockSpec((2 * B, tf), lambda i, j: (0, j)),
            pl.BlockSpec((None, tf, d), lambda i, j: (l, j, 0)),
            pl.BlockSpec((1, d), lambda i, j: (0, 0)),
        ],
        out_specs=[
            pl.BlockSpec((tm, d), lambda i, j: (i, 0)),
            pl.BlockSpec((2 * B, d_ff), lambda i, j: (0, 0)),
        ],
        out_shape=[
            jax.ShapeDtypeStruct((m, d), F32),
            jax.ShapeDtypeStruct((2 * B, d_ff), F32),
        ],
        scratch_shapes=[pltpu.VMEM((tm, d), BF16), pltpu.VMEM((tm, d), F32),
                        pltpu.VMEM((n_j, rows_c, tf), F32)],
        compiler_params=_cparams(("arbitrary", "arbitrary")),
        name="conv_ffn",
    )(x, sh, sc, gt, g_all, w_up_all, w_up_all, conv_w_all, conv_b_all, conv0, w_down_all, g_final)


def _pad_cols(a, n):
    return jnp.pad(a, [(0, 0)] * (a.ndim - 1) + [(0, n)])


def _regroup_rwkv_cols(a, d_rwkv):
    o = 3 * d_rwkv
    return jnp.concatenate([
        a[..., :o],
        _pad_cols(a[..., o:o + DECAY_LORA], LORA_PAD - DECAY_LORA),
        _pad_cols(a[..., o + DECAY_LORA:o + DECAY_LORA + AAA_LORA], LORA_PAD - AAA_LORA),
        a[..., o + DECAY_LORA + AAA_LORA:],
    ], axis=-1)


def _ungroup_rwkv_cols(a, d_rwkv):
    o = 3 * d_rwkv
    return jnp.concatenate([
        a[..., :o],
        a[..., o:o + DECAY_LORA],
        a[..., o + LORA_PAD:o + LORA_PAD + AAA_LORA],
        a[..., o + 2 * LORA_PAD:],
    ], axis=-1)


def kernel(x_prompt, x_sample, c_prompt, c_sample, cache_k, cache_v, state_wkv, state_shift, state_conv, w_ada, b_ada, g_mix, g_ffn, w_in, w_out, rwkv_mu, rwkv_w0, rwkv_w2, rwkv_a0, rwkv_a2, rwkv_g2, rwkv_kk, rwkv_ka, rwkv_rk, rwkv_ln_w, rwkv_ln_b, diff_lq1, diff_lk1, diff_lq2, diff_lk2, diff_subln, rel_table, ffn_up, ffn_conv_w, ffn_conv_b, ffn_down, g_final):
    depth, d_model, _ = w_in.shape
    bp, seq, _ = x_prompt.shape
    bs, dseq, _ = x_sample.shape
    past = cache_k.shape[2]
    d_rwkv = rwkv_w0.shape[1]
    d_diff = d_model - d_rwkv
    n_dheads = d_diff // (2 * DIFF_HEAD)
    n_rheads = d_rwkv // RWKV_HEAD
    n_rwkv_cols = rwkv_mu.shape[1]
    n_rwkv_pad = n_rwkv_cols + 2 * LORA_PAD - DECAY_LORA - AAA_LORA
    d_ff = ffn_down.shape[1]
    assert bp == 1, "prompt path handles one sequence"
    TB = min(256, seq // 2)
    assert seq % (2 * TB) == 0 and TB % CHUNK == 0 and TB >= MAX_DISTANCE
    tm_p = min(512, seq)
    tm_in = min(1024, seq)
    m_s = bs * dseq

    w_in_b = jnp.concatenate([_regroup_rwkv_cols(w_in[..., :n_rwkv_cols], d_rwkv), w_in[..., n_rwkv_cols:]],
                             axis=-1).astype(BF16)
    mu_p = _regroup_rwkv_cols(rwkv_mu, d_rwkv)[:, None]
    w2_b = jnp.pad(rwkv_w2, ((0, 0), (0, LORA_PAD - DECAY_LORA), (0, 0))).astype(BF16)
    a2_b = jnp.pad(rwkv_a2, ((0, 0), (0, LORA_PAD - AAA_LORA), (0, 0))).astype(BF16)
    g2_b = rwkv_g2.astype(BF16)
    w_out_b = _cast_bf16(w_out)
    up_b = _cast_bf16(ffn_up)
    down_b = _cast_bf16(ffn_down)
    vec = lambda a: a.reshape(depth, 1, -1)
    wkv_args = (w2_b, a2_b, g2_b, vec(rwkv_w0), vec(rwkv_a0), vec(rwkv_kk), vec(rwkv_ka), vec(rwkv_rk),
                vec(rwkv_ln_w), vec(rwkv_ln_b))
    g_mix_v, g_ffn_v, subln_v, conv_b_v = vec(g_mix), vec(g_ffn), vec(diff_subln), vec(ffn_conv_b)
    gf = g_final[None]
    lqk = jnp.stack([diff_lq1, diff_lk1, diff_lq2, diff_lk2], axis=1)

    n_c = bp + bs
    n_c_pad = -(-n_c // SUBLANES) * SUBLANES
    c_rows = jnp.pad(jnp.concatenate([c_prompt, c_sample], axis=0), ((0, n_c_pad - n_c), (0, 0)))
    mod = _adaln_mod(c_rows, w_ada, b_ada).reshape(depth, n_c_pad, 6, d_model)

    ql = jnp.arange(TB)
    rel_diag = ql[None, :] - ql[:, None]
    allowed = (ql[None, :] // CHUNK) <= (ql[:, None] // CHUNK)
    bucket_diag = jnp.where(allowed, _t5_bucket(rel_diag), -1)
    bucket_prev = _t5_bucket(rel_diag - TB)
    windows = [jnp.concatenate([bucket_prev, bucket_diag], axis=1),
               jnp.concatenate([bucket_diag, jnp.full((TB, TB), -1)], axis=1)]
    bucket_p = jnp.concatenate([w.T for w in windows], axis=0)
    bias_p = _bias_lookup(rel_table, bucket_p.astype(jnp.int32), FAR_BUCKET, col_repeats=2)
    bias_p = bias_p.reshape(n_dheads, 2, 2 * TB, 2 * TB)
    q_pos = past + jnp.arange(dseq)
    assert past >= MAX_DISTANCE
    k_pos = past - MAX_DISTANCE + jnp.arange(MAX_DISTANCE + dseq)
    bias_s = _bias_lookup(rel_table, _t5_bucket(k_pos[None, :] - q_pos[:, None]).astype(jnp.int32), FAR_BUCKET)
    bias_t = jnp.broadcast_to(bias_s.transpose(2, 0, 1)[:, None, :, None, :],
                              (MAX_DISTANCE + dseq, n_dheads, n_dheads, 2, dseq))
    bias_t = bias_t.reshape((MAX_DISTANCE + dseq) * n_dheads, n_dheads * 2 * dseq)

    xp = x_prompt.reshape(seq, d_model)
    xs = x_sample.reshape(m_s, d_model)
    zero_shift = jnp.zeros((1, 1, n_rwkv_pad), F32)
    zero_state = jnp.zeros((1, n_rheads, RWKV_HEAD, RWKV_HEAD), F32)
    zero_conv = jnp.zeros((2, d_ff), F32)
    shift_s_in = _regroup_rwkv_cols(state_shift, d_rwkv)
    outs_p = {k: [] for k in ("wkv", "shift", "conv")}
    outs_s = {k: [] for k in ("wkv", "shift", "conv")}
    kv_p = (jnp.zeros((depth, seq, d_diff), F32), jnp.zeros((depth, seq, d_diff), F32))
    kv_s = (jnp.zeros((depth, m_s, d_diff), F32), jnp.zeros((depth, m_s, d_diff), F32))

    def time_major(a):
        return a.reshape(bs, dseq, -1).swapaxes(0, 1).reshape(m_s, -1)

    def batch_major(a):
        return a.reshape(dseq, bs, -1).swapaxes(0, 1).reshape(m_s, -1)

    for l in range(depth):
        lam_init = 0.8 - 0.6 * math.exp(-0.3 * l)
        last = l == depth - 1

        mp = mod[l, 0:bp]
        sh1, sc1, gt1, sh2, sc2, gt2 = (mp[:, t] for t in range(6))
        p, q, *kv_p = _inproj(xp, sh1, sc1, g_mix_v, w_in_b, l, kv_p, n_rwkv_pad, d_diff, tm_in)
        y_r, wkv = _wkv(p, l, mu_p, zero_shift, *wkv_args, zero_state, 1, seq, min(64, seq))
        y_d = _attn_prompt(q, kv_p[0], kv_p[1], l, bias_p, lqk, subln_v, lam_init, TB)
        x1 = _outproj(xp, y_r, y_d, gt1, w_out_b, l, tm_p)
        xp, conv = _ffn(x1, sh2, sc2, gt2, g_ffn_v, up_b, ffn_conv_w, conv_b_v, zero_conv, down_b, gf, l,
                        1, tm_p, last)
        outs_p["wkv"].append(wkv)
        outs_p["shift"].append(_ungroup_rwkv_cols(p[seq - 1:seq], d_rwkv).reshape(bp, 1, n_rwkv_cols))
        outs_p["conv"].append(conv.reshape(bp, CONV_W - 1, d_ff))

        ms = mod[l, bp:bp + bs]
        rows_bm = jnp.repeat(ms, dseq, axis=0)
        rows_tm = jnp.tile(ms, (dseq, 1, 1))
        p, q, *kv_s = _inproj(xs, rows_bm[:, 0], rows_bm[:, 1], g_mix_v, w_in_b, l, kv_s, n_rwkv_pad, d_diff, m_s)
        y_r, wkv = _wkv(p, l, mu_p, shift_s_in[l], *wkv_args, state_wkv[l], bs, dseq, dseq)
        kn4, vn4 = (a.reshape(depth, m_s, n_dheads, 2 * DIFF_HEAD) for a in kv_s)
        y_d = _attn_sample(q, kn4, vn4, l, cache_k, cache_v, bias_t, lqk, subln_v, lam_init, bs, dseq,
                           MAX_DISTANCE)
        x1 = _outproj(xs, y_r, y_d, rows_bm[:, 2], w_out_b, l, m_s)
        conv0 = state_conv[l].swapaxes(0, 1).reshape((CONV_W - 1) * bs, d_ff)
        x2, conv = _ffn(time_major(x1), rows_tm[:, 3], rows_tm[:, 4], rows_tm[:, 5], g_ffn_v, up_b, ffn_conv_w,
                        conv_b_v, conv0, down_b, gf, l, bs, m_s, last)
        xs = batch_major(x2)
        outs_s["wkv"].append(wkv)
        p_last = p.reshape(bs, dseq, n_rwkv_pad)[:, dseq - 1:dseq]
        outs_s["shift"].append(_ungroup_rwkv_cols(p_last, d_rwkv))
        outs_s["conv"].append(conv.reshape(CONV_W - 1, bs, d_ff).swapaxes(0, 1))

    st = lambda xs_: jnp.stack(xs_)
    head_shape = (n_dheads, 2 * DIFF_HEAD)
    return (xp.reshape(bp, seq, d_model), xs.reshape(bs, dseq, d_model),
            kv_p[0].reshape(depth, bp, seq, *head_shape), kv_p[1].reshape(depth, bp, seq, *head_shape),
            st(outs_p["wkv"]), st(outs_p["shift"]), st(outs_p["conv"]),
            kv_s[0].reshape(depth, bs, dseq, *head_shape), kv_s[1].reshape(depth, bs, dseq, *head_shape),
            st(outs_s["wkv"]), st(outs_s["shift"]), st(outs_s["conv"]))
```

```python
import functools
import math

import jax
import jax.numpy as jnp
from jax import lax
from jax.experimental import pallas as pl
from jax.experimental.pallas import tpu as pltpu

F32 = jnp.float32
BF16 = jnp.bfloat16

CHUNK = 64
RWKV_HEAD = 64
DIFF_HEAD = 64
N_BUCKETS = 32
MAX_DISTANCE = 128
NORM_EPS = 1e-6
GN_EPS = 64e-5
SUBLN_EPS = 1e-5
DECAY_LORA = 96
AAA_LORA = 96
GATE_LORA = 256
LORA_PAD = 128
CONV_W = 3

LANES = 128
SUBLANES = 8
BF16_ROWS = 16
VMEM_LIMIT_BYTES = 56 * 1024 * 1024
MASK_VALUE = -1e30
WKV_PAIRS_PER_STEP = 8

_NN = (((1,), (0,)), ((), ()))
_NT = (((1,), (1,)), ((), ()))
_TN = (((0,), (0,)), ((), ()))


def _mm(a, b, dims=_NN):
    return lax.dot_general(a, b, dims, preferred_element_type=F32)


def _hi_lo(x):
    h = x.astype(BF16)
    return h, (x - h.astype(F32)).astype(BF16)


def _mm3(a, b, dims=_NN):
    ah, al = _hi_lo(a)
    bh, bl = _hi_lo(b)
    (ca,), (cb,) = dims[0]
    k = a.shape[ca]
    if (ca == 0 or k % LANES == 0) and (cb == 0 or k % LANES == 0):
        return _mm(jnp.concatenate([ah, al, ah], axis=ca), jnp.concatenate([bh, bh, bl], axis=cb), dims)
    return _mm(ah, bh, dims) + (_mm(al, bh, dims) + _mm(ah, bl, dims))


def _sel_rows(sel2, x):
    xh, xl = _hi_lo(x)
    return _mm(sel2, jnp.concatenate([xh, xl], axis=0))


def _sel_lanes(x, sel2):
    xh, xl = _hi_lo(x)
    return _mm(jnp.concatenate([xh, xl], axis=1), sel2)


def _cparams(sem):
    return pltpu.CompilerParams(dimension_semantics=sem, vmem_limit_bytes=VMEM_LIMIT_BYTES)


def _rms(x, g, eps):
    return x * lax.rsqrt(jnp.mean(x * x, axis=-1, keepdims=True) + eps) * g


def _cast_kernel(x_ref, o_ref):
    o_ref[...] = x_ref[...].astype(BF16)


CAST_BLOCK_BYTES = 8 * 1024 * 1024


def _cast_bf16(w):
    depth, rows, cols = w.shape
    tc = 1024 if cols % 1024 == 0 else cols
    tr = max(t for t in range(BF16_ROWS, rows + 1, BF16_ROWS)
             if rows % t == 0 and t * tc * 4 <= CAST_BLOCK_BYTES)
    spec = pl.BlockSpec((None, tr, tc), lambda l, i, j: (l, i, j))
    return pl.pallas_call(
        _cast_kernel,
        grid=(depth, rows // tr, cols // tc),
        in_specs=[spec],
        out_specs=spec,
        out_shape=jax.ShapeDtypeStruct(w.shape, BF16),
        compiler_params=_cparams(("arbitrary", "arbitrary", "arbitrary")),
        name="cast_bf16",
    )(w)


def _mod_kernel(c_ref, w_ref, b_ref, o_ref):
    c = c_ref[...]
    s = (c * jax.nn.sigmoid(c)).astype(BF16)
    o_ref[...] = _mm(s, w_ref[...].astype(BF16)) + b_ref[...]


def _adaln_mod(c_rows, w_ada, b_ada, tn=1024):
    depth, d, n = w_ada.shape
    rows = c_rows.shape[0]
    return pl.pallas_call(
        _mod_kernel,
        grid=(depth, n // tn),
        in_specs=[
            pl.BlockSpec((rows, d), lambda l, j: (0, 0)),
            pl.BlockSpec((None, d, tn), lambda l, j: (l, 0, j)),
            pl.BlockSpec((None, 1, tn), lambda l, j: (l, 0, j)),
        ],
        out_specs=pl.BlockSpec((None, rows, tn), lambda l, j: (l, 0, j)),
        out_shape=jax.ShapeDtypeStruct((depth, rows, n), F32),
        compiler_params=_cparams(("arbitrary", "arbitrary")),
        name="adaln_mod",
    )(c_rows, w_ada, b_ada.reshape(depth, 1, n))


def _bias_kernel(tab_ref, bucket_ref, o_ref, *, far_bucket):
    h = pl.program_id(0)
    bucket = bucket_ref[...]
    far = tab_ref[far_bucket, h]
    acc = jnp.full(bucket.shape, MASK_VALUE, F32)
    for b in range(N_BUCKETS):
        acc = jnp.where(bucket == b, tab_ref[b, h] - far, acc)
    c = bucket.shape[1]
    for k in range(o_ref.shape[1] // c):
        o_ref[:, k * c:(k + 1) * c] = acc


def _bias_lookup(table, bucket, far_bucket, col_repeats=1):
    n_heads = table.shape[1]
    r, c = bucket.shape
    return pl.pallas_call(
        functools.partial(_bias_kernel, far_bucket=far_bucket),
        grid=(n_heads,),
        in_specs=[
            pl.BlockSpec(memory_space=pltpu.SMEM),
            pl.BlockSpec((r, c), lambda h: (0, 0)),
        ],
        out_specs=pl.BlockSpec((None, r, c * col_repeats), lambda h: (h, 0, 0)),
        out_shape=jax.ShapeDtypeStruct((n_heads, r, c * col_repeats), F32),
        compiler_params=_cparams(("arbitrary",)),
        name="bias_lookup",
    )(table, bucket)


def _t5_bucket(rel):
    nb = N_BUCKETS // 2
    max_exact = nb // 2
    n = jnp.abs(rel)
    nf = jnp.maximum(n, 1).astype(F32)
    large = max_exact + (jnp.log(nf / max_exact) / math.log(MAX_DISTANCE / max_exact) * (nb - max_exact)).astype(jnp.int32)
    large = jnp.minimum(large, nb - 1)
    return jnp.where(rel > 0, nb, 0) + jnp.where(n < max_exact, n, large)


FAR_BUCKET = N_BUCKETS // 2 - 1


def _inproj_kernel(x_ref, sh_ref, sc_ref, g_ref, w_ref, *rest, n_p, n_q):
    p_ref, q_ref, k_ref, v_ref, h_scr = rest[-5:]
    j = pl.program_id(1)

    @pl.when(j == 0)
    def _():
        xn = _rms(x_ref[...], g_ref[...], NORM_EPS)
        h_scr[...] = (xn * (1.0 + sc_ref[...]) + sh_ref[...]).astype(BF16)

    @pl.when(j < n_p)
    def _():
        p_ref[...] = _mm(h_scr[...], w_ref[...])

    @pl.when((j >= n_p) & (j < n_p + n_q))
    def _():
        q_ref[...] = (_mm(h_scr[...], w_ref[...]) * (DIFF_HEAD ** -0.5)).astype(BF16)

    @pl.when((j >= n_p + n_q) & (j < n_p + 2 * n_q))
    def _():
        k_ref[...] = _mm(h_scr[...], w_ref[...])

    @pl.when(j >= n_p + 2 * n_q)
    def _():
        v_ref[...] = _mm(h_scr[...], w_ref[...])


def _mod_spec(mod, tm, d):
    if mod.shape[0] == 1:
        return pl.BlockSpec((1, d), lambda i, j: (0, 0))
    return pl.BlockSpec((tm, d), lambda i, j: (i, 0))


def _inproj(x, sh, sc, g_all, w_all, l, kv_all, n_rwkv_pad, d_diff, tm, tn=512):
    m, d = x.shape
    depth, _, n = w_all.shape
    n_p, n_q = n_rwkv_pad // tn, d_diff // tn
    clip = lambda j, lo: jnp.clip(j - lo, 0, n_q - 1)
    in_specs = [
        pl.BlockSpec((tm, d), lambda i, j: (i, 0)),
        _mod_spec(sh, tm, d),
        _mod_spec(sc, tm, d),
        pl.BlockSpec((None, 1, d), lambda i, j: (l, 0, 0)),
        pl.BlockSpec((None, d, tn), lambda i, j: (l, 0, j)),
        pl.BlockSpec(memory_space=pl.ANY),
        pl.BlockSpec(memory_space=pl.ANY),
    ]
    args = [x, sh, sc, g_all, w_all, kv_all[0], kv_all[1]]
    aliases = {5: 2, 6: 3}
    return pl.pallas_call(
        functools.partial(_inproj_kernel, n_p=n_p, n_q=n_q),
        grid=(m // tm, n // tn),
        in_specs=in_specs,
        out_specs=[
            pl.BlockSpec((tm, tn), lambda i, j: (i, jnp.minimum(j, n_p - 1))),
            pl.BlockSpec((tm, tn), lambda i, j: (i, clip(j, n_p))),
            pl.BlockSpec((None, tm, tn), lambda i, j: (l, i, clip(j, n_p + n_q))),
            pl.BlockSpec((None, tm, tn), lambda i, j: (l, i, clip(j, n_p + 2 * n_q))),
        ],
        out_shape=[
            jax.ShapeDtypeStruct((m, n_rwkv_pad), F32),
            jax.ShapeDtypeStruct((m, d_diff), BF16),
            jax.ShapeDtypeStruct((depth, m, d_diff), F32),
            jax.ShapeDtypeStruct((depth, m, d_diff), F32),
        ],
        scratch_shapes=[pltpu.VMEM((tm, d), BF16)],
        input_output_aliases=aliases,
        compiler_params=_cparams(("arbitrary", "arbitrary")),
        name="inproj",
    )(*args)


def _wkv_kernel(pr_ref, pk_ref, pv_ref, pw_ref, pa_ref, pg_ref,
                mur_ref, muk_ref, muv_ref, muw_ref, mua_ref, mug_ref,
                s0r_ref, s0k_ref, s0v_ref, s0w_ref, s0a_ref, s0g_ref,
                w2_ref, a2_ref, g2_ref,
                w0_ref, a0_ref, kkp_ref, ka_ref, rk_ref, lnw_ref, lnb_ref,
                st0_ref,
                y_ref, st_ref,
                sbd, cr, ck, cv, cw, ca, cg, *, C, n_chunks, n_pairs):
    chunk = pl.program_id(2)
    hd = RWKV_HEAD
    zero_blk = jnp.zeros((hd, hd), F32)

    @pl.when(chunk == 0)
    def _():
        for c_ref, s_ref in ((cr, s0r_ref), (ck, s0k_ref), (cv, s0v_ref), (cw, s0w_ref), (ca, s0a_ref), (cg, s0g_ref)):
            c_ref[0:1, :] = s_ref[...]
        for pi in range(n_pairs):
            top = jnp.concatenate([st0_ref[2 * pi], zero_blk], axis=1)
            bot = jnp.concatenate([zero_blk, st0_ref[2 * pi + 1]], axis=1)
            sbd[pi] = jnp.concatenate([top, bot], axis=0)

    def tshift(x_ref, mu_ref, c_ref):
        x = x_ref[...]
        row = lax.broadcasted_iota(jnp.int32, x.shape, 0)
        prev = jnp.where(row == 0, c_ref[0:1, :], pltpu.roll(x, 1, 0))
        c_ref[0:1, :] = x[C - 1:C, :]
        return x + (prev - x) * mu_ref[...]

    xr = tshift(pr_ref, mur_ref, cr)
    xk = tshift(pk_ref, muk_ref, ck)
    xv = tshift(pv_ref, muv_ref, cv)
    xw = tshift(pw_ref, muw_ref, cw)
    xa = tshift(pa_ref, mua_ref, ca)
    xg = tshift(pg_ref, mug_ref, cg)

    lane = lax.broadcasted_iota(jnp.int32, (1, LANES), 1)
    head0 = lane < hd
    li = lax.broadcasted_iota(jnp.int32, (LANES, LANES), 0)
    lj = lax.broadcasted_iota(jnp.int32, (LANES, LANES), 1)
    same_head = (li < hd) == (lj < hd)
    li2 = lax.broadcasted_iota(jnp.int32, (2 * LANES, LANES), 0)
    lj2 = lax.broadcasted_iota(jnp.int32, (2 * LANES, LANES), 1)
    li2 = jnp.where(li2 < LANES, li2, li2 - LANES)
    seg2 = jnp.where((li2 < hd) == (lj2 < hd), 1.0, 0.0).astype(BF16)

    def segsum(x):
        return _sel_lanes(x, seg2)

    wlin = w0_ref[...] + _mm(jnp.tanh(xw).astype(BF16), w2_ref[...])
    z = -wlin
    w_log = -(jnp.maximum(z, 0.0) + jnp.log1p(jnp.exp(-jnp.abs(z)))) - 0.5
    ne = -jnp.exp(w_log)
    a = jax.nn.sigmoid(a0_ref[...] + _mm(xa.astype(BF16), a2_ref[...]))
    g = _mm(jax.nn.sigmoid(xg).astype(BF16), g2_ref[...])
    kk_raw = xk * kkp_ref[...]
    k2 = xk * (1.0 + (a - 1.0) * ka_ref[...])
    rk_prod = xr * k2 * rk_ref[...]

    ti = lax.broadcasted_iota(jnp.int32, (C, 2 * C), 0)
    tj = lax.broadcasted_iota(jnp.int32, (C, 2 * C), 1)
    tj = jnp.where(tj < C, tj, tj - C)
    tri2 = jnp.where(tj <= ti, 1.0, 0.0).astype(BF16)
    cum = _sel_rows(tri2, ne)
    gam = jnp.exp(cum)
    gprev = jnp.exp(cum - ne)
    ginv = jnp.exp(-cum)
    dte = jnp.exp(cum[C - 1:C, :] - cum)
    rg_all = xr * gam
    ks_all = k2 * ginv
    kd_all = k2 * dte
    lnw = lnw_ref[...]
    lnb = lnb_ref[...]

    def stack_masked(x):
        return jnp.concatenate([jnp.where(head0, x, 0.0), jnp.where(head0, 0.0, x)], axis=0)

    def stack_twice(x):
        return jnp.concatenate([x, x], axis=0)

    R = 2 * C
    ri = lax.broadcasted_iota(jnp.int32, (R, R), 0)
    ci = lax.broadcasted_iota(jnp.int32, (R, R), 1)
    same_blk = (ri < C) == (ci < C)
    tr = jnp.where(ri < C, ri, ri - C)
    tc = jnp.where(ci < C, ci, ci - C)
    strict = same_blk & (tc < tr)
    incl = same_blk & (tc <= tr)
    n_steps = max(1, int(math.log2(C)))
    inv_hd = 1.0 / hd

    pairs = range(n_pairs)
    sls = [slice(pi * LANES, (pi + 1) * LANES) for pi in pairs]
    xv_p = [xv[:, sl] for sl in sls]
    sums = [segsum(jnp.concatenate([kk_raw[:, sl] * kk_raw[:, sl], rk_prod[:, sl]], axis=0)) for sl in sls]
    kk = [kk_raw[:, sl] * lax.rsqrt(jnp.maximum(sm[:C, :], 1e-24)) for sl, sm in zip(sls, sums)]
    bonus = [sm[C:, :] * v for sm, v in zip(sums, xv_p)]
    kka = [k * a[:, sl] for k, sl in zip(kk, sls)]
    la = [stack_masked(-k * gprev[:, sl]) for k, sl in zip(kk, sls)]
    lr = [stack_masked(rg_all[:, sl]).astype(BF16) for sl in sls]
    rhs = [jnp.concatenate([stack_twice(ka_ * ginv[:, sl]), stack_twice(ks_all[:, sl])], axis=0)
           for ka_, sl in zip(kka, sls)]
    vst = [stack_masked(v) for v in xv_p]
    s_old = [sbd[pi] for pi in pairs]
    sc_a = [_mm3(l_, r_, _NT) for l_, r_ in zip(la, rhs)]
    sc_r = [_mm(l_, r_.astype(BF16), _NT) for l_, r_ in zip(lr, rhs)]
    ls_a = [_mm3(l_, s_, _NT) for l_, s_ in zip(la, s_old)]
    ls_r = [_mm(l_, s_.astype(BF16), _NT) for l_, s_ in zip(lr, s_old)]
    n_ab = [jnp.where(strict, s4[:, :R], 0.0) for s4 in sc_a]
    m_ak = [jnp.where(strict, s4[:, R:], 0.0) for s4 in sc_a]
    m_r = [jnp.concatenate([jnp.where(incl, s4[:, :R], 0.0), jnp.where(incl, s4[:, R:], 0.0)],
                           axis=1).astype(BF16) for s4 in sc_r]
    x_u = [l_ + _mm3(mk, v) for l_, mk, v in zip(ls_a, m_ak, vst)]
    pw = n_ab
    for step in range(n_steps):
        if step + 1 < n_steps:
            t = [_mm3(p_, jnp.concatenate([x_, p_], axis=1)) for p_, x_ in zip(pw, x_u)]
            x_u = [x_ + t_[:, :LANES] for x_, t_ in zip(x_u, t)]
            pw = [t_[:, LANES:] for t_ in t]
        else:
            x_u = [x_ + _mm3(p_, x_) for p_, x_ in zip(pw, x_u)]
    y_st = [l_ + _mm(mr, jnp.concatenate([x_, v], axis=0).astype(BF16))
            for l_, mr, x_, v in zip(ls_r, m_r, x_u, vst)]
    upd = [_mm3(jnp.concatenate([x_[:C, :] + x_[C:, :], v], axis=0),
                jnp.concatenate([ka_ * dte[:, sl], kd_all[:, sl]], axis=0), _TN)
           for x_, v, ka_, sl in zip(x_u, xv_p, kka, sls)]
    for pi in pairs:
        sbd[pi] = s_old[pi] * gam[C - 1:C, sls[pi]] + jnp.where(same_head, upd[pi], 0.0)

    y = [ys[:C, :] + ys[C:, :] for ys in y_st]
    yc = [y_ - segsum(y_) * inv_hd for y_ in y]
    y_var = [segsum(c_ * c_) * inv_hd for c_ in yc]
    for pi in pairs:
        sl = sls[pi]
        yn = yc[pi] * lax.rsqrt(y_var[pi] + GN_EPS) * lnw[:, sl] + lnb[:, sl]
        y_ref[:, sl] = ((yn + bonus[pi]) * g[:, sl]).astype(BF16)

    @pl.when(chunk == n_chunks - 1)
    def _():
        for pi in range(n_pairs):
            s_new = sbd[pi]
            st_ref[2 * pi] = s_new[0:hd, 0:hd]
            st_ref[2 * pi + 1] = s_new[hd:2 * hd, hd:2 * hd]


def _wkv(p, l, mu, shift0, w2, a2, g2, w0, a0, kkp, ka, rk, lnw, lnb, state0, n_seq, seq_len, C):
    m, n_pad = p.shape
    d_rwkv = w0.shape[-1]
    n_heads = d_rwkv // RWKV_HEAD
    n_pairs = WKV_PAIRS_PER_STEP
    wd = n_pairs * LANES
    n_groups = d_rwkv // wd
    n_chunks = seq_len // C
    nb = n_groups
    blk_w = 3 * d_rwkv // LANES
    blk_a = blk_w + 1
    blk_g = (blk_a + 1) * LANES // GATE_LORA

    row = lambda s, h, c: s * n_chunks + c
    seg_specs = [
        pl.BlockSpec((C, wd), lambda s, h, c: (row(s, h, c), h)),
        pl.BlockSpec((C, wd), lambda s, h, c: (row(s, h, c), nb + h)),
        pl.BlockSpec((C, wd), lambda s, h, c: (row(s, h, c), 2 * nb + h)),
        pl.BlockSpec((C, LANES), lambda s, h, c: (row(s, h, c), blk_w)),
        pl.BlockSpec((C, LANES), lambda s, h, c: (row(s, h, c), blk_a)),
        pl.BlockSpec((C, GATE_LORA), lambda s, h, c: (row(s, h, c), blk_g)),
    ]
    mu_specs = [
        pl.BlockSpec((None, 1, wd), lambda s, h, c: (l, 0, h)),
        pl.BlockSpec((None, 1, wd), lambda s, h, c: (l, 0, nb + h)),
        pl.BlockSpec((None, 1, wd), lambda s, h, c: (l, 0, 2 * nb + h)),
        pl.BlockSpec((None, 1, LANES), lambda s, h, c: (l, 0, blk_w)),
        pl.BlockSpec((None, 1, LANES), lambda s, h, c: (l, 0, blk_a)),
        pl.BlockSpec((None, 1, GATE_LORA), lambda s, h, c: (l, 0, blk_g)),
    ]
    s0_specs = [
        pl.BlockSpec((None, 1, wd), lambda s, h, c: (s, 0, h)),
        pl.BlockSpec((None, 1, wd), lambda s, h, c: (s, 0, nb + h)),
        pl.BlockSpec((None, 1, wd), lambda s, h, c: (s, 0, 2 * nb + h)),
        pl.BlockSpec((None, 1, LANES), lambda s, h, c: (s, 0, blk_w)),
        pl.BlockSpec((None, 1, LANES), lambda s, h, c: (s, 0, blk_a)),
        pl.BlockSpec((None, 1, GATE_LORA), lambda s, h, c: (s, 0, blk_g)),
    ]
    lora_specs = [
        pl.BlockSpec((None, LORA_PAD, wd), lambda s, h, c: (l, 0, h)),
        pl.BlockSpec((None, LORA_PAD, wd), lambda s, h, c: (l, 0, h)),
        pl.BlockSpec((None, GATE_LORA, wd), lambda s, h, c: (l, 0, h)),
    ]
    vec_spec = pl.BlockSpec((None, 1, wd), lambda s, h, c: (l, 0, h))
    state_spec = pl.BlockSpec((None, 2 * n_pairs, RWKV_HEAD, RWKV_HEAD), lambda s, h, c: (s, h, 0, 0))
    y, st = pl.pallas_call(
        functools.partial(_wkv_kernel, C=C, n_chunks=n_chunks, n_pairs=n_pairs),
        grid=(n_seq, n_groups, n_chunks),
        in_specs=seg_specs + mu_specs + s0_specs + lora_specs + [vec_spec] * 7 + [state_spec],
        out_specs=[
            pl.BlockSpec((C, wd), lambda s, h, c: (row(s, h, c), h)),
            state_spec,
        ],
        out_shape=[
            jax.ShapeDtypeStruct((m, d_rwkv), BF16),
            jax.ShapeDtypeStruct((n_seq, n_heads, RWKV_HEAD, RWKV_HEAD), F32),
        ],
        scratch_shapes=[pltpu.VMEM((n_pairs, LANES, LANES), F32)]
        + [pltpu.VMEM((SUBLANES, wd), F32)] * 3 + [pltpu.VMEM((SUBLANES, LANES), F32)] * 2
        + [pltpu.VMEM((SUBLANES, GATE_LORA), F32)],
        compiler_params=_cparams(("arbitrary", "arbitrary", "arbitrary")),
        name="wkv",
    )(p, p, p, p, p, p, mu, mu, mu, mu, mu, mu,
      shift0, shift0, shift0, shift0, shift0, shift0,
      w2, a2, g2, w0, a0, kkp, ka, rk, lnw, lnb, state0)
    return y, st


def _lambda(lqk_ref, lam_init):
    t = lqk_ref[...]
    s1 = jnp.sum(t[0:1, :] * t[1:2, :], axis=-1, keepdims=True)
    s2 = jnp.sum(t[2:3, :] * t[3:4, :], axis=-1, keepdims=True)
    return jnp.exp(s1) - jnp.exp(s2) + lam_init


def _split_maps(q):
    lane = lax.broadcasted_iota(jnp.int32, q.shape, 1)
    zero = jnp.zeros_like(q)
    return jnp.where(lane < DIFF_HEAD, q, zero), jnp.where(lane < DIFF_HEAD, zero, q)


def _subln(o, subln, lam_init):
    return _rms(o, subln, SUBLN_EPS) * (1.0 - lam_init)


FAR_STEP_TILES = (4, 1)
ATTN_TILES_PER_STEP = 4


def _attn_kernel(q_ref, k_ref, v_ref, bias_ref, lqk_ref, subln_ref, o_ref,
                 kb, vt, m_s, a_s, *, TB, n_cast, lam_init):
    g = pl.program_id(1)

    @pl.when(g == 0)
    def _():
        vt[LANES:, :] = jnp.ones((vt.shape[0] - LANES, vt.shape[1]), BF16)

        def cast(c, carry):
            off = pl.multiple_of(c * TB, TB)
            kb[pl.ds(off, TB), :] = k_ref[pl.ds(off, TB), :].astype(BF16)
            vt[0:LANES, pl.ds(off, TB)] = v_ref[pl.ds(off, TB), :].T.astype(BF16)
            return carry
        lax.fori_loop(0, n_cast, cast, 0)

    nq = ATTN_TILES_PER_STEP
    chains = range(nq)
    qs = [jnp.concatenate(_split_maps(q_ref[c * TB:(c + 1) * TB, :]), axis=0) for c in chains]
    m_s[...] = jnp.full(m_s.shape, MASK_VALUE, F32)
    a_s[...] = jnp.zeros(a_s.shape, F32)

    def scores(c, off, width):
        return _mm(kb[pl.ds(off, width), :], qs[c], _NT)

    def update(c, s, off, width):
        m_prev = m_s[c]
        m_new = jnp.maximum(m_prev, jnp.max(s, axis=0, keepdims=True))
        alpha = jnp.exp(m_prev - m_new)
        p = jnp.exp(s - m_new).astype(BF16)
        a_s[c] = alpha * a_s[c] + _mm(vt[:, pl.ds(off, width)], p)
        m_s[c] = m_new

    def far_all(off, width):
        s = [scores(c, off, width) for c in chains]
        for c in chains:
            update(c, s[c], off, width)

    n_common = jnp.maximum(nq * g - 1, 0)
    done = 0
    for step_tiles in FAR_STEP_TILES:
        width = TB * step_tiles
        n_steps = (n_common - done) // step_tiles

        def far_step(j, carry, done=done, width=width):
            far_all(pl.multiple_of(done * TB + j * width, TB), width)
            return carry
        lax.fori_loop(0, n_steps, far_step, 0)
        done = done + n_steps * step_tiles

    def own_far(first_chain, off0):
        for k in range(nq - first_chain):
            off = pl.multiple_of(off0 + k * TB, TB)
            cs = range(first_chain + k, nq)
            s = [scores(c, off, TB) for c in cs]
            for c, s_ in zip(cs, s):
                update(c, s_, off, TB)

    @pl.when(g > 0)
    def _():
        own_far(1, n_common * TB)

    @pl.when(g == 0)
    def _():
        own_far(2, 0)

    offs = [pl.multiple_of(jnp.maximum(nq * g + c - 1, 0) * TB, TB) for c in chains]
    s_near = [scores(c, offs[c], 2 * TB) + (bias_ref[jnp.where(g == 0, 1, 0)] if c == 0 else bias_ref[0])
              for c in chains]
    for c in chains:
        update(c, s_near[c], offs[c], 2 * TB)

    lam = _lambda(lqk_ref, lam_init)
    for c in chains:
        acc = a_s[c]
        on = acc[0:LANES, :] / acc[LANES:LANES + 1, :]
        o = (on[:, :TB] - lam * on[:, TB:]).T
        o_ref[c * TB:(c + 1) * TB, :] = _subln(o, subln_ref[...], lam_init).astype(BF16)


def _attn_prompt(q, k_all, v_all, l, bias, lqk, subln, lam_init, TB):
    m, d_diff = q.shape
    n_heads = d_diff // LANES
    n_blk = m // TB
    nq = ATTN_TILES_PER_STEP
    assert n_blk % nq == 0
    return pl.pallas_call(
        functools.partial(_attn_kernel, TB=TB, n_cast=n_blk, lam_init=lam_init),
        grid=(n_heads, n_blk // nq),
        in_specs=[
            pl.BlockSpec((nq * TB, LANES), lambda h, g: (g, h)),
            pl.BlockSpec((None, m, LANES), lambda h, g: (l, 0, h)),
            pl.BlockSpec((None, m, LANES), lambda h, g: (l, 0, h)),
            pl.BlockSpec((None, 2, 2 * TB, 2 * TB), lambda h, g: (h, 0, 0, 0)),
            pl.BlockSpec((None, 4, DIFF_HEAD), lambda h, g: (l, 0, 0)),
            pl.BlockSpec((None, 1, LANES), lambda h, g: (l, 0, 0)),
        ],
        out_specs=pl.BlockSpec((nq * TB, LANES), lambda h, g: (g, h)),
        out_shape=jax.ShapeDtypeStruct((m, d_diff), BF16),
        scratch_shapes=[pltpu.VMEM((m, LANES), BF16), pltpu.VMEM((LANES + BF16_ROWS, m), BF16),
                        pltpu.VMEM((nq, 1, 2 * TB), F32), pltpu.VMEM((nq, LANES + BF16_ROWS, 2 * TB), F32)],
        compiler_params=_cparams(("arbitrary", "arbitrary")),
        name="attn_prompt",
    )(q, k_all, v_all, bias, lqk, subln)


SAMPLE_FAR_FRAMES = 384


def _attn_sample_kernel(q_ref, kn_ref, vn_ref, kc_ref, vc_ref, bias_ref, lqk_ref, subln_ref, o_ref,
                        kb, acc, *, T, n_heads, near, lam_init):
    past = kc_ref.shape[0]
    n_cols = 2 * T * n_heads
    fc = SAMPLE_FAR_FRAMES
    n_far = (past - near) // fc
    rows = fc * n_heads
    qs = jnp.concatenate(
        [m_ for h in range(n_heads) for m_ in _split_maps(q_ref[:, h * LANES:(h + 1) * LANES])], axis=0)
    sub = lax.broadcasted_iota(jnp.int32, (n_heads, n_cols), 0)
    col = lax.broadcasted_iota(jnp.int32, (n_heads, n_cols), 1)
    valid = (col >= sub * (2 * T)) & (col < (sub + 1) * (2 * T))

    def flat(x3):
        return x3.reshape(x3.shape[0] * n_heads, LANES)

    def scores(k_rows):
        s = _mm(k_rows, qs, _NT)
        return s.reshape(s.shape[0] // n_heads, n_heads, n_cols)

    near_k = jnp.concatenate([flat(kc_ref[past - near:past]), flat(kn_ref[...])], axis=0).astype(BF16)
    near_v = jnp.concatenate([flat(vc_ref[past - near:past]), flat(vn_ref[...])], axis=0).astype(BF16)
    s_near = scores(near_k) + bias_ref[...].reshape(near + T, n_heads, n_cols)

    def pass1(c, m3):
        f0 = pl.multiple_of(c * fc, fc)
        k_rows = flat(kc_ref[pl.ds(f0, fc)]).astype(BF16)
        kb[pl.ds(pl.multiple_of(c * rows, rows), rows), :] = k_rows
        return jnp.maximum(m3, jnp.max(scores(k_rows), axis=0))
    m3 = lax.fori_loop(0, n_far, pass1, jnp.max(s_near, axis=0))

    ones = jnp.ones((rows, LANES), BF16)

    def weights(s3):
        p = jnp.where(valid, jnp.exp(s3 - m3), 0.0)
        return p.reshape(p.shape[0] * n_heads, n_cols).astype(BF16)

    acc[...] = _mm(weights(s_near), jnp.concatenate([near_v, ones[:near_v.shape[0]]], axis=1), _TN)

    def pass2(c, carry):
        f0 = pl.multiple_of(c * fc, fc)
        p = weights(scores(kb[pl.ds(pl.multiple_of(c * rows, rows), rows), :]))
        v_aug = jnp.concatenate([flat(vc_ref[pl.ds(f0, fc)]).astype(BF16), ones], axis=1)
        acc[...] += _mm(p, v_aug, _TN)
        return carry
    lax.fori_loop(0, n_far, pass2, 0)

    lam = _lambda(lqk_ref, lam_init)
    for h in range(n_heads):
        a = acc[h * 2 * T:(h + 1) * 2 * T, :]
        on = a[:, 0:LANES] / a[:, LANES:2 * LANES]
        o = on[:T, :] - lam * on[T:, :]
        o_ref[:, h * LANES:(h + 1) * LANES] = _subln(o, subln_ref[...], lam_init).astype(BF16)


def _attn_sample(q, kn_all, vn_all, l, k_cache, v_cache, bias_t, lqk, subln, lam_init, n_seq, T, near):
    m, d_diff = q.shape
    n_heads = d_diff // LANES
    past = k_cache.shape[2]
    assert (past - near) % SAMPLE_FAR_FRAMES == 0
    cache_spec = pl.BlockSpec((None, None, past, n_heads, LANES), lambda b: (l, b, 0, 0, 0))
    new_spec = pl.BlockSpec((None, T, n_heads, LANES), lambda b: (l, b, 0, 0))
    return pl.pallas_call(
        functools.partial(_attn_sample_kernel, T=T, n_heads=n_heads, near=near, lam_init=lam_init),
        grid=(n_seq,),
        in_specs=[
            pl.BlockSpec((T, d_diff), lambda b: (b, 0)),
            new_spec,
            new_spec,
            cache_spec,
            cache_spec,
            pl.BlockSpec(bias_t.shape, lambda b: (0, 0)),
            pl.BlockSpec((None, 4, DIFF_HEAD), lambda b: (l, 0, 0)),
            pl.BlockSpec((None, 1, LANES), lambda b: (l, 0, 0)),
        ],
        out_specs=pl.BlockSpec((T, d_diff), lambda b: (b, 0)),
        out_shape=jax.ShapeDtypeStruct((m, d_diff), BF16),
        scratch_shapes=[pltpu.VMEM(((past - near) * n_heads, LANES), BF16),
                        pltpu.VMEM((2 * T * n_heads, 2 * LANES), F32)],
        compiler_params=_cparams(("arbitrary",)),
        name="attn_sample",
    )(q, kn_all, vn_all, k_cache, v_cache, bias_t, lqk, subln)


def _outproj_kernel(x_ref, yr_ref, yd_ref, gt_ref, wt_ref, wb_ref, o_ref):
    mixed = _mm(yr_ref[...], wt_ref[...]) + _mm(yd_ref[...], wb_ref[...])
    o_ref[...] = x_ref[...] + gt_ref[...] * mixed


def _outproj(x, yr, yd, gt, w_out_all, l, tm):
    m, d = x.shape
    dr = yr.shape[1]
    dd = yd.shape[1]
    gt_spec = (pl.BlockSpec((1, d), lambda i: (0, 0)) if gt.shape[0] == 1
               else pl.BlockSpec((tm, d), lambda i: (i, 0)))
    return pl.pallas_call(
        _outproj_kernel,
        grid=(m // tm,),
        in_specs=[
            pl.BlockSpec((tm, d), lambda i: (i, 0)),
            pl.BlockSpec((tm, dr), lambda i: (i, 0)),
            pl.BlockSpec((tm, dd), lambda i: (i, 0)),
            gt_spec,
            pl.BlockSpec((None, dr, d), lambda i: (l, 0, 0)),
            pl.BlockSpec((None, dd, d), lambda i: (l, dr // dd, 0)),
        ],
        out_specs=pl.BlockSpec((tm, d), lambda i: (i, 0)),
        out_shape=jax.ShapeDtypeStruct((m, d), F32),
        compiler_params=_cparams(("arbitrary",)),
        name="outproj",
    )(x, yr, yd, gt, w_out_all, w_out_all)


def _shift_rows(x, prev, n):
    if n % SUBLANES == 0:
        return jnp.concatenate([prev, x[:x.shape[0] - n, :]], axis=0)
    rolled = pltpu.roll(x, n, 0)
    row = lax.broadcasted_iota(jnp.int32, x.shape, 0)
    out = rolled
    for r in range(n):
        out = jnp.where(row == r, prev[r:r + 1, :], out)
    return out


def _ffn_kernel(x_ref, sh_ref, sc_ref, gt_ref, g_ref, wg_ref, wv_ref, cw_ref, cb_ref, c0_ref, wd_ref, gf_ref,
                o_ref, co_ref, h_scr, acc_scr, carry, *, B, n_j, final_norm):
    i = pl.program_id(0)
    j = pl.program_id(1)
    tm = x_ref.shape[0]

    @pl.when(j == 0)
    def _():
        xn = _rms(x_ref[...], g_ref[...], NORM_EPS)
        h_scr[...] = (xn * (1.0 + sc_ref[...]) + sh_ref[...]).astype(BF16)
        acc_scr[...] = jnp.zeros(acc_scr.shape, F32)

    @pl.when(i == 0)
    def _():
        carry[j, 0:2 * B, :] = c0_ref[...]

    h = h_scr[...]
    ug = _mm(h, wg_ref[...])
    uv = _mm(h, wv_ref[...])
    prev = carry[j, 0:2 * B, :]
    s1 = _shift_rows(ug, prev[B:2 * B, :], B)
    s2 = _shift_rows(ug, prev, 2 * B)
    cw = cw_ref[...]
    z = cb_ref[...] + s2 * cw[0:1, :] + s1 * cw[1:2, :] + ug * cw[2:3, :]
    act = 0.5 * z * (1.0 + lax.erf(z * (2.0 ** -0.5)))
    acc_scr[...] += _mm((act * uv).astype(BF16), wd_ref[...])
    last2 = ug[tm - 2 * B:tm, :]
    carry[j, 0:2 * B, :] = last2
    tf = last2.shape[1]
    co_ref[:, pl.ds(pl.multiple_of(j * tf, tf), tf)] = last2

    @pl.when(j == n_j - 1)
    def _():
        x2 = x_ref[...] + gt_ref[...] * acc_scr[...]
        if final_norm:
            x2 = _rms(x2, gf_ref[...], NORM_EPS)
        o_ref[...] = x2


def _ffn(x, sh, sc, gt, g_all, w_up_all, conv_w_all, conv_b_all, conv0, w_down_all, g_final, l, B, tm,
         final_norm, tf=512):
    m, d = x.shape
    d_ff = w_down_all.shape[1]
    n_j = d_ff // tf
    rows_c = max(SUBLANES, 2 * B)
    return pl.pallas_call(
        functools.partial(_ffn_kernel, B=B, n_j=n_j, final_norm=final_norm),
        grid=(m // tm, n_j),
        in_specs=[
            pl.BlockSpec((tm, d), lambda i, j: (i, 0)),
            _mod_spec(sh, tm, d),
            _mod_spec(sc, tm, d),
            _mod_spec(gt, tm, d),
            pl.BlockSpec((None, 1, d), lambda i, j: (l, 0, 0)),
            pl.BlockSpec((None, d, tf), lambda i, j: (l, 0, j)),
            pl.BlockSpec((None, d, tf), lambda i, j: (l, 0, n_j + j)),
            pl.BlockSpec((None, CONV_W, tf), lambda i, j: (l, 0, j)),
            pl.BlockSpec((None, 1, tf), lambda i, j: (l, 0, j)),
            pl.BlockSpec((2 * B, tf), lambda i, j: (0, j)),
            pl.BlockSpec((None, tf, d), lambda i, j: (l, j, 0)),
            pl.BlockSpec((1, d), lambda i, j: (0, 0)),
        ],
        out_specs=[
            pl.BlockSpec((tm, d), lambda i, j: (i, 0)),
            pl.BlockSpec((2 * B, d_ff), lambda i, j: (0, 0)),
        ],
        out_shape=[
            jax.ShapeDtypeStruct((m, d), F32),
            jax.ShapeDtypeStruct((2 * B, d_ff), F32),
        ],
        scratch_shapes=[pltpu.VMEM((tm, d), BF16), pltpu.VMEM((tm, d), F32),
                        pltpu.VMEM((n_j, rows_c, tf), F32)],
        compiler_params=_cparams(("arbitrary", "arbitrary")),
        name="conv_ffn",
    )(x, sh, sc, gt, g_all, w_up_all, w_up_all, conv_w_all, conv_b_all, conv0, w_down_all, g_final)


def _pad_cols(a, n):
    return jnp.pad(a, [(0, 0)] * (a.ndim - 1) + [(0, n)])


def _regroup_rwkv_cols(a, d_rwkv):
    o = 3 * d_rwkv
    return jnp.concatenate([
        a[..., :o],
        _pad_cols(a[..., o:o + DECAY_LORA], LORA_PAD - DECAY_LORA),
        _pad_cols(a[..., o + DECAY_LORA:o + DECAY_LORA + AAA_LORA], LORA_PAD - AAA_LORA),
        a[..., o + DECAY_LORA + AAA_LORA:],
    ], axis=-1)


def _ungroup_rwkv_cols(a, d_rwkv):
    o = 3 * d_rwkv
    return jnp.concatenate([
        a[..., :o],
        a[..., o:o + DECAY_LORA],
        a[..., o + LORA_PAD:o + LORA_PAD + AAA_LORA],
        a[..., o + 2 * LORA_PAD:],
    ], axis=-1)


def _prompt_tiles(seq):
    attn_tile = min(256, seq // 2)
    assert attn_tile % CHUNK == 0 and attn_tile >= MAX_DISTANCE
    assert seq % (ATTN_TILES_PER_STEP * attn_tile) == 0
    return attn_tile, min(512, seq), min(1024, seq), min(64, seq)


def kernel(x_prompt, x_sample, c_prompt, c_sample, cache_k, cache_v, state_wkv, state_shift, state_conv, w_ada, b_ada, g_mix, g_ffn, w_in, w_out, rwkv_mu, rwkv_w0, rwkv_w2, rwkv_a0, rwkv_a2, rwkv_g2, rwkv_kk, rwkv_ka, rwkv_rk, rwkv_ln_w, rwkv_ln_b, diff_lq1, diff_lk1, diff_lq2, diff_lk2, diff_subln, rel_table, ffn_up, ffn_conv_w, ffn_conv_b, ffn_down, g_final):
    depth, d_model, _ = w_in.shape
    bp, seq, _ = x_prompt.shape
    bs, dseq, _ = x_sample.shape
    past = cache_k.shape[2]
    d_rwkv = rwkv_w0.shape[1]
    d_diff = d_model - d_rwkv
    n_dheads = d_diff // (2 * DIFF_HEAD)
    n_rheads = d_rwkv // RWKV_HEAD
    n_rwkv_cols = rwkv_mu.shape[1]
    n_rwkv_pad = n_rwkv_cols + 2 * LORA_PAD - DECAY_LORA - AAA_LORA
    d_ff = ffn_down.shape[1]
    assert bp == 1, "prompt path handles one sequence"
    TB, tm_p, tm_in, wkv_chunk = _prompt_tiles(seq)
    m_s = bs * dseq

    w_in_b = jnp.concatenate([_regroup_rwkv_cols(w_in[..., :n_rwkv_cols], d_rwkv), w_in[..., n_rwkv_cols:]],
                             axis=-1).astype(BF16)
    mu_p = _regroup_rwkv_cols(rwkv_mu, d_rwkv)[:, None]
    w2_b = jnp.pad(rwkv_w2, ((0, 0), (0, LORA_PAD - DECAY_LORA), (0, 0))).astype(BF16)
    a2_b = jnp.pad(rwkv_a2, ((0, 0), (0, LORA_PAD - AAA_LORA), (0, 0))).astype(BF16)
    g2_b = rwkv_g2.astype(BF16)
    w_out_b = _cast_bf16(w_out)
    up_b = _cast_bf16(ffn_up)
    down_b = _cast_bf16(ffn_down)
    vec = lambda a: a.reshape(depth, 1, -1)
    wkv_args = (w2_b, a2_b, g2_b, vec(rwkv_w0), vec(rwkv_a0), vec(rwkv_kk), vec(rwkv_ka), vec(rwkv_rk),
                vec(rwkv_ln_w), vec(rwkv_ln_b))
    g_mix_v, g_ffn_v, subln_v, conv_b_v = vec(g_mix), vec(g_ffn), vec(diff_subln), vec(ffn_conv_b)
    gf = g_final[None]
    lqk = jnp.stack([diff_lq1, diff_lk1, diff_lq2, diff_lk2], axis=1)

    n_c = bp + bs
    n_c_pad = -(-n_c // SUBLANES) * SUBLANES
    c_rows = jnp.pad(jnp.concatenate([c_prompt, c_sample], axis=0), ((0, n_c_pad - n_c), (0, 0)))
    mod = _adaln_mod(c_rows, w_ada, b_ada).reshape(depth, n_c_pad, 6, d_model)

    ql = jnp.arange(TB)
    rel_diag = ql[None, :] - ql[:, None]
    allowed = (ql[None, :] // CHUNK) <= (ql[:, None] // CHUNK)
    bucket_diag = jnp.where(allowed, _t5_bucket(rel_diag), -1)
    bucket_prev = _t5_bucket(rel_diag - TB)
    windows = [jnp.concatenate([bucket_prev, bucket_diag], axis=1),
               jnp.concatenate([bucket_diag, jnp.full((TB, TB), -1)], axis=1)]
    bucket_p = jnp.concatenate([w.T for w in windows], axis=0)
    bias_p = _bias_lookup(rel_table, bucket_p.astype(jnp.int32), FAR_BUCKET, col_repeats=2)
    bias_p = bias_p.reshape(n_dheads, 2, 2 * TB, 2 * TB)
    q_pos = past + jnp.arange(dseq)
    assert past >= MAX_DISTANCE
    k_pos = past - MAX_DISTANCE + jnp.arange(MAX_DISTANCE + dseq)
    bias_s = _bias_lookup(rel_table, _t5_bucket(k_pos[None, :] - q_pos[:, None]).astype(jnp.int32), FAR_BUCKET)
    bias_t = jnp.broadcast_to(bias_s.transpose(2, 0, 1)[:, None, :, None, :],
                              (MAX_DISTANCE + dseq, n_dheads, n_dheads, 2, dseq))
    bias_t = bias_t.reshape((MAX_DISTANCE + dseq) * n_dheads, n_dheads * 2 * dseq)

    xp = x_prompt.reshape(seq, d_model)
    xs = x_sample.reshape(m_s, d_model)
    zero_shift = jnp.zeros((1, 1, n_rwkv_pad), F32)
    zero_state = jnp.zeros((1, n_rheads, RWKV_HEAD, RWKV_HEAD), F32)
    zero_conv = jnp.zeros((2, d_ff), F32)
    shift_s_in = _regroup_rwkv_cols(state_shift, d_rwkv)
    outs_p = {k: [] for k in ("wkv", "shift", "conv")}
    outs_s = {k: [] for k in ("wkv", "shift", "conv")}
    kv_p = (jnp.zeros((depth, seq, d_diff), F32), jnp.zeros((depth, seq, d_diff), F32))
    kv_s = (jnp.zeros((depth, m_s, d_diff), F32), jnp.zeros((depth, m_s, d_diff), F32))

    def time_major(a):
        return a.reshape(bs, dseq, -1).swapaxes(0, 1).reshape(m_s, -1)

    def batch_major(a):
        return a.reshape(dseq, bs, -1).swapaxes(0, 1).reshape(m_s, -1)

    for l in range(depth):
        lam_init = 0.8 - 0.6 * math.exp(-0.3 * l)
        last = l == depth - 1

        mp = mod[l, 0:bp]
        sh1, sc1, gt1, sh2, sc2, gt2 = (mp[:, t] for t in range(6))
        p, q, *kv_p = _inproj(xp, sh1, sc1, g_mix_v, w_in_b, l, kv_p, n_rwkv_pad, d_diff, tm_in)
        y_r, wkv = _wkv(p, l, mu_p, zero_shift, *wkv_args, zero_state, 1, seq, wkv_chunk)
        y_d = _attn_prompt(q, kv_p[0], kv_p[1], l, bias_p, lqk, subln_v, lam_init, TB)
        x1 = _outproj(xp, y_r, y_d, gt1, w_out_b, l, tm_p)
        xp, conv = _ffn(x1, sh2, sc2, gt2, g_ffn_v, up_b, ffn_conv_w, conv_b_v, zero_conv, down_b, gf, l,
                        1, tm_p, last)
        outs_p["wkv"].append(wkv)
        outs_p["shift"].append(_ungroup_rwkv_cols(p[seq - 1:seq], d_rwkv).reshape(bp, 1, n_rwkv_cols))
        outs_p["conv"].append(conv.reshape(bp, CONV_W - 1, d_ff))

        ms = mod[l, bp:bp + bs]
        rows_bm = jnp.repeat(ms, dseq, axis=0)
        rows_tm = jnp.tile(ms, (dseq, 1, 1))
        p, q, *kv_s = _inproj(xs, rows_bm[:, 0], rows_bm[:, 1], g_mix_v, w_in_b, l, kv_s, n_rwkv_pad, d_diff, m_s)
        y_r, wkv = _wkv(p, l, mu_p, shift_s_in[l], *wkv_args, state_wkv[l], bs, dseq, dseq)
        kn4, vn4 = (a.reshape(depth, m_s, n_dheads, 2 * DIFF_HEAD) for a in kv_s)
        y_d = _attn_sample(q, kn4, vn4, l, cache_k, cache_v, bias_t, lqk, subln_v, lam_init, bs, dseq,
                           MAX_DISTANCE)
        x1 = _outproj(xs, y_r, y_d, rows_bm[:, 2], w_out_b, l, m_s)
        conv0 = state_conv[l].swapaxes(0, 1).reshape((CONV_W - 1) * bs, d_ff)
        x2, conv = _ffn(time_major(x1), rows_tm[:, 3], rows_tm[:, 4], rows_tm[:, 5], g_ffn_v, up_b, ffn_conv_w,
                        conv_b_v, conv0, down_b, gf, l, bs, m_s, last)
        xs = batch_major(x2)
        outs_s["wkv"].append(wkv)
        p_last = p.reshape(bs, dseq, n_rwkv_pad)[:, dseq - 1:dseq]
        outs_s["shift"].append(_ungroup_rwkv_cols(p_last, d_rwkv))
        outs_s["conv"].append(conv.reshape(CONV_W - 1, bs, d_ff).swapaxes(0, 1))

    st = lambda xs_: jnp.stack(xs_)
    head_shape = (n_dheads, 2 * DIFF_HEAD)
    return (xp.reshape(bp, seq, d_model), xs.reshape(bs, dseq, d_model),
            kv_p[0].reshape(depth, bp, seq, *head_shape), kv_p[1].reshape(depth, bp, seq, *head_shape),
            st(outs_p["wkv"]), st(outs_p["shift"]), st(outs_p["conv"]),
            kv_s[0].reshape(depth, bs, dseq, *head_shape), kv_s[1].reshape(depth, bs, dseq, *head_shape),
            st(outs_s["wkv"]), st(outs_s["shift"]), st(outs_s["conv"]))
```

```python
import functools
import math

import jax
import jax.numpy as jnp
from jax import lax
from jax.experimental import pallas as pl
from jax.experimental.pallas import tpu as pltpu

F32 = jnp.float32
BF16 = jnp.bfloat16

CHUNK = 64
RWKV_HEAD = 64
DIFF_HEAD = 64
N_BUCKETS = 32
MAX_DISTANCE = 128
NORM_EPS = 1e-6
GN_EPS = 64e-5
SUBLN_EPS = 1e-5
DECAY_LORA = 96
AAA_LORA = 96
GATE_LORA = 256
LORA_PAD = 128
CONV_W = 3

LANES = 128
SUBLANES = 8
BF16_ROWS = 16
VMEM_LIMIT_BYTES = 56 * 1024 * 1024
MASK_VALUE = -1e30
WKV_CHUNKS_PER_STEP = 2
WKV_PAIRS_PER_STEP = 8

_NN = (((1,), (0,)), ((), ()))
_NT = (((1,), (1,)), ((), ()))
_TN = (((0,), (0,)), ((), ()))


def _mm(a, b, dims=_NN):
    return lax.dot_general(a, b, dims, preferred_element_type=F32)


def _hi_lo(x):
    h = x.astype(BF16)
    return h, (x - h.astype(F32)).astype(BF16)


def _dot3(a, b, dims=_NN):
    (ah, al), (bh, bl) = a, b
    (ca,), (cb,) = dims[0]
    k = ah.shape[ca]
    if (ca == 0 or k % LANES == 0) and (cb == 0 or k % LANES == 0):
        return _mm(jnp.concatenate([ah, al, ah], axis=ca), jnp.concatenate([bh, bh, bl], axis=cb), dims)
    return _mm(ah, bh, dims) + (_mm(al, bh, dims) + _mm(ah, bl, dims))


def _mm3(a, b, dims=_NN):
    return _dot3(_hi_lo(a), _hi_lo(b), dims)


def _sel_rows(sel2, x):
    xh, xl = _hi_lo(x)
    return _mm(sel2, jnp.concatenate([xh, xl], axis=0))


def _sel_lanes(x, sel2):
    xh, xl = _hi_lo(x)
    return _mm(jnp.concatenate([xh, xl], axis=1), sel2)


def _cparams(sem):
    return pltpu.CompilerParams(dimension_semantics=sem, vmem_limit_bytes=VMEM_LIMIT_BYTES)


def _rms(x, g, eps):
    return x * lax.rsqrt(jnp.mean(x * x, axis=-1, keepdims=True) + eps) * g


def _cast_kernel(x_ref, o_ref):
    o_ref[...] = x_ref[...].astype(BF16)


CAST_BLOCK_BYTES = 8 * 1024 * 1024


def _cast_bf16(w):
    depth, rows, cols = w.shape
    tc = 1024 if cols % 1024 == 0 else cols
    tr = max(t for t in range(BF16_ROWS, rows + 1, BF16_ROWS)
             if rows % t == 0 and t * tc * 4 <= CAST_BLOCK_BYTES)
    spec = pl.BlockSpec((None, tr, tc), lambda l, i, j: (l, i, j))
    return pl.pallas_call(
        _cast_kernel,
        grid=(depth, rows // tr, cols // tc),
        in_specs=[spec],
        out_specs=spec,
        out_shape=jax.ShapeDtypeStruct(w.shape, BF16),
        compiler_params=_cparams(("arbitrary", "arbitrary", "arbitrary")),
        name="cast_bf16",
    )(w)


def _mod_kernel(c_ref, w_ref, b_ref, o_ref):
    c = c_ref[...]
    s = (c * jax.nn.sigmoid(c)).astype(BF16)
    o_ref[...] = _mm(s, w_ref[...].astype(BF16)) + b_ref[...]


def _adaln_mod(c_rows, w_ada, b_ada, tn=1024):
    depth, d, n = w_ada.shape
    rows = c_rows.shape[0]
    return pl.pallas_call(
        _mod_kernel,
        grid=(depth, n // tn),
        in_specs=[
            pl.BlockSpec((rows, d), lambda l, j: (0, 0)),
            pl.BlockSpec((None, d, tn), lambda l, j: (l, 0, j)),
            pl.BlockSpec((None, 1, tn), lambda l, j: (l, 0, j)),
        ],
        out_specs=pl.BlockSpec((None, rows, tn), lambda l, j: (l, 0, j)),
        out_shape=jax.ShapeDtypeStruct((depth, rows, n), F32),
        compiler_params=_cparams(("arbitrary", "arbitrary")),
        name="adaln_mod",
    )(c_rows, w_ada, b_ada.reshape(depth, 1, n))


def _bias_kernel(tab_ref, bucket_ref, o_ref, *, far_bucket):
    h = pl.program_id(0)
    bucket = bucket_ref[...]
    far = tab_ref[far_bucket, h]
    acc = jnp.full(bucket.shape, MASK_VALUE, F32)
    for b in range(N_BUCKETS):
        acc = jnp.where(bucket == b, tab_ref[b, h] - far, acc)
    c = bucket.shape[1]
    for k in range(o_ref.shape[1] // c):
        o_ref[:, k * c:(k + 1) * c] = acc


def _bias_lookup(table, bucket, far_bucket, col_repeats=1):
    n_heads = table.shape[1]
    r, c = bucket.shape
    return pl.pallas_call(
        functools.partial(_bias_kernel, far_bucket=far_bucket),
        grid=(n_heads,),
        in_specs=[
            pl.BlockSpec(memory_space=pltpu.SMEM),
            pl.BlockSpec((r, c), lambda h: (0, 0)),
        ],
        out_specs=pl.BlockSpec((None, r, c * col_repeats), lambda h: (h, 0, 0)),
        out_shape=jax.ShapeDtypeStruct((n_heads, r, c * col_repeats), F32),
        compiler_params=_cparams(("arbitrary",)),
        name="bias_lookup",
    )(table, bucket)


def _t5_bucket(rel):
    nb = N_BUCKETS // 2
    max_exact = nb // 2
    n = jnp.abs(rel)
    nf = jnp.maximum(n, 1).astype(F32)
    large = max_exact + (jnp.log(nf / max_exact) / math.log(MAX_DISTANCE / max_exact) * (nb - max_exact)).astype(jnp.int32)
    large = jnp.minimum(large, nb - 1)
    return jnp.where(rel > 0, nb, 0) + jnp.where(n < max_exact, n, large)


FAR_BUCKET = N_BUCKETS // 2 - 1


def _inproj_kernel(x_ref, sh_ref, sc_ref, g_ref, w_ref, *rest, n_p, n_q):
    p_ref, q_ref, k_ref, v_ref, h_scr = rest[-5:]
    j = pl.program_id(1)

    @pl.when(j == 0)
    def _():
        xn = _rms(x_ref[...], g_ref[...], NORM_EPS)
        h_scr[...] = (xn * (1.0 + sc_ref[...]) + sh_ref[...]).astype(BF16)

    @pl.when(j < n_p)
    def _():
        p_ref[...] = _mm(h_scr[...], w_ref[...])

    @pl.when((j >= n_p) & (j < n_p + n_q))
    def _():
        q_ref[...] = (_mm(h_scr[...], w_ref[...]) * (DIFF_HEAD ** -0.5)).astype(BF16)

    @pl.when((j >= n_p + n_q) & (j < n_p + 2 * n_q))
    def _():
        k_ref[...] = _mm(h_scr[...], w_ref[...])

    @pl.when(j >= n_p + 2 * n_q)
    def _():
        v_ref[...] = _mm(h_scr[...], w_ref[...])


def _mod_spec(mod, tm, d):
    if mod.shape[0] == 1:
        return pl.BlockSpec((1, d), lambda i, j: (0, 0))
    return pl.BlockSpec((tm, d), lambda i, j: (i, 0))


def _inproj(x, sh, sc, g_all, w_all, l, kv_all, n_rwkv_pad, d_diff, tm, tn=512):
    m, d = x.shape
    depth, _, n = w_all.shape
    n_p, n_q = n_rwkv_pad // tn, d_diff // tn
    clip = lambda j, lo: jnp.clip(j - lo, 0, n_q - 1)
    in_specs = [
        pl.BlockSpec((tm, d), lambda i, j: (i, 0)),
        _mod_spec(sh, tm, d),
        _mod_spec(sc, tm, d),
        pl.BlockSpec((None, 1, d), lambda i, j: (l, 0, 0)),
        pl.BlockSpec((None, d, tn), lambda i, j: (l, 0, j)),
        pl.BlockSpec(memory_space=pl.ANY),
        pl.BlockSpec(memory_space=pl.ANY),
    ]
    args = [x, sh, sc, g_all, w_all, kv_all[0], kv_all[1]]
    aliases = {5: 2, 6: 3}
    return pl.pallas_call(
        functools.partial(_inproj_kernel, n_p=n_p, n_q=n_q),
        grid=(m // tm, n // tn),
        in_specs=in_specs,
        out_specs=[
            pl.BlockSpec((tm, tn), lambda i, j: (i, jnp.minimum(j, n_p - 1))),
            pl.BlockSpec((tm, tn), lambda i, j: (i, clip(j, n_p))),
            pl.BlockSpec((None, tm, tn), lambda i, j: (l, i, clip(j, n_p + n_q))),
            pl.BlockSpec((None, tm, tn), lambda i, j: (l, i, clip(j, n_p + 2 * n_q))),
        ],
        out_shape=[
            jax.ShapeDtypeStruct((m, n_rwkv_pad), F32),
            jax.ShapeDtypeStruct((m, d_diff), BF16),
            jax.ShapeDtypeStruct((depth, m, d_diff), F32),
            jax.ShapeDtypeStruct((depth, m, d_diff), F32),
        ],
        scratch_shapes=[pltpu.VMEM((tm, d), BF16)],
        input_output_aliases=aliases,
        compiler_params=_cparams(("arbitrary", "arbitrary")),
        name="inproj",
    )(*args)


def _wkv_kernel(pr_ref, pk_ref, pv_ref, pw_ref, pa_ref, pg_ref,
                mur_ref, muk_ref, muv_ref, muw_ref, mua_ref, mug_ref,
                s0r_ref, s0k_ref, s0v_ref, s0w_ref, s0a_ref, s0g_ref,
                w2_ref, a2_ref, g2_ref,
                w0_ref, a0_ref, kkp_ref, ka_ref, rk_ref, lnw_ref, lnb_ref,
                st0_ref,
                y_ref, st_ref,
                sbd, cr, ck, cv, cw, ca, cg, *, C, n_sub, n_chunks, n_pairs):
    chunk = pl.program_id(2)
    CB = n_sub * C
    hd = RWKV_HEAD
    zero_blk = jnp.zeros((hd, hd), F32)

    @pl.when(chunk == 0)
    def _():
        for c_ref, s_ref in ((cr, s0r_ref), (ck, s0k_ref), (cv, s0v_ref), (cw, s0w_ref), (ca, s0a_ref), (cg, s0g_ref)):
            c_ref[0:1, :] = s_ref[...]
        for pi in range(n_pairs):
            top = jnp.concatenate([st0_ref[2 * pi], zero_blk], axis=1)
            bot = jnp.concatenate([zero_blk, st0_ref[2 * pi + 1]], axis=1)
            sbd[pi] = jnp.concatenate([top, bot], axis=0)

    def tshift(x_ref, mu_ref, c_ref):
        x = x_ref[...]
        row = lax.broadcasted_iota(jnp.int32, x.shape, 0)
        prev = jnp.where(row == 0, c_ref[0:1, :], pltpu.roll(x, 1, 0))
        c_ref[0:1, :] = x[CB - 1:CB, :]
        return x + (prev - x) * mu_ref[...]

    xr = tshift(pr_ref, mur_ref, cr)
    xk = tshift(pk_ref, muk_ref, ck)
    xv = tshift(pv_ref, muv_ref, cv)
    xw = tshift(pw_ref, muw_ref, cw)
    xa = tshift(pa_ref, mua_ref, ca)
    xg = tshift(pg_ref, mug_ref, cg)

    li = lax.broadcasted_iota(jnp.int32, (LANES, LANES), 0)
    lj = lax.broadcasted_iota(jnp.int32, (LANES, LANES), 1)
    same_head = (li < hd) == (lj < hd)
    li2 = lax.broadcasted_iota(jnp.int32, (2 * LANES, LANES), 0)
    lj2 = lax.broadcasted_iota(jnp.int32, (2 * LANES, LANES), 1)
    li2 = jnp.where(li2 < LANES, li2, li2 - LANES)
    seg2 = jnp.where((li2 < hd) == (lj2 < hd), 1.0, 0.0).astype(BF16)

    def segsum(x):
        return _sel_lanes(x, seg2)

    wlin = w0_ref[...] + _mm(jnp.tanh(xw).astype(BF16), w2_ref[...])
    z = -wlin
    w_log = -(jnp.maximum(z, 0.0) + jnp.log1p(jnp.exp(-jnp.abs(z)))) - 0.5
    ne = -jnp.exp(w_log)
    a = jax.nn.sigmoid(a0_ref[...] + _mm(xa.astype(BF16), a2_ref[...]))
    g = _mm(jax.nn.sigmoid(xg).astype(BF16), g2_ref[...])
    kk_raw = xk * kkp_ref[...]
    k2 = xk * (1.0 + (a - 1.0) * ka_ref[...])
    rk_prod = xr * k2 * rk_ref[...]

    ti = lax.broadcasted_iota(jnp.int32, (CB, 2 * CB), 0)
    tj = lax.broadcasted_iota(jnp.int32, (CB, 2 * CB), 1)
    tj = jnp.where(tj < CB, tj, tj - CB)
    in_chunk = [(ti >= q * C) & (ti < (q + 1) * C) & (tj >= q * C) & (tj <= ti) for q in range(n_sub)]
    tri2 = jnp.where(functools.reduce(lambda x, y: x | y, in_chunk), 1.0, 0.0).astype(BF16)
    cum = _sel_rows(tri2, ne)
    gam = jnp.exp(cum)
    gprev = jnp.exp(cum - ne)
    ginv = jnp.exp(-cum)
    cum_end = jnp.concatenate([jnp.broadcast_to(cum[(q + 1) * C - 1:(q + 1) * C, :], (C, cum.shape[1]))
                               for q in range(n_sub)], axis=0)
    dte = jnp.exp(cum_end - cum)
    rg_all = xr * gam
    ks_all = k2 * ginv
    kd_all = k2 * dte
    lnw = lnw_ref[...]
    lnb = lnb_ref[...]

    def stack_masked(x):
        first = lax.broadcasted_iota(jnp.int32, (1, x.shape[1]), 1) < x.shape[1] // 2
        return jnp.concatenate([jnp.where(first, x, 0.0), jnp.where(first, 0.0, x)], axis=0)

    R = 2 * C
    ri = lax.broadcasted_iota(jnp.int32, (C, R), 0)
    ci = lax.broadcasted_iota(jnp.int32, (C, R), 1)
    ci = jnp.where(ci < C, ci, ci - C)
    strict = ci < ri
    incl = ci <= ri
    n_steps = max(1, int(math.log2(C)))

    pairs = range(n_pairs)
    sls = [slice(pi * LANES, (pi + 1) * LANES) for pi in pairs]
    sums_all = [segsum(jnp.concatenate([kk_raw[:, sl] * kk_raw[:, sl], rk_prod[:, sl]], axis=0)) for sl in sls]
    kk_all = [kk_raw[:, sl] * lax.rsqrt(jnp.maximum(sm[:CB, :], 1e-24)) for sl, sm in zip(sls, sums_all)]
    s_raw = [sbd[pi] for pi in pairs]
    for q in range(n_sub):
        s_raw = _wkv_chunk(
            q, s_raw, y_ref, sls, stack_masked, segsum, strict, incl, same_head,
            kk_all, [sm[CB:, :] for sm in sums_all], a, xv, gprev, ginv, dte, gam, rg_all, ks_all, kd_all,
            lnw, lnb, g, C=C, n_steps=n_steps)
    for pi in pairs:
        sbd[pi] = s_raw[pi]

    @pl.when(chunk == n_chunks - 1)
    def _():
        for pi in range(n_pairs):
            s_new = sbd[pi]
            st_ref[2 * pi] = s_new[0:hd, 0:hd]
            st_ref[2 * pi + 1] = s_new[hd:2 * hd, hd:2 * hd]


def _wkv_chunk(q, s_raw, y_ref, sls, stack_masked, segsum, strict, incl, same_head,
               kk_all, rksum_all, a, xv, gprev, ginv, dte, gam, rg_all, ks_all, kd_all, lnw, lnb, g, *, C, n_steps):
    rs = slice(q * C, (q + 1) * C)
    R = 2 * C
    inv_hd = 1.0 / RWKV_HEAD
    pairs = range(len(sls))
    xv_p = [xv[rs, sl] for sl in sls]
    kk = [k[rs, :] for k in kk_all]
    bonus = [sm[rs, :] * v for sm, v in zip(rksum_all, xv_p)]
    kka = [k * a[rs, sl] for k, sl in zip(kk, sls)]
    gprev, ginv, dte, rg_all, ks_all, kd_all = (t[rs, :] for t in (gprev, ginv, dte, rg_all, ks_all, kd_all))
    gam_end = gam[(q + 1) * C - 1:(q + 1) * C, :]
    def both(f, pieces):
        return tuple(f(p_) for p_ in pieces)

    def cat(axis, *pairs_of_pieces):
        return tuple(jnp.concatenate(ps, axis=axis) for ps in zip(*pairs_of_pieces))

    la = [_hi_lo(-k * gprev[:, sl]) for k, sl in zip(kk, sls)]
    lr = [rg_all[:, sl].astype(BF16) for sl in sls]
    rhs = [cat(0, both(stack_masked, _hi_lo(ka_ * ginv[:, sl])), both(stack_masked, _hi_lo(ks_all[:, sl])))
           for ka_, sl in zip(kka, sls)]
    xv_s = [_hi_lo(v) for v in xv_p]
    vst = [both(stack_masked, v) for v in xv_s]
    s_old = [_hi_lo(s_) for s_ in s_raw]
    sc_a = [_dot3(l_, r_, _NT) for l_, r_ in zip(la, rhs)]
    sc_r = [_mm(l_, r_[0], _NT) for l_, r_ in zip(lr, rhs)]
    ls_a = [_dot3(l_, s_, _NT) for l_, s_ in zip(la, s_old)]
    ls_r = [_mm(l_, s_[0], _NT) for l_, s_ in zip(lr, s_old)]
    n_ab = [jnp.where(strict, s4[:, :R], 0.0) for s4 in sc_a]
    m_ak = [jnp.where(strict, s4[:, R:], 0.0) for s4 in sc_a]
    m_r = [jnp.concatenate([jnp.where(incl, s4[:, :R], 0.0), jnp.where(incl, s4[:, R:], 0.0)],
                           axis=1).astype(BF16) for s4 in sc_r]
    x_u = [l_ + _dot3(_hi_lo(mk), v) for l_, mk, v in zip(ls_a, m_ak, vst)]
    pw = n_ab
    for step in range(n_steps):
        ps = [_hi_lo(p_) for p_ in pw]
        xs = [both(stack_masked, _hi_lo(x_)) for x_ in x_u]
        if step + 1 < n_steps:
            t = [_dot3(p_, cat(1, x_, both(stack_masked, p_))) for p_, x_ in zip(ps, xs)]
            x_u = [x_ + t_[:, :LANES] for x_, t_ in zip(x_u, t)]
            pw = [t_[:, LANES:] for t_ in t]
        else:
            x_u = [x_ + _dot3(p_, x_s) for x_, p_, x_s in zip(x_u, ps, xs)]
    xu_s = [_hi_lo(x_) for x_ in x_u]
    y = [l_ + _mm(mr, jnp.concatenate([stack_masked(x_[0]), v[0]], axis=0))
         for l_, mr, x_, v in zip(ls_r, m_r, xu_s, vst)]
    upd = [_dot3(cat(0, x_, v), _hi_lo(jnp.concatenate([ka_ * dte[:, sl], kd_all[:, sl]], axis=0)), _TN)
           for x_, v, ka_, sl in zip(xu_s, xv_s, kka, sls)]
    s_new = [s_raw[pi] * gam_end[:, sls[pi]] + jnp.where(same_head, upd[pi], 0.0) for pi in pairs]

    yc = [y_ - segsum(y_) * inv_hd for y_ in y]
    y_var = [segsum(c_ * c_) * inv_hd for c_ in yc]
    for pi in pairs:
        sl = sls[pi]
        yn = yc[pi] * lax.rsqrt(y_var[pi] + GN_EPS) * lnw[:, sl] + lnb[:, sl]
        y_ref[rs, sl] = ((yn + bonus[pi]) * g[rs, sl]).astype(BF16)
    return s_new


def _wkv(p, l, mu, shift0, w2, a2, g2, w0, a0, kkp, ka, rk, lnw, lnb, state0, n_seq, seq_len, C, n_sub):
    m, n_pad = p.shape
    d_rwkv = w0.shape[-1]
    n_heads = d_rwkv // RWKV_HEAD
    n_pairs = WKV_PAIRS_PER_STEP
    wd = n_pairs * LANES
    n_groups = d_rwkv // wd
    cb = C * n_sub
    n_chunks = seq_len // cb
    nb = n_groups
    blk_w = 3 * d_rwkv // LANES
    blk_a = blk_w + 1
    blk_g = (blk_a + 1) * LANES // GATE_LORA

    row = lambda s, h, c: s * n_chunks + c
    seg_specs = [
        pl.BlockSpec((cb, wd), lambda s, h, c: (row(s, h, c), h)),
        pl.BlockSpec((cb, wd), lambda s, h, c: (row(s, h, c), nb + h)),
        pl.BlockSpec((cb, wd), lambda s, h, c: (row(s, h, c), 2 * nb + h)),
        pl.BlockSpec((cb, LANES), lambda s, h, c: (row(s, h, c), blk_w)),
        pl.BlockSpec((cb, LANES), lambda s, h, c: (row(s, h, c), blk_a)),
        pl.BlockSpec((cb, GATE_LORA), lambda s, h, c: (row(s, h, c), blk_g)),
    ]
    mu_specs = [
        pl.BlockSpec((None, 1, wd), lambda s, h, c: (l, 0, h)),
        pl.BlockSpec((None, 1, wd), lambda s, h, c: (l, 0, nb + h)),
        pl.BlockSpec((None, 1, wd), lambda s, h, c: (l, 0, 2 * nb + h)),
        pl.BlockSpec((None, 1, LANES), lambda s, h, c: (l, 0, blk_w)),
        pl.BlockSpec((None, 1, LANES), lambda s, h, c: (l, 0, blk_a)),
        pl.BlockSpec((None, 1, GATE_LORA), lambda s, h, c: (l, 0, blk_g)),
    ]
    s0_specs = [
        pl.BlockSpec((None, 1, wd), lambda s, h, c: (s, 0, h)),
        pl.BlockSpec((None, 1, wd), lambda s, h, c: (s, 0, nb + h)),
        pl.BlockSpec((None, 1, wd), lambda s, h, c: (s, 0, 2 * nb + h)),
        pl.BlockSpec((None, 1, LANES), lambda s, h, c: (s, 0, blk_w)),
        pl.BlockSpec((None, 1, LANES), lambda s, h, c: (s, 0, blk_a)),
        pl.BlockSpec((None, 1, GATE_LORA), lambda s, h, c: (s, 0, blk_g)),
    ]
    lora_specs = [
        pl.BlockSpec((None, LORA_PAD, wd), lambda s, h, c: (l, 0, h)),
        pl.BlockSpec((None, LORA_PAD, wd), lambda s, h, c: (l, 0, h)),
        pl.BlockSpec((None, GATE_LORA, wd), lambda s, h, c: (l, 0, h)),
    ]
    vec_spec = pl.BlockSpec((None, 1, wd), lambda s, h, c: (l, 0, h))
    state_spec = pl.BlockSpec((None, 2 * n_pairs, RWKV_HEAD, RWKV_HEAD), lambda s, h, c: (s, h, 0, 0))
    y, st = pl.pallas_call(
        functools.partial(_wkv_kernel, C=C, n_sub=n_sub, n_chunks=n_chunks, n_pairs=n_pairs),
        grid=(n_seq, n_groups, n_chunks),
        in_specs=seg_specs + mu_specs + s0_specs + lora_specs + [vec_spec] * 7 + [state_spec],
        out_specs=[
            pl.BlockSpec((cb, wd), lambda s, h, c: (row(s, h, c), h)),
            state_spec,
        ],
        out_shape=[
            jax.ShapeDtypeStruct((m, d_rwkv), BF16),
            jax.ShapeDtypeStruct((n_seq, n_heads, RWKV_HEAD, RWKV_HEAD), F32),
        ],
        scratch_shapes=[pltpu.VMEM((n_pairs, LANES, LANES), F32)]
        + [pltpu.VMEM((SUBLANES, wd), F32)] * 3 + [pltpu.VMEM((SUBLANES, LANES), F32)] * 2
        + [pltpu.VMEM((SUBLANES, GATE_LORA), F32)],
        compiler_params=_cparams(("arbitrary", "arbitrary", "arbitrary")),
        name="wkv",
    )(p, p, p, p, p, p, mu, mu, mu, mu, mu, mu,
      shift0, shift0, shift0, shift0, shift0, shift0,
      w2, a2, g2, w0, a0, kkp, ka, rk, lnw, lnb, state0)
    return y, st


def _lambda(lqk_ref, lam_init):
    t = lqk_ref[...]
    s1 = jnp.sum(t[0:1, :] * t[1:2, :], axis=-1, keepdims=True)
    s2 = jnp.sum(t[2:3, :] * t[3:4, :], axis=-1, keepdims=True)
    return jnp.exp(s1) - jnp.exp(s2) + lam_init


def _split_maps(q):
    lane = lax.broadcasted_iota(jnp.int32, q.shape, 1)
    zero = jnp.zeros_like(q)
    return jnp.where(lane < DIFF_HEAD, q, zero), jnp.where(lane < DIFF_HEAD, zero, q)


def _subln(o, subln, lam_init):
    return _rms(o, subln, SUBLN_EPS) * (1.0 - lam_init)


FAR_STEP_TILES = (4, 1)
ATTN_TILES_PER_STEP = 4


def _attn_kernel(q_ref, k_ref, v_ref, bias_ref, lqk_ref, subln_ref, o_ref,
                 kb, vt, m_s, a_s, *, TB, n_cast, lam_init):
    g = pl.program_id(1)

    @pl.when(g == 0)
    def _():
        vt[LANES:, :] = jnp.ones((vt.shape[0] - LANES, vt.shape[1]), BF16)

        def cast(c, carry):
            off = pl.multiple_of(c * TB, TB)
            kb[pl.ds(off, TB), :] = k_ref[pl.ds(off, TB), :].astype(BF16)
            vt[0:LANES, pl.ds(off, TB)] = v_ref[pl.ds(off, TB), :].T.astype(BF16)
            return carry
        lax.fori_loop(0, n_cast, cast, 0)

    nq = ATTN_TILES_PER_STEP
    chains = range(nq)
    qs = [jnp.concatenate(_split_maps(q_ref[c * TB:(c + 1) * TB, :]), axis=0) for c in chains]
    m_s[...] = jnp.full(m_s.shape, MASK_VALUE, F32)
    a_s[...] = jnp.zeros(a_s.shape, F32)

    def scores(c, off, width):
        return _mm(kb[pl.ds(off, width), :], qs[c], _NT)

    def update(c, s, off, width):
        m_prev = m_s[c]
        m_new = jnp.maximum(m_prev, jnp.max(s, axis=0, keepdims=True))
        alpha = jnp.exp(m_prev - m_new)
        p = jnp.exp(s - m_new).astype(BF16)
        a_s[c] = alpha * a_s[c] + _mm(vt[:, pl.ds(off, width)], p)
        m_s[c] = m_new

    def far_all(off, width):
        s = [scores(c, off, width) for c in chains]
        for c in chains:
            update(c, s[c], off, width)

    n_common = jnp.maximum(nq * g - 1, 0)
    done = 0
    for step_tiles in FAR_STEP_TILES:
        width = TB * step_tiles
        n_steps = (n_common - done) // step_tiles

        def far_step(j, carry, done=done, width=width):
            far_all(pl.multiple_of(done * TB + j * width, TB), width)
            return carry
        lax.fori_loop(0, n_steps, far_step, 0)
        done = done + n_steps * step_tiles

    def own_far(first_chain, off0):
        for k in range(nq - first_chain):
            off = pl.multiple_of(off0 + k * TB, TB)
            cs = range(first_chain + k, nq)
            s = [scores(c, off, TB) for c in cs]
            for c, s_ in zip(cs, s):
                update(c, s_, off, TB)

    @pl.when(g > 0)
    def _():
        own_far(1, n_common * TB)

    @pl.when(g == 0)
    def _():
        own_far(2, 0)

    offs = [pl.multiple_of(jnp.maximum(nq * g + c - 1, 0) * TB, TB) for c in chains]
    s_near = [scores(c, offs[c], 2 * TB) + (bias_ref[jnp.where(g == 0, 1, 0)] if c == 0 else bias_ref[0])
              for c in chains]
    for c in chains:
        update(c, s_near[c], offs[c], 2 * TB)

    lam = _lambda(lqk_ref, lam_init)
    for c in chains:
        acc = a_s[c]
        on = acc[0:LANES, :] / acc[LANES:LANES + 1, :]
        o = (on[:, :TB] - lam * on[:, TB:]).T
        o_ref[c * TB:(c + 1) * TB, :] = _subln(o, subln_ref[...], lam_init).astype(BF16)


def _attn_prompt(q, k_all, v_all, l, bias, lqk, subln, lam_init, TB):
    m, d_diff = q.shape
    n_heads = d_diff // LANES
    n_blk = m // TB
    nq = ATTN_TILES_PER_STEP
    assert n_blk % nq == 0
    return pl.pallas_call(
        functools.partial(_attn_kernel, TB=TB, n_cast=n_blk, lam_init=lam_init),
        grid=(n_heads, n_blk // nq),
        in_specs=[
            pl.BlockSpec((nq * TB, LANES), lambda h, g: (g, h)),
            pl.BlockSpec((None, m, LANES), lambda h, g: (l, 0, h)),
            pl.BlockSpec((None, m, LANES), lambda h, g: (l, 0, h)),
            pl.BlockSpec((None, 2, 2 * TB, 2 * TB), lambda h, g: (h, 0, 0, 0)),
            pl.BlockSpec((None, 4, DIFF_HEAD), lambda h, g: (l, 0, 0)),
            pl.BlockSpec((None, 1, LANES), lambda h, g: (l, 0, 0)),
        ],
        out_specs=pl.BlockSpec((nq * TB, LANES), lambda h, g: (g, h)),
        out_shape=jax.ShapeDtypeStruct((m, d_diff), BF16),
        scratch_shapes=[pltpu.VMEM((m, LANES), BF16), pltpu.VMEM((LANES + BF16_ROWS, m), BF16),
                        pltpu.VMEM((nq, 1, 2 * TB), F32), pltpu.VMEM((nq, LANES + BF16_ROWS, 2 * TB), F32)],
        compiler_params=_cparams(("arbitrary", "arbitrary")),
        name="attn_prompt",
    )(q, k_all, v_all, bias, lqk, subln)


SAMPLE_FAR_FRAMES = 384


def _attn_sample_kernel(q_ref, kn_ref, vn_ref, kc_ref, vc_ref, bias_ref, lqk_ref, subln_ref, o_ref,
                        kb, acc, *, T, n_heads, near, lam_init):
    past = kc_ref.shape[0]
    n_cols = 2 * T * n_heads
    fc = SAMPLE_FAR_FRAMES
    n_far = (past - near) // fc
    rows = fc * n_heads
    qs = jnp.concatenate(
        [m_ for h in range(n_heads) for m_ in _split_maps(q_ref[:, h * LANES:(h + 1) * LANES])], axis=0)
    sub = lax.broadcasted_iota(jnp.int32, (n_heads, n_cols), 0)
    col = lax.broadcasted_iota(jnp.int32, (n_heads, n_cols), 1)
    valid = (col >= sub * (2 * T)) & (col < (sub + 1) * (2 * T))

    def flat(x3):
        return x3.reshape(x3.shape[0] * n_heads, LANES)

    def scores(k_rows):
        s = _mm(k_rows, qs, _NT)
        return s.reshape(s.shape[0] // n_heads, n_heads, n_cols)

    near_k = jnp.concatenate([flat(kc_ref[past - near:past]), flat(kn_ref[...])], axis=0).astype(BF16)
    near_v = jnp.concatenate([flat(vc_ref[past - near:past]), flat(vn_ref[...])], axis=0).astype(BF16)
    s_near = scores(near_k) + bias_ref[...].reshape(near + T, n_heads, n_cols)

    def pass1(c, m3):
        f0 = pl.multiple_of(c * fc, fc)
        k_rows = flat(kc_ref[pl.ds(f0, fc)]).astype(BF16)
        kb[pl.ds(pl.multiple_of(c * rows, rows), rows), :] = k_rows
        return jnp.maximum(m3, jnp.max(scores(k_rows), axis=0))
    m3 = lax.fori_loop(0, n_far, pass1, jnp.max(s_near, axis=0))

    ones = jnp.ones((rows, LANES), BF16)

    def weights(s3):
        p = jnp.where(valid, jnp.exp(s3 - m3), 0.0)
        return p.reshape(p.shape[0] * n_heads, n_cols).astype(BF16)

    acc[...] = _mm(weights(s_near), jnp.concatenate([near_v, ones[:near_v.shape[0]]], axis=1), _TN)

    def pass2(c, carry):
        f0 = pl.multiple_of(c * fc, fc)
        p = weights(scores(kb[pl.ds(pl.multiple_of(c * rows, rows), rows), :]))
        v_aug = jnp.concatenate([flat(vc_ref[pl.ds(f0, fc)]).astype(BF16), ones], axis=1)
        acc[...] += _mm(p, v_aug, _TN)
        return carry
    lax.fori_loop(0, n_far, pass2, 0)

    lam = _lambda(lqk_ref, lam_init)
    for h in range(n_heads):
        a = acc[h * 2 * T:(h + 1) * 2 * T, :]
        on = a[:, 0:LANES] / a[:, LANES:2 * LANES]
        o = on[:T, :] - lam * on[T:, :]
        o_ref[:, h * LANES:(h + 1) * LANES] = _subln(o, subln_ref[...], lam_init).astype(BF16)


def _attn_sample(q, kn_all, vn_all, l, k_cache, v_cache, bias_t, lqk, subln, lam_init, n_seq, T, near):
    m, d_diff = q.shape
    n_heads = d_diff // LANES
    past = k_cache.shape[2]
    assert (past - near) % SAMPLE_FAR_FRAMES == 0
    cache_spec = pl.BlockSpec((None, None, past, n_heads, LANES), lambda b: (l, b, 0, 0, 0))
    new_spec = pl.BlockSpec((None, T, n_heads, LANES), lambda b: (l, b, 0, 0))
    return pl.pallas_call(
        functools.partial(_attn_sample_kernel, T=T, n_heads=n_heads, near=near, lam_init=lam_init),
        grid=(n_seq,),
        in_specs=[
            pl.BlockSpec((T, d_diff), lambda b: (b, 0)),
            new_spec,
            new_spec,
            cache_spec,
            cache_spec,
            pl.BlockSpec(bias_t.shape, lambda b: (0, 0)),
            pl.BlockSpec((None, 4, DIFF_HEAD), lambda b: (l, 0, 0)),
            pl.BlockSpec((None, 1, LANES), lambda b: (l, 0, 0)),
        ],
        out_specs=pl.BlockSpec((T, d_diff), lambda b: (b, 0)),
        out_shape=jax.ShapeDtypeStruct((m, d_diff), BF16),
        scratch_shapes=[pltpu.VMEM(((past - near) * n_heads, LANES), BF16),
                        pltpu.VMEM((2 * T * n_heads, 2 * LANES), F32)],
        compiler_params=_cparams(("arbitrary",)),
        name="attn_sample",
    )(q, kn_all, vn_all, k_cache, v_cache, bias_t, lqk, subln)


def _outproj_kernel(x_ref, yr_ref, yd_ref, gt_ref, wt_ref, wb_ref, o_ref):
    mixed = _mm(yr_ref[...], wt_ref[...]) + _mm(yd_ref[...], wb_ref[...])
    o_ref[...] = x_ref[...] + gt_ref[...] * mixed


def _outproj(x, yr, yd, gt, w_out_all, l, tm):
    m, d = x.shape
    dr = yr.shape[1]
    dd = yd.shape[1]
    gt_spec = (pl.BlockSpec((1, d), lambda i: (0, 0)) if gt.shape[0] == 1
               else pl.BlockSpec((tm, d), lambda i: (i, 0)))
    return pl.pallas_call(
        _outproj_kernel,
        grid=(m // tm,),
        in_specs=[
            pl.BlockSpec((tm, d), lambda i: (i, 0)),
            pl.BlockSpec((tm, dr), lambda i: (i, 0)),
            pl.BlockSpec((tm, dd), lambda i: (i, 0)),
            gt_spec,
            pl.BlockSpec((None, dr, d), lambda i: (l, 0, 0)),
            pl.BlockSpec((None, dd, d), lambda i: (l, dr // dd, 0)),
        ],
        out_specs=pl.BlockSpec((tm, d), lambda i: (i, 0)),
        out_shape=jax.ShapeDtypeStruct((m, d), F32),
        compiler_params=_cparams(("arbitrary",)),
        name="outproj",
    )(x, yr, yd, gt, w_out_all, w_out_all)


def _shift_rows(x, prev, n):
    if n % SUBLANES == 0:
        return jnp.concatenate([prev, x[:x.shape[0] - n, :]], axis=0)
    rolled = pltpu.roll(x, n, 0)
    row = lax.broadcasted_iota(jnp.int32, x.shape, 0)
    out = rolled
    for r in range(n):
        out = jnp.where(row == r, prev[r:r + 1, :], out)
    return out


def _ffn_kernel(x_ref, sh_ref, sc_ref, gt_ref, g_ref, wg_ref, wv_ref, cw_ref, cb_ref, c0_ref, wd_ref, gf_ref,
                o_ref, co_ref, h_scr, acc_scr, carry, *, B, n_j, final_norm):
    i = pl.program_id(0)
    j = pl.program_id(1)
    tm = x_ref.shape[0]

    @pl.when(j == 0)
    def _():
        xn = _rms(x_ref[...], g_ref[...], NORM_EPS)
        h_scr[...] = (xn * (1.0 + sc_ref[...]) + sh_ref[...]).astype(BF16)
        acc_scr[...] = jnp.zeros(acc_scr.shape, F32)

    @pl.when(i == 0)
    def _():
        carry[j, 0:2 * B, :] = c0_ref[...]

    h = h_scr[...]
    ug = _mm(h, wg_ref[...])
    uv = _mm(h, wv_ref[...])
    prev = carry[j, 0:2 * B, :]
    s1 = _shift_rows(ug, prev[B:2 * B, :], B)
    s2 = _shift_rows(ug, prev, 2 * B)
    cw = cw_ref[...]
    z = cb_ref[...] + s2 * cw[0:1, :] + s1 * cw[1:2, :] + ug * cw[2:3, :]
    act = 0.5 * z * (1.0 + lax.erf(z * (2.0 ** -0.5)))
    acc_scr[...] += _mm((act * uv).astype(BF16), wd_ref[...])
    last2 = ug[tm - 2 * B:tm, :]
    carry[j, 0:2 * B, :] = last2
    tf = last2.shape[1]
    co_ref[:, pl.ds(pl.multiple_of(j * tf, tf), tf)] = last2

    @pl.when(j == n_j - 1)
    def _():
        x2 = x_ref[...] + gt_ref[...] * acc_scr[...]
        if final_norm:
            x2 = _rms(x2, gf_ref[...], NORM_EPS)
        o_ref[...] = x2


def _ffn(x, sh, sc, gt, g_all, w_up_all, conv_w_all, conv_b_all, conv0, w_down_all, g_final, l, B, tm,
         final_norm, tf=512):
    m, d = x.shape
    d_ff = w_down_all.shape[1]
    n_j = d_ff // tf
    rows_c = max(SUBLANES, 2 * B)
    return pl.pallas_call(
        functools.partial(_ffn_kernel, B=B, n_j=n_j, final_norm=final_norm),
        grid=(m // tm, n_j),
        in_specs=[
            pl.BlockSpec((tm, d), lambda i, j: (i, 0)),
            _mod_spec(sh, tm, d),
            _mod_spec(sc, tm, d),
            _mod_spec(gt, tm, d),
            pl.BlockSpec((None, 1, d), lambda i, j: (l, 0, 0)),
            pl.BlockSpec((None, d, tf), lambda i, j: (l, 0, j)),
            pl.BlockSpec((None, d, tf), lambda i, j: (l, 0, n_j + j)),
            pl.BlockSpec((None, CONV_W, tf), lambda i, j: (l, 0, j)),
            pl.BlockSpec((None, 1, tf), lambda i, j: (l, 0, j)),
            pl.BlockSpec((2 * B, tf), lambda i, j: (0, j)),
            pl.BlockSpec((None, tf, d), lambda i, j: (l, j, 0)),
            pl.BlockSpec((1, d), lambda i, j: (0, 0)),
        ],
        out_specs=[
            pl.BlockSpec((tm, d), lambda i, j: (i, 0)),
            pl.BlockSpec((2 * B, d_ff), lambda i, j: (0, 0)),
        ],
        out_shape=[
            jax.ShapeDtypeStruct((m, d), F32),
            jax.ShapeDtypeStruct((2 * B, d_ff), F32),
        ],
        scratch_shapes=[pltpu.VMEM((tm, d), BF16), pltpu.VMEM((tm, d), F32),
                        pltpu.VMEM((n_j, rows_c, tf), F32)],
        compiler_params=_cparams(("arbitrary", "arbitrary")),
        name="conv_ffn",
    )(x, sh, sc, gt, g_all, w_up_all, w_up_all, conv_w_all, conv_b_all, conv0, w_down_all, g_final)


def _pad_cols(a, n):
    return jnp.pad(a, [(0, 0)] * (a.ndim - 1) + [(0, n)])


def _regroup_rwkv_cols(a, d_rwkv):
    o = 3 * d_rwkv
    return jnp.concatenate([
        a[..., :o],
        _pad_cols(a[..., o:o + DECAY_LORA], LORA_PAD - DECAY_LORA),
        _pad_cols(a[..., o + DECAY_LORA:o + DECAY_LORA + AAA_LORA], LORA_PAD - AAA_LORA),
        a[..., o + DECAY_LORA + AAA_LORA:],
    ], axis=-1)


def _ungroup_rwkv_cols(a, d_rwkv):
    o = 3 * d_rwkv
    return jnp.concatenate([
        a[..., :o],
        a[..., o:o + DECAY_LORA],
        a[..., o + LORA_PAD:o + LORA_PAD + AAA_LORA],
        a[..., o + 2 * LORA_PAD:],
    ], axis=-1)


def _prompt_tiles(seq):
    attn_tile = min(256, seq // 2)
    assert attn_tile % CHUNK == 0 and attn_tile >= MAX_DISTANCE
    assert seq % (ATTN_TILES_PER_STEP * attn_tile) == 0
    return attn_tile, min(512, seq), min(1024, seq), min(64, seq)


def kernel(x_prompt, x_sample, c_prompt, c_sample, cache_k, cache_v, state_wkv, state_shift, state_conv, w_ada, b_ada, g_mix, g_ffn, w_in, w_out, rwkv_mu, rwkv_w0, rwkv_w2, rwkv_a0, rwkv_a2, rwkv_g2, rwkv_kk, rwkv_ka, rwkv_rk, rwkv_ln_w, rwkv_ln_b, diff_lq1, diff_lk1, diff_lq2, diff_lk2, diff_subln, rel_table, ffn_up, ffn_conv_w, ffn_conv_b, ffn_down, g_final):
    depth, d_model, _ = w_in.shape
    bp, seq, _ = x_prompt.shape
    bs, dseq, _ = x_sample.shape
    past = cache_k.shape[2]
    d_rwkv = rwkv_w0.shape[1]
    d_diff = d_model - d_rwkv
    n_dheads = d_diff // (2 * DIFF_HEAD)
    n_rheads = d_rwkv // RWKV_HEAD
    n_rwkv_cols = rwkv_mu.shape[1]
    n_rwkv_pad = n_rwkv_cols + 2 * LORA_PAD - DECAY_LORA - AAA_LORA
    d_ff = ffn_down.shape[1]
    assert bp == 1, "prompt path handles one sequence"
    TB, tm_p, tm_in, wkv_chunk = _prompt_tiles(seq)
    m_s = bs * dseq

    w_in_b = jnp.concatenate([_regroup_rwkv_cols(w_in[..., :n_rwkv_cols], d_rwkv), w_in[..., n_rwkv_cols:]],
                             axis=-1).astype(BF16)
    mu_p = _regroup_rwkv_cols(rwkv_mu, d_rwkv)[:, None]
    w2_b = jnp.pad(rwkv_w2, ((0, 0), (0, LORA_PAD - DECAY_LORA), (0, 0))).astype(BF16)
    a2_b = jnp.pad(rwkv_a2, ((0, 0), (0, LORA_PAD - AAA_LORA), (0, 0))).astype(BF16)
    g2_b = rwkv_g2.astype(BF16)
    w_out_b = _cast_bf16(w_out)
    up_b = _cast_bf16(ffn_up)
    down_b = _cast_bf16(ffn_down)
    vec = lambda a: a.reshape(depth, 1, -1)
    wkv_args = (w2_b, a2_b, g2_b, vec(rwkv_w0), vec(rwkv_a0), vec(rwkv_kk), vec(rwkv_ka), vec(rwkv_rk),
                vec(rwkv_ln_w), vec(rwkv_ln_b))
    g_mix_v, g_ffn_v, subln_v, conv_b_v = vec(g_mix), vec(g_ffn), vec(diff_subln), vec(ffn_conv_b)
    gf = g_final[None]
    lqk = jnp.stack([diff_lq1, diff_lk1, diff_lq2, diff_lk2], axis=1)

    n_c = bp + bs
    n_c_pad = -(-n_c // SUBLANES) * SUBLANES
    c_rows = jnp.pad(jnp.concatenate([c_prompt, c_sample], axis=0), ((0, n_c_pad - n_c), (0, 0)))
    mod = _adaln_mod(c_rows, w_ada, b_ada).reshape(depth, n_c_pad, 6, d_model)

    ql = jnp.arange(TB)
    rel_diag = ql[None, :] - ql[:, None]
    allowed = (ql[None, :] // CHUNK) <= (ql[:, None] // CHUNK)
    bucket_diag = jnp.where(allowed, _t5_bucket(rel_diag), -1)
    bucket_prev = _t5_bucket(rel_diag - TB)
    windows = [jnp.concatenate([bucket_prev, bucket_diag], axis=1),
               jnp.concatenate([bucket_diag, jnp.full((TB, TB), -1)], axis=1)]
    bucket_p = jnp.concatenate([w.T for w in windows], axis=0)
    bias_p = _bias_lookup(rel_table, bucket_p.astype(jnp.int32), FAR_BUCKET, col_repeats=2)
    bias_p = bias_p.reshape(n_dheads, 2, 2 * TB, 2 * TB)
    q_pos = past + jnp.arange(dseq)
    assert past >= MAX_DISTANCE
    k_pos = past - MAX_DISTANCE + jnp.arange(MAX_DISTANCE + dseq)
    bias_s = _bias_lookup(rel_table, _t5_bucket(k_pos[None, :] - q_pos[:, None]).astype(jnp.int32), FAR_BUCKET)
    bias_t = jnp.broadcast_to(bias_s.transpose(2, 0, 1)[:, None, :, None, :],
                              (MAX_DISTANCE + dseq, n_dheads, n_dheads, 2, dseq))
    bias_t = bias_t.reshape((MAX_DISTANCE + dseq) * n_dheads, n_dheads * 2 * dseq)

    xp = x_prompt.reshape(seq, d_model)
    xs = x_sample.reshape(m_s, d_model)
    zero_shift = jnp.zeros((1, 1, n_rwkv_pad), F32)
    zero_state = jnp.zeros((1, n_rheads, RWKV_HEAD, RWKV_HEAD), F32)
    zero_conv = jnp.zeros((2, d_ff), F32)
    shift_s_in = _regroup_rwkv_cols(state_shift, d_rwkv)
    outs_p = {k: [] for k in ("wkv", "shift", "conv")}
    outs_s = {k: [] for k in ("wkv", "shift", "conv")}
    kv_p = (jnp.zeros((depth, seq, d_diff), F32), jnp.zeros((depth, seq, d_diff), F32))
    kv_s = (jnp.zeros((depth, m_s, d_diff), F32), jnp.zeros((depth, m_s, d_diff), F32))

    def time_major(a):
        return a.reshape(bs, dseq, -1).swapaxes(0, 1).reshape(m_s, -1)

    def batch_major(a):
        return a.reshape(dseq, bs, -1).swapaxes(0, 1).reshape(m_s, -1)

    for l in range(depth):
        lam_init = 0.8 - 0.6 * math.exp(-0.3 * l)
        last = l == depth - 1

        mp = mod[l, 0:bp]
        sh1, sc1, gt1, sh2, sc2, gt2 = (mp[:, t] for t in range(6))
        p, q, *kv_p = _inproj(xp, sh1, sc1, g_mix_v, w_in_b, l, kv_p, n_rwkv_pad, d_diff, tm_in)
        y_r, wkv = _wkv(p, l, mu_p, zero_shift, *wkv_args, zero_state, 1, seq, wkv_chunk,
                        WKV_CHUNKS_PER_STEP if seq % (WKV_CHUNKS_PER_STEP * wkv_chunk) == 0 else 1)
        y_d = _attn_prompt(q, kv_p[0], kv_p[1], l, bias_p, lqk, subln_v, lam_init, TB)
        x1 = _outproj(xp, y_r, y_d, gt1, w_out_b, l, tm_p)
        xp, conv = _ffn(x1, sh2, sc2, gt2, g_ffn_v, up_b, ffn_conv_w, conv_b_v, zero_conv, down_b, gf, l,
                        1, tm_p, last)
        outs_p["wkv"].append(wkv)
        outs_p["shift"].append(_ungroup_rwkv_cols(p[seq - 1:seq], d_rwkv).reshape(bp, 1, n_rwkv_cols))
        outs_p["conv"].append(conv.reshape(bp, CONV_W - 1, d_ff))

        ms = mod[l, bp:bp + bs]
        rows_bm = jnp.repeat(ms, dseq, axis=0)
        rows_tm = jnp.tile(ms, (dseq, 1, 1))
        p, q, *kv_s = _inproj(xs, rows_bm[:, 0], rows_bm[:, 1], g_mix_v, w_in_b, l, kv_s, n_rwkv_pad, d_diff, m_s)
        y_r, wkv = _wkv(p, l, mu_p, shift_s_in[l], *wkv_args, state_wkv[l], bs, dseq, dseq, 1)
        kn4, vn4 = (a.reshape(depth, m_s, n_dheads, 2 * DIFF_HEAD) for a in kv_s)
        y_d = _attn_sample(q, kn4, vn4, l, cache_k, cache_v, bias_t, lqk, subln_v, lam_init, bs, dseq,
                           MAX_DISTANCE)
        x1 = _outproj(xs, y_r, y_d, rows_bm[:, 2], w_out_b, l, m_s)
        conv0 = state_conv[l].swapaxes(0, 1).reshape((CONV_W - 1) * bs, d_ff)
        x2, conv = _ffn(time_major(x1), rows_tm[:, 3], rows_tm[:, 4], rows_tm[:, 5], g_ffn_v, up_b, ffn_conv_w,
                        conv_b_v, conv0, down_b, gf, l, bs, m_s, last)
        xs = batch_major(x2)
        outs_s["wkv"].append(wkv)
        p_last = p.reshape(bs, dseq, n_rwkv_pad)[:, dseq - 1:dseq]
        outs_s["shift"].append(_ungroup_rwkv_cols(p_last, d_rwkv))
        outs_s["conv"].append(conv.reshape(CONV_W - 1, bs, d_ff).swapaxes(0, 1))

    st = lambda xs_: jnp.stack(xs_)
    head_shape = (n_dheads, 2 * DIFF_HEAD)
    return (xp.reshape(bp, seq, d_model), xs.reshape(bs, dseq, d_model),
            kv_p[0].reshape(depth, bp, seq, *head_shape), kv_p[1].reshape(depth, bp, seq, *head_shape),
            st(outs_p["wkv"]), st(outs_p["shift"]), st(outs_p["conv"]),
            kv_s[0].reshape(depth, bs, dseq, *head_shape), kv_s[1].reshape(depth, bs, dseq, *head_shape),
            st(outs_s["wkv"]), st(outs_s["shift"]), st(outs_s["conv"]))
```

```python
import functools
import math

import jax
import jax.numpy as jnp
from jax import lax
from jax.experimental import pallas as pl
from jax.experimental.pallas import tpu as pltpu

F32 = jnp.float32
BF16 = jnp.bfloat16

CHUNK = 64
RWKV_HEAD = 64
DIFF_HEAD = 64
N_BUCKETS = 32
MAX_DISTANCE = 128
NORM_EPS = 1e-6
GN_EPS = 64e-5
SUBLN_EPS = 1e-5
DECAY_LORA = 96
AAA_LORA = 96
GATE_LORA = 256
LORA_PAD = 128
CONV_W = 3

LANES = 128
SUBLANES = 8
BF16_ROWS = 16
VMEM_LIMIT_BYTES = 56 * 1024 * 1024
MASK_VALUE = -1e30
WKV_CHUNKS_PER_STEP = 1
WKV_PAIRS_PER_STEP = 8

_NN = (((1,), (0,)), ((), ()))
_NT = (((1,), (1,)), ((), ()))
_TN = (((0,), (0,)), ((), ()))


def _mm(a, b, dims=_NN):
    return lax.dot_general(a, b, dims, preferred_element_type=F32)


def _hi_lo(x):
    h = x.astype(BF16)
    return h, (x - h.astype(F32)).astype(BF16)


def _dot3(a, b, dims=_NN):
    (ah, al), (bh, bl) = a, b
    (ca,), (cb,) = dims[0]
    k = ah.shape[ca]
    if (ca == 0 or k % LANES == 0) and (cb == 0 or k % LANES == 0):
        return _mm(jnp.concatenate([ah, al, ah], axis=ca), jnp.concatenate([bh, bh, bl], axis=cb), dims)
    return _mm(ah, bh, dims) + (_mm(al, bh, dims) + _mm(ah, bl, dims))


def _mm3(a, b, dims=_NN):
    return _dot3(_hi_lo(a), _hi_lo(b), dims)


def _sel_rows(sel2, x):
    xh, xl = _hi_lo(x)
    return _mm(sel2, jnp.concatenate([xh, xl], axis=0))


def _sel_lanes(x, sel2):
    xh, xl = _hi_lo(x)
    return _mm(jnp.concatenate([xh, xl], axis=1), sel2)


def _cparams(sem):
    return pltpu.CompilerParams(dimension_semantics=sem, vmem_limit_bytes=VMEM_LIMIT_BYTES)


def _rms(x, g, eps):
    return x * lax.rsqrt(jnp.mean(x * x, axis=-1, keepdims=True) + eps) * g


def _cast_kernel(x_ref, o_ref):
    o_ref[...] = x_ref[...].astype(BF16)


CAST_BLOCK_BYTES = 8 * 1024 * 1024


def _cast_bf16(w):
    depth, rows, cols = w.shape
    tc = 1024 if cols % 1024 == 0 else cols
    tr = max(t for t in range(BF16_ROWS, rows + 1, BF16_ROWS)
             if rows % t == 0 and t * tc * 4 <= CAST_BLOCK_BYTES)
    spec = pl.BlockSpec((None, tr, tc), lambda l, i, j: (l, i, j))
    return pl.pallas_call(
        _cast_kernel,
        grid=(depth, rows // tr, cols // tc),
        in_specs=[spec],
        out_specs=spec,
        out_shape=jax.ShapeDtypeStruct(w.shape, BF16),
        compiler_params=_cparams(("arbitrary", "arbitrary", "arbitrary")),
        name="cast_bf16",
    )(w)


def _mod_kernel(c_ref, w_ref, b_ref, o_ref):
    c = c_ref[...]
    s = (c * jax.nn.sigmoid(c)).astype(BF16)
    o_ref[...] = _mm(s, w_ref[...].astype(BF16)) + b_ref[...]


def _adaln_mod(c_rows, w_ada, b_ada, tn=1024):
    depth, d, n = w_ada.shape
    rows = c_rows.shape[0]
    return pl.pallas_call(
        _mod_kernel,
        grid=(depth, n // tn),
        in_specs=[
            pl.BlockSpec((rows, d), lambda l, j: (0, 0)),
            pl.BlockSpec((None, d, tn), lambda l, j: (l, 0, j)),
            pl.BlockSpec((None, 1, tn), lambda l, j: (l, 0, j)),
        ],
        out_specs=pl.BlockSpec((None, rows, tn), lambda l, j: (l, 0, j)),
        out_shape=jax.ShapeDtypeStruct((depth, rows, n), F32),
        compiler_params=_cparams(("arbitrary", "arbitrary")),
        name="adaln_mod",
    )(c_rows, w_ada, b_ada.reshape(depth, 1, n))


def _bias_kernel(tab_ref, bucket_ref, o_ref, *, far_bucket):
    h = pl.program_id(0)
    bucket = bucket_ref[...]
    far = tab_ref[far_bucket, h]
    acc = jnp.full(bucket.shape, MASK_VALUE, F32)
    for b in range(N_BUCKETS):
        acc = jnp.where(bucket == b, tab_ref[b, h] - far, acc)
    c = bucket.shape[1]
    for k in range(o_ref.shape[1] // c):
        o_ref[:, k * c:(k + 1) * c] = acc


def _bias_lookup(table, bucket, far_bucket, col_repeats=1):
    n_heads = table.shape[1]
    r, c = bucket.shape
    return pl.pallas_call(
        functools.partial(_bias_kernel, far_bucket=far_bucket),
        grid=(n_heads,),
        in_specs=[
            pl.BlockSpec(memory_space=pltpu.SMEM),
            pl.BlockSpec((r, c), lambda h: (0, 0)),
        ],
        out_specs=pl.BlockSpec((None, r, c * col_repeats), lambda h: (h, 0, 0)),
        out_shape=jax.ShapeDtypeStruct((n_heads, r, c * col_repeats), F32),
        compiler_params=_cparams(("arbitrary",)),
        name="bias_lookup",
    )(table, bucket)


def _t5_bucket(rel):
    nb = N_BUCKETS // 2
    max_exact = nb // 2
    n = jnp.abs(rel)
    nf = jnp.maximum(n, 1).astype(F32)
    large = max_exact + (jnp.log(nf / max_exact) / math.log(MAX_DISTANCE / max_exact) * (nb - max_exact)).astype(jnp.int32)
    large = jnp.minimum(large, nb - 1)
    return jnp.where(rel > 0, nb, 0) + jnp.where(n < max_exact, n, large)


FAR_BUCKET = N_BUCKETS // 2 - 1


def _inproj_kernel(x_ref, sh_ref, sc_ref, g_ref, w_ref, *rest, n_p, n_q):
    p_ref, q_ref, k_ref, v_ref, h_scr = rest[-5:]
    j = pl.program_id(1)

    @pl.when(j == 0)
    def _():
        xn = _rms(x_ref[...], g_ref[...], NORM_EPS)
        h_scr[...] = (xn * (1.0 + sc_ref[...]) + sh_ref[...]).astype(BF16)

    @pl.when(j < n_p)
    def _():
        p_ref[...] = _mm(h_scr[...], w_ref[...])

    @pl.when((j >= n_p) & (j < n_p + n_q))
    def _():
        q_ref[...] = (_mm(h_scr[...], w_ref[...]) * (DIFF_HEAD ** -0.5)).astype(BF16)

    @pl.when((j >= n_p + n_q) & (j < n_p + 2 * n_q))
    def _():
        k_ref[...] = _mm(h_scr[...], w_ref[...])

    @pl.when(j >= n_p + 2 * n_q)
    def _():
        v_ref[...] = _mm(h_scr[...], w_ref[...])


def _mod_spec(mod, tm, d):
    if mod.shape[0] == 1:
        return pl.BlockSpec((1, d), lambda i, j: (0, 0))
    return pl.BlockSpec((tm, d), lambda i, j: (i, 0))


def _inproj(x, sh, sc, g_all, w_all, l, kv_all, n_rwkv_pad, d_diff, tm, tn=512):
    m, d = x.shape
    depth, _, n = w_all.shape
    n_p, n_q = n_rwkv_pad // tn, d_diff // tn
    clip = lambda j, lo: jnp.clip(j - lo, 0, n_q - 1)
    in_specs = [
        pl.BlockSpec((tm, d), lambda i, j: (i, 0)),
        _mod_spec(sh, tm, d),
        _mod_spec(sc, tm, d),
        pl.BlockSpec((None, 1, d), lambda i, j: (l, 0, 0)),
        pl.BlockSpec((None, d, tn), lambda i, j: (l, 0, j)),
        pl.BlockSpec(memory_space=pl.ANY),
        pl.BlockSpec(memory_space=pl.ANY),
    ]
    args = [x, sh, sc, g_all, w_all, kv_all[0], kv_all[1]]
    aliases = {5: 2, 6: 3}
    return pl.pallas_call(
        functools.partial(_inproj_kernel, n_p=n_p, n_q=n_q),
        grid=(m // tm, n // tn),
        in_specs=in_specs,
        out_specs=[
            pl.BlockSpec((tm, tn), lambda i, j: (i, jnp.minimum(j, n_p - 1))),
            pl.BlockSpec((tm, tn), lambda i, j: (i, clip(j, n_p))),
            pl.BlockSpec((None, tm, tn), lambda i, j: (l, i, clip(j, n_p + n_q))),
            pl.BlockSpec((None, tm, tn), lambda i, j: (l, i, clip(j, n_p + 2 * n_q))),
        ],
        out_shape=[
            jax.ShapeDtypeStruct((m, n_rwkv_pad), F32),
            jax.ShapeDtypeStruct((m, d_diff), BF16),
            jax.ShapeDtypeStruct((depth, m, d_diff), F32),
            jax.ShapeDtypeStruct((depth, m, d_diff), F32),
        ],
        scratch_shapes=[pltpu.VMEM((tm, d), BF16)],
        input_output_aliases=aliases,
        compiler_params=_cparams(("arbitrary", "arbitrary")),
        name="inproj",
    )(*args)


def _wkv_kernel(pr_ref, pk_ref, pv_ref, pw_ref, pa_ref, pg_ref,
                mur_ref, muk_ref, muv_ref, muw_ref, mua_ref, mug_ref,
                s0r_ref, s0k_ref, s0v_ref, s0w_ref, s0a_ref, s0g_ref,
                w2_ref, a2_ref, g2_ref,
                w0_ref, a0_ref, kkp_ref, ka_ref, rk_ref, lnw_ref, lnb_ref,
                st0_ref,
                y_ref, st_ref,
                sbd, cr, ck, cv, cw, ca, cg, *, C, n_sub, n_chunks, n_pairs):
    chunk = pl.program_id(2)
    CB = n_sub * C
    hd = RWKV_HEAD
    zero_blk = jnp.zeros((hd, hd), F32)

    @pl.when(chunk == 0)
    def _():
        for c_ref, s_ref in ((cr, s0r_ref), (ck, s0k_ref), (cv, s0v_ref), (cw, s0w_ref), (ca, s0a_ref), (cg, s0g_ref)):
            c_ref[0:1, :] = s_ref[...]
        for pi in range(n_pairs):
            top = jnp.concatenate([st0_ref[2 * pi], zero_blk], axis=1)
            bot = jnp.concatenate([zero_blk, st0_ref[2 * pi + 1]], axis=1)
            sbd[pi] = jnp.concatenate([top, bot], axis=0)

    def tshift(x_ref, mu_ref, c_ref):
        x = x_ref[...]
        row = lax.broadcasted_iota(jnp.int32, x.shape, 0)
        prev = jnp.where(row == 0, c_ref[0:1, :], pltpu.roll(x, 1, 0))
        c_ref[0:1, :] = x[CB - 1:CB, :]
        return x + (prev - x) * mu_ref[...]

    xr = tshift(pr_ref, mur_ref, cr)
    xk = tshift(pk_ref, muk_ref, ck)
    xv = tshift(pv_ref, muv_ref, cv)
    xw = tshift(pw_ref, muw_ref, cw)
    xa = tshift(pa_ref, mua_ref, ca)
    xg = tshift(pg_ref, mug_ref, cg)

    li = lax.broadcasted_iota(jnp.int32, (LANES, LANES), 0)
    lj = lax.broadcasted_iota(jnp.int32, (LANES, LANES), 1)
    same_head = (li < hd) == (lj < hd)
    li2 = lax.broadcasted_iota(jnp.int32, (2 * LANES, LANES), 0)
    lj2 = lax.broadcasted_iota(jnp.int32, (2 * LANES, LANES), 1)
    li2 = jnp.where(li2 < LANES, li2, li2 - LANES)
    seg2 = jnp.where((li2 < hd) == (lj2 < hd), 1.0, 0.0).astype(BF16)

    def segsum(x):
        return _sel_lanes(x, seg2)

    wlin = w0_ref[...] + _mm(jnp.tanh(xw).astype(BF16), w2_ref[...])
    z = -wlin
    w_log = -(jnp.maximum(z, 0.0) + jnp.log1p(jnp.exp(-jnp.abs(z)))) - 0.5
    ne = -jnp.exp(w_log)
    a = jax.nn.sigmoid(a0_ref[...] + _mm(xa.astype(BF16), a2_ref[...]))
    g = _mm(jax.nn.sigmoid(xg).astype(BF16), g2_ref[...])
    kk_raw = xk * kkp_ref[...]
    k2 = xk * (1.0 + (a - 1.0) * ka_ref[...])
    rk_prod = xr * k2 * rk_ref[...]

    ti = lax.broadcasted_iota(jnp.int32, (CB, 2 * CB), 0)
    tj = lax.broadcasted_iota(jnp.int32, (CB, 2 * CB), 1)
    tj = jnp.where(tj < CB, tj, tj - CB)
    in_chunk = [(ti >= q * C) & (ti < (q + 1) * C) & (tj >= q * C) & (tj <= ti) for q in range(n_sub)]
    tri2 = jnp.where(functools.reduce(lambda x, y: x | y, in_chunk), 1.0, 0.0).astype(BF16)
    cum = _sel_rows(tri2, ne)
    gam = jnp.exp(cum)
    gprev = jnp.exp(cum - ne)
    ginv = jnp.exp(-cum)
    cum_end = jnp.concatenate([jnp.broadcast_to(cum[(q + 1) * C - 1:(q + 1) * C, :], (C, cum.shape[1]))
                               for q in range(n_sub)], axis=0)
    dte = jnp.exp(cum_end - cum)
    rg_all = xr * gam
    ks_all = k2 * ginv
    kd_all = k2 * dte
    lnw = lnw_ref[...]
    lnb = lnb_ref[...]

    def stack_masked(x):
        first = lax.broadcasted_iota(jnp.int32, (1, x.shape[1]), 1) < x.shape[1] // 2
        return jnp.concatenate([jnp.where(first, x, 0.0), jnp.where(first, 0.0, x)], axis=0)

    R = 2 * C
    ri = lax.broadcasted_iota(jnp.int32, (C, R), 0)
    ci = lax.broadcasted_iota(jnp.int32, (C, R), 1)
    ci = jnp.where(ci < C, ci, ci - C)
    strict = ci < ri
    incl = ci <= ri
    n_steps = max(1, int(math.log2(C)))

    pairs = range(n_pairs)
    sls = [slice(pi * LANES, (pi + 1) * LANES) for pi in pairs]
    sums_all = [segsum(jnp.concatenate([kk_raw[:, sl] * kk_raw[:, sl], rk_prod[:, sl]], axis=0)) for sl in sls]
    kk_all = [kk_raw[:, sl] * lax.rsqrt(jnp.maximum(sm[:CB, :], 1e-24)) for sl, sm in zip(sls, sums_all)]
    s_raw = [sbd[pi] for pi in pairs]
    for q in range(n_sub):
        s_raw = _wkv_chunk(
            q, s_raw, y_ref, sls, stack_masked, segsum, strict, incl, same_head,
            kk_all, [sm[CB:, :] for sm in sums_all], a, xv, gprev, ginv, dte, gam, rg_all, ks_all, kd_all,
            lnw, lnb, g, C=C, n_steps=n_steps)
    for pi in pairs:
        sbd[pi] = s_raw[pi]

    @pl.when(chunk == n_chunks - 1)
    def _():
        for pi in range(n_pairs):
            s_new = sbd[pi]
            st_ref[2 * pi] = s_new[0:hd, 0:hd]
            st_ref[2 * pi + 1] = s_new[hd:2 * hd, hd:2 * hd]


def _wkv_chunk(q, s_raw, y_ref, sls, stack_masked, segsum, strict, incl, same_head,
               kk_all, rksum_all, a, xv, gprev, ginv, dte, gam, rg_all, ks_all, kd_all, lnw, lnb, g, *, C, n_steps):
    rs = slice(q * C, (q + 1) * C)
    R = 2 * C
    inv_hd = 1.0 / RWKV_HEAD
    pairs = range(len(sls))
    xv_p = [xv[rs, sl] for sl in sls]
    kk = [k[rs, :] for k in kk_all]
    bonus = [sm[rs, :] * v for sm, v in zip(rksum_all, xv_p)]
    kka = [k * a[rs, sl] for k, sl in zip(kk, sls)]
    gprev, ginv, dte, rg_all, ks_all, kd_all = (t[rs, :] for t in (gprev, ginv, dte, rg_all, ks_all, kd_all))
    gam_end = gam[(q + 1) * C - 1:(q + 1) * C, :]
    def both(f, pieces):
        return tuple(f(p_) for p_ in pieces)

    def cat(axis, *pairs_of_pieces):
        return tuple(jnp.concatenate(ps, axis=axis) for ps in zip(*pairs_of_pieces))

    la = [_hi_lo(-k * gprev[:, sl]) for k, sl in zip(kk, sls)]
    lr = [rg_all[:, sl].astype(BF16) for sl in sls]
    rhs = [cat(0, both(stack_masked, _hi_lo(ka_ * ginv[:, sl])), both(stack_masked, _hi_lo(ks_all[:, sl])))
           for ka_, sl in zip(kka, sls)]
    xv_s = [_hi_lo(v) for v in xv_p]
    vst = [both(stack_masked, v) for v in xv_s]
    s_old = [_hi_lo(s_) for s_ in s_raw]
    sc_a = [_dot3(l_, r_, _NT) for l_, r_ in zip(la, rhs)]
    sc_r = [_mm(l_, r_[0], _NT) for l_, r_ in zip(lr, rhs)]
    ls_a = [_dot3(l_, s_, _NT) for l_, s_ in zip(la, s_old)]
    ls_r = [_mm(l_, s_[0], _NT) for l_, s_ in zip(lr, s_old)]
    n_ab = [jnp.where(strict, s4[:, :R], 0.0) for s4 in sc_a]
    m_ak = [jnp.where(strict, s4[:, R:], 0.0) for s4 in sc_a]
    m_r = [jnp.concatenate([jnp.where(incl, s4[:, :R], 0.0), jnp.where(incl, s4[:, R:], 0.0)],
                           axis=1).astype(BF16) for s4 in sc_r]
    x_u = [l_ + _dot3(_hi_lo(mk), v) for l_, mk, v in zip(ls_a, m_ak, vst)]
    pw = n_ab
    for step in range(n_steps):
        ps = [_hi_lo(p_) for p_ in pw]
        xs = [both(stack_masked, _hi_lo(x_)) for x_ in x_u]
        if step + 1 < n_steps:
            t = [_dot3(p_, cat(1, x_, both(stack_masked, p_))) for p_, x_ in zip(ps, xs)]
            x_u = [x_ + t_[:, :LANES] for x_, t_ in zip(x_u, t)]
            pw = [t_[:, LANES:] for t_ in t]
        else:
            x_u = [x_ + _dot3(p_, x_s) for x_, p_, x_s in zip(x_u, ps, xs)]
    xu_s = [_hi_lo(x_) for x_ in x_u]
    y = [l_ + _mm(mr, jnp.concatenate([stack_masked(x_[0]), v[0]], axis=0))
         for l_, mr, x_, v in zip(ls_r, m_r, xu_s, vst)]
    upd = [_dot3(cat(0, x_, v), _hi_lo(jnp.concatenate([ka_ * dte[:, sl], kd_all[:, sl]], axis=0)), _TN)
           for x_, v, ka_, sl in zip(xu_s, xv_s, kka, sls)]
    s_new = [s_raw[pi] * gam_end[:, sls[pi]] + jnp.where(same_head, upd[pi], 0.0) for pi in pairs]

    yc = [y_ - segsum(y_) * inv_hd for y_ in y]
    y_var = [segsum(c_ * c_) * inv_hd for c_ in yc]
    for pi in pairs:
        sl = sls[pi]
        yn = yc[pi] * lax.rsqrt(y_var[pi] + GN_EPS) * lnw[:, sl] + lnb[:, sl]
        y_ref[rs, sl] = ((yn + bonus[pi]) * g[rs, sl]).astype(BF16)
    return s_new


def _wkv(p, l, mu, shift0, w2, a2, g2, w0, a0, kkp, ka, rk, lnw, lnb, state0, n_seq, seq_len, C, n_sub):
    m, n_pad = p.shape
    d_rwkv = w0.shape[-1]
    n_heads = d_rwkv // RWKV_HEAD
    n_pairs = WKV_PAIRS_PER_STEP
    wd = n_pairs * LANES
    n_groups = d_rwkv // wd
    cb = C * n_sub
    n_chunks = seq_len // cb
    nb = n_groups
    blk_w = 3 * d_rwkv // LANES
    blk_a = blk_w + 1
    blk_g = (blk_a + 1) * LANES // GATE_LORA

    row = lambda s, h, c: s * n_chunks + c
    seg_specs = [
        pl.BlockSpec((cb, wd), lambda s, h, c: (row(s, h, c), h)),
        pl.BlockSpec((cb, wd), lambda s, h, c: (row(s, h, c), nb + h)),
        pl.BlockSpec((cb, wd), lambda s, h, c: (row(s, h, c), 2 * nb + h)),
        pl.BlockSpec((cb, LANES), lambda s, h, c: (row(s, h, c), blk_w)),
        pl.BlockSpec((cb, LANES), lambda s, h, c: (row(s, h, c), blk_a)),
        pl.BlockSpec((cb, GATE_LORA), lambda s, h, c: (row(s, h, c), blk_g)),
    ]
    mu_specs = [
        pl.BlockSpec((None, 1, wd), lambda s, h, c: (l, 0, h)),
        pl.BlockSpec((None, 1, wd), lambda s, h, c: (l, 0, nb + h)),
        pl.BlockSpec((None, 1, wd), lambda s, h, c: (l, 0, 2 * nb + h)),
        pl.BlockSpec((None, 1, LANES), lambda s, h, c: (l, 0, blk_w)),
        pl.BlockSpec((None, 1, LANES), lambda s, h, c: (l, 0, blk_a)),
        pl.BlockSpec((None, 1, GATE_LORA), lambda s, h, c: (l, 0, blk_g)),
    ]
    s0_specs = [
        pl.BlockSpec((None, 1, wd), lambda s, h, c: (s, 0, h)),
        pl.BlockSpec((None, 1, wd), lambda s, h, c: (s, 0, nb + h)),
        pl.BlockSpec((None, 1, wd), lambda s, h, c: (s, 0, 2 * nb + h)),
        pl.BlockSpec((None, 1, LANES), lambda s, h, c: (s, 0, blk_w)),
        pl.BlockSpec((None, 1, LANES), lambda s, h, c: (s, 0, blk_a)),
        pl.BlockSpec((None, 1, GATE_LORA), lambda s, h, c: (s, 0, blk_g)),
    ]
    lora_specs = [
        pl.BlockSpec((None, LORA_PAD, wd), lambda s, h, c: (l, 0, h)),
        pl.BlockSpec((None, LORA_PAD, wd), lambda s, h, c: (l, 0, h)),
        pl.BlockSpec((None, GATE_LORA, wd), lambda s, h, c: (l, 0, h)),
    ]
    vec_spec = pl.BlockSpec((None, 1, wd), lambda s, h, c: (l, 0, h))
    state_spec = pl.BlockSpec((None, 2 * n_pairs, RWKV_HEAD, RWKV_HEAD), lambda s, h, c: (s, h, 0, 0))
    y, st = pl.pallas_call(
        functools.partial(_wkv_kernel, C=C, n_sub=n_sub, n_chunks=n_chunks, n_pairs=n_pairs),
        grid=(n_seq, n_groups, n_chunks),
        in_specs=seg_specs + mu_specs + s0_specs + lora_specs + [vec_spec] * 7 + [state_spec],
        out_specs=[
            pl.BlockSpec((cb, wd), lambda s, h, c: (row(s, h, c), h)),
            state_spec,
        ],
        out_shape=[
            jax.ShapeDtypeStruct((m, d_rwkv), BF16),
            jax.ShapeDtypeStruct((n_seq, n_heads, RWKV_HEAD, RWKV_HEAD), F32),
        ],
        scratch_shapes=[pltpu.VMEM((n_pairs, LANES, LANES), F32)]
        + [pltpu.VMEM((SUBLANES, wd), F32)] * 3 + [pltpu.VMEM((SUBLANES, LANES), F32)] * 2
        + [pltpu.VMEM((SUBLANES, GATE_LORA), F32)],
        compiler_params=_cparams(("arbitrary", "arbitrary", "arbitrary")),
        name="wkv",
    )(p, p, p, p, p, p, mu, mu, mu, mu, mu, mu,
      shift0, shift0, shift0, shift0, shift0, shift0,
      w2, a2, g2, w0, a0, kkp, ka, rk, lnw, lnb, state0)
    return y, st


def _lambda(lqk_ref, lam_init):
    t = lqk_ref[...]
    s1 = jnp.sum(t[0:1, :] * t[1:2, :], axis=-1, keepdims=True)
    s2 = jnp.sum(t[2:3, :] * t[3:4, :], axis=-1, keepdims=True)
    return jnp.exp(s1) - jnp.exp(s2) + lam_init


def _split_maps(q):
    lane = lax.broadcasted_iota(jnp.int32, q.shape, 1)
    zero = jnp.zeros_like(q)
    return jnp.where(lane < DIFF_HEAD, q, zero), jnp.where(lane < DIFF_HEAD, zero, q)


def _subln(o, subln, lam_init):
    return _rms(o, subln, SUBLN_EPS) * (1.0 - lam_init)


FAR_STEP_TILES = (4, 1)
ATTN_TILES_PER_STEP = 4


def _attn_kernel(q_ref, k_ref, v_ref, bias_ref, lqk_ref, subln_ref, o_ref,
                 kb, vt, m_s, a_s, *, TB, n_cast, lam_init):
    g = pl.program_id(1)

    @pl.when(g == 0)
    def _():
        vt[LANES:, :] = jnp.ones((vt.shape[0] - LANES, vt.shape[1]), BF16)

        def cast(c, carry):
            off = pl.multiple_of(c * TB, TB)
            kb[pl.ds(off, TB), :] = k_ref[pl.ds(off, TB), :].astype(BF16)
            vt[0:LANES, pl.ds(off, TB)] = v_ref[pl.ds(off, TB), :].T.astype(BF16)
            return carry
        lax.fori_loop(0, n_cast, cast, 0)

    nq = ATTN_TILES_PER_STEP
    chains = range(nq)
    qs = [jnp.concatenate(_split_maps(q_ref[c * TB:(c + 1) * TB, :]), axis=0) for c in chains]
    m_s[...] = jnp.full(m_s.shape, MASK_VALUE, F32)
    a_s[...] = jnp.zeros(a_s.shape, F32)

    def scores(c, off, width):
        return _mm(kb[pl.ds(off, width), :], qs[c], _NT)

    def update(c, s, off, width):
        m_prev = m_s[c]
        m_new = jnp.maximum(m_prev, jnp.max(s, axis=0, keepdims=True))
        alpha = jnp.exp(m_prev - m_new)
        p = jnp.exp(s - m_new).astype(BF16)
        a_s[c] = alpha * a_s[c] + _mm(vt[:, pl.ds(off, width)], p)
        m_s[c] = m_new

    def far_all(off, width):
        s = [scores(c, off, width) for c in chains]
        for c in chains:
            update(c, s[c], off, width)

    n_common = jnp.maximum(nq * g - 1, 0)
    done = 0
    for step_tiles in FAR_STEP_TILES:
        width = TB * step_tiles
        n_steps = (n_common - done) // step_tiles

        def far_step(j, carry, done=done, width=width):
            far_all(pl.multiple_of(done * TB + j * width, TB), width)
            return carry
        lax.fori_loop(0, n_steps, far_step, 0)
        done = done + n_steps * step_tiles

    def own_far(first_chain, off0):
        for k in range(nq - first_chain):
            off = pl.multiple_of(off0 + k * TB, TB)
            cs = range(first_chain + k, nq)
            s = [scores(c, off, TB) for c in cs]
            for c, s_ in zip(cs, s):
                update(c, s_, off, TB)

    @pl.when(g > 0)
    def _():
        own_far(1, n_common * TB)

    @pl.when(g == 0)
    def _():
        own_far(2, 0)

    offs = [pl.multiple_of(jnp.maximum(nq * g + c - 1, 0) * TB, TB) for c in chains]
    s_near = [scores(c, offs[c], 2 * TB) + (bias_ref[jnp.where(g == 0, 1, 0)] if c == 0 else bias_ref[0])
              for c in chains]
    for c in chains:
        update(c, s_near[c], offs[c], 2 * TB)

    lam = _lambda(lqk_ref, lam_init)
    for c in chains:
        acc = a_s[c]
        on = acc[0:LANES, :] / acc[LANES:LANES + 1, :]
        o = (on[:, :TB] - lam * on[:, TB:]).T
        o_ref[c * TB:(c + 1) * TB, :] = _subln(o, subln_ref[...], lam_init).astype(BF16)


def _attn_prompt(q, k_all, v_all, l, bias, lqk, subln, lam_init, TB):
    m, d_diff = q.shape
    n_heads = d_diff // LANES
    n_blk = m // TB
    nq = ATTN_TILES_PER_STEP
    assert n_blk % nq == 0
    return pl.pallas_call(
        functools.partial(_attn_kernel, TB=TB, n_cast=n_blk, lam_init=lam_init),
        grid=(n_heads, n_blk // nq),
        in_specs=[
            pl.BlockSpec((nq * TB, LANES), lambda h, g: (g, h)),
            pl.BlockSpec((None, m, LANES), lambda h, g: (l, 0, h)),
            pl.BlockSpec((None, m, LANES), lambda h, g: (l, 0, h)),
            pl.BlockSpec((None, 2, 2 * TB, 2 * TB), lambda h, g: (h, 0, 0, 0)),
            pl.BlockSpec((None, 4, DIFF_HEAD), lambda h, g: (l, 0, 0)),
            pl.BlockSpec((None, 1, LANES), lambda h, g: (l, 0, 0)),
        ],
        out_specs=pl.BlockSpec((nq * TB, LANES), lambda h, g: (g, h)),
        out_shape=jax.ShapeDtypeStruct((m, d_diff), BF16),
        scratch_shapes=[pltpu.VMEM((m, LANES), BF16), pltpu.VMEM((LANES + BF16_ROWS, m), BF16),
                        pltpu.VMEM((nq, 1, 2 * TB), F32), pltpu.VMEM((nq, LANES + BF16_ROWS, 2 * TB), F32)],
        compiler_params=_cparams(("arbitrary", "arbitrary")),
        name="attn_prompt",
    )(q, k_all, v_all, bias, lqk, subln)


SAMPLE_FAR_FRAMES = 384


def _attn_sample_kernel(q_ref, kn_ref, vn_ref, kc_ref, vc_ref, bias_ref, lqk_ref, subln_ref, o_ref,
                        kb, acc, *, T, n_heads, near, lam_init):
    past = kc_ref.shape[0]
    n_cols = 2 * T * n_heads
    fc = SAMPLE_FAR_FRAMES
    n_far = (past - near) // fc
    rows = fc * n_heads
    qs = jnp.concatenate(
        [m_ for h in range(n_heads) for m_ in _split_maps(q_ref[:, h * LANES:(h + 1) * LANES])], axis=0)
    sub = lax.broadcasted_iota(jnp.int32, (n_heads, n_cols), 0)
    col = lax.broadcasted_iota(jnp.int32, (n_heads, n_cols), 1)
    valid = (col >= sub * (2 * T)) & (col < (sub + 1) * (2 * T))

    def flat(x3):
        return x3.reshape(x3.shape[0] * n_heads, LANES)

    def scores(k_rows):
        s = _mm(k_rows, qs, _NT)
        return s.reshape(s.shape[0] // n_heads, n_heads, n_cols)

    near_k = jnp.concatenate([flat(kc_ref[past - near:past]), flat(kn_ref[...])], axis=0).astype(BF16)
    near_v = jnp.concatenate([flat(vc_ref[past - near:past]), flat(vn_ref[...])], axis=0).astype(BF16)
    s_near = scores(near_k) + bias_ref[...].reshape(near + T, n_heads, n_cols)

    def pass1(c, m3):
        f0 = pl.multiple_of(c * fc, fc)
        k_rows = flat(kc_ref[pl.ds(f0, fc)]).astype(BF16)
        kb[pl.ds(pl.multiple_of(c * rows, rows), rows), :] = k_rows
        return jnp.maximum(m3, jnp.max(scores(k_rows), axis=0))
    m3 = lax.fori_loop(0, n_far, pass1, jnp.max(s_near, axis=0))

    ones = jnp.ones((rows, LANES), BF16)

    def weights(s3):
        p = jnp.where(valid, jnp.exp(s3 - m3), 0.0)
        return p.reshape(p.shape[0] * n_heads, n_cols).astype(BF16)

    acc[...] = _mm(weights(s_near), jnp.concatenate([near_v, ones[:near_v.shape[0]]], axis=1), _TN)

    def pass2(c, carry):
        f0 = pl.multiple_of(c * fc, fc)
        p = weights(scores(kb[pl.ds(pl.multiple_of(c * rows, rows), rows), :]))
        v_aug = jnp.concatenate([flat(vc_ref[pl.ds(f0, fc)]).astype(BF16), ones], axis=1)
        acc[...] += _mm(p, v_aug, _TN)
        return carry
    lax.fori_loop(0, n_far, pass2, 0)

    lam = _lambda(lqk_ref, lam_init)
    for h in range(n_heads):
        a = acc[h * 2 * T:(h + 1) * 2 * T, :]
        on = a[:, 0:LANES] / a[:, LANES:2 * LANES]
        o = on[:T, :] - lam * on[T:, :]
        o_ref[:, h * LANES:(h + 1) * LANES] = _subln(o, subln_ref[...], lam_init).astype(BF16)


def _attn_sample(q, kn_all, vn_all, l, k_cache, v_cache, bias_t, lqk, subln, lam_init, n_seq, T, near):
    m, d_diff = q.shape
    n_heads = d_diff // LANES
    past = k_cache.shape[2]
    assert (past - near) % SAMPLE_FAR_FRAMES == 0
    cache_spec = pl.BlockSpec((None, None, past, n_heads, LANES), lambda b: (l, b, 0, 0, 0))
    new_spec = pl.BlockSpec((None, T, n_heads, LANES), lambda b: (l, b, 0, 0))
    return pl.pallas_call(
        functools.partial(_attn_sample_kernel, T=T, n_heads=n_heads, near=near, lam_init=lam_init),
        grid=(n_seq,),
        in_specs=[
            pl.BlockSpec((T, d_diff), lambda b: (b, 0)),
            new_spec,
            new_spec,
            cache_spec,
            cache_spec,
            pl.BlockSpec(bias_t.shape, lambda b: (0, 0)),
            pl.BlockSpec((None, 4, DIFF_HEAD), lambda b: (l, 0, 0)),
            pl.BlockSpec((None, 1, LANES), lambda b: (l, 0, 0)),
        ],
        out_specs=pl.BlockSpec((T, d_diff), lambda b: (b, 0)),
        out_shape=jax.ShapeDtypeStruct((m, d_diff), BF16),
        scratch_shapes=[pltpu.VMEM(((past - near) * n_heads, LANES), BF16),
                        pltpu.VMEM((2 * T * n_heads, 2 * LANES), F32)],
        compiler_params=_cparams(("arbitrary",)),
        name="attn_sample",
    )(q, kn_all, vn_all, k_cache, v_cache, bias_t, lqk, subln)


def _outproj_kernel(x_ref, yr_ref, yd_ref, gt_ref, wt_ref, wb_ref, o_ref):
    mixed = _mm(yr_ref[...], wt_ref[...]) + _mm(yd_ref[...], wb_ref[...])
    o_ref[...] = x_ref[...] + gt_ref[...] * mixed


def _outproj(x, yr, yd, gt, w_out_all, l, tm):
    m, d = x.shape
    dr = yr.shape[1]
    dd = yd.shape[1]
    gt_spec = (pl.BlockSpec((1, d), lambda i: (0, 0)) if gt.shape[0] == 1
               else pl.BlockSpec((tm, d), lambda i: (i, 0)))
    return pl.pallas_call(
        _outproj_kernel,
        grid=(m // tm,),
        in_specs=[
            pl.BlockSpec((tm, d), lambda i: (i, 0)),
            pl.BlockSpec((tm, dr), lambda i: (i, 0)),
            pl.BlockSpec((tm, dd), lambda i: (i, 0)),
            gt_spec,
            pl.BlockSpec((None, dr, d), lambda i: (l, 0, 0)),
            pl.BlockSpec((None, dd, d), lambda i: (l, dr // dd, 0)),
        ],
        out_specs=pl.BlockSpec((tm, d), lambda i: (i, 0)),
        out_shape=jax.ShapeDtypeStruct((m, d), F32),
        compiler_params=_cparams(("arbitrary",)),
        name="outproj",
    )(x, yr, yd, gt, w_out_all, w_out_all)


def _shift_rows(x, prev, n):
    if n % SUBLANES == 0:
        return jnp.concatenate([prev, x[:x.shape[0] - n, :]], axis=0)
    rolled = pltpu.roll(x, n, 0)
    row = lax.broadcasted_iota(jnp.int32, x.shape, 0)
    out = rolled
    for r in range(n):
        out = jnp.where(row == r, prev[r:r + 1, :], out)
    return out


def _ffn_kernel(x_ref, sh_ref, sc_ref, gt_ref, g_ref, wg_ref, wv_ref, cw_ref, cb_ref, c0_ref, wd_ref, gf_ref,
                o_ref, co_ref, h_scr, acc_scr, carry, *, B, n_j, final_norm):
    i = pl.program_id(0)
    j = pl.program_id(1)
    tm = x_ref.shape[0]

    @pl.when(j == 0)
    def _():
        xn = _rms(x_ref[...], g_ref[...], NORM_EPS)
        h_scr[...] = (xn * (1.0 + sc_ref[...]) + sh_ref[...]).astype(BF16)
        acc_scr[...] = jnp.zeros(acc_scr.shape, F32)

    @pl.when(i == 0)
    def _():
        carry[j, 0:2 * B, :] = c0_ref[...]

    h = h_scr[...]
    ug = _mm(h, wg_ref[...])
    uv = _mm(h, wv_ref[...])
    prev = carry[j, 0:2 * B, :]
    s1 = _shift_rows(ug, prev[B:2 * B, :], B)
    s2 = _shift_rows(ug, prev, 2 * B)
    cw = cw_ref[...]
    z = cb_ref[...] + s2 * cw[0:1, :] + s1 * cw[1:2, :] + ug * cw[2:3, :]
    act = 0.5 * z * (1.0 + lax.erf(z * (2.0 ** -0.5)))
    acc_scr[...] += _mm((act * uv).astype(BF16), wd_ref[...])
    last2 = ug[tm - 2 * B:tm, :]
    carry[j, 0:2 * B, :] = last2
    tf = last2.shape[1]
    co_ref[:, pl.ds(pl.multiple_of(j * tf, tf), tf)] = last2

    @pl.when(j == n_j - 1)
    def _():
        x2 = x_ref[...] + gt_ref[...] * acc_scr[...]
        if final_norm:
            x2 = _rms(x2, gf_ref[...], NORM_EPS)
        o_ref[...] = x2


def _ffn(x, sh, sc, gt, g_all, w_up_all, conv_w_all, conv_b_all, conv0, w_down_all, g_final, l, B, tm,
         final_norm, tf=512):
    m, d = x.shape
    d_ff = w_down_all.shape[1]
    n_j = d_ff // tf
    rows_c = max(SUBLANES, 2 * B)
    return pl.pallas_call(
        functools.partial(_ffn_kernel, B=B, n_j=n_j, final_norm=final_norm),
        grid=(m // tm, n_j),
        in_specs=[
            pl.BlockSpec((tm, d), lambda i, j: (i, 0)),
            _mod_spec(sh, tm, d),
            _mod_spec(sc, tm, d),
            _mod_spec(gt, tm, d),
            pl.BlockSpec((None, 1, d), lambda i, j: (l, 0, 0)),
            pl.BlockSpec((None, d, tf), lambda i, j: (l, 0, j)),
            pl.BlockSpec((None, d, tf), lambda i, j: (l, 0, n_j + j)),
            pl.BlockSpec((None, CONV_W, tf), lambda i, j: (l, 0, j)),
            pl.BlockSpec((None, 1, tf), lambda i, j: (l, 0, j)),
            pl.BlockSpec((2 * B, tf), lambda i, j: (0, j)),
            pl.BlockSpec((None, tf, d), lambda i, j: (l, j, 0)),
            pl.BlockSpec((1, d), lambda i, j: (0, 0)),
        ],
        out_specs=[
            pl.BlockSpec((tm, d), lambda i, j: (i, 0)),
            pl.BlockSpec((2 * B, d_ff), lambda i, j: (0, 0)),
        ],
        out_shape=[
            jax.ShapeDtypeStruct((m, d), F32),
            jax.ShapeDtypeStruct((2 * B, d_ff), F32),
        ],
        scratch_shapes=[pltpu.VMEM((tm, d), BF16), pltpu.VMEM((tm, d), F32),
                        pltpu.VMEM((n_j, rows_c, tf), F32)],
        compiler_params=_cparams(("arbitrary", "arbitrary")),
        name="conv_ffn",
    )(x, sh, sc, gt, g_all, w_up_all, w_up_all, conv_w_all, conv_b_all, conv0, w_down_all, g_final)


def _pad_cols(a, n):
    return jnp.pad(a, [(0, 0)] * (a.ndim - 1) + [(0, n)])


def _regroup_rwkv_cols(a, d_rwkv):
    o = 3 * d_rwkv
    return jnp.concatenate([
        a[..., :o],
        _pad_cols(a[..., o:o + DECAY_LORA], LORA_PAD - DECAY_LORA),
        _pad_cols(a[..., o + DECAY_LORA:o + DECAY_LORA + AAA_LORA], LORA_PAD - AAA_LORA),
        a[..., o + DECAY_LORA + AAA_LORA:],
    ], axis=-1)


def _ungroup_rwkv_cols(a, d_rwkv):
    o = 3 * d_rwkv
    return jnp.concatenate([
        a[..., :o],
        a[..., o:o + DECAY_LORA],
        a[..., o + LORA_PAD:o + LORA_PAD + AAA_LORA],
        a[..., o + 2 * LORA_PAD:],
    ], axis=-1)


def _prompt_tiles(seq):
    attn_tile = min(256, seq // 2)
    assert attn_tile % CHUNK == 0 and attn_tile >= MAX_DISTANCE
    assert seq % (ATTN_TILES_PER_STEP * attn_tile) == 0
    return attn_tile, min(512, seq), min(1024, seq), min(64, seq)


def kernel(x_prompt, x_sample, c_prompt, c_sample, cache_k, cache_v, state_wkv, state_shift, state_conv, w_ada, b_ada, g_mix, g_ffn, w_in, w_out, rwkv_mu, rwkv_w0, rwkv_w2, rwkv_a0, rwkv_a2, rwkv_g2, rwkv_kk, rwkv_ka, rwkv_rk, rwkv_ln_w, rwkv_ln_b, diff_lq1, diff_lk1, diff_lq2, diff_lk2, diff_subln, rel_table, ffn_up, ffn_conv_w, ffn_conv_b, ffn_down, g_final):
    depth, d_model, _ = w_in.shape
    bp, seq, _ = x_prompt.shape
    bs, dseq, _ = x_sample.shape
    past = cache_k.shape[2]
    d_rwkv = rwkv_w0.shape[1]
    d_diff = d_model - d_rwkv
    n_dheads = d_diff // (2 * DIFF_HEAD)
    n_rheads = d_rwkv // RWKV_HEAD
    n_rwkv_cols = rwkv_mu.shape[1]
    n_rwkv_pad = n_rwkv_cols + 2 * LORA_PAD - DECAY_LORA - AAA_LORA
    d_ff = ffn_down.shape[1]
    assert bp == 1, "prompt path handles one sequence"
    TB, tm_p, tm_in, wkv_chunk = _prompt_tiles(seq)
    m_s = bs * dseq

    w_in_b = jnp.concatenate([_regroup_rwkv_cols(w_in[..., :n_rwkv_cols], d_rwkv), w_in[..., n_rwkv_cols:]],
                             axis=-1).astype(BF16)
    mu_p = _regroup_rwkv_cols(rwkv_mu, d_rwkv)[:, None]
    w2_b = jnp.pad(rwkv_w2, ((0, 0), (0, LORA_PAD - DECAY_LORA), (0, 0))).astype(BF16)
    a2_b = jnp.pad(rwkv_a2, ((0, 0), (0, LORA_PAD - AAA_LORA), (0, 0))).astype(BF16)
    g2_b = rwkv_g2.astype(BF16)
    w_out_b = _cast_bf16(w_out)
    up_b = _cast_bf16(ffn_up)
    down_b = _cast_bf16(ffn_down)
    vec = lambda a: a.reshape(depth, 1, -1)
    wkv_args = (w2_b, a2_b, g2_b, vec(rwkv_w0), vec(rwkv_a0), vec(rwkv_kk), vec(rwkv_ka), vec(rwkv_rk),
                vec(rwkv_ln_w), vec(rwkv_ln_b))
    g_mix_v, g_ffn_v, subln_v, conv_b_v = vec(g_mix), vec(g_ffn), vec(diff_subln), vec(ffn_conv_b)
    gf = g_final[None]
    lqk = jnp.stack([diff_lq1, diff_lk1, diff_lq2, diff_lk2], axis=1)

    n_c = bp + bs
    n_c_pad = -(-n_c // SUBLANES) * SUBLANES
    c_rows = jnp.pad(jnp.concatenate([c_prompt, c_sample], axis=0), ((0, n_c_pad - n_c), (0, 0)))
    mod = _adaln_mod(c_rows, w_ada, b_ada).reshape(depth, n_c_pad, 6, d_model)

    ql = jnp.arange(TB)
    rel_diag = ql[None, :] - ql[:, None]
    allowed = (ql[None, :] // CHUNK) <= (ql[:, None] // CHUNK)
    bucket_diag = jnp.where(allowed, _t5_bucket(rel_diag), -1)
    bucket_prev = _t5_bucket(rel_diag - TB)
    windows = [jnp.concatenate([bucket_prev, bucket_diag], axis=1),
               jnp.concatenate([bucket_diag, jnp.full((TB, TB), -1)], axis=1)]
    bucket_p = jnp.concatenate([w.T for w in windows], axis=0)
    bias_p = _bias_lookup(rel_table, bucket_p.astype(jnp.int32), FAR_BUCKET, col_repeats=2)
    bias_p = bias_p.reshape(n_dheads, 2, 2 * TB, 2 * TB)
    q_pos = past + jnp.arange(dseq)
    assert past >= MAX_DISTANCE
    k_pos = past - MAX_DISTANCE + jnp.arange(MAX_DISTANCE + dseq)
    bias_s = _bias_lookup(rel_table, _t5_bucket(k_pos[None, :] - q_pos[:, None]).astype(jnp.int32), FAR_BUCKET)
    bias_t = jnp.broadcast_to(bias_s.transpose(2, 0, 1)[:, None, :, None, :],
                              (MAX_DISTANCE + dseq, n_dheads, n_dheads, 2, dseq))
    bias_t = bias_t.reshape((MAX_DISTANCE + dseq) * n_dheads, n_dheads * 2 * dseq)

    xp = x_prompt.reshape(seq, d_model)
    xs = x_sample.reshape(m_s, d_model)
    zero_shift = jnp.zeros((1, 1, n_rwkv_pad), F32)
    zero_state = jnp.zeros((1, n_rheads, RWKV_HEAD, RWKV_HEAD), F32)
    zero_conv = jnp.zeros((2, d_ff), F32)
    shift_s_in = _regroup_rwkv_cols(state_shift, d_rwkv)
    outs_p = {k: [] for k in ("wkv", "shift", "conv")}
    outs_s = {k: [] for k in ("wkv", "shift", "conv")}
    kv_p = (jnp.zeros((depth, seq, d_diff), F32), jnp.zeros((depth, seq, d_diff), F32))
    kv_s = (jnp.zeros((depth, m_s, d_diff), F32), jnp.zeros((depth, m_s, d_diff), F32))

    def time_major(a):
        return a.reshape(bs, dseq, -1).swapaxes(0, 1).reshape(m_s, -1)

    def batch_major(a):
        return a.reshape(dseq, bs, -1).swapaxes(0, 1).reshape(m_s, -1)

    for l in range(depth):
        lam_init = 0.8 - 0.6 * math.exp(-0.3 * l)
        last = l == depth - 1

        mp = mod[l, 0:bp]
        sh1, sc1, gt1, sh2, sc2, gt2 = (mp[:, t] for t in range(6))
        p, q, *kv_p = _inproj(xp, sh1, sc1, g_mix_v, w_in_b, l, kv_p, n_rwkv_pad, d_diff, tm_in)
        y_r, wkv = _wkv(p, l, mu_p, zero_shift, *wkv_args, zero_state, 1, seq, wkv_chunk,
                        WKV_CHUNKS_PER_STEP if seq % (WKV_CHUNKS_PER_STEP * wkv_chunk) == 0 else 1)
        y_d = _attn_prompt(q, kv_p[0], kv_p[1], l, bias_p, lqk, subln_v, lam_init, TB)
        x1 = _outproj(xp, y_r, y_d, gt1, w_out_b, l, tm_p)
        xp, conv = _ffn(x1, sh2, sc2, gt2, g_ffn_v, up_b, ffn_conv_w, conv_b_v, zero_conv, down_b, gf, l,
                        1, tm_p, last)
        outs_p["wkv"].append(wkv)
        outs_p["shift"].append(_ungroup_rwkv_cols(p[seq - 1:seq], d_rwkv).reshape(bp, 1, n_rwkv_cols))
        outs_p["conv"].append(conv.reshape(bp, CONV_W - 1, d_ff))

        ms = mod[l, bp:bp + bs]
        rows_bm = jnp.repeat(ms, dseq, axis=0)
        rows_tm = jnp.tile(ms, (dseq, 1, 1))
        p, q, *kv_s = _inproj(xs, rows_bm[:, 0], rows_bm[:, 1], g_mix_v, w_in_b, l, kv_s, n_rwkv_pad, d_diff, m_s)
        y_r, wkv = _wkv(p, l, mu_p, shift_s_in[l], *wkv_args, state_wkv[l], bs, dseq, dseq, 1)
        kn4, vn4 = (a.reshape(depth, m_s, n_dheads, 2 * DIFF_HEAD) for a in kv_s)
        y_d = _attn_sample(q, kn4, vn4, l, cache_k, cache_v, bias_t, lqk, subln_v, lam_init, bs, dseq,
                           MAX_DISTANCE)
        x1 = _outproj(xs, y_r, y_d, rows_bm[:, 2], w_out_b, l, m_s)
        conv0 = state_conv[l].swapaxes(0, 1).reshape((CONV_W - 1) * bs, d_ff)
        x2, conv = _ffn(time_major(x1), rows_tm[:, 3], rows_tm[:, 4], rows_tm[:, 5], g_ffn_v, up_b, ffn_conv_w,
                        conv_b_v, conv0, down_b, gf, l, bs, m_s, last)
        xs = batch_major(x2)
        outs_s["wkv"].append(wkv)
        p_last = p.reshape(bs, dseq, n_rwkv_pad)[:, dseq - 1:dseq]
        outs_s["shift"].append(_ungroup_rwkv_cols(p_last, d_rwkv))
        outs_s["conv"].append(conv.reshape(CONV_W - 1, bs, d_ff).swapaxes(0, 1))

    st = lambda xs_: jnp.stack(xs_)
    head_shape = (n_dheads, 2 * DIFF_HEAD)
    return (xp.reshape(bp, seq, d_model), xs.reshape(bs, dseq, d_model),
            kv_p[0].reshape(depth, bp, seq, *head_shape), kv_p[1].reshape(depth, bp, seq, *head_shape),
            st(outs_p["wkv"]), st(outs_p["shift"]), st(outs_p["conv"]),
            kv_s[0].reshape(depth, bs, dseq, *head_shape), kv_s[1].reshape(depth, bs, dseq, *head_shape),
            st(outs_s["wkv"]), st(outs_s["shift"]), st(outs_s["conv"]))
```

```python
import functools
import math

import jax
import jax.numpy as jnp
from jax import lax
from jax.experimental import pallas as pl
from jax.experimental.pallas import tpu as pltpu

F32 = jnp.float32
BF16 = jnp.bfloat16

CHUNK = 64
RWKV_HEAD = 64
DIFF_HEAD = 64
N_BUCKETS = 32
MAX_DISTANCE = 128
NORM_EPS = 1e-6
GN_EPS = 64e-5
SUBLN_EPS = 1e-5
DECAY_LORA = 96
AAA_LORA = 96
GATE_LORA = 256
LORA_PAD = 128
CONV_W = 3

LANES = 128
SUBLANES = 8
BF16_ROWS = 16
VMEM_LIMIT_BYTES = 56 * 1024 * 1024
MASK_VALUE = -1e30
WKV_CHUNKS_PER_STEP = 2
WKV_PAIRS_PER_STEP = 8

_NN = (((1,), (0,)), ((), ()))
_NT = (((1,), (1,)), ((), ()))
_TN = (((0,), (0,)), ((), ()))


def _mm(a, b, dims=_NN):
    return lax.dot_general(a, b, dims, preferred_element_type=F32)


def _hi_lo(x):
    h = x.astype(BF16)
    return h, (x - h.astype(F32)).astype(BF16)


def _dot3(a, b, dims=_NN):
    (ah, al), (bh, bl) = a, b
    (ca,), (cb,) = dims[0]
    k = ah.shape[ca]
    if (ca == 0 or k % LANES == 0) and (cb == 0 or k % LANES == 0):
        return _mm(jnp.concatenate([ah, al, ah], axis=ca), jnp.concatenate([bh, bh, bl], axis=cb), dims)
    return _mm(ah, bh, dims) + (_mm(al, bh, dims) + _mm(ah, bl, dims))


def _mm3(a, b, dims=_NN):
    return _dot3(_hi_lo(a), _hi_lo(b), dims)


def _sel_rows(sel2, x):
    xh, xl = _hi_lo(x)
    return _mm(sel2, jnp.concatenate([xh, xl], axis=0))


def _sel_lanes(x, sel2):
    xh, xl = _hi_lo(x)
    return _mm(jnp.concatenate([xh, xl], axis=1), sel2)


def _cparams(sem):
    return pltpu.CompilerParams(dimension_semantics=sem, vmem_limit_bytes=VMEM_LIMIT_BYTES)


def _rms(x, g, eps):
    return x * lax.rsqrt(jnp.mean(x * x, axis=-1, keepdims=True) + eps) * g


def _cast_kernel(x_ref, o_ref):
    o_ref[...] = x_ref[...].astype(BF16)


CAST_BLOCK_BYTES = 8 * 1024 * 1024


def _cast_bf16(w):
    depth, rows, cols = w.shape
    tc = 1024 if cols % 1024 == 0 else cols
    tr = max(t for t in range(BF16_ROWS, rows + 1, BF16_ROWS)
             if rows % t == 0 and t * tc * 4 <= CAST_BLOCK_BYTES)
    spec = pl.BlockSpec((None, tr, tc), lambda l, i, j: (l, i, j))
    return pl.pallas_call(
        _cast_kernel,
        grid=(depth, rows // tr, cols // tc),
        in_specs=[spec],
        out_specs=spec,
        out_shape=jax.ShapeDtypeStruct(w.shape, BF16),
        compiler_params=_cparams(("arbitrary", "arbitrary", "arbitrary")),
        name="cast_bf16",
    )(w)


def _mod_kernel(c_ref, w_ref, b_ref, o_ref):
    c = c_ref[...]
    s = (c * jax.nn.sigmoid(c)).astype(BF16)
    o_ref[...] = _mm(s, w_ref[...].astype(BF16)) + b_ref[...]


def _adaln_mod(c_rows, w_ada, b_ada, tn=1024):
    depth, d, n = w_ada.shape
    rows = c_rows.shape[0]
    return pl.pallas_call(
        _mod_kernel,
        grid=(depth, n // tn),
        in_specs=[
            pl.BlockSpec((rows, d), lambda l, j: (0, 0)),
            pl.BlockSpec((None, d, tn), lambda l, j: (l, 0, j)),
            pl.BlockSpec((None, 1, tn), lambda l, j: (l, 0, j)),
        ],
        out_specs=pl.BlockSpec((None, rows, tn), lambda l, j: (l, 0, j)),
        out_shape=jax.ShapeDtypeStruct((depth, rows, n), F32),
        compiler_params=_cparams(("arbitrary", "arbitrary")),
        name="adaln_mod",
    )(c_rows, w_ada, b_ada.reshape(depth, 1, n))


def _bias_kernel(tab_ref, bucket_ref, o_ref, *, far_bucket):
    h = pl.program_id(0)
    bucket = bucket_ref[...]
    far = tab_ref[far_bucket, h]
    acc = jnp.full(bucket.shape, MASK_VALUE, F32)
    for b in range(N_BUCKETS):
        acc = jnp.where(bucket == b, tab_ref[b, h] - far, acc)
    c = bucket.shape[1]
    for k in range(o_ref.shape[1] // c):
        o_ref[:, k * c:(k + 1) * c] = acc


def _bias_lookup(table, bucket, far_bucket, col_repeats=1):
    n_heads = table.shape[1]
    r, c = bucket.shape
    return pl.pallas_call(
        functools.partial(_bias_kernel, far_bucket=far_bucket),
        grid=(n_heads,),
        in_specs=[
            pl.BlockSpec(memory_space=pltpu.SMEM),
            pl.BlockSpec((r, c), lambda h: (0, 0)),
        ],
        out_specs=pl.BlockSpec((None, r, c * col_repeats), lambda h: (h, 0, 0)),
        out_shape=jax.ShapeDtypeStruct((n_heads, r, c * col_repeats), F32),
        compiler_params=_cparams(("arbitrary",)),
        name="bias_lookup",
    )(table, bucket)


def _t5_bucket(rel):
    nb = N_BUCKETS // 2
    max_exact = nb // 2
    n = jnp.abs(rel)
    nf = jnp.maximum(n, 1).astype(F32)
    large = max_exact + (jnp.log(nf / max_exact) / math.log(MAX_DISTANCE / max_exact) * (nb - max_exact)).astype(jnp.int32)
    large = jnp.minimum(large, nb - 1)
    return jnp.where(rel > 0, nb, 0) + jnp.where(n < max_exact, n, large)


FAR_BUCKET = N_BUCKETS // 2 - 1


def _inproj_kernel(x_ref, sh_ref, sc_ref, g_ref, w_ref, *rest, n_p, n_q):
    p_ref, q_ref, k_ref, v_ref, h_scr = rest[-5:]
    j = pl.program_id(1)

    @pl.when(j == 0)
    def _():
        xn = _rms(x_ref[...], g_ref[...], NORM_EPS)
        h_scr[...] = (xn * (1.0 + sc_ref[...]) + sh_ref[...]).astype(BF16)

    @pl.when(j < n_p)
    def _():
        p_ref[...] = _mm(h_scr[...], w_ref[...])

    @pl.when((j >= n_p) & (j < n_p + n_q))
    def _():
        q_ref[...] = (_mm(h_scr[...], w_ref[...]) * (DIFF_HEAD ** -0.5)).astype(BF16)

    @pl.when((j >= n_p + n_q) & (j < n_p + 2 * n_q))
    def _():
        k_ref[...] = _mm(h_scr[...], w_ref[...])

    @pl.when(j >= n_p + 2 * n_q)
    def _():
        v_ref[...] = _mm(h_scr[...], w_ref[...])


def _mod_spec(mod, tm, d):
    if mod.shape[0] == 1:
        return pl.BlockSpec((1, d), lambda i, j: (0, 0))
    return pl.BlockSpec((tm, d), lambda i, j: (i, 0))


def _inproj(x, sh, sc, g_all, w_all, l, kv_all, n_rwkv_pad, d_diff, tm, tn=512):
    m, d = x.shape
    depth, _, n = w_all.shape
    n_p, n_q = n_rwkv_pad // tn, d_diff // tn
    clip = lambda j, lo: jnp.clip(j - lo, 0, n_q - 1)
    in_specs = [
        pl.BlockSpec((tm, d), lambda i, j: (i, 0)),
        _mod_spec(sh, tm, d),
        _mod_spec(sc, tm, d),
        pl.BlockSpec((None, 1, d), lambda i, j: (l, 0, 0)),
        pl.BlockSpec((None, d, tn), lambda i, j: (l, 0, j)),
        pl.BlockSpec(memory_space=pl.ANY),
        pl.BlockSpec(memory_space=pl.ANY),
    ]
    args = [x, sh, sc, g_all, w_all, kv_all[0], kv_all[1]]
    aliases = {5: 2, 6: 3}
    return pl.pallas_call(
        functools.partial(_inproj_kernel, n_p=n_p, n_q=n_q),
        grid=(m // tm, n // tn),
        in_specs=in_specs,
        out_specs=[
            pl.BlockSpec((tm, tn), lambda i, j: (i, jnp.minimum(j, n_p - 1))),
            pl.BlockSpec((tm, tn), lambda i, j: (i, clip(j, n_p))),
            pl.BlockSpec((None, tm, tn), lambda i, j: (l, i, clip(j, n_p + n_q))),
            pl.BlockSpec((None, tm, tn), lambda i, j: (l, i, clip(j, n_p + 2 * n_q))),
        ],
        out_shape=[
            jax.ShapeDtypeStruct((m, n_rwkv_pad), F32),
            jax.ShapeDtypeStruct((m, d_diff), BF16),
            jax.ShapeDtypeStruct((depth, m, d_diff), F32),
            jax.ShapeDtypeStruct((depth, m, d_diff), F32),
        ],
        scratch_shapes=[pltpu.VMEM((tm, d), BF16)],
        input_output_aliases=aliases,
        compiler_params=_cparams(("arbitrary", "arbitrary")),
        name="inproj",
    )(*args)


def _wkv_kernel(pr_ref, pk_ref, pv_ref, pw_ref, pa_ref, pg_ref,
                mur_ref, muk_ref, muv_ref, muw_ref, mua_ref, mug_ref,
                s0r_ref, s0k_ref, s0v_ref, s0w_ref, s0a_ref, s0g_ref,
                w2_ref, a2_ref, g2_ref,
                w0_ref, a0_ref, kkp_ref, ka_ref, rk_ref, lnw_ref, lnb_ref,
                st0_ref,
                y_ref, st_ref,
                sbd, cr, ck, cv, cw, ca, cg, *, C, n_sub, n_chunks, n_pairs):
    chunk = pl.program_id(2)
    CB = n_sub * C
    hd = RWKV_HEAD
    zero_blk = jnp.zeros((hd, hd), F32)

    @pl.when(chunk == 0)
    def _():
        for c_ref, s_ref in ((cr, s0r_ref), (ck, s0k_ref), (cv, s0v_ref), (cw, s0w_ref), (ca, s0a_ref), (cg, s0g_ref)):
            c_ref[0:1, :] = s_ref[...]
        for pi in range(n_pairs):
            top = jnp.concatenate([st0_ref[2 * pi], zero_blk], axis=1)
            bot = jnp.concatenate([zero_blk, st0_ref[2 * pi + 1]], axis=1)
            sbd[pi] = jnp.concatenate([top, bot], axis=0)

    def tshift(x_ref, mu_ref, c_ref):
        x = x_ref[...]
        row = lax.broadcasted_iota(jnp.int32, x.shape, 0)
        prev = jnp.where(row == 0, c_ref[0:1, :], pltpu.roll(x, 1, 0))
        c_ref[0:1, :] = x[CB - 1:CB, :]
        return x + (prev - x) * mu_ref[...]

    xr = tshift(pr_ref, mur_ref, cr)
    xk = tshift(pk_ref, muk_ref, ck)
    xv = tshift(pv_ref, muv_ref, cv)
    xw = tshift(pw_ref, muw_ref, cw)
    xa = tshift(pa_ref, mua_ref, ca)
    xg = tshift(pg_ref, mug_ref, cg)

    li = lax.broadcasted_iota(jnp.int32, (LANES, LANES), 0)
    lj = lax.broadcasted_iota(jnp.int32, (LANES, LANES), 1)
    same_head = (li < hd) == (lj < hd)
    li2 = lax.broadcasted_iota(jnp.int32, (2 * LANES, LANES), 0)
    lj2 = lax.broadcasted_iota(jnp.int32, (2 * LANES, LANES), 1)
    li2 = jnp.where(li2 < LANES, li2, li2 - LANES)
    seg2 = jnp.where((li2 < hd) == (lj2 < hd), 1.0, 0.0).astype(BF16)

    def segsum(x):
        return _sel_lanes(x, seg2)

    wlin = w0_ref[...] + _mm(jnp.tanh(xw).astype(BF16), w2_ref[...])
    z = -wlin
    w_log = -(jnp.maximum(z, 0.0) + jnp.log1p(jnp.exp(-jnp.abs(z)))) - 0.5
    ne = -jnp.exp(w_log)
    a = jax.nn.sigmoid(a0_ref[...] + _mm(xa.astype(BF16), a2_ref[...]))
    g = _mm(jax.nn.sigmoid(xg).astype(BF16), g2_ref[...])
    kk_raw = xk * kkp_ref[...]
    k2 = xk * (1.0 + (a - 1.0) * ka_ref[...])
    rk_prod = xr * k2 * rk_ref[...]

    ti = lax.broadcasted_iota(jnp.int32, (CB, 2 * CB), 0)
    tj = lax.broadcasted_iota(jnp.int32, (CB, 2 * CB), 1)
    tj = jnp.where(tj < CB, tj, tj - CB)
    in_chunk = [(ti >= q * C) & (ti < (q + 1) * C) & (tj >= q * C) & (tj <= ti) for q in range(n_sub)]
    tri2 = jnp.where(functools.reduce(lambda x, y: x | y, in_chunk), 1.0, 0.0).astype(BF16)
    cum = _sel_rows(tri2, ne)
    gam = jnp.exp(cum)
    gprev = jnp.exp(cum - ne)
    ginv = jnp.exp(-cum)
    cum_end = jnp.concatenate([jnp.broadcast_to(cum[(q + 1) * C - 1:(q + 1) * C, :], (C, cum.shape[1]))
                               for q in range(n_sub)], axis=0)
    dte = jnp.exp(cum_end - cum)
    rg_all = xr * gam
    ks_all = k2 * ginv
    kd_all = k2 * dte
    lnw = lnw_ref[...]
    lnb = lnb_ref[...]

    def stack_masked(x):
        first = lax.broadcasted_iota(jnp.int32, (1, x.shape[1]), 1) < x.shape[1] // 2
        return jnp.concatenate([jnp.where(first, x, 0.0), jnp.where(first, 0.0, x)], axis=0)

    R = 2 * C
    ri = lax.broadcasted_iota(jnp.int32, (C, R), 0)
    ci = lax.broadcasted_iota(jnp.int32, (C, R), 1)
    ci = jnp.where(ci < C, ci, ci - C)
    strict = ci < ri
    incl = ci <= ri
    n_steps = max(1, int(math.log2(C)))

    pairs = range(n_pairs)
    sls = [slice(pi * LANES, (pi + 1) * LANES) for pi in pairs]
    sums_all = [segsum(jnp.concatenate([kk_raw[:, sl] * kk_raw[:, sl], rk_prod[:, sl]], axis=0)) for sl in sls]
    kk_all = [kk_raw[:, sl] * lax.rsqrt(jnp.maximum(sm[:CB, :], 1e-24)) for sl, sm in zip(sls, sums_all)]
    s_raw = [sbd[pi] for pi in pairs]
    for q in range(n_sub):
        s_raw = _wkv_chunk(
            q, s_raw, y_ref, sls, stack_masked, segsum, strict, incl, same_head,
            kk_all, [sm[CB:, :] for sm in sums_all], a, xv, gprev, ginv, dte, gam, rg_all, ks_all, kd_all,
            lnw, lnb, g, C=C, n_steps=n_steps)
    for pi in pairs:
        sbd[pi] = s_raw[pi]

    @pl.when(chunk == n_chunks - 1)
    def _():
        for pi in range(n_pairs):
            s_new = sbd[pi]
            st_ref[2 * pi] = s_new[0:hd, 0:hd]
            st_ref[2 * pi + 1] = s_new[hd:2 * hd, hd:2 * hd]


def _wkv_chunk(q, s_raw, y_ref, sls, stack_masked, segsum, strict, incl, same_head,
               kk_all, rksum_all, a, xv, gprev, ginv, dte, gam, rg_all, ks_all, kd_all, lnw, lnb, g, *, C, n_steps):
    rs = slice(q * C, (q + 1) * C)
    R = 2 * C
    inv_hd = 1.0 / RWKV_HEAD
    pairs = range(len(sls))
    xv_p = [xv[rs, sl] for sl in sls]
    kk = [k[rs, :] for k in kk_all]
    bonus = [sm[rs, :] * v for sm, v in zip(rksum_all, xv_p)]
    kka = [k * a[rs, sl] for k, sl in zip(kk, sls)]
    gprev, ginv, dte, rg_all, ks_all, kd_all = (t[rs, :] for t in (gprev, ginv, dte, rg_all, ks_all, kd_all))
    gam_end = gam[(q + 1) * C - 1:(q + 1) * C, :]
    def both(f, pieces):
        return tuple(f(p_) for p_ in pieces)

    def cat(axis, *pairs_of_pieces):
        return tuple(jnp.concatenate(ps, axis=axis) for ps in zip(*pairs_of_pieces))

    la = [_hi_lo(-k * gprev[:, sl]) for k, sl in zip(kk, sls)]
    lr = [rg_all[:, sl].astype(BF16) for sl in sls]
    rhs = [cat(0, both(stack_masked, _hi_lo(ka_ * ginv[:, sl])), both(stack_masked, _hi_lo(ks_all[:, sl])))
           for ka_, sl in zip(kka, sls)]
    xv_s = [_hi_lo(v) for v in xv_p]
    vst = [both(stack_masked, v) for v in xv_s]
    s_old = [_hi_lo(s_) for s_ in s_raw]
    sc_a = [_dot3(l_, r_, _NT) for l_, r_ in zip(la, rhs)]
    sc_r = [_mm(l_, r_[0], _NT) for l_, r_ in zip(lr, rhs)]
    ls_a = [_dot3(l_, s_, _NT) for l_, s_ in zip(la, s_old)]
    ls_r = [_mm(l_, s_[0], _NT) for l_, s_ in zip(lr, s_old)]
    n_ab = [jnp.where(strict, s4[:, :R], 0.0) for s4 in sc_a]
    m_ak = [jnp.where(strict, s4[:, R:], 0.0) for s4 in sc_a]
    m_r = [jnp.concatenate([jnp.where(incl, s4[:, :R], 0.0), jnp.where(incl, s4[:, R:], 0.0)],
                           axis=1).astype(BF16) for s4 in sc_r]
    x_u = [l_ + _dot3(_hi_lo(mk), v) for l_, mk, v in zip(ls_a, m_ak, vst)]
    pw = n_ab
    for step in range(n_steps):
        ps = [_hi_lo(p_) for p_ in pw]
        xs = [both(stack_masked, _hi_lo(x_)) for x_ in x_u]
        if step + 1 < n_steps:
            t = [_dot3(p_, cat(1, x_, both(stack_masked, p_))) for p_, x_ in zip(ps, xs)]
            x_u = [x_ + t_[:, :LANES] for x_, t_ in zip(x_u, t)]
            pw = [t_[:, LANES:] for t_ in t]
        else:
            x_u = [x_ + _dot3(p_, x_s) for x_, p_, x_s in zip(x_u, ps, xs)]
    xu_s = [_hi_lo(x_) for x_ in x_u]
    y = [l_ + _mm(mr, jnp.concatenate([stack_masked(x_[0]), v[0]], axis=0))
         for l_, mr, x_, v in zip(ls_r, m_r, xu_s, vst)]
    upd = [_dot3(cat(0, x_, v), _hi_lo(jnp.concatenate([ka_ * dte[:, sl], kd_all[:, sl]], axis=0)), _TN)
           for x_, v, ka_, sl in zip(xu_s, xv_s, kka, sls)]
    s_new = [s_raw[pi] * gam_end[:, sls[pi]] + jnp.where(same_head, upd[pi], 0.0) for pi in pairs]

    yc = [y_ - segsum(y_) * inv_hd for y_ in y]
    y_var = [segsum(c_ * c_) * inv_hd for c_ in yc]
    for pi in pairs:
        sl = sls[pi]
        yn = yc[pi] * lax.rsqrt(y_var[pi] + GN_EPS) * lnw[:, sl] + lnb[:, sl]
        y_ref[rs, sl] = ((yn + bonus[pi]) * g[rs, sl]).astype(BF16)
    return s_new


def _wkv(p, l, mu, shift0, w2, a2, g2, w0, a0, kkp, ka, rk, lnw, lnb, state0, n_seq, seq_len, C, n_sub):
    m, n_pad = p.shape
    d_rwkv = w0.shape[-1]
    n_heads = d_rwkv // RWKV_HEAD
    n_pairs = WKV_PAIRS_PER_STEP
    wd = n_pairs * LANES
    n_groups = d_rwkv // wd
    cb = C * n_sub
    n_chunks = seq_len // cb
    nb = n_groups
    blk_w = 3 * d_rwkv // LANES
    blk_a = blk_w + 1
    blk_g = (blk_a + 1) * LANES // GATE_LORA

    row = lambda s, h, c: s * n_chunks + c
    seg_specs = [
        pl.BlockSpec((cb, wd), lambda s, h, c: (row(s, h, c), h)),
        pl.BlockSpec((cb, wd), lambda s, h, c: (row(s, h, c), nb + h)),
        pl.BlockSpec((cb, wd), lambda s, h, c: (row(s, h, c), 2 * nb + h)),
        pl.BlockSpec((cb, LANES), lambda s, h, c: (row(s, h, c), blk_w)),
        pl.BlockSpec((cb, LANES), lambda s, h, c: (row(s, h, c), blk_a)),
        pl.BlockSpec((cb, GATE_LORA), lambda s, h, c: (row(s, h, c), blk_g)),
    ]
    mu_specs = [
        pl.BlockSpec((None, 1, wd), lambda s, h, c: (l, 0, h)),
        pl.BlockSpec((None, 1, wd), lambda s, h, c: (l, 0, nb + h)),
        pl.BlockSpec((None, 1, wd), lambda s, h, c: (l, 0, 2 * nb + h)),
        pl.BlockSpec((None, 1, LANES), lambda s, h, c: (l, 0, blk_w)),
        pl.BlockSpec((None, 1, LANES), lambda s, h, c: (l, 0, blk_a)),
        pl.BlockSpec((None, 1, GATE_LORA), lambda s, h, c: (l, 0, blk_g)),
    ]
    s0_specs = [
        pl.BlockSpec((None, 1, wd), lambda s, h, c: (s, 0, h)),
        pl.BlockSpec((None, 1, wd), lambda s, h, c: (s, 0, nb + h)),
        pl.BlockSpec((None, 1, wd), lambda s, h, c: (s, 0, 2 * nb + h)),
        pl.BlockSpec((None, 1, LANES), lambda s, h, c: (s, 0, blk_w)),
        pl.BlockSpec((None, 1, LANES), lambda s, h, c: (s, 0, blk_a)),
        pl.BlockSpec((None, 1, GATE_LORA), lambda s, h, c: (s, 0, blk_g)),
    ]
    lora_specs = [
        pl.BlockSpec((None, LORA_PAD, wd), lambda s, h, c: (l, 0, h)),
        pl.BlockSpec((None, LORA_PAD, wd), lambda s, h, c: (l, 0, h)),
        pl.BlockSpec((None, GATE_LORA, wd), lambda s, h, c: (l, 0, h)),
    ]
    vec_spec = pl.BlockSpec((None, 1, wd), lambda s, h, c: (l, 0, h))
    state_spec = pl.BlockSpec((None, 2 * n_pairs, RWKV_HEAD, RWKV_HEAD), lambda s, h, c: (s, h, 0, 0))
    ls = min(l, state0.shape[0] - 1)
    state0_spec = pl.BlockSpec((None, None, 2 * n_pairs, RWKV_HEAD, RWKV_HEAD), lambda s, h, c: (ls, s, h, 0, 0))
    y, st = pl.pallas_call(
        functools.partial(_wkv_kernel, C=C, n_sub=n_sub, n_chunks=n_chunks, n_pairs=n_pairs),
        grid=(n_seq, n_groups, n_chunks),
        in_specs=seg_specs + mu_specs + s0_specs + lora_specs + [vec_spec] * 7 + [state0_spec],
        out_specs=[
            pl.BlockSpec((cb, wd), lambda s, h, c: (row(s, h, c), h)),
            state_spec,
        ],
        out_shape=[
            jax.ShapeDtypeStruct((m, d_rwkv), BF16),
            jax.ShapeDtypeStruct((n_seq, n_heads, RWKV_HEAD, RWKV_HEAD), F32),
        ],
        scratch_shapes=[pltpu.VMEM((n_pairs, LANES, LANES), F32)]
        + [pltpu.VMEM((SUBLANES, wd), F32)] * 3 + [pltpu.VMEM((SUBLANES, LANES), F32)] * 2
        + [pltpu.VMEM((SUBLANES, GATE_LORA), F32)],
        compiler_params=_cparams(("arbitrary", "arbitrary", "arbitrary")),
        name="wkv",
    )(p, p, p, p, p, p, mu, mu, mu, mu, mu, mu,
      shift0, shift0, shift0, shift0, shift0, shift0,
      w2, a2, g2, w0, a0, kkp, ka, rk, lnw, lnb, state0)
    return y, st


def _lambda(lqk_ref, lam_init):
    t = lqk_ref[...]
    s1 = jnp.sum(t[0:1, :] * t[1:2, :], axis=-1, keepdims=True)
    s2 = jnp.sum(t[2:3, :] * t[3:4, :], axis=-1, keepdims=True)
    return jnp.exp(s1) - jnp.exp(s2) + lam_init


def _split_maps(q):
    lane = lax.broadcasted_iota(jnp.int32, q.shape, 1)
    zero = jnp.zeros_like(q)
    return jnp.where(lane < DIFF_HEAD, q, zero), jnp.where(lane < DIFF_HEAD, zero, q)


def _subln(o, subln, lam_init):
    return _rms(o, subln, SUBLN_EPS) * (1.0 - lam_init)


FAR_STEP_TILES = (4, 1)
ATTN_TILES_PER_STEP = 4


def _attn_kernel(q_ref, k_ref, v_ref, bias_ref, lqk_ref, subln_ref, o_ref,
                 kb, vt, m_s, a_s, *, TB, n_cast, lam_init):
    g = pl.program_id(1)

    @pl.when(g == 0)
    def _():
        vt[LANES:, :] = jnp.ones((vt.shape[0] - LANES, vt.shape[1]), BF16)

        def cast(c, carry):
            off = pl.multiple_of(c * TB, TB)
            kb[pl.ds(off, TB), :] = k_ref[pl.ds(off, TB), :].astype(BF16)
            vt[0:LANES, pl.ds(off, TB)] = v_ref[pl.ds(off, TB), :].T.astype(BF16)
            return carry
        lax.fori_loop(0, n_cast, cast, 0)

    nq = ATTN_TILES_PER_STEP
    chains = range(nq)
    qs = [jnp.concatenate(_split_maps(q_ref[c * TB:(c + 1) * TB, :]), axis=0) for c in chains]
    m_s[...] = jnp.full(m_s.shape, MASK_VALUE, F32)
    a_s[...] = jnp.zeros(a_s.shape, F32)

    def scores(c, off, width):
        return _mm(kb[pl.ds(off, width), :], qs[c], _NT)

    def update(c, s, off, width):
        m_prev = m_s[c]
        m_new = jnp.maximum(m_prev, jnp.max(s, axis=0, keepdims=True))
        alpha = jnp.exp(m_prev - m_new)
        p = jnp.exp(s - m_new).astype(BF16)
        a_s[c] = alpha * a_s[c] + _mm(vt[:, pl.ds(off, width)], p)
        m_s[c] = m_new

    def far_all(off, width):
        s = [scores(c, off, width) for c in chains]
        for c in chains:
            update(c, s[c], off, width)

    n_common = jnp.maximum(nq * g - 1, 0)
    done = 0
    for step_tiles in FAR_STEP_TILES:
        width = TB * step_tiles
        n_steps = (n_common - done) // step_tiles

        def far_step(j, carry, done=done, width=width):
            far_all(pl.multiple_of(done * TB + j * width, TB), width)
            return carry
        lax.fori_loop(0, n_steps, far_step, 0)
        done = done + n_steps * step_tiles

    def own_far(first_chain, off0):
        for k in range(nq - first_chain):
            off = pl.multiple_of(off0 + k * TB, TB)
            cs = range(first_chain + k, nq)
            s = [scores(c, off, TB) for c in cs]
            for c, s_ in zip(cs, s):
                update(c, s_, off, TB)

    @pl.when(g > 0)
    def _():
        own_far(1, n_common * TB)

    @pl.when(g == 0)
    def _():
        own_far(2, 0)

    offs = [pl.multiple_of(jnp.maximum(nq * g + c - 1, 0) * TB, TB) for c in chains]
    s_near = [scores(c, offs[c], 2 * TB) + (bias_ref[jnp.where(g == 0, 1, 0)] if c == 0 else bias_ref[0])
              for c in chains]
    for c in chains:
        update(c, s_near[c], offs[c], 2 * TB)

    lam = _lambda(lqk_ref, lam_init)
    for c in chains:
        acc = a_s[c]
        on = acc[0:LANES, :] / acc[LANES:LANES + 1, :]
        o = (on[:, :TB] - lam * on[:, TB:]).T
        o_ref[c * TB:(c + 1) * TB, :] = _subln(o, subln_ref[...], lam_init).astype(BF16)


def _attn_prompt(q, k_all, v_all, l, bias, lqk, subln, lam_init, TB):
    m, d_diff = q.shape
    n_heads = d_diff // LANES
    n_blk = m // TB
    nq = ATTN_TILES_PER_STEP
    assert n_blk % nq == 0
    return pl.pallas_call(
        functools.partial(_attn_kernel, TB=TB, n_cast=n_blk, lam_init=lam_init),
        grid=(n_heads, n_blk // nq),
        in_specs=[
            pl.BlockSpec((nq * TB, LANES), lambda h, g: (g, h)),
            pl.BlockSpec((None, m, LANES), lambda h, g: (l, 0, h)),
            pl.BlockSpec((None, m, LANES), lambda h, g: (l, 0, h)),
            pl.BlockSpec((None, 2, 2 * TB, 2 * TB), lambda h, g: (h, 0, 0, 0)),
            pl.BlockSpec((None, 4, DIFF_HEAD), lambda h, g: (l, 0, 0)),
            pl.BlockSpec((None, 1, LANES), lambda h, g: (l, 0, 0)),
        ],
        out_specs=pl.BlockSpec((nq * TB, LANES), lambda h, g: (g, h)),
        out_shape=jax.ShapeDtypeStruct((m, d_diff), BF16),
        scratch_shapes=[pltpu.VMEM((m, LANES), BF16), pltpu.VMEM((LANES + BF16_ROWS, m), BF16),
                        pltpu.VMEM((nq, 1, 2 * TB), F32), pltpu.VMEM((nq, LANES + BF16_ROWS, 2 * TB), F32)],
        compiler_params=_cparams(("arbitrary", "arbitrary")),
        name="attn_prompt",
    )(q, k_all, v_all, bias, lqk, subln)


SAMPLE_FAR_FRAMES = 384


def _attn_sample_kernel(q_ref, kn_ref, vn_ref, kc_ref, vc_ref, bias_ref, lqk_ref, subln_ref, o_ref,
                        kb, acc, *, T, n_heads, near, lam_init):
    past = kc_ref.shape[0]
    n_cols = 2 * T * n_heads
    fc = SAMPLE_FAR_FRAMES
    n_far = (past - near) // fc
    rows = fc * n_heads
    qs = jnp.concatenate(
        [m_ for h in range(n_heads) for m_ in _split_maps(q_ref[:, h * LANES:(h + 1) * LANES])], axis=0)
    sub = lax.broadcasted_iota(jnp.int32, (n_heads, n_cols), 0)
    col = lax.broadcasted_iota(jnp.int32, (n_heads, n_cols), 1)
    valid = (col >= sub * (2 * T)) & (col < (sub + 1) * (2 * T))

    def flat(x3):
        return x3.reshape(x3.shape[0] * n_heads, LANES)

    def scores(k_rows):
        s = _mm(k_rows, qs, _NT)
        return s.reshape(s.shape[0] // n_heads, n_heads, n_cols)

    near_k = jnp.concatenate([flat(kc_ref[past - near:past]), flat(kn_ref[...])], axis=0).astype(BF16)
    near_v = jnp.concatenate([flat(vc_ref[past - near:past]), flat(vn_ref[...])], axis=0).astype(BF16)
    s_near = scores(near_k) + bias_ref[...].reshape(near + T, n_heads, n_cols)

    def pass1(c, m3):
        f0 = pl.multiple_of(c * fc, fc)
        k_rows = flat(kc_ref[pl.ds(f0, fc)]).astype(BF16)
        kb[pl.ds(pl.multiple_of(c * rows, rows), rows), :] = k_rows
        return jnp.maximum(m3, jnp.max(scores(k_rows), axis=0))
    m3 = lax.fori_loop(0, n_far, pass1, jnp.max(s_near, axis=0))

    ones = jnp.ones((rows, LANES), BF16)

    def weights(s3):
        p = jnp.where(valid, jnp.exp(s3 - m3), 0.0)
        return p.reshape(p.shape[0] * n_heads, n_cols).astype(BF16)

    acc[...] = _mm(weights(s_near), jnp.concatenate([near_v, ones[:near_v.shape[0]]], axis=1), _TN)

    def pass2(c, carry):
        f0 = pl.multiple_of(c * fc, fc)
        p = weights(scores(kb[pl.ds(pl.multiple_of(c * rows, rows), rows), :]))
        v_aug = jnp.concatenate([flat(vc_ref[pl.ds(f0, fc)]).astype(BF16), ones], axis=1)
        acc[...] += _mm(p, v_aug, _TN)
        return carry
    lax.fori_loop(0, n_far, pass2, 0)

    lam = _lambda(lqk_ref, lam_init)
    for h in range(n_heads):
        a = acc[h * 2 * T:(h + 1) * 2 * T, :]
        on = a[:, 0:LANES] / a[:, LANES:2 * LANES]
        o = on[:T, :] - lam * on[T:, :]
        o_ref[:, h * LANES:(h + 1) * LANES] = _subln(o, subln_ref[...], lam_init).astype(BF16)


def _attn_sample(q, kn_all, vn_all, l, k_cache, v_cache, bias_t, lqk, subln, lam_init, n_seq, T, near):
    m, d_diff = q.shape
    n_heads = d_diff // LANES
    past = k_cache.shape[2]
    assert (past - near) % SAMPLE_FAR_FRAMES == 0
    cache_spec = pl.BlockSpec((None, None, past, n_heads, LANES), lambda b: (l, b, 0, 0, 0))
    new_spec = pl.BlockSpec((None, T, n_heads, LANES), lambda b: (l, b, 0, 0))
    return pl.pallas_call(
        functools.partial(_attn_sample_kernel, T=T, n_heads=n_heads, near=near, lam_init=lam_init),
        grid=(n_seq,),
        in_specs=[
            pl.BlockSpec((T, d_diff), lambda b: (b, 0)),
            new_spec,
            new_spec,
            cache_spec,
            cache_spec,
            pl.BlockSpec(bias_t.shape, lambda b: (0, 0)),
            pl.BlockSpec((None, 4, DIFF_HEAD), lambda b: (l, 0, 0)),
            pl.BlockSpec((None, 1, LANES), lambda b: (l, 0, 0)),
        ],
        out_specs=pl.BlockSpec((T, d_diff), lambda b: (b, 0)),
        out_shape=jax.ShapeDtypeStruct((m, d_diff), BF16),
        scratch_shapes=[pltpu.VMEM(((past - near) * n_heads, LANES), BF16),
                        pltpu.VMEM((2 * T * n_heads, 2 * LANES), F32)],
        compiler_params=_cparams(("arbitrary",)),
        name="attn_sample",
    )(q, kn_all, vn_all, k_cache, v_cache, bias_t, lqk, subln)


def _outproj_kernel(x_ref, yr_ref, yd_ref, gt_ref, wt_ref, wb_ref, o_ref):
    mixed = _mm(yr_ref[...], wt_ref[...]) + _mm(yd_ref[...], wb_ref[...])
    o_ref[...] = x_ref[...] + gt_ref[...] * mixed


def _outproj(x, yr, yd, gt, w_out_all, l, tm):
    m, d = x.shape
    dr = yr.shape[1]
    dd = yd.shape[1]
    gt_spec = (pl.BlockSpec((1, d), lambda i: (0, 0)) if gt.shape[0] == 1
               else pl.BlockSpec((tm, d), lambda i: (i, 0)))
    return pl.pallas_call(
        _outproj_kernel,
        grid=(m // tm,),
        in_specs=[
            pl.BlockSpec((tm, d), lambda i: (i, 0)),
            pl.BlockSpec((tm, dr), lambda i: (i, 0)),
            pl.BlockSpec((tm, dd), lambda i: (i, 0)),
            gt_spec,
            pl.BlockSpec((None, dr, d), lambda i: (l, 0, 0)),
            pl.BlockSpec((None, dd, d), lambda i: (l, dr // dd, 0)),
        ],
        out_specs=pl.BlockSpec((tm, d), lambda i: (i, 0)),
        out_shape=jax.ShapeDtypeStruct((m, d), F32),
        compiler_params=_cparams(("arbitrary",)),
        name="outproj",
    )(x, yr, yd, gt, w_out_all, w_out_all)


def _shift_rows(x, prev, n):
    if n % SUBLANES == 0:
        return jnp.concatenate([prev, x[:x.shape[0] - n, :]], axis=0)
    rolled = pltpu.roll(x, n, 0)
    row = lax.broadcasted_iota(jnp.int32, x.shape, 0)
    out = rolled
    for r in range(n):
        out = jnp.where(row == r, prev[r:r + 1, :], out)
    return out


def _ffn_kernel(x_ref, sh_ref, sc_ref, gt_ref, g_ref, wg_ref, wv_ref, cw_ref, cb_ref, c0_ref, wd_ref, gf_ref,
                o_ref, co_ref, h_scr, acc_scr, carry, *, B, n_j, final_norm):
    i = pl.program_id(0)
    j = pl.program_id(1)
    tm = x_ref.shape[0]

    @pl.when(j == 0)
    def _():
        xn = _rms(x_ref[...], g_ref[...], NORM_EPS)
        h_scr[...] = (xn * (1.0 + sc_ref[...]) + sh_ref[...]).astype(BF16)
        acc_scr[...] = jnp.zeros(acc_scr.shape, F32)

    @pl.when(i == 0)
    def _():
        carry[j, 0:2 * B, :] = c0_ref[...]

    h = h_scr[...]
    ug = _mm(h, wg_ref[...])
    uv = _mm(h, wv_ref[...])
    prev = carry[j, 0:2 * B, :]
    s1 = _shift_rows(ug, prev[B:2 * B, :], B)
    s2 = _shift_rows(ug, prev, 2 * B)
    cw = cw_ref[...]
    z = cb_ref[...] + s2 * cw[0:1, :] + s1 * cw[1:2, :] + ug * cw[2:3, :]
    act = 0.5 * z * (1.0 + lax.erf(z * (2.0 ** -0.5)))
    acc_scr[...] += _mm((act * uv).astype(BF16), wd_ref[...])
    last2 = ug[tm - 2 * B:tm, :]
    carry[j, 0:2 * B, :] = last2
    tf = last2.shape[1]
    co_ref[:, pl.ds(pl.multiple_of(j * tf, tf), tf)] = last2

    @pl.when(j == n_j - 1)
    def _():
        x2 = x_ref[...] + gt_ref[...] * acc_scr[...]
        if final_norm:
            x2 = _rms(x2, gf_ref[...], NORM_EPS)
        o_ref[...] = x2


def _ffn(x, sh, sc, gt, g_all, w_up_all, conv_w_all, conv_b_all, conv0, w_down_all, g_final, l, B, tm,
         final_norm, tf=512):
    m, d = x.shape
    d_ff = w_down_all.shape[1]
    n_j = d_ff // tf
    rows_c = max(SUBLANES, 2 * B)
    return pl.pallas_call(
        functools.partial(_ffn_kernel, B=B, n_j=n_j, final_norm=final_norm),
        grid=(m // tm, n_j),
        in_specs=[
            pl.BlockSpec((tm, d), lambda i, j: (i, 0)),
            _mod_spec(sh, tm, d),
            _mod_spec(sc, tm, d),
            _mod_spec(gt, tm, d),
            pl.BlockSpec((None, 1, d), lambda i, j: (l, 0, 0)),
            pl.BlockSpec((None, d, tf), lambda i, j: (l, 0, j)),
            pl.BlockSpec((None, d, tf), lambda i, j: (l, 0, n_j + j)),
            pl.BlockSpec((None, CONV_W, tf), lambda i, j: (l, 0, j)),
            pl.BlockSpec((None, 1, tf), lambda i, j: (l, 0, j)),
            pl.BlockSpec((2 * B, tf), lambda i, j: (0, j)),
            pl.BlockSpec((None, tf, d), lambda i, j: (l, j, 0)),
            pl.BlockSpec((1, d), lambda i, j: (0, 0)),
        ],
        out_specs=[
            pl.BlockSpec((tm, d), lambda i, j: (i, 0)),
            pl.BlockSpec((2 * B, d_ff), lambda i, j: (0, 0)),
        ],
        out_shape=[
            jax.ShapeDtypeStruct((m, d), F32),
            jax.ShapeDtypeStruct((2 * B, d_ff), F32),
        ],
        scratch_shapes=[pltpu.VMEM((tm, d), BF16), pltpu.VMEM((tm, d), F32),
                        pltpu.VMEM((n_j, rows_c, tf), F32)],
        compiler_params=_cparams(("arbitrary", "arbitrary")),
        name="conv_ffn",
    )(x, sh, sc, gt, g_all, w_up_all, w_up_all, conv_w_all, conv_b_all, conv0, w_down_all, g_final)


def _pad_cols(a, n):
    return jnp.pad(a, [(0, 0)] * (a.ndim - 1) + [(0, n)])


def _regroup_rwkv_cols(a, d_rwkv):
    o = 3 * d_rwkv
    return jnp.concatenate([
        a[..., :o],
        _pad_cols(a[..., o:o + DECAY_LORA], LORA_PAD - DECAY_LORA),
        _pad_cols(a[..., o + DECAY_LORA:o + DECAY_LORA + AAA_LORA], LORA_PAD - AAA_LORA),
        a[..., o + DECAY_LORA + AAA_LORA:],
    ], axis=-1)


def _ungroup_rwkv_cols(a, d_rwkv):
    o = 3 * d_rwkv
    return jnp.concatenate([
        a[..., :o],
        a[..., o:o + DECAY_LORA],
        a[..., o + LORA_PAD:o + LORA_PAD + AAA_LORA],
        a[..., o + 2 * LORA_PAD:],
    ], axis=-1)


def _prompt_tiles(seq):
    attn_tile = min(256, seq // 2)
    assert attn_tile % CHUNK == 0 and attn_tile >= MAX_DISTANCE
    assert seq % (ATTN_TILES_PER_STEP * attn_tile) == 0
    return attn_tile, min(512, seq), min(1024, seq), min(64, seq)


def kernel(x_prompt, x_sample, c_prompt, c_sample, cache_k, cache_v, state_wkv, state_shift, state_conv, w_ada, b_ada, g_mix, g_ffn, w_in, w_out, rwkv_mu, rwkv_w0, rwkv_w2, rwkv_a0, rwkv_a2, rwkv_g2, rwkv_kk, rwkv_ka, rwkv_rk, rwkv_ln_w, rwkv_ln_b, diff_lq1, diff_lk1, diff_lq2, diff_lk2, diff_subln, rel_table, ffn_up, ffn_conv_w, ffn_conv_b, ffn_down, g_final):
    depth, d_model, _ = w_in.shape
    bp, seq, _ = x_prompt.shape
    bs, dseq, _ = x_sample.shape
    past = cache_k.shape[2]
    d_rwkv = rwkv_w0.shape[1]
    d_diff = d_model - d_rwkv
    n_dheads = d_diff // (2 * DIFF_HEAD)
    n_rheads = d_rwkv // RWKV_HEAD
    n_rwkv_cols = rwkv_mu.shape[1]
    n_rwkv_pad = n_rwkv_cols + 2 * LORA_PAD - DECAY_LORA - AAA_LORA
    d_ff = ffn_down.shape[1]
    assert bp == 1, "prompt path handles one sequence"
    TB, tm_p, tm_in, wkv_chunk = _prompt_tiles(seq)
    m_s = bs * dseq

    w_in_b = jnp.concatenate([_regroup_rwkv_cols(w_in[..., :n_rwkv_cols], d_rwkv), w_in[..., n_rwkv_cols:]],
                             axis=-1).astype(BF16)
    mu_p = _regroup_rwkv_cols(rwkv_mu, d_rwkv)[:, None]
    w2_b = jnp.pad(rwkv_w2, ((0, 0), (0, LORA_PAD - DECAY_LORA), (0, 0))).astype(BF16)
    a2_b = jnp.pad(rwkv_a2, ((0, 0), (0, LORA_PAD - AAA_LORA), (0, 0))).astype(BF16)
    g2_b = rwkv_g2.astype(BF16)
    w_out_b = _cast_bf16(w_out)
    up_b = _cast_bf16(ffn_up)
    down_b = _cast_bf16(ffn_down)
    vec = lambda a: a.reshape(depth, 1, -1)
    wkv_args = (w2_b, a2_b, g2_b, vec(rwkv_w0), vec(rwkv_a0), vec(rwkv_kk), vec(rwkv_ka), vec(rwkv_rk),
                vec(rwkv_ln_w), vec(rwkv_ln_b))
    g_mix_v, g_ffn_v, subln_v, conv_b_v = vec(g_mix), vec(g_ffn), vec(diff_subln), vec(ffn_conv_b)
    gf = g_final[None]
    lqk = jnp.stack([diff_lq1, diff_lk1, diff_lq2, diff_lk2], axis=1)

    n_c = bp + bs
    n_c_pad = -(-n_c // SUBLANES) * SUBLANES
    c_rows = jnp.pad(jnp.concatenate([c_prompt, c_sample], axis=0), ((0, n_c_pad - n_c), (0, 0)))
    mod = _adaln_mod(c_rows, w_ada, b_ada).reshape(depth, n_c_pad, 6, d_model)

    ql = jnp.arange(TB)
    rel_diag = ql[None, :] - ql[:, None]
    allowed = (ql[None, :] // CHUNK) <= (ql[:, None] // CHUNK)
    bucket_diag = jnp.where(allowed, _t5_bucket(rel_diag), -1)
    bucket_prev = _t5_bucket(rel_diag - TB)
    windows = [jnp.concatenate([bucket_prev, bucket_diag], axis=1),
               jnp.concatenate([bucket_diag, jnp.full((TB, TB), -1)], axis=1)]
    bucket_p = jnp.concatenate([w.T for w in windows], axis=0)
    bias_p = _bias_lookup(rel_table, bucket_p.astype(jnp.int32), FAR_BUCKET, col_repeats=2)
    bias_p = bias_p.reshape(n_dheads, 2, 2 * TB, 2 * TB)
    q_pos = past + jnp.arange(dseq)
    assert past >= MAX_DISTANCE
    k_pos = past - MAX_DISTANCE + jnp.arange(MAX_DISTANCE + dseq)
    bias_s = _bias_lookup(rel_table, _t5_bucket(k_pos[None, :] - q_pos[:, None]).astype(jnp.int32), FAR_BUCKET)
    bias_t = jnp.broadcast_to(bias_s.transpose(2, 0, 1)[:, None, :, None, :],
                              (MAX_DISTANCE + dseq, n_dheads, n_dheads, 2, dseq))
    bias_t = bias_t.reshape((MAX_DISTANCE + dseq) * n_dheads, n_dheads * 2 * dseq)

    xp = x_prompt.reshape(seq, d_model)
    xs = x_sample.reshape(m_s, d_model)
    zero_shift = jnp.zeros((1, 1, n_rwkv_pad), F32)
    zero_state = jnp.zeros((1, 1, n_rheads, RWKV_HEAD, RWKV_HEAD), F32)
    zero_conv = jnp.zeros((2, d_ff), F32)
    shift_s_in = _regroup_rwkv_cols(state_shift, d_rwkv)
    outs_p = {k: [] for k in ("wkv", "shift", "conv")}
    outs_s = {k: [] for k in ("wkv", "shift", "conv")}
    kv_p = (jnp.zeros((depth, seq, d_diff), F32), jnp.zeros((depth, seq, d_diff), F32))
    kv_s = (jnp.zeros((depth, m_s, d_diff), F32), jnp.zeros((depth, m_s, d_diff), F32))

    def time_major(a):
        return a.reshape(bs, dseq, -1).swapaxes(0, 1).reshape(m_s, -1)

    def batch_major(a):
        return a.reshape(dseq, bs, -1).swapaxes(0, 1).reshape(m_s, -1)

    for l in range(depth):
        lam_init = 0.8 - 0.6 * math.exp(-0.3 * l)
        last = l == depth - 1

        mp = mod[l, 0:bp]
        sh1, sc1, gt1, sh2, sc2, gt2 = (mp[:, t] for t in range(6))
        p, q, *kv_p = _inproj(xp, sh1, sc1, g_mix_v, w_in_b, l, kv_p, n_rwkv_pad, d_diff, tm_in)
        y_r, wkv = _wkv(p, l, mu_p, zero_shift, *wkv_args, zero_state, 1, seq, wkv_chunk,
                        WKV_CHUNKS_PER_STEP if seq % (WKV_CHUNKS_PER_STEP * wkv_chunk) == 0 else 1)
        y_d = _attn_prompt(q, kv_p[0], kv_p[1], l, bias_p, lqk, subln_v, lam_init, TB)
        x1 = _outproj(xp, y_r, y_d, gt1, w_out_b, l, tm_p)
        xp, conv = _ffn(x1, sh2, sc2, gt2, g_ffn_v, up_b, ffn_conv_w, conv_b_v, zero_conv, down_b, gf, l,
                        1, tm_p, last)
        outs_p["wkv"].append(wkv)
        outs_p["shift"].append(_ungroup_rwkv_cols(p[seq - 1:seq], d_rwkv).reshape(bp, 1, n_rwkv_cols))
        outs_p["conv"].append(conv.reshape(bp, CONV_W - 1, d_ff))

        ms = mod[l, bp:bp + bs]
        rows_bm = jnp.repeat(ms, dseq, axis=0)
        rows_tm = jnp.tile(ms, (dseq, 1, 1))
        p, q, *kv_s = _inproj(xs, rows_bm[:, 0], rows_bm[:, 1], g_mix_v, w_in_b, l, kv_s, n_rwkv_pad, d_diff, m_s)
        y_r, wkv = _wkv(p, l, mu_p, shift_s_in[l], *wkv_args, state_wkv, bs, dseq, dseq, 1)
        kn4, vn4 = (a.reshape(depth, m_s, n_dheads, 2 * DIFF_HEAD) for a in kv_s)
        y_d = _attn_sample(q, kn4, vn4, l, cache_k, cache_v, bias_t, lqk, subln_v, lam_init, bs, dseq,
                           MAX_DISTANCE)
        x1 = _outproj(xs, y_r, y_d, rows_bm[:, 2], w_out_b, l, m_s)
        conv0 = state_conv[l].swapaxes(0, 1).reshape((CONV_W - 1) * bs, d_ff)
        x2, conv = _ffn(time_major(x1), rows_tm[:, 3], rows_tm[:, 4], rows_tm[:, 5], g_ffn_v, up_b, ffn_conv_w,
                        conv_b_v, conv0, down_b, gf, l, bs, m_s, last)
        xs = batch_major(x2)
        outs_s["wkv"].append(wkv)
        p_last = p.reshape(bs, dseq, n_rwkv_pad)[:, dseq - 1:dseq]
        outs_s["shift"].append(_ungroup_rwkv_cols(p_last, d_rwkv))
        outs_s["conv"].append(conv.reshape(CONV_W - 1, bs, d_ff).swapaxes(0, 1))

    st = lambda xs_: jnp.stack(xs_)
    head_shape = (n_dheads, 2 * DIFF_HEAD)
    return (xp.reshape(bp, seq, d_model), xs.reshape(bs, dseq, d_model),
            kv_p[0].reshape(depth, bp, seq, *head_shape), kv_p[1].reshape(depth, bp, seq, *head_shape),
            st(outs_p["wkv"]), st(outs_p["shift"]), st(outs_p["conv"]),
            kv_s[0].reshape(depth, bs, dseq, *head_shape), kv_s[1].reshape(depth, bs, dseq, *head_shape),
            st(outs_s["wkv"]), st(outs_s["shift"]), st(outs_s["conv"]))
```

```python
import functools
import math

import jax
import jax.numpy as jnp
from jax import lax
from jax.experimental import pallas as pl
from jax.experimental.pallas import tpu as pltpu

F32 = jnp.float32
BF16 = jnp.bfloat16

CHUNK = 64
RWKV_HEAD = 64
DIFF_HEAD = 64
N_BUCKETS = 32
MAX_DISTANCE = 128
NORM_EPS = 1e-6
GN_EPS = 64e-5
SUBLN_EPS = 1e-5
DECAY_LORA = 96
AAA_LORA = 96
GATE_LORA = 256
LORA_PAD = 128
CONV_W = 3

LANES = 128
SUBLANES = 8
BF16_ROWS = 16
VMEM_LIMIT_BYTES = 56 * 1024 * 1024
MASK_VALUE = -1e30
WKV_CHUNKS_PER_STEP = 4
WKV_PAIRS_PER_STEP = 8

_NN = (((1,), (0,)), ((), ()))
_NT = (((1,), (1,)), ((), ()))
_TN = (((0,), (0,)), ((), ()))


def _mm(a, b, dims=_NN):
    return lax.dot_general(a, b, dims, preferred_element_type=F32)


def _hi_lo(x):
    h = x.astype(BF16)
    return h, (x - h.astype(F32)).astype(BF16)


def _dot3(a, b, dims=_NN):
    (ah, al), (bh, bl) = a, b
    (ca,), (cb,) = dims[0]
    k = ah.shape[ca]
    if (ca == 0 or k % LANES == 0) and (cb == 0 or k % LANES == 0):
        return _mm(jnp.concatenate([ah, al, ah], axis=ca), jnp.concatenate([bh, bh, bl], axis=cb), dims)
    return _mm(ah, bh, dims) + (_mm(al, bh, dims) + _mm(ah, bl, dims))


def _mm3(a, b, dims=_NN):
    return _dot3(_hi_lo(a), _hi_lo(b), dims)


def _sel_rows(sel2, x):
    xh, xl = _hi_lo(x)
    return _mm(sel2, jnp.concatenate([xh, xl], axis=0))


def _sel_lanes(x, sel2):
    xh, xl = _hi_lo(x)
    return _mm(jnp.concatenate([xh, xl], axis=1), sel2)


def _cparams(sem):
    return pltpu.CompilerParams(dimension_semantics=sem, vmem_limit_bytes=VMEM_LIMIT_BYTES)


def _rms(x, g, eps):
    return x * lax.rsqrt(jnp.mean(x * x, axis=-1, keepdims=True) + eps) * g


def _cast_kernel(x_ref, o_ref):
    o_ref[...] = x_ref[...].astype(BF16)


CAST_BLOCK_BYTES = 8 * 1024 * 1024


def _cast_bf16(w):
    depth, rows, cols = w.shape
    tc = 1024 if cols % 1024 == 0 else cols
    tr = max(t for t in range(BF16_ROWS, rows + 1, BF16_ROWS)
             if rows % t == 0 and t * tc * 4 <= CAST_BLOCK_BYTES)
    spec = pl.BlockSpec((None, tr, tc), lambda l, i, j: (l, i, j))
    return pl.pallas_call(
        _cast_kernel,
        grid=(depth, rows // tr, cols // tc),
        in_specs=[spec],
        out_specs=spec,
        out_shape=jax.ShapeDtypeStruct(w.shape, BF16),
        compiler_params=_cparams(("arbitrary", "arbitrary", "arbitrary")),
        name="cast_bf16",
    )(w)


def _mod_kernel(c_ref, w_ref, b_ref, o_ref):
    c = c_ref[...]
    s = (c * jax.nn.sigmoid(c)).astype(BF16)
    o_ref[...] = _mm(s, w_ref[...].astype(BF16)) + b_ref[...]


def _adaln_mod(c_rows, w_ada, b_ada, tn=1024):
    depth, d, n = w_ada.shape
    rows = c_rows.shape[0]
    return pl.pallas_call(
        _mod_kernel,
        grid=(depth, n // tn),
        in_specs=[
            pl.BlockSpec((rows, d), lambda l, j: (0, 0)),
            pl.BlockSpec((None, d, tn), lambda l, j: (l, 0, j)),
            pl.BlockSpec((None, 1, tn), lambda l, j: (l, 0, j)),
        ],
        out_specs=pl.BlockSpec((None, rows, tn), lambda l, j: (l, 0, j)),
        out_shape=jax.ShapeDtypeStruct((depth, rows, n), F32),
        compiler_params=_cparams(("arbitrary", "arbitrary")),
        name="adaln_mod",
    )(c_rows, w_ada, b_ada.reshape(depth, 1, n))


def _bias_kernel(tab_ref, bucket_ref, o_ref, *, far_bucket):
    h = pl.program_id(0)
    bucket = bucket_ref[...]
    far = tab_ref[far_bucket, h]
    acc = jnp.full(bucket.shape, MASK_VALUE, F32)
    for b in range(N_BUCKETS):
        acc = jnp.where(bucket == b, tab_ref[b, h] - far, acc)
    c = bucket.shape[1]
    for k in range(o_ref.shape[1] // c):
        o_ref[:, k * c:(k + 1) * c] = acc


def _bias_lookup(table, bucket, far_bucket, col_repeats=1):
    n_heads = table.shape[1]
    r, c = bucket.shape
    return pl.pallas_call(
        functools.partial(_bias_kernel, far_bucket=far_bucket),
        grid=(n_heads,),
        in_specs=[
            pl.BlockSpec(memory_space=pltpu.SMEM),
            pl.BlockSpec((r, c), lambda h: (0, 0)),
        ],
        out_specs=pl.BlockSpec((None, r, c * col_repeats), lambda h: (h, 0, 0)),
        out_shape=jax.ShapeDtypeStruct((n_heads, r, c * col_repeats), F32),
        compiler_params=_cparams(("arbitrary",)),
        name="bias_lookup",
    )(table, bucket)


def _t5_bucket(rel):
    nb = N_BUCKETS // 2
    max_exact = nb // 2
    n = jnp.abs(rel)
    nf = jnp.maximum(n, 1).astype(F32)
    large = max_exact + (jnp.log(nf / max_exact) / math.log(MAX_DISTANCE / max_exact) * (nb - max_exact)).astype(jnp.int32)
    large = jnp.minimum(large, nb - 1)
    return jnp.where(rel > 0, nb, 0) + jnp.where(n < max_exact, n, large)


FAR_BUCKET = N_BUCKETS // 2 - 1


def _inproj_kernel(x_ref, sh_ref, sc_ref, g_ref, w_ref, *rest, n_p, n_q):
    p_ref, q_ref, k_ref, v_ref, h_scr = rest[-5:]
    j = pl.program_id(1)

    @pl.when(j == 0)
    def _():
        xn = _rms(x_ref[...], g_ref[...], NORM_EPS)
        h_scr[...] = (xn * (1.0 + sc_ref[...]) + sh_ref[...]).astype(BF16)

    @pl.when(j < n_p)
    def _():
        p_ref[...] = _mm(h_scr[...], w_ref[...])

    @pl.when((j >= n_p) & (j < n_p + n_q))
    def _():
        q_ref[...] = (_mm(h_scr[...], w_ref[...]) * (DIFF_HEAD ** -0.5)).astype(BF16)

    @pl.when((j >= n_p + n_q) & (j < n_p + 2 * n_q))
    def _():
        k_ref[...] = _mm(h_scr[...], w_ref[...])

    @pl.when(j >= n_p + 2 * n_q)
    def _():
        v_ref[...] = _mm(h_scr[...], w_ref[...])


def _mod_spec(mod, tm, d):
    if mod.shape[0] == 1:
        return pl.BlockSpec((1, d), lambda i, j: (0, 0))
    return pl.BlockSpec((tm, d), lambda i, j: (i, 0))


def _inproj(x, sh, sc, g_all, w_all, l, kv_all, n_rwkv_pad, d_diff, tm, tn=512):
    m, d = x.shape
    depth, _, n = w_all.shape
    n_p, n_q = n_rwkv_pad // tn, d_diff // tn
    clip = lambda j, lo: jnp.clip(j - lo, 0, n_q - 1)
    in_specs = [
        pl.BlockSpec((tm, d), lambda i, j: (i, 0)),
        _mod_spec(sh, tm, d),
        _mod_spec(sc, tm, d),
        pl.BlockSpec((None, 1, d), lambda i, j: (l, 0, 0)),
        pl.BlockSpec((None, d, tn), lambda i, j: (l, 0, j)),
        pl.BlockSpec(memory_space=pl.ANY),
        pl.BlockSpec(memory_space=pl.ANY),
    ]
    args = [x, sh, sc, g_all, w_all, kv_all[0], kv_all[1]]
    aliases = {5: 2, 6: 3}
    return pl.pallas_call(
        functools.partial(_inproj_kernel, n_p=n_p, n_q=n_q),
        grid=(m // tm, n // tn),
        in_specs=in_specs,
        out_specs=[
            pl.BlockSpec((tm, tn), lambda i, j: (i, jnp.minimum(j, n_p - 1))),
            pl.BlockSpec((tm, tn), lambda i, j: (i, clip(j, n_p))),
            pl.BlockSpec((None, tm, tn), lambda i, j: (l, i, clip(j, n_p + n_q))),
            pl.BlockSpec((None, tm, tn), lambda i, j: (l, i, clip(j, n_p + 2 * n_q))),
        ],
        out_shape=[
            jax.ShapeDtypeStruct((m, n_rwkv_pad), F32),
            jax.ShapeDtypeStruct((m, d_diff), BF16),
            jax.ShapeDtypeStruct((depth, m, d_diff), F32),
            jax.ShapeDtypeStruct((depth, m, d_diff), F32),
        ],
        scratch_shapes=[pltpu.VMEM((tm, d), BF16)],
        input_output_aliases=aliases,
        compiler_params=_cparams(("arbitrary", "arbitrary")),
        name="inproj",
    )(*args)


def _wkv_kernel(pr_ref, pk_ref, pv_ref, pw_ref, pa_ref, pg_ref,
                mur_ref, muk_ref, muv_ref, muw_ref, mua_ref, mug_ref,
                s0r_ref, s0k_ref, s0v_ref, s0w_ref, s0a_ref, s0g_ref,
                w2_ref, a2_ref, g2_ref,
                w0_ref, a0_ref, kkp_ref, ka_ref, rk_ref, lnw_ref, lnb_ref,
                st0_ref,
                y_ref, st_ref,
                sbd, cr, ck, cv, cw, ca, cg, *, C, n_sub, n_chunks, n_pairs):
    chunk = pl.program_id(2)
    CB = n_sub * C
    hd = RWKV_HEAD
    zero_blk = jnp.zeros((hd, hd), F32)

    @pl.when(chunk == 0)
    def _():
        for c_ref, s_ref in ((cr, s0r_ref), (ck, s0k_ref), (cv, s0v_ref), (cw, s0w_ref), (ca, s0a_ref), (cg, s0g_ref)):
            c_ref[0:1, :] = s_ref[...]
        for pi in range(n_pairs):
            top = jnp.concatenate([st0_ref[2 * pi], zero_blk], axis=1)
            bot = jnp.concatenate([zero_blk, st0_ref[2 * pi + 1]], axis=1)
            sbd[pi] = jnp.concatenate([top, bot], axis=0)

    def tshift(x_ref, mu_ref, c_ref):
        x = x_ref[...]
        row = lax.broadcasted_iota(jnp.int32, x.shape, 0)
        prev = jnp.where(row == 0, c_ref[0:1, :], pltpu.roll(x, 1, 0))
        c_ref[0:1, :] = x[CB - 1:CB, :]
        return x + (prev - x) * mu_ref[...]

    xr = tshift(pr_ref, mur_ref, cr)
    xk = tshift(pk_ref, muk_ref, ck)
    xv = tshift(pv_ref, muv_ref, cv)
    xw = tshift(pw_ref, muw_ref, cw)
    xa = tshift(pa_ref, mua_ref, ca)
    xg = tshift(pg_ref, mug_ref, cg)

    li = lax.broadcasted_iota(jnp.int32, (LANES, LANES), 0)
    lj = lax.broadcasted_iota(jnp.int32, (LANES, LANES), 1)
    same_head = (li < hd) == (lj < hd)
    li2 = lax.broadcasted_iota(jnp.int32, (2 * LANES, LANES), 0)
    lj2 = lax.broadcasted_iota(jnp.int32, (2 * LANES, LANES), 1)
    li2 = jnp.where(li2 < LANES, li2, li2 - LANES)
    seg2 = jnp.where((li2 < hd) == (lj2 < hd), 1.0, 0.0).astype(BF16)

    def segsum(x):
        return _sel_lanes(x, seg2)

    wlin = w0_ref[...] + _mm(jnp.tanh(xw).astype(BF16), w2_ref[...])
    z = -wlin
    w_log = -(jnp.maximum(z, 0.0) + jnp.log1p(jnp.exp(-jnp.abs(z)))) - 0.5
    ne = -jnp.exp(w_log)
    a = jax.nn.sigmoid(a0_ref[...] + _mm(xa.astype(BF16), a2_ref[...]))
    g = _mm(jax.nn.sigmoid(xg).astype(BF16), g2_ref[...])
    kk_raw = xk * kkp_ref[...]
    k2 = xk * (1.0 + (a - 1.0) * ka_ref[...])
    rk_prod = xr * k2 * rk_ref[...]

    ti = lax.broadcasted_iota(jnp.int32, (CB, 2 * CB), 0)
    tj = lax.broadcasted_iota(jnp.int32, (CB, 2 * CB), 1)
    tj = jnp.where(tj < CB, tj, tj - CB)
    in_chunk = [(ti >= q * C) & (ti < (q + 1) * C) & (tj >= q * C) & (tj <= ti) for q in range(n_sub)]
    tri2 = jnp.where(functools.reduce(lambda x, y: x | y, in_chunk), 1.0, 0.0).astype(BF16)
    cum = _sel_rows(tri2, ne)
    gam = jnp.exp(cum)
    gprev = jnp.exp(cum - ne)
    ginv = jnp.exp(-cum)
    cum_end = jnp.concatenate([jnp.broadcast_to(cum[(q + 1) * C - 1:(q + 1) * C, :], (C, cum.shape[1]))
                               for q in range(n_sub)], axis=0)
    dte = jnp.exp(cum_end - cum)
    rg_all = xr * gam
    ks_all = k2 * ginv
    kd_all = k2 * dte
    lnw = lnw_ref[...]
    lnb = lnb_ref[...]

    def stack_masked(x):
        first = lax.broadcasted_iota(jnp.int32, (1, x.shape[1]), 1) < x.shape[1] // 2
        return jnp.concatenate([jnp.where(first, x, 0.0), jnp.where(first, 0.0, x)], axis=0)

    R = 2 * C
    ri = lax.broadcasted_iota(jnp.int32, (C, R), 0)
    ci = lax.broadcasted_iota(jnp.int32, (C, R), 1)
    ci = jnp.where(ci < C, ci, ci - C)
    strict = ci < ri
    incl = ci <= ri
    n_steps = max(1, int(math.log2(C)))

    pairs = range(n_pairs)
    sls = [slice(pi * LANES, (pi + 1) * LANES) for pi in pairs]
    sums_all = [segsum(jnp.concatenate([kk_raw[:, sl] * kk_raw[:, sl], rk_prod[:, sl]], axis=0)) for sl in sls]
    kk_all = [kk_raw[:, sl] * lax.rsqrt(jnp.maximum(sm[:CB, :], 1e-24)) for sl, sm in zip(sls, sums_all)]
    s_raw = [sbd[pi] for pi in pairs]
    for q in range(n_sub):
        s_raw = _wkv_chunk(
            q, s_raw, y_ref, sls, stack_masked, segsum, strict, incl, same_head,
            kk_all, [sm[CB:, :] for sm in sums_all], a, xv, gprev, ginv, dte, gam, rg_all, ks_all, kd_all,
            lnw, lnb, g, C=C, n_steps=n_steps)
    for pi in pairs:
        sbd[pi] = s_raw[pi]

    @pl.when(chunk == n_chunks - 1)
    def _():
        for pi in range(n_pairs):
            s_new = sbd[pi]
            st_ref[2 * pi] = s_new[0:hd, 0:hd]
            st_ref[2 * pi + 1] = s_new[hd:2 * hd, hd:2 * hd]


def _wkv_chunk(q, s_raw, y_ref, sls, stack_masked, segsum, strict, incl, same_head,
               kk_all, rksum_all, a, xv, gprev, ginv, dte, gam, rg_all, ks_all, kd_all, lnw, lnb, g, *, C, n_steps):
    rs = slice(q * C, (q + 1) * C)
    R = 2 * C
    inv_hd = 1.0 / RWKV_HEAD
    pairs = range(len(sls))
    xv_p = [xv[rs, sl] for sl in sls]
    kk = [k[rs, :] for k in kk_all]
    bonus = [sm[rs, :] * v for sm, v in zip(rksum_all, xv_p)]
    kka = [k * a[rs, sl] for k, sl in zip(kk, sls)]
    gprev, ginv, dte, rg_all, ks_all, kd_all = (t[rs, :] for t in (gprev, ginv, dte, rg_all, ks_all, kd_all))
    gam_end = gam[(q + 1) * C - 1:(q + 1) * C, :]
    def both(f, pieces):
        return tuple(f(p_) for p_ in pieces)

    def cat(axis, *pairs_of_pieces):
        return tuple(jnp.concatenate(ps, axis=axis) for ps in zip(*pairs_of_pieces))

    la = [_hi_lo(-k * gprev[:, sl]) for k, sl in zip(kk, sls)]
    lr = [rg_all[:, sl].astype(BF16) for sl in sls]
    rhs = [cat(0, both(stack_masked, _hi_lo(ka_ * ginv[:, sl])), both(stack_masked, _hi_lo(ks_all[:, sl])))
           for ka_, sl in zip(kka, sls)]
    xv_s = [_hi_lo(v) for v in xv_p]
    vst = [both(stack_masked, v) for v in xv_s]
    s_old = [_hi_lo(s_) for s_ in s_raw]
    sc_a = [_dot3(l_, r_, _NT) for l_, r_ in zip(la, rhs)]
    sc_r = [_mm(l_, r_[0], _NT) for l_, r_ in zip(lr, rhs)]
    ls_a = [_dot3(l_, s_, _NT) for l_, s_ in zip(la, s_old)]
    ls_r = [_mm(l_, s_[0], _NT) for l_, s_ in zip(lr, s_old)]
    n_ab = [jnp.where(strict, s4[:, :R], 0.0) for s4 in sc_a]
    m_ak = [jnp.where(strict, s4[:, R:], 0.0) for s4 in sc_a]
    m_r = [jnp.concatenate([jnp.where(incl, s4[:, :R], 0.0), jnp.where(incl, s4[:, R:], 0.0)],
                           axis=1).astype(BF16) for s4 in sc_r]
    x_u = [l_ + _dot3(_hi_lo(mk), v) for l_, mk, v in zip(ls_a, m_ak, vst)]
    pw = n_ab
    for step in range(n_steps):
        ps = [_hi_lo(p_) for p_ in pw]
        xs = [both(stack_masked, _hi_lo(x_)) for x_ in x_u]
        if step + 1 < n_steps:
            t = [_dot3(p_, cat(1, x_, both(stack_masked, p_))) for p_, x_ in zip(ps, xs)]
            x_u = [x_ + t_[:, :LANES] for x_, t_ in zip(x_u, t)]
            pw = [t_[:, LANES:] for t_ in t]
        else:
            x_u = [x_ + _dot3(p_, x_s) for x_, p_, x_s in zip(x_u, ps, xs)]
    xu_s = [_hi_lo(x_) for x_ in x_u]
    y = [l_ + _mm(mr, jnp.concatenate([stack_masked(x_[0]), v[0]], axis=0))
         for l_, mr, x_, v in zip(ls_r, m_r, xu_s, vst)]
    upd = [_dot3(cat(0, x_, v), _hi_lo(jnp.concatenate([ka_ * dte[:, sl], kd_all[:, sl]], axis=0)), _TN)
           for x_, v, ka_, sl in zip(xu_s, xv_s, kka, sls)]
    s_new = [s_raw[pi] * gam_end[:, sls[pi]] + jnp.where(same_head, upd[pi], 0.0) for pi in pairs]

    yc = [y_ - segsum(y_) * inv_hd for y_ in y]
    y_var = [segsum(c_ * c_) * inv_hd for c_ in yc]
    for pi in pairs:
        sl = sls[pi]
        yn = yc[pi] * lax.rsqrt(y_var[pi] + GN_EPS) * lnw[:, sl] + lnb[:, sl]
        y_ref[rs, sl] = ((yn + bonus[pi]) * g[rs, sl]).astype(BF16)
    return s_new


def _wkv(p, l, mu, shift0, w2, a2, g2, w0, a0, kkp, ka, rk, lnw, lnb, state0, n_seq, seq_len, C, n_sub):
    m, n_pad = p.shape
    d_rwkv = w0.shape[-1]
    n_heads = d_rwkv // RWKV_HEAD
    n_pairs = WKV_PAIRS_PER_STEP
    wd = n_pairs * LANES
    n_groups = d_rwkv // wd
    cb = C * n_sub
    n_chunks = seq_len // cb
    nb = n_groups
    blk_w = 3 * d_rwkv // LANES
    blk_a = blk_w + 1
    blk_g = (blk_a + 1) * LANES // GATE_LORA

    row = lambda s, h, c: s * n_chunks + c
    seg_specs = [
        pl.BlockSpec((cb, wd), lambda s, h, c: (row(s, h, c), h)),
        pl.BlockSpec((cb, wd), lambda s, h, c: (row(s, h, c), nb + h)),
        pl.BlockSpec((cb, wd), lambda s, h, c: (row(s, h, c), 2 * nb + h)),
        pl.BlockSpec((cb, LANES), lambda s, h, c: (row(s, h, c), blk_w)),
        pl.BlockSpec((cb, LANES), lambda s, h, c: (row(s, h, c), blk_a)),
        pl.BlockSpec((cb, GATE_LORA), lambda s, h, c: (row(s, h, c), blk_g)),
    ]
    mu_specs = [
        pl.BlockSpec((None, 1, wd), lambda s, h, c: (l, 0, h)),
        pl.BlockSpec((None, 1, wd), lambda s, h, c: (l, 0, nb + h)),
        pl.BlockSpec((None, 1, wd), lambda s, h, c: (l, 0, 2 * nb + h)),
        pl.BlockSpec((None, 1, LANES), lambda s, h, c: (l, 0, blk_w)),
        pl.BlockSpec((None, 1, LANES), lambda s, h, c: (l, 0, blk_a)),
        pl.BlockSpec((None, 1, GATE_LORA), lambda s, h, c: (l, 0, blk_g)),
    ]
    s0_specs = [
        pl.BlockSpec((None, 1, wd), lambda s, h, c: (s, 0, h)),
        pl.BlockSpec((None, 1, wd), lambda s, h, c: (s, 0, nb + h)),
        pl.BlockSpec((None, 1, wd), lambda s, h, c: (s, 0, 2 * nb + h)),
        pl.BlockSpec((None, 1, LANES), lambda s, h, c: (s, 0, blk_w)),
        pl.BlockSpec((None, 1, LANES), lambda s, h, c: (s, 0, blk_a)),
        pl.BlockSpec((None, 1, GATE_LORA), lambda s, h, c: (s, 0, blk_g)),
    ]
    lora_specs = [
        pl.BlockSpec((None, LORA_PAD, wd), lambda s, h, c: (l, 0, h)),
        pl.BlockSpec((None, LORA_PAD, wd), lambda s, h, c: (l, 0, h)),
        pl.BlockSpec((None, GATE_LORA, wd), lambda s, h, c: (l, 0, h)),
    ]
    vec_spec = pl.BlockSpec((None, 1, wd), lambda s, h, c: (l, 0, h))
    state_spec = pl.BlockSpec((None, 2 * n_pairs, RWKV_HEAD, RWKV_HEAD), lambda s, h, c: (s, h, 0, 0))
    ls = min(l, state0.shape[0] - 1)
    state0_spec = pl.BlockSpec((None, None, 2 * n_pairs, RWKV_HEAD, RWKV_HEAD), lambda s, h, c: (ls, s, h, 0, 0))
    y, st = pl.pallas_call(
        functools.partial(_wkv_kernel, C=C, n_sub=n_sub, n_chunks=n_chunks, n_pairs=n_pairs),
        grid=(n_seq, n_groups, n_chunks),
        in_specs=seg_specs + mu_specs + s0_specs + lora_specs + [vec_spec] * 7 + [state0_spec],
        out_specs=[
            pl.BlockSpec((cb, wd), lambda s, h, c: (row(s, h, c), h)),
            state_spec,
        ],
        out_shape=[
            jax.ShapeDtypeStruct((m, d_rwkv), BF16),
            jax.ShapeDtypeStruct((n_seq, n_heads, RWKV_HEAD, RWKV_HEAD), F32),
        ],
        scratch_shapes=[pltpu.VMEM((n_pairs, LANES, LANES), F32)]
        + [pltpu.VMEM((SUBLANES, wd), F32)] * 3 + [pltpu.VMEM((SUBLANES, LANES), F32)] * 2
        + [pltpu.VMEM((SUBLANES, GATE_LORA), F32)],
        compiler_params=_cparams(("arbitrary", "arbitrary", "arbitrary")),
        name="wkv",
    )(p, p, p, p, p, p, mu, mu, mu, mu, mu, mu,
      shift0, shift0, shift0, shift0, shift0, shift0,
      w2, a2, g2, w0, a0, kkp, ka, rk, lnw, lnb, state0)
    return y, st


def _lambda(lqk_ref, lam_init):
    t = lqk_ref[...]
    s1 = jnp.sum(t[0:1, :] * t[1:2, :], axis=-1, keepdims=True)
    s2 = jnp.sum(t[2:3, :] * t[3:4, :], axis=-1, keepdims=True)
    return jnp.exp(s1) - jnp.exp(s2) + lam_init


def _split_maps(q):
    lane = lax.broadcasted_iota(jnp.int32, q.shape, 1)
    zero = jnp.zeros_like(q)
    return jnp.where(lane < DIFF_HEAD, q, zero), jnp.where(lane < DIFF_HEAD, zero, q)


def _subln(o, subln, lam_init):
    return _rms(o, subln, SUBLN_EPS) * (1.0 - lam_init)


FAR_STEP_TILES = (4, 2, 1)
ATTN_TILES_PER_STEP = 4


def _attn_kernel(q_ref, k_ref, v_ref, bias_ref, lqk_ref, subln_ref, o_ref,
                 kb, vt, m_s, a_s, *, TB, n_cast, lam_init):
    g = pl.program_id(1)

    @pl.when(g == 0)
    def _():
        vt[LANES:, :] = jnp.ones((vt.shape[0] - LANES, vt.shape[1]), BF16)

        def cast(c, carry):
            off = pl.multiple_of(c * TB, TB)
            kb[pl.ds(off, TB), :] = k_ref[pl.ds(off, TB), :].astype(BF16)
            vt[0:LANES, pl.ds(off, TB)] = v_ref[pl.ds(off, TB), :].T.astype(BF16)
            return carry
        lax.fori_loop(0, n_cast, cast, 0)

    nq = ATTN_TILES_PER_STEP
    chains = range(nq)
    qs = [jnp.concatenate(_split_maps(q_ref[c * TB:(c + 1) * TB, :]), axis=0) for c in chains]
    m_s[...] = jnp.full(m_s.shape, MASK_VALUE, F32)
    a_s[...] = jnp.zeros(a_s.shape, F32)

    def scores(c, off, width):
        return _mm(kb[pl.ds(off, width), :], qs[c], _NT)

    def update(c, s, off, width):
        m_prev = m_s[c]
        m_new = jnp.maximum(m_prev, jnp.max(s, axis=0, keepdims=True))
        alpha = jnp.exp(m_prev - m_new)
        p = jnp.exp(s - m_new).astype(BF16)
        a_s[c] = alpha * a_s[c] + _mm(vt[:, pl.ds(off, width)], p)
        m_s[c] = m_new

    def far_all(off, width):
        s = [scores(c, off, width) for c in chains]
        for c in chains:
            update(c, s[c], off, width)

    n_common = jnp.maximum(nq * g - 1, 0)
    done = 0
    for step_tiles in FAR_STEP_TILES:
        width = TB * step_tiles
        n_steps = (n_common - done) // step_tiles

        def far_step(j, carry, done=done, width=width):
            far_all(pl.multiple_of(done * TB + j * width, TB), width)
            return carry
        lax.fori_loop(0, n_steps, far_step, 0)
        done = done + n_steps * step_tiles

    def own_far(first_chain, off0):
        for k in range(nq - first_chain):
            off = pl.multiple_of(off0 + k * TB, TB)
            cs = range(first_chain + k, nq)
            s = [scores(c, off, TB) for c in cs]
            for c, s_ in zip(cs, s):
                update(c, s_, off, TB)

    @pl.when(g > 0)
    def _():
        own_far(1, n_common * TB)

    @pl.when(g == 0)
    def _():
        own_far(2, 0)

    offs = [pl.multiple_of(jnp.maximum(nq * g + c - 1, 0) * TB, TB) for c in chains]
    s_near = [scores(c, offs[c], 2 * TB) + (bias_ref[jnp.where(g == 0, 1, 0)] if c == 0 else bias_ref[0])
              for c in chains]
    for c in chains:
        update(c, s_near[c], offs[c], 2 * TB)

    lam = _lambda(lqk_ref, lam_init)
    for c in chains:
        acc = a_s[c]
        on = acc[0:LANES, :] / acc[LANES:LANES + 1, :]
        o = (on[:, :TB] - lam * on[:, TB:]).T
        o_ref[c * TB:(c + 1) * TB, :] = _subln(o, subln_ref[...], lam_init).astype(BF16)


def _attn_prompt(q, k_all, v_all, l, bias, lqk, subln, lam_init, TB):
    m, d_diff = q.shape
    n_heads = d_diff // LANES
    n_blk = m // TB
    nq = ATTN_TILES_PER_STEP
    assert n_blk % nq == 0
    return pl.pallas_call(
        functools.partial(_attn_kernel, TB=TB, n_cast=n_blk, lam_init=lam_init),
        grid=(n_heads, n_blk // nq),
        in_specs=[
            pl.BlockSpec((nq * TB, LANES), lambda h, g: (g, h)),
            pl.BlockSpec((None, m, LANES), lambda h, g: (l, 0, h)),
            pl.BlockSpec((None, m, LANES), lambda h, g: (l, 0, h)),
            pl.BlockSpec((None, 2, 2 * TB, 2 * TB), lambda h, g: (h, 0, 0, 0)),
            pl.BlockSpec((None, 4, DIFF_HEAD), lambda h, g: (l, 0, 0)),
            pl.BlockSpec((None, 1, LANES), lambda h, g: (l, 0, 0)),
        ],
        out_specs=pl.BlockSpec((nq * TB, LANES), lambda h, g: (g, h)),
        out_shape=jax.ShapeDtypeStruct((m, d_diff), BF16),
        scratch_shapes=[pltpu.VMEM((m, LANES), BF16), pltpu.VMEM((LANES + BF16_ROWS, m), BF16),
                        pltpu.VMEM((nq, 1, 2 * TB), F32), pltpu.VMEM((nq, LANES + BF16_ROWS, 2 * TB), F32)],
        compiler_params=_cparams(("arbitrary", "arbitrary")),
        name="attn_prompt",
    )(q, k_all, v_all, bias, lqk, subln)


SAMPLE_FAR_FRAMES = 384


def _attn_sample_kernel(q_ref, kn_ref, vn_ref, kc_ref, vc_ref, bias_ref, lqk_ref, subln_ref, o_ref,
                        kb, acc, *, T, n_heads, near, lam_init):
    past = kc_ref.shape[0]
    n_cols = 2 * T * n_heads
    fc = SAMPLE_FAR_FRAMES
    n_far = (past - near) // fc
    rows = fc * n_heads
    qs = jnp.concatenate(
        [m_ for h in range(n_heads) for m_ in _split_maps(q_ref[:, h * LANES:(h + 1) * LANES])], axis=0)
    sub = lax.broadcasted_iota(jnp.int32, (n_heads, n_cols), 0)
    col = lax.broadcasted_iota(jnp.int32, (n_heads, n_cols), 1)
    valid = (col >= sub * (2 * T)) & (col < (sub + 1) * (2 * T))

    def flat(x3):
        return x3.reshape(x3.shape[0] * n_heads, LANES)

    def scores(k_rows):
        s = _mm(k_rows, qs, _NT)
        return s.reshape(s.shape[0] // n_heads, n_heads, n_cols)

    near_k = jnp.concatenate([flat(kc_ref[past - near:past]), flat(kn_ref[...])], axis=0).astype(BF16)
    near_v = jnp.concatenate([flat(vc_ref[past - near:past]), flat(vn_ref[...])], axis=0).astype(BF16)
    s_near = scores(near_k) + bias_ref[...].reshape(near + T, n_heads, n_cols)

    def pass1(c, m3):
        f0 = pl.multiple_of(c * fc, fc)
        k_rows = flat(kc_ref[pl.ds(f0, fc)]).astype(BF16)
        kb[pl.ds(pl.multiple_of(c * rows, rows), rows), :] = k_rows
        return jnp.maximum(m3, jnp.max(scores(k_rows), axis=0))
    m3 = lax.fori_loop(0, n_far, pass1, jnp.max(s_near, axis=0))

    ones = jnp.ones((rows, LANES), BF16)

    def weights(s3):
        p = jnp.where(valid, jnp.exp(s3 - m3), 0.0)
        return p.reshape(p.shape[0] * n_heads, n_cols).astype(BF16)

    acc[...] = _mm(weights(s_near), jnp.concatenate([near_v, ones[:near_v.shape[0]]], axis=1), _TN)

    def pass2(c, carry):
        f0 = pl.multiple_of(c * fc, fc)
        p = weights(scores(kb[pl.ds(pl.multiple_of(c * rows, rows), rows), :]))
        v_aug = jnp.concatenate([flat(vc_ref[pl.ds(f0, fc)]).astype(BF16), ones], axis=1)
        acc[...] += _mm(p, v_aug, _TN)
        return carry
    lax.fori_loop(0, n_far, pass2, 0)

    lam = _lambda(lqk_ref, lam_init)
    for h in range(n_heads):
        a = acc[h * 2 * T:(h + 1) * 2 * T, :]
        on = a[:, 0:LANES] / a[:, LANES:2 * LANES]
        o = on[:T, :] - lam * on[T:, :]
        o_ref[:, h * LANES:(h + 1) * LANES] = _subln(o, subln_ref[...], lam_init).astype(BF16)


def _attn_sample(q, kn_all, vn_all, l, k_cache, v_cache, bias_t, lqk, subln, lam_init, n_seq, T, near):
    m, d_diff = q.shape
    n_heads = d_diff // LANES
    past = k_cache.shape[2]
    assert (past - near) % SAMPLE_FAR_FRAMES == 0
    cache_spec = pl.BlockSpec((None, None, past, n_heads, LANES), lambda b: (l, b, 0, 0, 0))
    new_spec = pl.BlockSpec((None, T, n_heads, LANES), lambda b: (l, b, 0, 0))
    return pl.pallas_call(
        functools.partial(_attn_sample_kernel, T=T, n_heads=n_heads, near=near, lam_init=lam_init),
        grid=(n_seq,),
        in_specs=[
            pl.BlockSpec((T, d_diff), lambda b: (b, 0)),
            new_spec,
            new_spec,
            cache_spec,
            cache_spec,
            pl.BlockSpec(bias_t.shape, lambda b: (0, 0)),
            pl.BlockSpec((None, 4, DIFF_HEAD), lambda b: (l, 0, 0)),
            pl.BlockSpec((None, 1, LANES), lambda b: (l, 0, 0)),
        ],
        out_specs=pl.BlockSpec((T, d_diff), lambda b: (b, 0)),
        out_shape=jax.ShapeDtypeStruct((m, d_diff), BF16),
        scratch_shapes=[pltpu.VMEM(((past - near) * n_heads, LANES), BF16),
                        pltpu.VMEM((2 * T * n_heads, 2 * LANES), F32)],
        compiler_params=_cparams(("arbitrary",)),
        name="attn_sample",
    )(q, kn_all, vn_all, k_cache, v_cache, bias_t, lqk, subln)


def _outproj_kernel(x_ref, yr_ref, yd_ref, gt_ref, wt_ref, wb_ref, o_ref):
    mixed = _mm(yr_ref[...], wt_ref[...]) + _mm(yd_ref[...], wb_ref[...])
    o_ref[...] = x_ref[...] + gt_ref[...] * mixed


def _outproj(x, yr, yd, gt, w_out_all, l, tm):
    m, d = x.shape
    dr = yr.shape[1]
    dd = yd.shape[1]
    gt_spec = (pl.BlockSpec((1, d), lambda i: (0, 0)) if gt.shape[0] == 1
               else pl.BlockSpec((tm, d), lambda i: (i, 0)))
    return pl.pallas_call(
        _outproj_kernel,
        grid=(m // tm,),
        in_specs=[
            pl.BlockSpec((tm, d), lambda i: (i, 0)),
            pl.BlockSpec((tm, dr), lambda i: (i, 0)),
            pl.BlockSpec((tm, dd), lambda i: (i, 0)),
            gt_spec,
            pl.BlockSpec((None, dr, d), lambda i: (l, 0, 0)),
            pl.BlockSpec((None, dd, d), lambda i: (l, dr // dd, 0)),
        ],
        out_specs=pl.BlockSpec((tm, d), lambda i: (i, 0)),
        out_shape=jax.ShapeDtypeStruct((m, d), F32),
        compiler_params=_cparams(("arbitrary",)),
        name="outproj",
    )(x, yr, yd, gt, w_out_all, w_out_all)


def _shift_rows(x, prev, n):
    if n % SUBLANES == 0:
        return jnp.concatenate([prev, x[:x.shape[0] - n, :]], axis=0)
    rolled = pltpu.roll(x, n, 0)
    row = lax.broadcasted_iota(jnp.int32, x.shape, 0)
    out = rolled
    for r in range(n):
        out = jnp.where(row == r, prev[r:r + 1, :], out)
    return out


def _ffn_kernel(x_ref, sh_ref, sc_ref, gt_ref, g_ref, wg_ref, wv_ref, cw_ref, cb_ref, c0_ref, wd_ref, gf_ref,
                o_ref, co_ref, h_scr, acc_scr, carry, *, B, n_j, final_norm):
    i = pl.program_id(0)
    j = pl.program_id(1)
    tm = x_ref.shape[0]

    @pl.when(j == 0)
    def _():
        xn = _rms(x_ref[...], g_ref[...], NORM_EPS)
        h_scr[...] = (xn * (1.0 + sc_ref[...]) + sh_ref[...]).astype(BF16)
        acc_scr[...] = jnp.zeros(acc_scr.shape, F32)

    @pl.when(i == 0)
    def _():
        carry[j, 0:2 * B, :] = c0_ref[...]

    h = h_scr[...]
    ug = _mm(h, wg_ref[...])
    uv = _mm(h, wv_ref[...])
    prev = carry[j, 0:2 * B, :]
    s1 = _shift_rows(ug, prev[B:2 * B, :], B)
    s2 = _shift_rows(ug, prev, 2 * B)
    cw = cw_ref[...]
    z = cb_ref[...] + s2 * cw[0:1, :] + s1 * cw[1:2, :] + ug * cw[2:3, :]
    act = 0.5 * z * (1.0 + lax.erf(z * (2.0 ** -0.5)))
    acc_scr[...] += _mm((act * uv).astype(BF16), wd_ref[...])
    last2 = ug[tm - 2 * B:tm, :]
    carry[j, 0:2 * B, :] = last2
    tf = last2.shape[1]
    co_ref[:, pl.ds(pl.multiple_of(j * tf, tf), tf)] = last2

    @pl.when(j == n_j - 1)
    def _():
        x2 = x_ref[...] + gt_ref[...] * acc_scr[...]
        if final_norm:
            x2 = _rms(x2, gf_ref[...], NORM_EPS)
        o_ref[...] = x2


def _ffn(x, sh, sc, gt, g_all, w_up_all, conv_w_all, conv_b_all, conv0, w_down_all, g_final, l, B, tm,
         final_norm, tf=512):
    m, d = x.shape
    d_ff = w_down_all.shape[1]
    n_j = d_ff // tf
    rows_c = max(SUBLANES, 2 * B)
    return pl.pallas_call(
        functools.partial(_ffn_kernel, B=B, n_j=n_j, final_norm=final_norm),
        grid=(m // tm, n_j),
        in_specs=[
            pl.BlockSpec((tm, d), lambda i, j: (i, 0)),
            _mod_spec(sh, tm, d),
            _mod_spec(sc, tm, d),
            _mod_spec(gt, tm, d),
            pl.BlockSpec((None, 1, d), lambda i, j: (l, 0, 0)),
            pl.BlockSpec((None, d, tf), lambda i, j: (l, 0, j)),
            pl.BlockSpec((None, d, tf), lambda i, j: (l, 0, n_j + j)),
            pl.BlockSpec((None, CONV_W, tf), lambda i, j: (l, 0, j)),
            pl.BlockSpec((None, 1, tf), lambda i, j: (l, 0, j)),
            pl.BlockSpec((2 * B, tf), lambda i, j: (0, j)),
            pl.BlockSpec((None, tf, d), lambda i, j: (l, j, 0)),
            pl.BlockSpec((1, d), lambda i, j: (0, 0)),
        ],
        out_specs=[
            pl.BlockSpec((tm, d), lambda i, j: (i, 0)),
            pl.BlockSpec((2 * B, d_ff), lambda i, j: (0, 0)),
        ],
        out_shape=[
            jax.ShapeDtypeStruct((m, d), F32),
            jax.ShapeDtypeStruct((2 * B, d_ff), F32),
        ],
        scratch_shapes=[pltpu.VMEM((tm, d), BF16), pltpu.VMEM((tm, d), F32),
                        pltpu.VMEM((n_j, rows_c, tf), F32)],
        compiler_params=_cparams(("arbitrary", "arbitrary")),
        name="conv_ffn",
    )(x, sh, sc, gt, g_all, w_up_all, w_up_all, conv_w_all, conv_b_all, conv0, w_down_all, g_final)


def _pad_cols(a, n):
    return jnp.pad(a, [(0, 0)] * (a.ndim - 1) + [(0, n)])


def _regroup_rwkv_cols(a, d_rwkv):
    o = 3 * d_rwkv
    return jnp.concatenate([
        a[..., :o],
        _pad_cols(a[..., o:o + DECAY_LORA], LORA_PAD - DECAY_LORA),
        _pad_cols(a[..., o + DECAY_LORA:o + DECAY_LORA + AAA_LORA], LORA_PAD - AAA_LORA),
        a[..., o + DECAY_LORA + AAA_LORA:],
    ], axis=-1)


def _ungroup_rwkv_cols(a, d_rwkv):
    o = 3 * d_rwkv
    return jnp.concatenate([
        a[..., :o],
        a[..., o:o + DECAY_LORA],
        a[..., o + LORA_PAD:o + LORA_PAD + AAA_LORA],
        a[..., o + 2 * LORA_PAD:],
    ], axis=-1)


def _prompt_tiles(seq):
    attn_tile = min(256, seq // 2)
    assert attn_tile % CHUNK == 0 and attn_tile >= MAX_DISTANCE
    assert seq % (ATTN_TILES_PER_STEP * attn_tile) == 0
    return attn_tile, min(512, seq), min(1024, seq), min(64, seq)


def kernel(x_prompt, x_sample, c_prompt, c_sample, cache_k, cache_v, state_wkv, state_shift, state_conv, w_ada, b_ada, g_mix, g_ffn, w_in, w_out, rwkv_mu, rwkv_w0, rwkv_w2, rwkv_a0, rwkv_a2, rwkv_g2, rwkv_kk, rwkv_ka, rwkv_rk, rwkv_ln_w, rwkv_ln_b, diff_lq1, diff_lk1, diff_lq2, diff_lk2, diff_subln, rel_table, ffn_up, ffn_conv_w, ffn_conv_b, ffn_down, g_final):
    depth, d_model, _ = w_in.shape
    bp, seq, _ = x_prompt.shape
    bs, dseq, _ = x_sample.shape
    past = cache_k.shape[2]
    d_rwkv = rwkv_w0.shape[1]
    d_diff = d_model - d_rwkv
    n_dheads = d_diff // (2 * DIFF_HEAD)
    n_rheads = d_rwkv // RWKV_HEAD
    n_rwkv_cols = rwkv_mu.shape[1]
    n_rwkv_pad = n_rwkv_cols + 2 * LORA_PAD - DECAY_LORA - AAA_LORA
    d_ff = ffn_down.shape[1]
    assert bp == 1, "prompt path handles one sequence"
    TB, tm_p, tm_in, wkv_chunk = _prompt_tiles(seq)
    m_s = bs * dseq

    w_in_b = jnp.concatenate([_regroup_rwkv_cols(w_in[..., :n_rwkv_cols], d_rwkv), w_in[..., n_rwkv_cols:]],
                             axis=-1).astype(BF16)
    mu_p = _regroup_rwkv_cols(rwkv_mu, d_rwkv)[:, None]
    w2_b = jnp.pad(rwkv_w2, ((0, 0), (0, LORA_PAD - DECAY_LORA), (0, 0))).astype(BF16)
    a2_b = jnp.pad(rwkv_a2, ((0, 0), (0, LORA_PAD - AAA_LORA), (0, 0))).astype(BF16)
    g2_b = rwkv_g2.astype(BF16)
    w_out_b = _cast_bf16(w_out)
    up_b = _cast_bf16(ffn_up)
    down_b = _cast_bf16(ffn_down)
    vec = lambda a: a.reshape(depth, 1, -1)
    wkv_args = (w2_b, a2_b, g2_b, vec(rwkv_w0), vec(rwkv_a0), vec(rwkv_kk), vec(rwkv_ka), vec(rwkv_rk),
                vec(rwkv_ln_w), vec(rwkv_ln_b))
    g_mix_v, g_ffn_v, subln_v, conv_b_v = vec(g_mix), vec(g_ffn), vec(diff_subln), vec(ffn_conv_b)
    gf = g_final[None]
    lqk = jnp.stack([diff_lq1, diff_lk1, diff_lq2, diff_lk2], axis=1)

    n_c = bp + bs
    n_c_pad = -(-n_c // SUBLANES) * SUBLANES
    c_rows = jnp.pad(jnp.concatenate([c_prompt, c_sample], axis=0), ((0, n_c_pad - n_c), (0, 0)))
    mod = _adaln_mod(c_rows, w_ada, b_ada).reshape(depth, n_c_pad, 6, d_model)

    ql = jnp.arange(TB)
    rel_diag = ql[None, :] - ql[:, None]
    allowed = (ql[None, :] // CHUNK) <= (ql[:, None] // CHUNK)
    bucket_diag = jnp.where(allowed, _t5_bucket(rel_diag), -1)
    bucket_prev = _t5_bucket(rel_diag - TB)
    windows = [jnp.concatenate([bucket_prev, bucket_diag], axis=1),
               jnp.concatenate([bucket_diag, jnp.full((TB, TB), -1)], axis=1)]
    bucket_p = jnp.concatenate([w.T for w in windows], axis=0)
    bias_p = _bias_lookup(rel_table, bucket_p.astype(jnp.int32), FAR_BUCKET, col_repeats=2)
    bias_p = bias_p.reshape(n_dheads, 2, 2 * TB, 2 * TB)
    q_pos = past + jnp.arange(dseq)
    assert past >= MAX_DISTANCE
    k_pos = past - MAX_DISTANCE + jnp.arange(MAX_DISTANCE + dseq)
    bias_s = _bias_lookup(rel_table, _t5_bucket(k_pos[None, :] - q_pos[:, None]).astype(jnp.int32), FAR_BUCKET)
    bias_t = jnp.broadcast_to(bias_s.transpose(2, 0, 1)[:, None, :, None, :],
                              (MAX_DISTANCE + dseq, n_dheads, n_dheads, 2, dseq))
    bias_t = bias_t.reshape((MAX_DISTANCE + dseq) * n_dheads, n_dheads * 2 * dseq)

    xp = x_prompt.reshape(seq, d_model)
    xs = x_sample.reshape(m_s, d_model)
    zero_shift = jnp.zeros((1, 1, n_rwkv_pad), F32)
    zero_state = jnp.zeros((1, 1, n_rheads, RWKV_HEAD, RWKV_HEAD), F32)
    zero_conv = jnp.zeros((2, d_ff), F32)
    shift_s_in = _regroup_rwkv_cols(state_shift, d_rwkv)
    outs_p = {k: [] for k in ("wkv", "shift", "conv")}
    outs_s = {k: [] for k in ("wkv", "shift", "conv")}
    kv_p = (jnp.zeros((depth, seq, d_diff), F32), jnp.zeros((depth, seq, d_diff), F32))
    kv_s = (jnp.zeros((depth, m_s, d_diff), F32), jnp.zeros((depth, m_s, d_diff), F32))

    def time_major(a):
        return a.reshape(bs, dseq, -1).swapaxes(0, 1).reshape(m_s, -1)

    def batch_major(a):
        return a.reshape(dseq, bs, -1).swapaxes(0, 1).reshape(m_s, -1)

    for l in range(depth):
        lam_init = 0.8 - 0.6 * math.exp(-0.3 * l)
        last = l == depth - 1

        mp = mod[l, 0:bp]
        sh1, sc1, gt1, sh2, sc2, gt2 = (mp[:, t] for t in range(6))
        p, q, *kv_p = _inproj(xp, sh1, sc1, g_mix_v, w_in_b, l, kv_p, n_rwkv_pad, d_diff, tm_in)
        y_r, wkv = _wkv(p, l, mu_p, zero_shift, *wkv_args, zero_state, 1, seq, wkv_chunk,
                        WKV_CHUNKS_PER_STEP if seq % (WKV_CHUNKS_PER_STEP * wkv_chunk) == 0 else 1)
        y_d = _attn_prompt(q, kv_p[0], kv_p[1], l, bias_p, lqk, subln_v, lam_init, TB)
        x1 = _outproj(xp, y_r, y_d, gt1, w_out_b, l, tm_p)
        xp, conv = _ffn(x1, sh2, sc2, gt2, g_ffn_v, up_b, ffn_conv_w, conv_b_v, zero_conv, down_b, gf, l,
                        1, tm_p, last)
        outs_p["wkv"].append(wkv)
        outs_p["shift"].append(_ungroup_rwkv_cols(p[seq - 1:seq], d_rwkv).reshape(bp, 1, n_rwkv_cols))
        outs_p["conv"].append(conv.reshape(bp, CONV_W - 1, d_ff))

        ms = mod[l, bp:bp + bs]
        rows_bm = jnp.repeat(ms, dseq, axis=0)
        rows_tm = jnp.tile(ms, (dseq, 1, 1))
        p, q, *kv_s = _inproj(xs, rows_bm[:, 0], rows_bm[:, 1], g_mix_v, w_in_b, l, kv_s, n_rwkv_pad, d_diff, m_s)
        y_r, wkv = _wkv(p, l, mu_p, shift_s_in[l], *wkv_args, state_wkv, bs, dseq, dseq, 1)
        kn4, vn4 = (a.reshape(depth, m_s, n_dheads, 2 * DIFF_HEAD) for a in kv_s)
        y_d = _attn_sample(q, kn4, vn4, l, cache_k, cache_v, bias_t, lqk, subln_v, lam_init, bs, dseq,
                           MAX_DISTANCE)
        x1 = _outproj(xs, y_r, y_d, rows_bm[:, 2], w_out_b, l, m_s)
        conv0 = state_conv[l].swapaxes(0, 1).reshape((CONV_W - 1) * bs, d_ff)
        x2, conv = _ffn(time_major(x1), rows_tm[:, 3], rows_tm[:, 4], rows_tm[:, 5], g_ffn_v, up_b, ffn_conv_w,
                        conv_b_v, conv0, down_b, gf, l, bs, m_s, last)
        xs = batch_major(x2)
        outs_s["wkv"].append(wkv)
        p_last = p.reshape(bs, dseq, n_rwkv_pad)[:, dseq - 1:dseq]
        outs_s["shift"].append(_ungroup_rwkv_cols(p_last, d_rwkv))
        outs_s["conv"].append(conv.reshape(CONV_W - 1, bs, d_ff).swapaxes(0, 1))

    st = lambda xs_: jnp.stack(xs_)
    head_shape = (n_dheads, 2 * DIFF_HEAD)
    return (xp.reshape(bp, seq, d_model), xs.reshape(bs, dseq, d_model),
            kv_p[0].reshape(depth, bp, seq, *head_shape), kv_p[1].reshape(depth, bp, seq, *head_shape),
            st(outs_p["wkv"]), st(outs_p["shift"]), st(outs_p["conv"]),
            kv_s[0].reshape(depth, bs, dseq, *head_shape), kv_s[1].reshape(depth, bs, dseq, *head_shape),
            st(outs_s["wkv"]), st(outs_s["shift"]), st(outs_s["conv"]))
```

```python
import functools
import math

import jax
import jax.numpy as jnp
from jax import lax
from jax.experimental import pallas as pl
from jax.experimental.pallas import tpu as pltpu

F32 = jnp.float32
BF16 = jnp.bfloat16

CHUNK = 64
RWKV_HEAD = 64
DIFF_HEAD = 64
N_BUCKETS = 32
MAX_DISTANCE = 128
NORM_EPS = 1e-6
GN_EPS = 64e-5
SUBLN_EPS = 1e-5
DECAY_LORA = 96
AAA_LORA = 96
GATE_LORA = 256
LORA_PAD = 128
CONV_W = 3

LANES = 128
SUBLANES = 8
BF16_ROWS = 16
VMEM_LIMIT_BYTES = 56 * 1024 * 1024
MASK_VALUE = -1e30
WKV_CHUNKS_PER_STEP = 4
WKV_PAIRS_PER_STEP = 8

_NN = (((1,), (0,)), ((), ()))
_NT = (((1,), (1,)), ((), ()))
_TN = (((0,), (0,)), ((), ()))


def _mm(a, b, dims=_NN):
    return lax.dot_general(a, b, dims, preferred_element_type=F32)


def _hi_lo(x):
    h = x.astype(BF16)
    return h, (x - h.astype(F32)).astype(BF16)


def _dot3(a, b, dims=_NN):
    (ah, al), (bh, bl) = a, b
    (ca,), (cb,) = dims[0]
    k = ah.shape[ca]
    if (ca == 0 or k % LANES == 0) and (cb == 0 or k % LANES == 0):
        return _mm(jnp.concatenate([ah, al, ah], axis=ca), jnp.concatenate([bh, bh, bl], axis=cb), dims)
    return _mm(ah, bh, dims) + (_mm(al, bh, dims) + _mm(ah, bl, dims))


def _mm3(a, b, dims=_NN):
    return _dot3(_hi_lo(a), _hi_lo(b), dims)


def _sel_rows(sel2, x):
    xh, xl = _hi_lo(x)
    return _mm(sel2, jnp.concatenate([xh, xl], axis=0))


def _sel_lanes(x, sel2):
    xh, xl = _hi_lo(x)
    return _mm(jnp.concatenate([xh, xl], axis=1), sel2)


def _cparams(sem):
    return pltpu.CompilerParams(dimension_semantics=sem, vmem_limit_bytes=VMEM_LIMIT_BYTES)


def _rms(x, g, eps):
    return x * lax.rsqrt(jnp.mean(x * x, axis=-1, keepdims=True) + eps) * g


def _cast_kernel(x_ref, o_ref):
    o_ref[...] = x_ref[...].astype(BF16)


WEIGHT_COL_TILE = 512
CAST_BLOCK_BYTES = 8 * 1024 * 1024


def _cast_bf16(w, col_tile=None):
    depth, rows, cols = w.shape
    tc = col_tile or (1024 if cols % 1024 == 0 else cols)
    tr = max(t for t in range(BF16_ROWS, rows + 1, BF16_ROWS)
             if rows % t == 0 and t * tc * 4 <= CAST_BLOCK_BYTES)
    spec = pl.BlockSpec((None, tr, tc), lambda l, i, j: (l, i, j))
    if col_tile:
        out_spec = pl.BlockSpec((None, None, tr, tc), lambda l, i, j: (l, j, i, 0))
        out_shape = (depth, cols // tc, rows, tc)
    else:
        out_spec, out_shape = spec, w.shape
    return pl.pallas_call(
        _cast_kernel,
        grid=(depth, rows // tr, cols // tc),
        in_specs=[spec],
        out_specs=out_spec,
        out_shape=jax.ShapeDtypeStruct(out_shape, BF16),
        compiler_params=_cparams(("arbitrary", "arbitrary", "arbitrary")),
        name="cast_bf16",
    )(w)


def _mod_kernel(c_ref, w_ref, b_ref, o_ref):
    c = c_ref[...]
    s = (c * jax.nn.sigmoid(c)).astype(BF16)
    o_ref[...] = _mm(s, w_ref[...].astype(BF16)) + b_ref[...]


def _adaln_mod(c_rows, w_ada, b_ada, tn=1024):
    depth, d, n = w_ada.shape
    rows = c_rows.shape[0]
    return pl.pallas_call(
        _mod_kernel,
        grid=(depth, n // tn),
        in_specs=[
            pl.BlockSpec((rows, d), lambda l, j: (0, 0)),
            pl.BlockSpec((None, d, tn), lambda l, j: (l, 0, j)),
            pl.BlockSpec((None, 1, tn), lambda l, j: (l, 0, j)),
        ],
        out_specs=pl.BlockSpec((None, rows, tn), lambda l, j: (l, 0, j)),
        out_shape=jax.ShapeDtypeStruct((depth, rows, n), F32),
        compiler_params=_cparams(("arbitrary", "arbitrary")),
        name="adaln_mod",
    )(c_rows, w_ada, b_ada.reshape(depth, 1, n))


def _bias_kernel(tab_ref, bucket_ref, o_ref, *, far_bucket):
    h = pl.program_id(0)
    bucket = bucket_ref[...]
    far = tab_ref[far_bucket, h]
    acc = jnp.full(bucket.shape, MASK_VALUE, F32)
    for b in range(N_BUCKETS):
        acc = jnp.where(bucket == b, tab_ref[b, h] - far, acc)
    c = bucket.shape[1]
    for k in range(o_ref.shape[1] // c):
        o_ref[:, k * c:(k + 1) * c] = acc


def _bias_lookup(table, bucket, far_bucket, col_repeats=1):
    n_heads = table.shape[1]
    r, c = bucket.shape
    return pl.pallas_call(
        functools.partial(_bias_kernel, far_bucket=far_bucket),
        grid=(n_heads,),
        in_specs=[
            pl.BlockSpec(memory_space=pltpu.SMEM),
            pl.BlockSpec((r, c), lambda h: (0, 0)),
        ],
        out_specs=pl.BlockSpec((None, r, c * col_repeats), lambda h: (h, 0, 0)),
        out_shape=jax.ShapeDtypeStruct((n_heads, r, c * col_repeats), F32),
        compiler_params=_cparams(("arbitrary",)),
        name="bias_lookup",
    )(table, bucket)


def _t5_bucket(rel):
    nb = N_BUCKETS // 2
    max_exact = nb // 2
    n = jnp.abs(rel)
    nf = jnp.maximum(n, 1).astype(F32)
    large = max_exact + (jnp.log(nf / max_exact) / math.log(MAX_DISTANCE / max_exact) * (nb - max_exact)).astype(jnp.int32)
    large = jnp.minimum(large, nb - 1)
    return jnp.where(rel > 0, nb, 0) + jnp.where(n < max_exact, n, large)


FAR_BUCKET = N_BUCKETS // 2 - 1


def _inproj_kernel(x_ref, sh_ref, sc_ref, g_ref, w_ref, *rest, n_p, n_q):
    p_ref, q_ref, k_ref, v_ref, h_scr = rest[-5:]
    j = pl.program_id(1)

    @pl.when(j == 0)
    def _():
        xn = _rms(x_ref[...], g_ref[...], NORM_EPS)
        h_scr[...] = (xn * (1.0 + sc_ref[...]) + sh_ref[...]).astype(BF16)

    @pl.when(j < n_p)
    def _():
        p_ref[...] = _mm(h_scr[...], w_ref[...])

    @pl.when((j >= n_p) & (j < n_p + n_q))
    def _():
        q_ref[...] = (_mm(h_scr[...], w_ref[...]) * (DIFF_HEAD ** -0.5)).astype(BF16)

    @pl.when((j >= n_p + n_q) & (j < n_p + 2 * n_q))
    def _():
        k_ref[...] = _mm(h_scr[...], w_ref[...])

    @pl.when(j >= n_p + 2 * n_q)
    def _():
        v_ref[...] = _mm(h_scr[...], w_ref[...])


def _mod_spec(mod, tm, d):
    if mod.shape[0] == 1:
        return pl.BlockSpec((1, d), lambda i, j: (0, 0))
    return pl.BlockSpec((tm, d), lambda i, j: (i, 0))


def _inproj(x, sh, sc, g_all, w_all, l, kv_all, n_rwkv_pad, d_diff, tm):
    m, d = x.shape
    depth, n_tiles, _, tn = w_all.shape
    n = n_tiles * tn
    n_p, n_q = n_rwkv_pad // tn, d_diff // tn
    clip = lambda j, lo: jnp.clip(j - lo, 0, n_q - 1)
    in_specs = [
        pl.BlockSpec((tm, d), lambda i, j: (i, 0)),
        _mod_spec(sh, tm, d),
        _mod_spec(sc, tm, d),
        pl.BlockSpec((None, 1, d), lambda i, j: (l, 0, 0)),
        pl.BlockSpec((None, None, d, tn), lambda i, j: (l, j, 0, 0)),
        pl.BlockSpec(memory_space=pl.ANY),
        pl.BlockSpec(memory_space=pl.ANY),
    ]
    args = [x, sh, sc, g_all, w_all, kv_all[0], kv_all[1]]
    aliases = {5: 2, 6: 3}
    return pl.pallas_call(
        functools.partial(_inproj_kernel, n_p=n_p, n_q=n_q),
        grid=(m // tm, n // tn),
        in_specs=in_specs,
        out_specs=[
            pl.BlockSpec((tm, tn), lambda i, j: (i, jnp.minimum(j, n_p - 1))),
            pl.BlockSpec((tm, tn), lambda i, j: (i, clip(j, n_p))),
            pl.BlockSpec((None, tm, tn), lambda i, j: (l, i, clip(j, n_p + n_q))),
            pl.BlockSpec((None, tm, tn), lambda i, j: (l, i, clip(j, n_p + 2 * n_q))),
        ],
        out_shape=[
            jax.ShapeDtypeStruct((m, n_rwkv_pad), F32),
            jax.ShapeDtypeStruct((m, d_diff), BF16),
            jax.ShapeDtypeStruct((depth, m, d_diff), F32),
            jax.ShapeDtypeStruct((depth, m, d_diff), F32),
        ],
        scratch_shapes=[pltpu.VMEM((tm, d), BF16)],
        input_output_aliases=aliases,
        compiler_params=_cparams(("arbitrary", "arbitrary")),
        name="inproj",
    )(*args)


def _wkv_kernel(pr_ref, pk_ref, pv_ref, pw_ref, pa_ref, pg_ref,
                mur_ref, muk_ref, muv_ref, muw_ref, mua_ref, mug_ref,
                s0r_ref, s0k_ref, s0v_ref, s0w_ref, s0a_ref, s0g_ref,
                w2_ref, a2_ref, g2_ref,
                w0_ref, a0_ref, kkp_ref, ka_ref, rk_ref, lnw_ref, lnb_ref,
                st0_ref,
                y_ref, st_ref,
                sbd, cr, ck, cv, cw, ca, cg, *, C, n_sub, n_chunks, n_pairs):
    chunk = pl.program_id(2)
    CB = n_sub * C
    hd = RWKV_HEAD
    zero_blk = jnp.zeros((hd, hd), F32)

    @pl.when(chunk == 0)
    def _():
        for c_ref, s_ref in ((cr, s0r_ref), (ck, s0k_ref), (cv, s0v_ref), (cw, s0w_ref), (ca, s0a_ref), (cg, s0g_ref)):
            c_ref[0:1, :] = s_ref[...]
        for pi in range(n_pairs):
            top = jnp.concatenate([st0_ref[2 * pi], zero_blk], axis=1)
            bot = jnp.concatenate([zero_blk, st0_ref[2 * pi + 1]], axis=1)
            sbd[pi] = jnp.concatenate([top, bot], axis=0)

    def tshift(x_ref, mu_ref, c_ref):
        x = x_ref[...]
        row = lax.broadcasted_iota(jnp.int32, x.shape, 0)
        prev = jnp.where(row == 0, c_ref[0:1, :], pltpu.roll(x, 1, 0))
        c_ref[0:1, :] = x[CB - 1:CB, :]
        return x + (prev - x) * mu_ref[...]

    xr = tshift(pr_ref, mur_ref, cr)
    xk = tshift(pk_ref, muk_ref, ck)
    xv = tshift(pv_ref, muv_ref, cv)
    xw = tshift(pw_ref, muw_ref, cw)
    xa = tshift(pa_ref, mua_ref, ca)
    xg = tshift(pg_ref, mug_ref, cg)

    li = lax.broadcasted_iota(jnp.int32, (LANES, LANES), 0)
    lj = lax.broadcasted_iota(jnp.int32, (LANES, LANES), 1)
    same_head = (li < hd) == (lj < hd)
    li2 = lax.broadcasted_iota(jnp.int32, (2 * LANES, LANES), 0)
    lj2 = lax.broadcasted_iota(jnp.int32, (2 * LANES, LANES), 1)
    li2 = jnp.where(li2 < LANES, li2, li2 - LANES)
    seg2 = jnp.where((li2 < hd) == (lj2 < hd), 1.0, 0.0).astype(BF16)

    def segsum(x):
        return _sel_lanes(x, seg2)

    wlin = w0_ref[...] + _mm(jnp.tanh(xw).astype(BF16), w2_ref[...])
    z = -wlin
    w_log = -(jnp.maximum(z, 0.0) + jnp.log1p(jnp.exp(-jnp.abs(z)))) - 0.5
    ne = -jnp.exp(w_log)
    a = jax.nn.sigmoid(a0_ref[...] + _mm(xa.astype(BF16), a2_ref[...]))
    g = _mm(jax.nn.sigmoid(xg).astype(BF16), g2_ref[...])
    kk_raw = xk * kkp_ref[...]
    k2 = xk * (1.0 + (a - 1.0) * ka_ref[...])
    rk_prod = xr * k2 * rk_ref[...]

    ti = lax.broadcasted_iota(jnp.int32, (CB, 2 * CB), 0)
    tj = lax.broadcasted_iota(jnp.int32, (CB, 2 * CB), 1)
    tj = jnp.where(tj < CB, tj, tj - CB)
    in_chunk = [(ti >= q * C) & (ti < (q + 1) * C) & (tj >= q * C) & (tj <= ti) for q in range(n_sub)]
    tri2 = jnp.where(functools.reduce(lambda x, y: x | y, in_chunk), 1.0, 0.0).astype(BF16)
    cum = _sel_rows(tri2, ne)
    gam = jnp.exp(cum)
    gprev = jnp.exp(cum - ne)
    ginv = jnp.exp(-cum)
    cum_end = jnp.concatenate([jnp.broadcast_to(cum[(q + 1) * C - 1:(q + 1) * C, :], (C, cum.shape[1]))
                               for q in range(n_sub)], axis=0)
    dte = jnp.exp(cum_end - cum)
    rg_all = xr * gam
    ks_all = k2 * ginv
    kd_all = k2 * dte
    lnw = lnw_ref[...]
    lnb = lnb_ref[...]

    def stack_masked(x):
        first = lax.broadcasted_iota(jnp.int32, (1, x.shape[1]), 1) < x.shape[1] // 2
        return jnp.concatenate([jnp.where(first, x, 0.0), jnp.where(first, 0.0, x)], axis=0)

    R = 2 * C
    ri = lax.broadcasted_iota(jnp.int32, (C, R), 0)
    ci = lax.broadcasted_iota(jnp.int32, (C, R), 1)
    ci = jnp.where(ci < C, ci, ci - C)
    strict = ci < ri
    incl = ci <= ri
    n_steps = max(1, int(math.log2(C)))

    pairs = range(n_pairs)
    sls = [slice(pi * LANES, (pi + 1) * LANES) for pi in pairs]
    sums_all = [segsum(jnp.concatenate([kk_raw[:, sl] * kk_raw[:, sl], rk_prod[:, sl]], axis=0)) for sl in sls]
    kk_all = [kk_raw[:, sl] * lax.rsqrt(jnp.maximum(sm[:CB, :], 1e-24)) for sl, sm in zip(sls, sums_all)]
    s_raw = [sbd[pi] for pi in pairs]
    for q in range(n_sub):
        s_raw = _wkv_chunk(
            q, s_raw, y_ref, sls, stack_masked, segsum, strict, incl, same_head,
            kk_all, [sm[CB:, :] for sm in sums_all], a, xv, gprev, ginv, dte, gam, rg_all, ks_all, kd_all,
            lnw, lnb, g, C=C, n_steps=n_steps)
    for pi in pairs:
        sbd[pi] = s_raw[pi]

    @pl.when(chunk == n_chunks - 1)
    def _():
        for pi in range(n_pairs):
            s_new = sbd[pi]
            st_ref[2 * pi] = s_new[0:hd, 0:hd]
            st_ref[2 * pi + 1] = s_new[hd:2 * hd, hd:2 * hd]


def _wkv_chunk(q, s_raw, y_ref, sls, stack_masked, segsum, strict, incl, same_head,
               kk_all, rksum_all, a, xv, gprev, ginv, dte, gam, rg_all, ks_all, kd_all, lnw, lnb, g, *, C, n_steps):
    rs = slice(q * C, (q + 1) * C)
    R = 2 * C
    inv_hd = 1.0 / RWKV_HEAD
    pairs = range(len(sls))
    xv_p = [xv[rs, sl] for sl in sls]
    kk = [k[rs, :] for k in kk_all]
    bonus = [sm[rs, :] * v for sm, v in zip(rksum_all, xv_p)]
    kka = [k * a[rs, sl] for k, sl in zip(kk, sls)]
    gprev, ginv, dte, rg_all, ks_all, kd_all = (t[rs, :] for t in (gprev, ginv, dte, rg_all, ks_all, kd_all))
    gam_end = gam[(q + 1) * C - 1:(q + 1) * C, :]
    def both(f, pieces):
        return tuple(f(p_) for p_ in pieces)

    def cat(axis, *pairs_of_pieces):
        return tuple(jnp.concatenate(ps, axis=axis) for ps in zip(*pairs_of_pieces))

    la = [_hi_lo(-k * gprev[:, sl]) for k, sl in zip(kk, sls)]
    lr = [rg_all[:, sl].astype(BF16) for sl in sls]
    rhs = [cat(0, both(stack_masked, _hi_lo(ka_ * ginv[:, sl])), both(stack_masked, _hi_lo(ks_all[:, sl])))
           for ka_, sl in zip(kka, sls)]
    xv_s = [_hi_lo(v) for v in xv_p]
    vst = [both(stack_masked, v) for v in xv_s]
    s_old = [_hi_lo(s_) for s_ in s_raw]
    sc_a = [_dot3(l_, r_, _NT) for l_, r_ in zip(la, rhs)]
    sc_r = [_mm(l_, r_[0], _NT) for l_, r_ in zip(lr, rhs)]
    ls_a = [_dot3(l_, s_, _NT) for l_, s_ in zip(la, s_old)]
    ls_r = [_mm(l_, s_[0], _NT) for l_, s_ in zip(lr, s_old)]
    n_ab = [jnp.where(strict, s4[:, :R], 0.0) for s4 in sc_a]
    m_ak = [jnp.where(strict, s4[:, R:], 0.0) for s4 in sc_a]
    m_r = [jnp.concatenate([jnp.where(incl, s4[:, :R], 0.0), jnp.where(incl, s4[:, R:], 0.0)],
                           axis=1).astype(BF16) for s4 in sc_r]
    x_u = [l_ + _dot3(_hi_lo(mk), v) for l_, mk, v in zip(ls_a, m_ak, vst)]
    pw = n_ab
    for step in range(n_steps):
        ps = [_hi_lo(p_) for p_ in pw]
        xs = [both(stack_masked, _hi_lo(x_)) for x_ in x_u]
        if step + 1 < n_steps:
            t = [_dot3(p_, cat(1, x_, both(stack_masked, p_))) for p_, x_ in zip(ps, xs)]
            x_u = [x_ + t_[:, :LANES] for x_, t_ in zip(x_u, t)]
            pw = [t_[:, LANES:] for t_ in t]
        else:
            x_u = [x_ + _dot3(p_, x_s) for x_, p_, x_s in zip(x_u, ps, xs)]
    xu_s = [_hi_lo(x_) for x_ in x_u]
    y = [l_ + _mm(mr, jnp.concatenate([stack_masked(x_[0]), v[0]], axis=0))
         for l_, mr, x_, v in zip(ls_r, m_r, xu_s, vst)]
    upd = [_dot3(cat(0, x_, v), _hi_lo(jnp.concatenate([ka_ * dte[:, sl], kd_all[:, sl]], axis=0)), _TN)
           for x_, v, ka_, sl in zip(xu_s, xv_s, kka, sls)]
    s_new = [s_raw[pi] * gam_end[:, sls[pi]] + jnp.where(same_head, upd[pi], 0.0) for pi in pairs]

    yc = [y_ - segsum(y_) * inv_hd for y_ in y]
    y_var = [segsum(c_ * c_) * inv_hd for c_ in yc]
    for pi in pairs:
        sl = sls[pi]
        yn = yc[pi] * lax.rsqrt(y_var[pi] + GN_EPS) * lnw[:, sl] + lnb[:, sl]
        y_ref[rs, sl] = ((yn + bonus[pi]) * g[rs, sl]).astype(BF16)
    return s_new


def _wkv(p, l, mu, shift0, w2, a2, g2, w0, a0, kkp, ka, rk, lnw, lnb, state0, n_seq, seq_len, C, n_sub):
    m, n_pad = p.shape
    d_rwkv = w0.shape[-1]
    n_heads = d_rwkv // RWKV_HEAD
    n_pairs = WKV_PAIRS_PER_STEP
    wd = n_pairs * LANES
    n_groups = d_rwkv // wd
    cb = C * n_sub
    n_chunks = seq_len // cb
    nb = n_groups
    blk_w = 3 * d_rwkv // LANES
    blk_a = blk_w + 1
    blk_g = (blk_a + 1) * LANES // GATE_LORA

    row = lambda s, h, c: s * n_chunks + c
    seg_specs = [
        pl.BlockSpec((cb, wd), lambda s, h, c: (row(s, h, c), h)),
        pl.BlockSpec((cb, wd), lambda s, h, c: (row(s, h, c), nb + h)),
        pl.BlockSpec((cb, wd), lambda s, h, c: (row(s, h, c), 2 * nb + h)),
        pl.BlockSpec((cb, LANES), lambda s, h, c: (row(s, h, c), blk_w)),
        pl.BlockSpec((cb, LANES), lambda s, h, c: (row(s, h, c), blk_a)),
        pl.BlockSpec((cb, GATE_LORA), lambda s, h, c: (row(s, h, c), blk_g)),
    ]
    mu_specs = [
        pl.BlockSpec((None, 1, wd), lambda s, h, c: (l, 0, h)),
        pl.BlockSpec((None, 1, wd), lambda s, h, c: (l, 0, nb + h)),
        pl.BlockSpec((None, 1, wd), lambda s, h, c: (l, 0, 2 * nb + h)),
        pl.BlockSpec((None, 1, LANES), lambda s, h, c: (l, 0, blk_w)),
        pl.BlockSpec((None, 1, LANES), lambda s, h, c: (l, 0, blk_a)),
        pl.BlockSpec((None, 1, GATE_LORA), lambda s, h, c: (l, 0, blk_g)),
    ]
    s0_specs = [
        pl.BlockSpec((None, 1, wd), lambda s, h, c: (s, 0, h)),
        pl.BlockSpec((None, 1, wd), lambda s, h, c: (s, 0, nb + h)),
        pl.BlockSpec((None, 1, wd), lambda s, h, c: (s, 0, 2 * nb + h)),
        pl.BlockSpec((None, 1, LANES), lambda s, h, c: (s, 0, blk_w)),
        pl.BlockSpec((None, 1, LANES), lambda s, h, c: (s, 0, blk_a)),
        pl.BlockSpec((None, 1, GATE_LORA), lambda s, h, c: (s, 0, blk_g)),
    ]
    lora_specs = [
        pl.BlockSpec((None, LORA_PAD, wd), lambda s, h, c: (l, 0, h)),
        pl.BlockSpec((None, LORA_PAD, wd), lambda s, h, c: (l, 0, h)),
        pl.BlockSpec((None, GATE_LORA, wd), lambda s, h, c: (l, 0, h)),
    ]
    vec_spec = pl.BlockSpec((None, 1, wd), lambda s, h, c: (l, 0, h))
    state_spec = pl.BlockSpec((None, 2 * n_pairs, RWKV_HEAD, RWKV_HEAD), lambda s, h, c: (s, h, 0, 0))
    ls = min(l, state0.shape[0] - 1)
    state0_spec = pl.BlockSpec((None, None, 2 * n_pairs, RWKV_HEAD, RWKV_HEAD), lambda s, h, c: (ls, s, h, 0, 0))
    y, st = pl.pallas_call(
        functools.partial(_wkv_kernel, C=C, n_sub=n_sub, n_chunks=n_chunks, n_pairs=n_pairs),
        grid=(n_seq, n_groups, n_chunks),
        in_specs=seg_specs + mu_specs + s0_specs + lora_specs + [vec_spec] * 7 + [state0_spec],
        out_specs=[
            pl.BlockSpec((cb, wd), lambda s, h, c: (row(s, h, c), h)),
            state_spec,
        ],
        out_shape=[
            jax.ShapeDtypeStruct((m, d_rwkv), BF16),
            jax.ShapeDtypeStruct((n_seq, n_heads, RWKV_HEAD, RWKV_HEAD), F32),
        ],
        scratch_shapes=[pltpu.VMEM((n_pairs, LANES, LANES), F32)]
        + [pltpu.VMEM((SUBLANES, wd), F32)] * 3 + [pltpu.VMEM((SUBLANES, LANES), F32)] * 2
        + [pltpu.VMEM((SUBLANES, GATE_LORA), F32)],
        compiler_params=_cparams(("arbitrary", "arbitrary", "arbitrary")),
        name="wkv",
    )(p, p, p, p, p, p, mu, mu, mu, mu, mu, mu,
      shift0, shift0, shift0, shift0, shift0, shift0,
      w2, a2, g2, w0, a0, kkp, ka, rk, lnw, lnb, state0)
    return y, st


def _lambda(lqk_ref, lam_init):
    t = lqk_ref[...]
    s1 = jnp.sum(t[0:1, :] * t[1:2, :], axis=-1, keepdims=True)
    s2 = jnp.sum(t[2:3, :] * t[3:4, :], axis=-1, keepdims=True)
    return jnp.exp(s1) - jnp.exp(s2) + lam_init


def _split_maps(q):
    lane = lax.broadcasted_iota(jnp.int32, q.shape, 1)
    zero = jnp.zeros_like(q)
    return jnp.where(lane < DIFF_HEAD, q, zero), jnp.where(lane < DIFF_HEAD, zero, q)


def _subln(o, subln, lam_init):
    return _rms(o, subln, SUBLN_EPS) * (1.0 - lam_init)


FAR_STEP_TILES = (4, 2, 1)
ATTN_TILES_PER_STEP = 4


def _attn_kernel(q_ref, k_ref, v_ref, bias_ref, lqk_ref, subln_ref, o_ref,
                 kb, vt, m_s, a_s, *, TB, n_cast, lam_init):
    g = pl.program_id(1)

    @pl.when(g == 0)
    def _():
        vt[LANES:, :] = jnp.ones((vt.shape[0] - LANES, vt.shape[1]), BF16)

        def cast(c, carry):
            off = pl.multiple_of(c * TB, TB)
            kb[pl.ds(off, TB), :] = k_ref[pl.ds(off, TB), :].astype(BF16)
            vt[0:LANES, pl.ds(off, TB)] = v_ref[pl.ds(off, TB), :].T.astype(BF16)
            return carry
        lax.fori_loop(0, n_cast, cast, 0)

    nq = ATTN_TILES_PER_STEP
    chains = range(nq)
    qs = [jnp.concatenate(_split_maps(q_ref[c * TB:(c + 1) * TB, :]), axis=0) for c in chains]
    m_s[...] = jnp.full(m_s.shape, MASK_VALUE, F32)
    a_s[...] = jnp.zeros(a_s.shape, F32)

    def scores(c, off, width):
        return _mm(kb[pl.ds(off, width), :], qs[c], _NT)

    def update(c, s, off, width):
        m_prev = m_s[c]
        m_new = jnp.maximum(m_prev, jnp.max(s, axis=0, keepdims=True))
        alpha = jnp.exp(m_prev - m_new)
        p = jnp.exp(s - m_new).astype(BF16)
        a_s[c] = alpha * a_s[c] + _mm(vt[:, pl.ds(off, width)], p)
        m_s[c] = m_new

    def far_all(off, width):
        s = [scores(c, off, width) for c in chains]
        for c in chains:
            update(c, s[c], off, width)

    n_common = jnp.maximum(nq * g - 1, 0)
    done = 0
    for step_tiles in FAR_STEP_TILES:
        width = TB * step_tiles
        n_steps = (n_common - done) // step_tiles

        def far_step(j, carry, done=done, width=width):
            far_all(pl.multiple_of(done * TB + j * width, TB), width)
            return carry
        lax.fori_loop(0, n_steps, far_step, 0)
        done = done + n_steps * step_tiles

    def own_far(first_chain, off0):
        for k in range(nq - first_chain):
            off = pl.multiple_of(off0 + k * TB, TB)
            cs = range(first_chain + k, nq)
            s = [scores(c, off, TB) for c in cs]
            for c, s_ in zip(cs, s):
                update(c, s_, off, TB)

    @pl.when(g > 0)
    def _():
        own_far(1, n_common * TB)

    @pl.when(g == 0)
    def _():
        own_far(2, 0)

    offs = [pl.multiple_of(jnp.maximum(nq * g + c - 1, 0) * TB, TB) for c in chains]
    s_near = [scores(c, offs[c], 2 * TB) + (bias_ref[jnp.where(g == 0, 1, 0)] if c == 0 else bias_ref[0])
              for c in chains]
    for c in chains:
        update(c, s_near[c], offs[c], 2 * TB)

    lam = _lambda(lqk_ref, lam_init)
    for c in chains:
        acc = a_s[c]
        on = acc[0:LANES, :] / acc[LANES:LANES + 1, :]
        o = (on[:, :TB] - lam * on[:, TB:]).T
        o_ref[c * TB:(c + 1) * TB, :] = _subln(o, subln_ref[...], lam_init).astype(BF16)


def _attn_prompt(q, k_all, v_all, l, bias, lqk, subln, lam_init, TB):
    m, d_diff = q.shape
    n_heads = d_diff // LANES
    n_blk = m // TB
    nq = ATTN_TILES_PER_STEP
    assert n_blk % nq == 0
    return pl.pallas_call(
        functools.partial(_attn_kernel, TB=TB, n_cast=n_blk, lam_init=lam_init),
        grid=(n_heads, n_blk // nq),
        in_specs=[
            pl.BlockSpec((nq * TB, LANES), lambda h, g: (g, h)),
            pl.BlockSpec((None, m, LANES), lambda h, g: (l, 0, h)),
            pl.BlockSpec((None, m, LANES), lambda h, g: (l, 0, h)),
            pl.BlockSpec((None, 2, 2 * TB, 2 * TB), lambda h, g: (h, 0, 0, 0)),
            pl.BlockSpec((None, 4, DIFF_HEAD), lambda h, g: (l, 0, 0)),
            pl.BlockSpec((None, 1, LANES), lambda h, g: (l, 0, 0)),
        ],
        out_specs=pl.BlockSpec((nq * TB, LANES), lambda h, g: (g, h)),
        out_shape=jax.ShapeDtypeStruct((m, d_diff), BF16),
        scratch_shapes=[pltpu.VMEM((m, LANES), BF16), pltpu.VMEM((LANES + BF16_ROWS, m), BF16),
                        pltpu.VMEM((nq, 1, 2 * TB), F32), pltpu.VMEM((nq, LANES + BF16_ROWS, 2 * TB), F32)],
        compiler_params=_cparams(("arbitrary", "arbitrary")),
        name="attn_prompt",
    )(q, k_all, v_all, bias, lqk, subln)


SAMPLE_FAR_FRAMES = 384


def _attn_sample_kernel(q_ref, kn_ref, vn_ref, kc_ref, vc_ref, bias_ref, lqk_ref, subln_ref, o_ref,
                        kb, acc, *, T, n_heads, near, lam_init):
    past = kc_ref.shape[0]
    n_cols = 2 * T * n_heads
    fc = SAMPLE_FAR_FRAMES
    n_far = (past - near) // fc
    rows = fc * n_heads
    qs = jnp.concatenate(
        [m_ for h in range(n_heads) for m_ in _split_maps(q_ref[:, h * LANES:(h + 1) * LANES])], axis=0)
    sub = lax.broadcasted_iota(jnp.int32, (n_heads, n_cols), 0)
    col = lax.broadcasted_iota(jnp.int32, (n_heads, n_cols), 1)
    valid = (col >= sub * (2 * T)) & (col < (sub + 1) * (2 * T))

    def flat(x3):
        return x3.reshape(x3.shape[0] * n_heads, LANES)

    def scores(k_rows):
        s = _mm(k_rows, qs, _NT)
        return s.reshape(s.shape[0] // n_heads, n_heads, n_cols)

    near_k = jnp.concatenate([flat(kc_ref[past - near:past]), flat(kn_ref[...])], axis=0).astype(BF16)
    near_v = jnp.concatenate([flat(vc_ref[past - near:past]), flat(vn_ref[...])], axis=0).astype(BF16)
    s_near = scores(near_k) + bias_ref[...].reshape(near + T, n_heads, n_cols)

    def pass1(c, m3):
        f0 = pl.multiple_of(c * fc, fc)
        k_rows = flat(kc_ref[pl.ds(f0, fc)]).astype(BF16)
        kb[pl.ds(pl.multiple_of(c * rows, rows), rows), :] = k_rows
        return jnp.maximum(m3, jnp.max(scores(k_rows), axis=0))
    m3 = lax.fori_loop(0, n_far, pass1, jnp.max(s_near, axis=0))

    ones = jnp.ones((rows, LANES), BF16)

    def weights(s3):
        p = jnp.where(valid, jnp.exp(s3 - m3), 0.0)
        return p.reshape(p.shape[0] * n_heads, n_cols).astype(BF16)

    acc[...] = _mm(weights(s_near), jnp.concatenate([near_v, ones[:near_v.shape[0]]], axis=1), _TN)

    def pass2(c, carry):
        f0 = pl.multiple_of(c * fc, fc)
        p = weights(scores(kb[pl.ds(pl.multiple_of(c * rows, rows), rows), :]))
        v_aug = jnp.concatenate([flat(vc_ref[pl.ds(f0, fc)]).astype(BF16), ones], axis=1)
        acc[...] += _mm(p, v_aug, _TN)
        return carry
    lax.fori_loop(0, n_far, pass2, 0)

    lam = _lambda(lqk_ref, lam_init)
    for h in range(n_heads):
        a = acc[h * 2 * T:(h + 1) * 2 * T, :]
        on = a[:, 0:LANES] / a[:, LANES:2 * LANES]
        o = on[:T, :] - lam * on[T:, :]
        o_ref[:, h * LANES:(h + 1) * LANES] = _subln(o, subln_ref[...], lam_init).astype(BF16)


def _attn_sample(q, kn_all, vn_all, l, k_cache, v_cache, bias_t, lqk, subln, lam_init, n_seq, T, near):
    m, d_diff = q.shape
    n_heads = d_diff // LANES
    past = k_cache.shape[2]
    assert (past - near) % SAMPLE_FAR_FRAMES == 0
    cache_spec = pl.BlockSpec((None, None, past, n_heads, LANES), lambda b: (l, b, 0, 0, 0))
    new_spec = pl.BlockSpec((None, T, n_heads, LANES), lambda b: (l, b, 0, 0))
    return pl.pallas_call(
        functools.partial(_attn_sample_kernel, T=T, n_heads=n_heads, near=near, lam_init=lam_init),
        grid=(n_seq,),
        in_specs=[
            pl.BlockSpec((T, d_diff), lambda b: (b, 0)),
            new_spec,
            new_spec,
            cache_spec,
            cache_spec,
            pl.BlockSpec(bias_t.shape, lambda b: (0, 0)),
            pl.BlockSpec((None, 4, DIFF_HEAD), lambda b: (l, 0, 0)),
            pl.BlockSpec((None, 1, LANES), lambda b: (l, 0, 0)),
        ],
        out_specs=pl.BlockSpec((T, d_diff), lambda b: (b, 0)),
        out_shape=jax.ShapeDtypeStruct((m, d_diff), BF16),
        scratch_shapes=[pltpu.VMEM(((past - near) * n_heads, LANES), BF16),
                        pltpu.VMEM((2 * T * n_heads, 2 * LANES), F32)],
        compiler_params=_cparams(("arbitrary",)),
        name="attn_sample",
    )(q, kn_all, vn_all, k_cache, v_cache, bias_t, lqk, subln)


def _outproj_kernel(x_ref, yr_ref, yd_ref, gt_ref, wt_ref, wb_ref, o_ref):
    mixed = _mm(yr_ref[...], wt_ref[...]) + _mm(yd_ref[...], wb_ref[...])
    o_ref[...] = x_ref[...] + gt_ref[...] * mixed


def _outproj(x, yr, yd, gt, w_out_all, l, tm):
    m, d = x.shape
    dr = yr.shape[1]
    dd = yd.shape[1]
    gt_spec = (pl.BlockSpec((1, d), lambda i: (0, 0)) if gt.shape[0] == 1
               else pl.BlockSpec((tm, d), lambda i: (i, 0)))
    return pl.pallas_call(
        _outproj_kernel,
        grid=(m // tm,),
        in_specs=[
            pl.BlockSpec((tm, d), lambda i: (i, 0)),
            pl.BlockSpec((tm, dr), lambda i: (i, 0)),
            pl.BlockSpec((tm, dd), lambda i: (i, 0)),
            gt_spec,
            pl.BlockSpec((None, dr, d), lambda i: (l, 0, 0)),
            pl.BlockSpec((None, dd, d), lambda i: (l, dr // dd, 0)),
        ],
        out_specs=pl.BlockSpec((tm, d), lambda i: (i, 0)),
        out_shape=jax.ShapeDtypeStruct((m, d), F32),
        compiler_params=_cparams(("arbitrary",)),
        name="outproj",
    )(x, yr, yd, gt, w_out_all, w_out_all)


def _shift_rows(x, prev, n):
    if n % SUBLANES == 0:
        return jnp.concatenate([prev, x[:x.shape[0] - n, :]], axis=0)
    rolled = pltpu.roll(x, n, 0)
    row = lax.broadcasted_iota(jnp.int32, x.shape, 0)
    out = rolled
    for r in range(n):
        out = jnp.where(row == r, prev[r:r + 1, :], out)
    return out


def _ffn_kernel(x_ref, sh_ref, sc_ref, gt_ref, g_ref, wg_ref, wv_ref, cw_ref, cb_ref, c0_ref, wd_ref, gf_ref,
                o_ref, co_ref, h_scr, acc_scr, carry, *, B, n_j, final_norm):
    i = pl.program_id(0)
    j = pl.program_id(1)
    tm = x_ref.shape[0]

    @pl.when(j == 0)
    def _():
        xn = _rms(x_ref[...], g_ref[...], NORM_EPS)
        h_scr[...] = (xn * (1.0 + sc_ref[...]) + sh_ref[...]).astype(BF16)
        acc_scr[...] = jnp.zeros(acc_scr.shape, F32)

    @pl.when(i == 0)
    def _():
        carry[j, 0:2 * B, :] = c0_ref[...]

    h = h_scr[...]
    ug = _mm(h, wg_ref[...])
    uv = _mm(h, wv_ref[...])
    prev = carry[j, 0:2 * B, :]
    s1 = _shift_rows(ug, prev[B:2 * B, :], B)
    s2 = _shift_rows(ug, prev, 2 * B)
    cw = cw_ref[...]
    z = cb_ref[...] + s2 * cw[0:1, :] + s1 * cw[1:2, :] + ug * cw[2:3, :]
    act = 0.5 * z * (1.0 + lax.erf(z * (2.0 ** -0.5)))
    acc_scr[...] += _mm((act * uv).astype(BF16), wd_ref[...])
    last2 = ug[tm - 2 * B:tm, :]
    carry[j, 0:2 * B, :] = last2
    tf = last2.shape[1]
    co_ref[:, pl.ds(pl.multiple_of(j * tf, tf), tf)] = last2

    @pl.when(j == n_j - 1)
    def _():
        x2 = x_ref[...] + gt_ref[...] * acc_scr[...]
        if final_norm:
            x2 = _rms(x2, gf_ref[...], NORM_EPS)
        o_ref[...] = x2


def _ffn(x, sh, sc, gt, g_all, w_up_all, conv_w_all, conv_b_all, conv0, w_down_all, g_final, l, B, tm,
         final_norm):
    m, d = x.shape
    d_ff = w_down_all.shape[1]
    tf = w_up_all.shape[-1]
    n_j = d_ff // tf
    rows_c = max(SUBLANES, 2 * B)
    return pl.pallas_call(
        functools.partial(_ffn_kernel, B=B, n_j=n_j, final_norm=final_norm),
        grid=(m // tm, n_j),
        in_specs=[
            pl.BlockSpec((tm, d), lambda i, j: (i, 0)),
            _mod_spec(sh, tm, d),
            _mod_spec(sc, tm, d),
            _mod_spec(gt, tm, d),
            pl.BlockSpec((None, 1, d), lambda i, j: (l, 0, 0)),
            pl.BlockSpec((None, None, d, tf), lambda i, j: (l, j, 0, 0)),
            pl.BlockSpec((None, None, d, tf), lambda i, j: (l, n_j + j, 0, 0)),
            pl.BlockSpec((None, CONV_W, tf), lambda i, j: (l, 0, j)),
            pl.BlockSpec((None, 1, tf), lambda i, j: (l, 0, j)),
            pl.BlockSpec((2 * B, tf), lambda i, j: (0, j)),
            pl.BlockSpec((None, tf, d), lambda i, j: (l, j, 0)),
            pl.BlockSpec((1, d), lambda i, j: (0, 0)),
        ],
        out_specs=[
            pl.BlockSpec((tm, d), lambda i, j: (i, 0)),
            pl.BlockSpec((2 * B, d_ff), lambda i, j: (0, 0)),
        ],
        out_shape=[
            jax.ShapeDtypeStruct((m, d), F32),
            jax.ShapeDtypeStruct((2 * B, d_ff), F32),
        ],
        scratch_shapes=[pltpu.VMEM((tm, d), BF16), pltpu.VMEM((tm, d), F32),
                        pltpu.VMEM((n_j, rows_c, tf), F32)],
        compiler_params=_cparams(("arbitrary", "arbitrary")),
        name="conv_ffn",
    )(x, sh, sc, gt, g_all, w_up_all, w_up_all, conv_w_all, conv_b_all, conv0, w_down_all, g_final)


def _pad_cols(a, n):
    return jnp.pad(a, [(0, 0)] * (a.ndim - 1) + [(0, n)])


def _regroup_rwkv_cols(a, d_rwkv):
    o = 3 * d_rwkv
    return jnp.concatenate([
        a[..., :o],
        _pad_cols(a[..., o:o + DECAY_LORA], LORA_PAD - DECAY_LORA),
        _pad_cols(a[..., o + DECAY_LORA:o + DECAY_LORA + AAA_LORA], LORA_PAD - AAA_LORA),
        a[..., o + DECAY_LORA + AAA_LORA:],
    ], axis=-1)


def _ungroup_rwkv_cols(a, d_rwkv):
    o = 3 * d_rwkv
    return jnp.concatenate([
        a[..., :o],
        a[..., o:o + DECAY_LORA],
        a[..., o + LORA_PAD:o + LORA_PAD + AAA_LORA],
        a[..., o + 2 * LORA_PAD:],
    ], axis=-1)


def _prompt_tiles(seq):
    attn_tile = min(256, seq // 2)
    assert attn_tile % CHUNK == 0 and attn_tile >= MAX_DISTANCE
    assert seq % (ATTN_TILES_PER_STEP * attn_tile) == 0
    return attn_tile, min(512, seq), min(1024, seq), min(64, seq)


def kernel(x_prompt, x_sample, c_prompt, c_sample, cache_k, cache_v, state_wkv, state_shift, state_conv, w_ada, b_ada, g_mix, g_ffn, w_in, w_out, rwkv_mu, rwkv_w0, rwkv_w2, rwkv_a0, rwkv_a2, rwkv_g2, rwkv_kk, rwkv_ka, rwkv_rk, rwkv_ln_w, rwkv_ln_b, diff_lq1, diff_lk1, diff_lq2, diff_lk2, diff_subln, rel_table, ffn_up, ffn_conv_w, ffn_conv_b, ffn_down, g_final):
    depth, d_model, _ = w_in.shape
    bp, seq, _ = x_prompt.shape
    bs, dseq, _ = x_sample.shape
    past = cache_k.shape[2]
    d_rwkv = rwkv_w0.shape[1]
    d_diff = d_model - d_rwkv
    n_dheads = d_diff // (2 * DIFF_HEAD)
    n_rheads = d_rwkv // RWKV_HEAD
    n_rwkv_cols = rwkv_mu.shape[1]
    n_rwkv_pad = n_rwkv_cols + 2 * LORA_PAD - DECAY_LORA - AAA_LORA
    d_ff = ffn_down.shape[1]
    assert bp == 1, "prompt path handles one sequence"
    TB, tm_p, tm_in, wkv_chunk = _prompt_tiles(seq)
    m_s = bs * dseq

    w_in_b = jnp.concatenate([_regroup_rwkv_cols(w_in[..., :n_rwkv_cols], d_rwkv), w_in[..., n_rwkv_cols:]],
                             axis=-1).astype(BF16)
    w_in_b = w_in_b.reshape(depth, d_model, -1, WEIGHT_COL_TILE).swapaxes(1, 2)
    mu_p = _regroup_rwkv_cols(rwkv_mu, d_rwkv)[:, None]
    w2_b = jnp.pad(rwkv_w2, ((0, 0), (0, LORA_PAD - DECAY_LORA), (0, 0))).astype(BF16)
    a2_b = jnp.pad(rwkv_a2, ((0, 0), (0, LORA_PAD - AAA_LORA), (0, 0))).astype(BF16)
    g2_b = rwkv_g2.astype(BF16)
    w_out_b = _cast_bf16(w_out)
    up_b = _cast_bf16(ffn_up, col_tile=WEIGHT_COL_TILE)
    down_b = _cast_bf16(ffn_down)
    vec = lambda a: a.reshape(depth, 1, -1)
    wkv_args = (w2_b, a2_b, g2_b, vec(rwkv_w0), vec(rwkv_a0), vec(rwkv_kk), vec(rwkv_ka), vec(rwkv_rk),
                vec(rwkv_ln_w), vec(rwkv_ln_b))
    g_mix_v, g_ffn_v, subln_v, conv_b_v = vec(g_mix), vec(g_ffn), vec(diff_subln), vec(ffn_conv_b)
    gf = g_final[None]
    lqk = jnp.stack([diff_lq1, diff_lk1, diff_lq2, diff_lk2], axis=1)

    n_c = bp + bs
    n_c_pad = -(-n_c // SUBLANES) * SUBLANES
    c_rows = jnp.pad(jnp.concatenate([c_prompt, c_sample], axis=0), ((0, n_c_pad - n_c), (0, 0)))
    mod = _adaln_mod(c_rows, w_ada, b_ada).reshape(depth, n_c_pad, 6, d_model)

    ql = jnp.arange(TB)
    rel_diag = ql[None, :] - ql[:, None]
    allowed = (ql[None, :] // CHUNK) <= (ql[:, None] // CHUNK)
    bucket_diag = jnp.where(allowed, _t5_bucket(rel_diag), -1)
    bucket_prev = _t5_bucket(rel_diag - TB)
    windows = [jnp.concatenate([bucket_prev, bucket_diag], axis=1),
               jnp.concatenate([bucket_diag, jnp.full((TB, TB), -1)], axis=1)]
    bucket_p = jnp.concatenate([w.T for w in windows], axis=0)
    bias_p = _bias_lookup(rel_table, bucket_p.astype(jnp.int32), FAR_BUCKET, col_repeats=2)
    bias_p = bias_p.reshape(n_dheads, 2, 2 * TB, 2 * TB)
    q_pos = past + jnp.arange(dseq)
    assert past >= MAX_DISTANCE
    k_pos = past - MAX_DISTANCE + jnp.arange(MAX_DISTANCE + dseq)
    bias_s = _bias_lookup(rel_table, _t5_bucket(k_pos[None, :] - q_pos[:, None]).astype(jnp.int32), FAR_BUCKET)
    bias_t = jnp.broadcast_to(bias_s.transpose(2, 0, 1)[:, None, :, None, :],
                              (MAX_DISTANCE + dseq, n_dheads, n_dheads, 2, dseq))
    bias_t = bias_t.reshape((MAX_DISTANCE + dseq) * n_dheads, n_dheads * 2 * dseq)

    xp = x_prompt.reshape(seq, d_model)
    xs = x_sample.reshape(m_s, d_model)
    zero_shift = jnp.zeros((1, 1, n_rwkv_pad), F32)
    zero_state = jnp.zeros((1, 1, n_rheads, RWKV_HEAD, RWKV_HEAD), F32)
    zero_conv = jnp.zeros((2, d_ff), F32)
    shift_s_in = _regroup_rwkv_cols(state_shift, d_rwkv)
    outs_p = {k: [] for k in ("wkv", "shift", "conv")}
    outs_s = {k: [] for k in ("wkv", "shift", "conv")}
    kv_p = (jnp.zeros((depth, seq, d_diff), F32), jnp.zeros((depth, seq, d_diff), F32))
    kv_s = (jnp.zeros((depth, m_s, d_diff), F32), jnp.zeros((depth, m_s, d_diff), F32))

    def time_major(a):
        return a.reshape(bs, dseq, -1).swapaxes(0, 1).reshape(m_s, -1)

    def batch_major(a):
        return a.reshape(dseq, bs, -1).swapaxes(0, 1).reshape(m_s, -1)

    for l in range(depth):
        lam_init = 0.8 - 0.6 * math.exp(-0.3 * l)
        last = l == depth - 1

        mp = mod[l, 0:bp]
        sh1, sc1, gt1, sh2, sc2, gt2 = (mp[:, t] for t in range(6))
        p, q, *kv_p = _inproj(xp, sh1, sc1, g_mix_v, w_in_b, l, kv_p, n_rwkv_pad, d_diff, tm_in)
        y_r, wkv = _wkv(p, l, mu_p, zero_shift, *wkv_args, zero_state, 1, seq, wkv_chunk,
                        WKV_CHUNKS_PER_STEP if seq % (WKV_CHUNKS_PER_STEP * wkv_chunk) == 0 else 1)
        y_d = _attn_prompt(q, kv_p[0], kv_p[1], l, bias_p, lqk, subln_v, lam_init, TB)
        x1 = _outproj(xp, y_r, y_d, gt1, w_out_b, l, tm_p)
        xp, conv = _ffn(x1, sh2, sc2, gt2, g_ffn_v, up_b, ffn_conv_w, conv_b_v, zero_conv, down_b, gf, l,
                        1, tm_p, last)
        outs_p["wkv"].append(wkv)
        outs_p["shift"].append(_ungroup_rwkv_cols(p[seq - 1:seq], d_rwkv).reshape(bp, 1, n_rwkv_cols))
        outs_p["conv"].append(conv.reshape(bp, CONV_W - 1, d_ff))

        ms = mod[l, bp:bp + bs]
        rows_bm = jnp.repeat(ms, dseq, axis=0)
        rows_tm = jnp.tile(ms, (dseq, 1, 1))
        p, q, *kv_s = _inproj(xs, rows_bm[:, 0], rows_bm[:, 1], g_mix_v, w_in_b, l, kv_s, n_rwkv_pad, d_diff, m_s)
        y_r, wkv = _wkv(p, l, mu_p, shift_s_in[l], *wkv_args, state_wkv, bs, dseq, dseq, 1)
        kn4, vn4 = (a.reshape(depth, m_s, n_dheads, 2 * DIFF_HEAD) for a in kv_s)
        y_d = _attn_sample(q, kn4, vn4, l, cache_k, cache_v, bias_t, lqk, subln_v, lam_init, bs, dseq,
                           MAX_DISTANCE)
        x1 = _outproj(xs, y_r, y_d, rows_bm[:, 2], w_out_b, l, m_s)
        conv0 = state_conv[l].swapaxes(0, 1).reshape((CONV_W - 1) * bs, d_ff)
        x2, conv = _ffn(time_major(x1), rows_tm[:, 3], rows_tm[:, 4], rows_tm[:, 5], g_ffn_v, up_b, ffn_conv_w,
                        conv_b_v, conv0, down_b, gf, l, bs, m_s, last)
        xs = batch_major(x2)
        outs_s["wkv"].append(wkv)
        p_last = p.reshape(bs, dseq, n_rwkv_pad)[:, dseq - 1:dseq]
        outs_s["shift"].append(_ungroup_rwkv_cols(p_last, d_rwkv))
        outs_s["conv"].append(conv.reshape(CONV_W - 1, bs, d_ff).swapaxes(0, 1))

    st = lambda xs_: jnp.stack(xs_)
    head_shape = (n_dheads, 2 * DIFF_HEAD)
    return (xp.reshape(bp, seq, d_model), xs.reshape(bs, dseq, d_model),
            kv_p[0].reshape(depth, bp, seq, *head_shape), kv_p[1].reshape(depth, bp, seq, *head_shape),
            st(outs_p["wkv"]), st(outs_p["shift"]), st(outs_p["conv"]),
            kv_s[0].reshape(depth, bs, dseq, *head_shape), kv_s[1].reshape(depth, bs, dseq, *head_shape),
            st(outs_s["wkv"]), st(outs_s["shift"]), st(outs_s["conv"]))
```

```python
import functools
import math

import jax
import jax.numpy as jnp
from jax import lax
from jax.experimental import pallas as pl
from jax.experimental.pallas import tpu as pltpu

F32 = jnp.float32
BF16 = jnp.bfloat16

CHUNK = 64
RWKV_HEAD = 64
DIFF_HEAD = 64
N_BUCKETS = 32
MAX_DISTANCE = 128
NORM_EPS = 1e-6
GN_EPS = 64e-5
SUBLN_EPS = 1e-5
DECAY_LORA = 96
AAA_LORA = 96
GATE_LORA = 256
LORA_PAD = 128
CONV_W = 3

LANES = 128
SUBLANES = 8
BF16_ROWS = 16
VMEM_LIMIT_BYTES = 56 * 1024 * 1024
MASK_VALUE = -1e30
WKV_CHUNKS_PER_STEP = 4
WKV_PAIRS_PER_STEP = 8

_NN = (((1,), (0,)), ((), ()))
_NT = (((1,), (1,)), ((), ()))
_TN = (((0,), (0,)), ((), ()))


def _mm(a, b, dims=_NN):
    return lax.dot_general(a, b, dims, preferred_element_type=F32)


def _hi_lo(x):
    h = x.astype(BF16)
    return h, (x - h.astype(F32)).astype(BF16)


def _dot3(a, b, dims=_NN):
    (ah, al), (bh, bl) = a, b
    (ca,), (cb,) = dims[0]
    k = ah.shape[ca]
    if (ca == 0 or k % LANES == 0) and (cb == 0 or k % LANES == 0):
        return _mm(jnp.concatenate([ah, al, ah], axis=ca), jnp.concatenate([bh, bh, bl], axis=cb), dims)
    return _mm(ah, bh, dims) + (_mm(al, bh, dims) + _mm(ah, bl, dims))


def _mm3(a, b, dims=_NN):
    return _dot3(_hi_lo(a), _hi_lo(b), dims)


def _sel_rows(sel2, x):
    xh, xl = _hi_lo(x)
    return _mm(sel2, jnp.concatenate([xh, xl], axis=0))


def _sel_lanes(x, sel2):
    xh, xl = _hi_lo(x)
    return _mm(jnp.concatenate([xh, xl], axis=1), sel2)


def _cparams(sem):
    return pltpu.CompilerParams(dimension_semantics=sem, vmem_limit_bytes=VMEM_LIMIT_BYTES)


def _rms(x, g, eps):
    return x * lax.rsqrt(jnp.mean(x * x, axis=-1, keepdims=True) + eps) * g


def _cast_kernel(x_ref, o_ref):
    o_ref[...] = x_ref[...].astype(BF16)


WEIGHT_COL_TILE = 512
CAST_BLOCK_BYTES = 8 * 1024 * 1024


def _cast_bf16(w, col_tile=None):
    depth, rows, cols = w.shape
    tc = col_tile or (1024 if cols % 1024 == 0 else cols)
    tr = max(t for t in range(BF16_ROWS, rows + 1, BF16_ROWS)
             if rows % t == 0 and t * tc * 4 <= CAST_BLOCK_BYTES)
    spec = pl.BlockSpec((None, tr, tc), lambda l, i, j: (l, i, j))
    if col_tile:
        out_spec = pl.BlockSpec((None, None, tr, tc), lambda l, i, j: (l, j, i, 0))
        out_shape = (depth, cols // tc, rows, tc)
    else:
        out_spec, out_shape = spec, w.shape
    return pl.pallas_call(
        _cast_kernel,
        grid=(depth, rows // tr, cols // tc),
        in_specs=[spec],
        out_specs=out_spec,
        out_shape=jax.ShapeDtypeStruct(out_shape, BF16),
        compiler_params=_cparams(("arbitrary", "arbitrary", "arbitrary")),
        name="cast_bf16",
    )(w)


def _mod_kernel(c_ref, w_ref, b_ref, o_ref):
    c = c_ref[...]
    s = (c * jax.nn.sigmoid(c)).astype(BF16)
    o_ref[...] = _mm(s, w_ref[...].astype(BF16)) + b_ref[...]


def _adaln_mod(c_rows, w_ada, b_ada, tn=1024):
    depth, d, n = w_ada.shape
    rows = c_rows.shape[0]
    return pl.pallas_call(
        _mod_kernel,
        grid=(depth, n // tn),
        in_specs=[
            pl.BlockSpec((rows, d), lambda l, j: (0, 0)),
            pl.BlockSpec((None, d, tn), lambda l, j: (l, 0, j)),
            pl.BlockSpec((None, 1, tn), lambda l, j: (l, 0, j)),
        ],
        out_specs=pl.BlockSpec((None, rows, tn), lambda l, j: (l, 0, j)),
        out_shape=jax.ShapeDtypeStruct((depth, rows, n), F32),
        compiler_params=_cparams(("arbitrary", "arbitrary")),
        name="adaln_mod",
    )(c_rows, w_ada, b_ada.reshape(depth, 1, n))


def _bias_kernel(tab_ref, bucket_ref, o_ref, *, far_bucket):
    h = pl.program_id(0)
    bucket = bucket_ref[...]
    far = tab_ref[far_bucket, h]
    acc = jnp.full(bucket.shape, MASK_VALUE, F32)
    for b in range(N_BUCKETS):
        acc = jnp.where(bucket == b, tab_ref[b, h] - far, acc)
    c = bucket.shape[1]
    for k in range(o_ref.shape[1] // c):
        o_ref[:, k * c:(k + 1) * c] = acc


def _bias_lookup(table, bucket, far_bucket, col_repeats=1):
    n_heads = table.shape[1]
    r, c = bucket.shape
    return pl.pallas_call(
        functools.partial(_bias_kernel, far_bucket=far_bucket),
        grid=(n_heads,),
        in_specs=[
            pl.BlockSpec(memory_space=pltpu.SMEM),
            pl.BlockSpec((r, c), lambda h: (0, 0)),
        ],
        out_specs=pl.BlockSpec((None, r, c * col_repeats), lambda h: (h, 0, 0)),
        out_shape=jax.ShapeDtypeStruct((n_heads, r, c * col_repeats), F32),
        compiler_params=_cparams(("arbitrary",)),
        name="bias_lookup",
    )(table, bucket)


def _t5_bucket(rel):
    nb = N_BUCKETS // 2
    max_exact = nb // 2
    n = jnp.abs(rel)
    nf = jnp.maximum(n, 1).astype(F32)
    large = max_exact + (jnp.log(nf / max_exact) / math.log(MAX_DISTANCE / max_exact) * (nb - max_exact)).astype(jnp.int32)
    large = jnp.minimum(large, nb - 1)
    return jnp.where(rel > 0, nb, 0) + jnp.where(n < max_exact, n, large)


FAR_BUCKET = N_BUCKETS // 2 - 1


def _inproj_kernel(x_ref, sh_ref, sc_ref, g_ref, w_ref, *rest, n_p, n_q):
    p_ref, q_ref, k_ref, v_ref, h_scr = rest[-5:]
    j = pl.program_id(1)

    @pl.when(j == 0)
    def _():
        xn = _rms(x_ref[...], g_ref[...], NORM_EPS)
        h_scr[...] = (xn * (1.0 + sc_ref[...]) + sh_ref[...]).astype(BF16)

    @pl.when(j < n_p)
    def _():
        p_ref[...] = _mm(h_scr[...], w_ref[...])

    @pl.when((j >= n_p) & (j < n_p + n_q))
    def _():
        q_ref[...] = (_mm(h_scr[...], w_ref[...]) * (DIFF_HEAD ** -0.5)).astype(BF16)

    @pl.when((j >= n_p + n_q) & (j < n_p + 2 * n_q))
    def _():
        k_ref[...] = _mm(h_scr[...], w_ref[...])

    @pl.when(j >= n_p + 2 * n_q)
    def _():
        v_ref[...] = _mm(h_scr[...], w_ref[...])


def _mod_spec(mod, tm, d):
    if mod.shape[0] == 1:
        return pl.BlockSpec((1, d), lambda i, j: (0, 0))
    return pl.BlockSpec((tm, d), lambda i, j: (i, 0))


def _inproj(x, sh, sc, g_all, w_all, l, kv_all, n_rwkv_pad, d_diff, tm):
    m, d = x.shape
    depth, n_tiles, _, tn = w_all.shape
    n = n_tiles * tn
    n_p, n_q = n_rwkv_pad // tn, d_diff // tn
    clip = lambda j, lo: jnp.clip(j - lo, 0, n_q - 1)
    in_specs = [
        pl.BlockSpec((tm, d), lambda i, j: (i, 0)),
        _mod_spec(sh, tm, d),
        _mod_spec(sc, tm, d),
        pl.BlockSpec((None, 1, d), lambda i, j: (l, 0, 0)),
        pl.BlockSpec((None, None, d, tn), lambda i, j: (l, j, 0, 0)),
        pl.BlockSpec(memory_space=pl.ANY),
        pl.BlockSpec(memory_space=pl.ANY),
    ]
    args = [x, sh, sc, g_all, w_all, kv_all[0], kv_all[1]]
    aliases = {5: 2, 6: 3}
    return pl.pallas_call(
        functools.partial(_inproj_kernel, n_p=n_p, n_q=n_q),
        grid=(m // tm, n // tn),
        in_specs=in_specs,
        out_specs=[
            pl.BlockSpec((tm, tn), lambda i, j: (i, jnp.minimum(j, n_p - 1))),
            pl.BlockSpec((tm, tn), lambda i, j: (i, clip(j, n_p))),
            pl.BlockSpec((None, tm, tn), lambda i, j: (l, i, clip(j, n_p + n_q))),
            pl.BlockSpec((None, tm, tn), lambda i, j: (l, i, clip(j, n_p + 2 * n_q))),
        ],
        out_shape=[
            jax.ShapeDtypeStruct((m, n_rwkv_pad), F32),
            jax.ShapeDtypeStruct((m, d_diff), BF16),
            jax.ShapeDtypeStruct((depth, m, d_diff), F32),
            jax.ShapeDtypeStruct((depth, m, d_diff), F32),
        ],
        scratch_shapes=[pltpu.VMEM((tm, d), BF16)],
        input_output_aliases=aliases,
        compiler_params=_cparams(("arbitrary", "arbitrary")),
        name="inproj",
    )(*args)


def _wkv_kernel(pr_ref, pk_ref, pv_ref, pw_ref, pa_ref, pg_ref,
                mur_ref, muk_ref, muv_ref, muw_ref, mua_ref, mug_ref,
                s0r_ref, s0k_ref, s0v_ref, s0w_ref, s0a_ref, s0g_ref,
                w2_ref, a2_ref, g2_ref,
                w0_ref, a0_ref, kkp_ref, ka_ref, rk_ref, lnw_ref, lnb_ref,
                st0_ref,
                y_ref, st_ref,
                sbd, cr, ck, cv, cw, ca, cg, *, C, n_sub, n_chunks, n_pairs):
    chunk = pl.program_id(2)
    CB = n_sub * C
    hd = RWKV_HEAD
    zero_blk = jnp.zeros((hd, hd), F32)

    @pl.when(chunk == 0)
    def _():
        for c_ref, s_ref in ((cr, s0r_ref), (ck, s0k_ref), (cv, s0v_ref), (cw, s0w_ref), (ca, s0a_ref), (cg, s0g_ref)):
            c_ref[0:1, :] = s_ref[...]
        for pi in range(n_pairs):
            top = jnp.concatenate([st0_ref[2 * pi], zero_blk], axis=1)
            bot = jnp.concatenate([zero_blk, st0_ref[2 * pi + 1]], axis=1)
            sbd[pi] = jnp.concatenate([top, bot], axis=0)

    def tshift(x_ref, mu_ref, c_ref):
        x = x_ref[...]
        row = lax.broadcasted_iota(jnp.int32, x.shape, 0)
        prev = jnp.where(row == 0, c_ref[0:1, :], pltpu.roll(x, 1, 0))
        c_ref[0:1, :] = x[CB - 1:CB, :]
        return x + (prev - x) * mu_ref[...]

    xr = tshift(pr_ref, mur_ref, cr)
    xk = tshift(pk_ref, muk_ref, ck)
    xv = tshift(pv_ref, muv_ref, cv)
    xw = tshift(pw_ref, muw_ref, cw)
    xa = tshift(pa_ref, mua_ref, ca)
    xg = tshift(pg_ref, mug_ref, cg)

    li = lax.broadcasted_iota(jnp.int32, (LANES, LANES), 0)
    lj = lax.broadcasted_iota(jnp.int32, (LANES, LANES), 1)
    same_head = (li < hd) == (lj < hd)
    li2 = lax.broadcasted_iota(jnp.int32, (2 * LANES, LANES), 0)
    lj2 = lax.broadcasted_iota(jnp.int32, (2 * LANES, LANES), 1)
    li2 = jnp.where(li2 < LANES, li2, li2 - LANES)
    seg2 = jnp.where((li2 < hd) == (lj2 < hd), 1.0, 0.0).astype(BF16)

    def segsum(x):
        return _sel_lanes(x, seg2)

    wlin = w0_ref[...] + _mm(jnp.tanh(xw).astype(BF16), w2_ref[...])
    z = -wlin
    w_log = -(jnp.maximum(z, 0.0) + jnp.log1p(jnp.exp(-jnp.abs(z)))) - 0.5
    ne = -jnp.exp(w_log)
    a = jax.nn.sigmoid(a0_ref[...] + _mm(xa.astype(BF16), a2_ref[...]))
    g = _mm(jax.nn.sigmoid(xg).astype(BF16), g2_ref[...])
    kk_raw = xk * kkp_ref[...]
    k2 = xk * (1.0 + (a - 1.0) * ka_ref[...])
    rk_prod = xr * k2 * rk_ref[...]

    ti = lax.broadcasted_iota(jnp.int32, (CB, 2 * CB), 0)
    tj = lax.broadcasted_iota(jnp.int32, (CB, 2 * CB), 1)
    tj = jnp.where(tj < CB, tj, tj - CB)
    in_chunk = [(ti >= q * C) & (ti < (q + 1) * C) & (tj >= q * C) & (tj <= ti) for q in range(n_sub)]
    tri2 = jnp.where(functools.reduce(lambda x, y: x | y, in_chunk), 1.0, 0.0).astype(BF16)
    cum = _sel_rows(tri2, ne)
    gam = jnp.exp(cum)
    gprev = jnp.exp(cum - ne)
    ginv = jnp.exp(-cum)
    cum_end = jnp.concatenate([jnp.broadcast_to(cum[(q + 1) * C - 1:(q + 1) * C, :], (C, cum.shape[1]))
                               for q in range(n_sub)], axis=0)
    dte = jnp.exp(cum_end - cum)
    rg_all = xr * gam
    ks_all = k2 * ginv
    kd_all = k2 * dte
    lnw = lnw_ref[...]
    lnb = lnb_ref[...]

    def stack_masked(x):
        first = lax.broadcasted_iota(jnp.int32, (1, x.shape[1]), 1) < x.shape[1] // 2
        return jnp.concatenate([jnp.where(first, x, 0.0), jnp.where(first, 0.0, x)], axis=0)

    R = 2 * C
    ri = lax.broadcasted_iota(jnp.int32, (C, R), 0)
    ci = lax.broadcasted_iota(jnp.int32, (C, R), 1)
    ci = jnp.where(ci < C, ci, ci - C)
    strict = ci < ri
    incl = ci <= ri
    n_steps = max(1, int(math.log2(C)))

    pairs = range(n_pairs)
    sls = [slice(pi * LANES, (pi + 1) * LANES) for pi in pairs]
    sums_all = [segsum(jnp.concatenate([kk_raw[:, sl] * kk_raw[:, sl], rk_prod[:, sl]], axis=0)) for sl in sls]
    kk_all = [kk_raw[:, sl] * lax.rsqrt(jnp.maximum(sm[:CB, :], 1e-24)) for sl, sm in zip(sls, sums_all)]
    s_raw = [sbd[pi] for pi in pairs]
    for q in range(n_sub):
        s_raw = _wkv_chunk(
            q, s_raw, y_ref, sls, stack_masked, segsum, strict, incl, same_head,
            kk_all, [sm[CB:, :] for sm in sums_all], a, xv, gprev, ginv, dte, gam, rg_all, ks_all, kd_all,
            lnw, lnb, g, C=C, n_steps=n_steps)
    for pi in pairs:
        sbd[pi] = s_raw[pi]

    @pl.when(chunk == n_chunks - 1)
    def _():
        for pi in range(n_pairs):
            s_new = sbd[pi]
            st_ref[2 * pi] = s_new[0:hd, 0:hd]
            st_ref[2 * pi + 1] = s_new[hd:2 * hd, hd:2 * hd]


def _wkv_chunk(q, s_raw, y_ref, sls, stack_masked, segsum, strict, incl, same_head,
               kk_all, rksum_all, a, xv, gprev, ginv, dte, gam, rg_all, ks_all, kd_all, lnw, lnb, g, *, C, n_steps):
    rs = slice(q * C, (q + 1) * C)
    R = 2 * C
    inv_hd = 1.0 / RWKV_HEAD
    pairs = range(len(sls))
    xv_p = [xv[rs, sl] for sl in sls]
    kk = [k[rs, :] for k in kk_all]
    bonus = [sm[rs, :] * v for sm, v in zip(rksum_all, xv_p)]
    kka = [k * a[rs, sl] for k, sl in zip(kk, sls)]
    gprev, ginv, dte, rg_all, ks_all, kd_all = (t[rs, :] for t in (gprev, ginv, dte, rg_all, ks_all, kd_all))
    gam_end = gam[(q + 1) * C - 1:(q + 1) * C, :]
    def both(f, pieces):
        return tuple(f(p_) for p_ in pieces)

    def cat(axis, *pairs_of_pieces):
        return tuple(jnp.concatenate(ps, axis=axis) for ps in zip(*pairs_of_pieces))

    la = [_hi_lo(-k * gprev[:, sl]) for k, sl in zip(kk, sls)]
    lr = [rg_all[:, sl].astype(BF16) for sl in sls]
    rhs = [cat(0, both(stack_masked, _hi_lo(ka_ * ginv[:, sl])), both(stack_masked, _hi_lo(ks_all[:, sl])))
           for ka_, sl in zip(kka, sls)]
    xv_s = [_hi_lo(v) for v in xv_p]
    vst = [both(stack_masked, v) for v in xv_s]
    s_old = [_hi_lo(s_) for s_ in s_raw]
    sc_a = [_dot3(l_, r_, _NT) for l_, r_ in zip(la, rhs)]
    sc_r = [_mm(l_, r_[0], _NT) for l_, r_ in zip(lr, rhs)]
    ls_a = [_dot3(l_, s_, _NT) for l_, s_ in zip(la, s_old)]
    ls_r = [_mm(l_, s_[0], _NT) for l_, s_ in zip(lr, s_old)]
    n_ab = [jnp.where(strict, s4[:, :R], 0.0) for s4 in sc_a]
    m_ak = [jnp.where(strict, s4[:, R:], 0.0) for s4 in sc_a]
    m_r = [jnp.concatenate([jnp.where(incl, s4[:, :R], 0.0), jnp.where(incl, s4[:, R:], 0.0)],
                           axis=1).astype(BF16) for s4 in sc_r]
    x_u = [l_ + _dot3(_hi_lo(mk), v) for l_, mk, v in zip(ls_a, m_ak, vst)]
    pw = n_ab
    for step in range(n_steps):
        ps = [_hi_lo(p_) for p_ in pw]
        xs = [both(stack_masked, _hi_lo(x_)) for x_ in x_u]
        if step + 1 < n_steps:
            t = [_dot3(p_, cat(1, x_, both(stack_masked, p_))) for p_, x_ in zip(ps, xs)]
            x_u = [x_ + t_[:, :LANES] for x_, t_ in zip(x_u, t)]
            pw = [t_[:, LANES:] for t_ in t]
        else:
            x_u = [x_ + _dot3(p_, x_s) for x_, p_, x_s in zip(x_u, ps, xs)]
    xu_s = [_hi_lo(x_) for x_ in x_u]
    y = [l_ + _mm(mr, jnp.concatenate([stack_masked(x_[0]), v[0]], axis=0))
         for l_, mr, x_, v in zip(ls_r, m_r, xu_s, vst)]
    upd = [_dot3(cat(0, x_, v), _hi_lo(jnp.concatenate([ka_ * dte[:, sl], kd_all[:, sl]], axis=0)), _TN)
           for x_, v, ka_, sl in zip(xu_s, xv_s, kka, sls)]
    s_new = [s_raw[pi] * gam_end[:, sls[pi]] + jnp.where(same_head, upd[pi], 0.0) for pi in pairs]

    yc = [y_ - segsum(y_) * inv_hd for y_ in y]
    y_var = [segsum(c_ * c_) * inv_hd for c_ in yc]
    for pi in pairs:
        sl = sls[pi]
        yn = yc[pi] * lax.rsqrt(y_var[pi] + GN_EPS) * lnw[:, sl] + lnb[:, sl]
        y_ref[rs, sl] = ((yn + bonus[pi]) * g[rs, sl]).astype(BF16)
    return s_new


def _wkv(p, l, mu, shift0, w2, a2, g2, w0, a0, kkp, ka, rk, lnw, lnb, state0, n_seq, seq_len, C, n_sub):
    m, n_pad = p.shape
    d_rwkv = w0.shape[-1]
    n_heads = d_rwkv // RWKV_HEAD
    n_pairs = WKV_PAIRS_PER_STEP
    wd = n_pairs * LANES
    n_groups = d_rwkv // wd
    cb = C * n_sub
    n_chunks = seq_len // cb
    nb = n_groups
    blk_w = 3 * d_rwkv // LANES
    blk_a = blk_w + 1
    blk_g = (blk_a + 1) * LANES // GATE_LORA

    row = lambda s, h, c: s * n_chunks + c
    seg_specs = [
        pl.BlockSpec((cb, wd), lambda s, h, c: (row(s, h, c), h)),
        pl.BlockSpec((cb, wd), lambda s, h, c: (row(s, h, c), nb + h)),
        pl.BlockSpec((cb, wd), lambda s, h, c: (row(s, h, c), 2 * nb + h)),
        pl.BlockSpec((cb, LANES), lambda s, h, c: (row(s, h, c), blk_w)),
        pl.BlockSpec((cb, LANES), lambda s, h, c: (row(s, h, c), blk_a)),
        pl.BlockSpec((cb, GATE_LORA), lambda s, h, c: (row(s, h, c), blk_g)),
    ]
    mu_specs = [
        pl.BlockSpec((None, 1, wd), lambda s, h, c: (l, 0, h)),
        pl.BlockSpec((None, 1, wd), lambda s, h, c: (l, 0, nb + h)),
        pl.BlockSpec((None, 1, wd), lambda s, h, c: (l, 0, 2 * nb + h)),
        pl.BlockSpec((None, 1, LANES), lambda s, h, c: (l, 0, blk_w)),
        pl.BlockSpec((None, 1, LANES), lambda s, h, c: (l, 0, blk_a)),
        pl.BlockSpec((None, 1, GATE_LORA), lambda s, h, c: (l, 0, blk_g)),
    ]
    s0_specs = [
        pl.BlockSpec((None, 1, wd), lambda s, h, c: (s, 0, h)),
        pl.BlockSpec((None, 1, wd), lambda s, h, c: (s, 0, nb + h)),
        pl.BlockSpec((None, 1, wd), lambda s, h, c: (s, 0, 2 * nb + h)),
        pl.BlockSpec((None, 1, LANES), lambda s, h, c: (s, 0, blk_w)),
        pl.BlockSpec((None, 1, LANES), lambda s, h, c: (s, 0, blk_a)),
        pl.BlockSpec((None, 1, GATE_LORA), lambda s, h, c: (s, 0, blk_g)),
    ]
    lora_specs = [
        pl.BlockSpec((None, LORA_PAD, wd), lambda s, h, c: (l, 0, h)),
        pl.BlockSpec((None, LORA_PAD, wd), lambda s, h, c: (l, 0, h)),
        pl.BlockSpec((None, GATE_LORA, wd), lambda s, h, c: (l, 0, h)),
    ]
    vec_spec = pl.BlockSpec((None, 1, wd), lambda s, h, c: (l, 0, h))
    state_spec = pl.BlockSpec((None, 2 * n_pairs, RWKV_HEAD, RWKV_HEAD), lambda s, h, c: (s, h, 0, 0))
    ls = min(l, state0.shape[0] - 1)
    state0_spec = pl.BlockSpec((None, None, 2 * n_pairs, RWKV_HEAD, RWKV_HEAD), lambda s, h, c: (ls, s, h, 0, 0))
    y, st = pl.pallas_call(
        functools.partial(_wkv_kernel, C=C, n_sub=n_sub, n_chunks=n_chunks, n_pairs=n_pairs),
        grid=(n_seq, n_groups, n_chunks),
        in_specs=seg_specs + mu_specs + s0_specs + lora_specs + [vec_spec] * 7 + [state0_spec],
        out_specs=[
            pl.BlockSpec((cb, wd), lambda s, h, c: (row(s, h, c), h)),
            state_spec,
        ],
        out_shape=[
            jax.ShapeDtypeStruct((m, d_rwkv), BF16),
            jax.ShapeDtypeStruct((n_seq, n_heads, RWKV_HEAD, RWKV_HEAD), F32),
        ],
        scratch_shapes=[pltpu.VMEM((n_pairs, LANES, LANES), F32)]
        + [pltpu.VMEM((SUBLANES, wd), F32)] * 3 + [pltpu.VMEM((SUBLANES, LANES), F32)] * 2
        + [pltpu.VMEM((SUBLANES, GATE_LORA), F32)],
        compiler_params=_cparams(("arbitrary", "arbitrary", "arbitrary")),
        name="wkv",
    )(p, p, p, p, p, p, mu, mu, mu, mu, mu, mu,
      shift0, shift0, shift0, shift0, shift0, shift0,
      w2, a2, g2, w0, a0, kkp, ka, rk, lnw, lnb, state0)
    return y, st


def _lambda(lqk_ref, lam_init):
    t = lqk_ref[...]
    s1 = jnp.sum(t[0:1, :] * t[1:2, :], axis=-1, keepdims=True)
    s2 = jnp.sum(t[2:3, :] * t[3:4, :], axis=-1, keepdims=True)
    return jnp.exp(s1) - jnp.exp(s2) + lam_init


def _split_maps(q):
    lane = lax.broadcasted_iota(jnp.int32, q.shape, 1)
    zero = jnp.zeros_like(q)
    return jnp.where(lane < DIFF_HEAD, q, zero), jnp.where(lane < DIFF_HEAD, zero, q)


def _subln(o, subln, lam_init):
    return _rms(o, subln, SUBLN_EPS) * (1.0 - lam_init)


FAR_STEP_TILES = (4, 2, 1)
ATTN_TILES_PER_STEP = 4


def _attn_kernel(q_ref, k_ref, v_ref, bias_ref, lqk_ref, subln_ref, o_ref,
                 kb, vt, m_s, a_s, *, TB, n_cast, lam_init):
    g = pl.program_id(1)

    @pl.when(g == 0)
    def _():
        vt[LANES:, :] = jnp.ones((vt.shape[0] - LANES, vt.shape[1]), BF16)

        def cast(c, carry):
            off = pl.multiple_of(c * TB, TB)
            kb[pl.ds(off, TB), :] = k_ref[pl.ds(off, TB), :].astype(BF16)
            vt[0:LANES, pl.ds(off, TB)] = v_ref[pl.ds(off, TB), :].T.astype(BF16)
            return carry
        lax.fori_loop(0, n_cast, cast, 0)

    nq = ATTN_TILES_PER_STEP
    chains = range(nq)
    qs = [jnp.concatenate(_split_maps(q_ref[c * TB:(c + 1) * TB, :]), axis=0) for c in chains]
    m_s[...] = jnp.full(m_s.shape, MASK_VALUE, F32)
    a_s[...] = jnp.zeros(a_s.shape, F32)

    def scores(c, off, width):
        return _mm(kb[pl.ds(off, width), :], qs[c], _NT)

    def update(c, s, off, width):
        m_prev = m_s[c]
        m_new = jnp.maximum(m_prev, jnp.max(s, axis=0, keepdims=True))
        alpha = jnp.exp(m_prev - m_new)
        p = jnp.exp(s - m_new).astype(BF16)
        a_s[c] = alpha * a_s[c] + _mm(vt[:, pl.ds(off, width)], p)
        m_s[c] = m_new

    def far_all(off, width):
        s = [scores(c, off, width) for c in chains]
        for c in chains:
            update(c, s[c], off, width)

    n_common = jnp.maximum(nq * g - 1, 0)
    done = 0
    for step_tiles in FAR_STEP_TILES:
        width = TB * step_tiles
        n_steps = (n_common - done) // step_tiles

        def far_step(j, carry, done=done, width=width):
            far_all(pl.multiple_of(done * TB + j * width, TB), width)
            return carry
        lax.fori_loop(0, n_steps, far_step, 0)
        done = done + n_steps * step_tiles

    def own_far(first_chain, off0):
        for k in range(nq - first_chain):
            off = pl.multiple_of(off0 + k * TB, TB)
            cs = range(first_chain + k, nq)
            s = [scores(c, off, TB) for c in cs]
            for c, s_ in zip(cs, s):
                update(c, s_, off, TB)

    @pl.when(g > 0)
    def _():
        own_far(1, n_common * TB)

    @pl.when(g == 0)
    def _():
        own_far(2, 0)

    offs = [pl.multiple_of(jnp.maximum(nq * g + c - 1, 0) * TB, TB) for c in chains]
    s_near = [scores(c, offs[c], 2 * TB) + (bias_ref[jnp.where(g == 0, 1, 0)] if c == 0 else bias_ref[0])
              for c in chains]
    for c in chains:
        update(c, s_near[c], offs[c], 2 * TB)

    lam = _lambda(lqk_ref, lam_init)
    for c in chains:
        acc = a_s[c]
        on = acc[0:LANES, :] / acc[LANES:LANES + 1, :]
        o = (on[:, :TB] - lam * on[:, TB:]).T
        o_ref[c * TB:(c + 1) * TB, :] = _subln(o, subln_ref[...], lam_init).astype(BF16)


def _attn_prompt(q, k_all, v_all, l, bias, lqk, subln, lam_init, TB):
    m, d_diff = q.shape
    n_heads = d_diff // LANES
    n_blk = m // TB
    nq = ATTN_TILES_PER_STEP
    assert n_blk % nq == 0
    return pl.pallas_call(
        functools.partial(_attn_kernel, TB=TB, n_cast=n_blk, lam_init=lam_init),
        grid=(n_heads, n_blk // nq),
        in_specs=[
            pl.BlockSpec((nq * TB, LANES), lambda h, g: (g, h)),
            pl.BlockSpec((None, m, LANES), lambda h, g: (l, 0, h)),
            pl.BlockSpec((None, m, LANES), lambda h, g: (l, 0, h)),
            pl.BlockSpec((None, 2, 2 * TB, 2 * TB), lambda h, g: (h, 0, 0, 0)),
            pl.BlockSpec((None, 4, DIFF_HEAD), lambda h, g: (l, 0, 0)),
            pl.BlockSpec((None, 1, LANES), lambda h, g: (l, 0, 0)),
        ],
        out_specs=pl.BlockSpec((nq * TB, LANES), lambda h, g: (g, h)),
        out_shape=jax.ShapeDtypeStruct((m, d_diff), BF16),
        scratch_shapes=[pltpu.VMEM((m, LANES), BF16), pltpu.VMEM((LANES + BF16_ROWS, m), BF16),
                        pltpu.VMEM((nq, 1, 2 * TB), F32), pltpu.VMEM((nq, LANES + BF16_ROWS, 2 * TB), F32)],
        compiler_params=_cparams(("arbitrary", "arbitrary")),
        name="attn_prompt",
    )(q, k_all, v_all, bias, lqk, subln)


SAMPLE_FAR_FRAMES = 640


def _attn_sample_kernel(q_ref, kn_ref, vn_ref, kc_ref, vc_ref, bias_ref, lqk_ref, subln_ref, o_ref,
                        kb, acc, *, T, n_heads, near, lam_init):
    past = kc_ref.shape[0]
    n_cols = 2 * T * n_heads
    fc = SAMPLE_FAR_FRAMES
    n_far = (past - near) // fc
    rows = fc * n_heads
    qs = jnp.concatenate(
        [m_ for h in range(n_heads) for m_ in _split_maps(q_ref[:, h * LANES:(h + 1) * LANES])], axis=0)
    sub = lax.broadcasted_iota(jnp.int32, (n_heads, n_cols), 0)
    col = lax.broadcasted_iota(jnp.int32, (n_heads, n_cols), 1)
    valid = (col >= sub * (2 * T)) & (col < (sub + 1) * (2 * T))

    def flat(x3):
        return x3.reshape(x3.shape[0] * n_heads, LANES)

    def scores(k_rows):
        s = _mm(k_rows, qs, _NT)
        return s.reshape(s.shape[0] // n_heads, n_heads, n_cols)

    near_k = jnp.concatenate([flat(kc_ref[past - near:past]), flat(kn_ref[...])], axis=0).astype(BF16)
    near_v = jnp.concatenate([flat(vc_ref[past - near:past]), flat(vn_ref[...])], axis=0).astype(BF16)
    s_near = scores(near_k) + bias_ref[...].reshape(near + T, n_heads, n_cols)

    def pass1(c, m3):
        f0 = pl.multiple_of(c * fc, fc)
        k_rows = flat(kc_ref[pl.ds(f0, fc)]).astype(BF16)
        kb[pl.ds(pl.multiple_of(c * rows, rows), rows), :] = k_rows
        return jnp.maximum(m3, jnp.max(scores(k_rows), axis=0))
    m3 = lax.fori_loop(0, n_far, pass1, jnp.max(s_near, axis=0))

    ones = jnp.ones((rows, LANES), BF16)

    def weights(s3):
        p = jnp.where(valid, jnp.exp(s3 - m3), 0.0)
        return p.reshape(p.shape[0] * n_heads, n_cols).astype(BF16)

    acc[...] = _mm(weights(s_near), jnp.concatenate([near_v, ones[:near_v.shape[0]]], axis=1), _TN)

    def pass2(c, carry):
        f0 = pl.multiple_of(c * fc, fc)
        p = weights(scores(kb[pl.ds(pl.multiple_of(c * rows, rows), rows), :]))
        v_aug = jnp.concatenate([flat(vc_ref[pl.ds(f0, fc)]).astype(BF16), ones], axis=1)
        acc[...] += _mm(p, v_aug, _TN)
        return carry
    lax.fori_loop(0, n_far, pass2, 0)

    lam = _lambda(lqk_ref, lam_init)
    for h in range(n_heads):
        a = acc[h * 2 * T:(h + 1) * 2 * T, :]
        on = a[:, 0:LANES] / a[:, LANES:2 * LANES]
        o = on[:T, :] - lam * on[T:, :]
        o_ref[:, h * LANES:(h + 1) * LANES] = _subln(o, subln_ref[...], lam_init).astype(BF16)


def _attn_sample(q, kn_all, vn_all, l, k_cache, v_cache, bias_t, lqk, subln, lam_init, n_seq, T, near):
    m, d_diff = q.shape
    n_heads = d_diff // LANES
    past = k_cache.shape[2]
    assert (past - near) % SAMPLE_FAR_FRAMES == 0
    cache_spec = pl.BlockSpec((None, None, past, n_heads, LANES), lambda b: (l, b, 0, 0, 0))
    new_spec = pl.BlockSpec((None, T, n_heads, LANES), lambda b: (l, b, 0, 0))
    return pl.pallas_call(
        functools.partial(_attn_sample_kernel, T=T, n_heads=n_heads, near=near, lam_init=lam_init),
        grid=(n_seq,),
        in_specs=[
            pl.BlockSpec((T, d_diff), lambda b: (b, 0)),
            new_spec,
            new_spec,
            cache_spec,
            cache_spec,
            pl.BlockSpec(bias_t.shape, lambda b: (0, 0)),
            pl.BlockSpec((None, 4, DIFF_HEAD), lambda b: (l, 0, 0)),
            pl.BlockSpec((None, 1, LANES), lambda b: (l, 0, 0)),
        ],
        out_specs=pl.BlockSpec((T, d_diff), lambda b: (b, 0)),
        out_shape=jax.ShapeDtypeStruct((m, d_diff), BF16),
        scratch_shapes=[pltpu.VMEM(((past - near) * n_heads, LANES), BF16),
                        pltpu.VMEM((2 * T * n_heads, 2 * LANES), F32)],
        compiler_params=_cparams(("arbitrary",)),
        name="attn_sample",
    )(q, kn_all, vn_all, k_cache, v_cache, bias_t, lqk, subln)


def _outproj_kernel(x_ref, yr_ref, yd_ref, gt_ref, wt_ref, wb_ref, o_ref):
    mixed = _mm(yr_ref[...], wt_ref[...]) + _mm(yd_ref[...], wb_ref[...])
    o_ref[...] = x_ref[...] + gt_ref[...] * mixed


def _outproj(x, yr, yd, gt, w_out_all, l, tm):
    m, d = x.shape
    dr = yr.shape[1]
    dd = yd.shape[1]
    gt_spec = (pl.BlockSpec((1, d), lambda i: (0, 0)) if gt.shape[0] == 1
               else pl.BlockSpec((tm, d), lambda i: (i, 0)))
    return pl.pallas_call(
        _outproj_kernel,
        grid=(m // tm,),
        in_specs=[
            pl.BlockSpec((tm, d), lambda i: (i, 0)),
            pl.BlockSpec((tm, dr), lambda i: (i, 0)),
            pl.BlockSpec((tm, dd), lambda i: (i, 0)),
            gt_spec,
            pl.BlockSpec((None, dr, d), lambda i: (l, 0, 0)),
            pl.BlockSpec((None, dd, d), lambda i: (l, dr // dd, 0)),
        ],
        out_specs=pl.BlockSpec((tm, d), lambda i: (i, 0)),
        out_shape=jax.ShapeDtypeStruct((m, d), F32),
        compiler_params=_cparams(("arbitrary",)),
        name="outproj",
    )(x, yr, yd, gt, w_out_all, w_out_all)


def _shift_rows(x, prev, n):
    if n % SUBLANES == 0:
        return jnp.concatenate([prev, x[:x.shape[0] - n, :]], axis=0)
    rolled = pltpu.roll(x, n, 0)
    row = lax.broadcasted_iota(jnp.int32, x.shape, 0)
    out = rolled
    for r in range(n):
        out = jnp.where(row == r, prev[r:r + 1, :], out)
    return out


def _ffn_kernel(x_ref, sh_ref, sc_ref, gt_ref, g_ref, wg_ref, wv_ref, cw_ref, cb_ref, c0_ref, wd_ref, gf_ref,
                o_ref, co_ref, h_scr, acc_scr, carry, *, B, n_j, final_norm):
    i = pl.program_id(0)
    j = pl.program_id(1)
    tm = x_ref.shape[0]

    @pl.when(j == 0)
    def _():
        xn = _rms(x_ref[...], g_ref[...], NORM_EPS)
        h_scr[...] = (xn * (1.0 + sc_ref[...]) + sh_ref[...]).astype(BF16)
        acc_scr[...] = jnp.zeros(acc_scr.shape, F32)

    @pl.when(i == 0)
    def _():
        carry[j, 0:2 * B, :] = c0_ref[...]

    h = h_scr[...]
    ug = _mm(h, wg_ref[...])
    uv = _mm(h, wv_ref[...])
    prev = carry[j, 0:2 * B, :]
    s1 = _shift_rows(ug, prev[B:2 * B, :], B)
    s2 = _shift_rows(ug, prev, 2 * B)
    cw = cw_ref[...]
    z = cb_ref[...] + s2 * cw[0:1, :] + s1 * cw[1:2, :] + ug * cw[2:3, :]
    act = 0.5 * z * (1.0 + lax.erf(z * (2.0 ** -0.5)))
    acc_scr[...] += _mm((act * uv).astype(BF16), wd_ref[...])
    last2 = ug[tm - 2 * B:tm, :]
    carry[j, 0:2 * B, :] = last2
    tf = last2.shape[1]
    co_ref[:, pl.ds(pl.multiple_of(j * tf, tf), tf)] = last2

    @pl.when(j == n_j - 1)
    def _():
        x2 = x_ref[...] + gt_ref[...] * acc_scr[...]
        if final_norm:
            x2 = _rms(x2, gf_ref[...], NORM_EPS)
        o_ref[...] = x2


def _ffn(x, sh, sc, gt, g_all, w_up_all, conv_w_all, conv_b_all, conv0, w_down_all, g_final, l, B, tm,
         final_norm):
    m, d = x.shape
    d_ff = w_down_all.shape[1]
    tf = w_up_all.shape[-1]
    n_j = d_ff // tf
    rows_c = max(SUBLANES, 2 * B)
    return pl.pallas_call(
        functools.partial(_ffn_kernel, B=B, n_j=n_j, final_norm=final_norm),
        grid=(m // tm, n_j),
        in_specs=[
            pl.BlockSpec((tm, d), lambda i, j: (i, 0)),
            _mod_spec(sh, tm, d),
            _mod_spec(sc, tm, d),
            _mod_spec(gt, tm, d),
            pl.BlockSpec((None, 1, d), lambda i, j: (l, 0, 0)),
            pl.BlockSpec((None, None, d, tf), lambda i, j: (l, j, 0, 0)),
            pl.BlockSpec((None, None, d, tf), lambda i, j: (l, n_j + j, 0, 0)),
            pl.BlockSpec((None, CONV_W, tf), lambda i, j: (l, 0, j)),
            pl.BlockSpec((None, 1, tf), lambda i, j: (l, 0, j)),
            pl.BlockSpec((2 * B, tf), lambda i, j: (0, j)),
            pl.BlockSpec((None, tf, d), lambda i, j: (l, j, 0)),
            pl.BlockSpec((1, d), lambda i, j: (0, 0)),
        ],
        out_specs=[
            pl.BlockSpec((tm, d), lambda i, j: (i, 0)),
            pl.BlockSpec((2 * B, d_ff), lambda i, j: (0, 0)),
        ],
        out_shape=[
            jax.ShapeDtypeStruct((m, d), F32),
            jax.ShapeDtypeStruct((2 * B, d_ff), F32),
        ],
        scratch_shapes=[pltpu.VMEM((tm, d), BF16), pltpu.VMEM((tm, d), F32),
                        pltpu.VMEM((n_j, rows_c, tf), F32)],
        compiler_params=_cparams(("arbitrary", "arbitrary")),
        name="conv_ffn",
    )(x, sh, sc, gt, g_all, w_up_all, w_up_all, conv_w_all, conv_b_all, conv0, w_down_all, g_final)


def _pad_cols(a, n):
    return jnp.pad(a, [(0, 0)] * (a.ndim - 1) + [(0, n)])


def _regroup_rwkv_cols(a, d_rwkv):
    o = 3 * d_rwkv
    return jnp.concatenate([
        a[..., :o],
        _pad_cols(a[..., o:o + DECAY_LORA], LORA_PAD - DECAY_LORA),
        _pad_cols(a[..., o + DECAY_LORA:o + DECAY_LORA + AAA_LORA], LORA_PAD - AAA_LORA),
        a[..., o + DECAY_LORA + AAA_LORA:],
    ], axis=-1)


def _ungroup_rwkv_cols(a, d_rwkv):
    o = 3 * d_rwkv
    return jnp.concatenate([
        a[..., :o],
        a[..., o:o + DECAY_LORA],
        a[..., o + LORA_PAD:o + LORA_PAD + AAA_LORA],
        a[..., o + 2 * LORA_PAD:],
    ], axis=-1)


def _prompt_tiles(seq):
    attn_tile = min(256, seq // 2)
    assert attn_tile % CHUNK == 0 and attn_tile >= MAX_DISTANCE
    assert seq % (ATTN_TILES_PER_STEP * attn_tile) == 0
    return attn_tile, min(512, seq), min(1024, seq), min(64, seq)


def kernel(x_prompt, x_sample, c_prompt, c_sample, cache_k, cache_v, state_wkv, state_shift, state_conv, w_ada, b_ada, g_mix, g_ffn, w_in, w_out, rwkv_mu, rwkv_w0, rwkv_w2, rwkv_a0, rwkv_a2, rwkv_g2, rwkv_kk, rwkv_ka, rwkv_rk, rwkv_ln_w, rwkv_ln_b, diff_lq1, diff_lk1, diff_lq2, diff_lk2, diff_subln, rel_table, ffn_up, ffn_conv_w, ffn_conv_b, ffn_down, g_final):
    depth, d_model, _ = w_in.shape
    bp, seq, _ = x_prompt.shape
    bs, dseq, _ = x_sample.shape
    past = cache_k.shape[2]
    d_rwkv = rwkv_w0.shape[1]
    d_diff = d_model - d_rwkv
    n_dheads = d_diff // (2 * DIFF_HEAD)
    n_rheads = d_rwkv // RWKV_HEAD
    n_rwkv_cols = rwkv_mu.shape[1]
    n_rwkv_pad = n_rwkv_cols + 2 * LORA_PAD - DECAY_LORA - AAA_LORA
    d_ff = ffn_down.shape[1]
    assert bp == 1, "prompt path handles one sequence"
    TB, tm_p, tm_in, wkv_chunk = _prompt_tiles(seq)
    m_s = bs * dseq

    w_in_b = jnp.concatenate([_regroup_rwkv_cols(w_in[..., :n_rwkv_cols], d_rwkv), w_in[..., n_rwkv_cols:]],
                             axis=-1).astype(BF16)
    w_in_b = w_in_b.reshape(depth, d_model, -1, WEIGHT_COL_TILE).swapaxes(1, 2)
    mu_p = _regroup_rwkv_cols(rwkv_mu, d_rwkv)[:, None]
    w2_b = jnp.pad(rwkv_w2, ((0, 0), (0, LORA_PAD - DECAY_LORA), (0, 0))).astype(BF16)
    a2_b = jnp.pad(rwkv_a2, ((0, 0), (0, LORA_PAD - AAA_LORA), (0, 0))).astype(BF16)
    g2_b = rwkv_g2.astype(BF16)
    w_out_b = _cast_bf16(w_out)
    up_b = _cast_bf16(ffn_up, col_tile=WEIGHT_COL_TILE)
    down_b = _cast_bf16(ffn_down)
    vec = lambda a: a.reshape(depth, 1, -1)
    wkv_args = (w2_b, a2_b, g2_b, vec(rwkv_w0), vec(rwkv_a0), vec(rwkv_kk), vec(rwkv_ka), vec(rwkv_rk),
                vec(rwkv_ln_w), vec(rwkv_ln_b))
    g_mix_v, g_ffn_v, subln_v, conv_b_v = vec(g_mix), vec(g_ffn), vec(diff_subln), vec(ffn_conv_b)
    gf = g_final[None]
    lqk = jnp.stack([diff_lq1, diff_lk1, diff_lq2, diff_lk2], axis=1)

    n_c = bp + bs
    n_c_pad = -(-n_c // SUBLANES) * SUBLANES
    c_rows = jnp.pad(jnp.concatenate([c_prompt, c_sample], axis=0), ((0, n_c_pad - n_c), (0, 0)))
    mod = _adaln_mod(c_rows, w_ada, b_ada).reshape(depth, n_c_pad, 6, d_model)

    ql = jnp.arange(TB)
    rel_diag = ql[None, :] - ql[:, None]
    allowed = (ql[None, :] // CHUNK) <= (ql[:, None] // CHUNK)
    bucket_diag = jnp.where(allowed, _t5_bucket(rel_diag), -1)
    bucket_prev = _t5_bucket(rel_diag - TB)
    windows = [jnp.concatenate([bucket_prev, bucket_diag], axis=1),
               jnp.concatenate([bucket_diag, jnp.full((TB, TB), -1)], axis=1)]
    bucket_p = jnp.concatenate([w.T for w in windows], axis=0)
    bias_p = _bias_lookup(rel_table, bucket_p.astype(jnp.int32), FAR_BUCKET, col_repeats=2)
    bias_p = bias_p.reshape(n_dheads, 2, 2 * TB, 2 * TB)
    q_pos = past + jnp.arange(dseq)
    assert past >= MAX_DISTANCE
    k_pos = past - MAX_DISTANCE + jnp.arange(MAX_DISTANCE + dseq)
    bias_s = _bias_lookup(rel_table, _t5_bucket(k_pos[None, :] - q_pos[:, None]).astype(jnp.int32), FAR_BUCKET)
    bias_t = jnp.broadcast_to(bias_s.transpose(2, 0, 1)[:, None, :, None, :],
                              (MAX_DISTANCE + dseq, n_dheads, n_dheads, 2, dseq))
    bias_t = bias_t.reshape((MAX_DISTANCE + dseq) * n_dheads, n_dheads * 2 * dseq)

    xp = x_prompt.reshape(seq, d_model)
    xs = x_sample.reshape(m_s, d_model)
    zero_shift = jnp.zeros((1, 1, n_rwkv_pad), F32)
    zero_state = jnp.zeros((1, 1, n_rheads, RWKV_HEAD, RWKV_HEAD), F32)
    zero_conv = jnp.zeros((2, d_ff), F32)
    shift_s_in = _regroup_rwkv_cols(state_shift, d_rwkv)
    outs_p = {k: [] for k in ("wkv", "shift", "conv")}
    outs_s = {k: [] for k in ("wkv", "shift", "conv")}
    kv_p = (jnp.zeros((depth, seq, d_diff), F32), jnp.zeros((depth, seq, d_diff), F32))
    kv_s = (jnp.zeros((depth, m_s, d_diff), F32), jnp.zeros((depth, m_s, d_diff), F32))

    def time_major(a):
        return a.reshape(bs, dseq, -1).swapaxes(0, 1).reshape(m_s, -1)

    def batch_major(a):
        return a.reshape(dseq, bs, -1).swapaxes(0, 1).reshape(m_s, -1)

    for l in range(depth):
        lam_init = 0.8 - 0.6 * math.exp(-0.3 * l)
        last = l == depth - 1

        mp = mod[l, 0:bp]
        sh1, sc1, gt1, sh2, sc2, gt2 = (mp[:, t] for t in range(6))
        p, q, *kv_p = _inproj(xp, sh1, sc1, g_mix_v, w_in_b, l, kv_p, n_rwkv_pad, d_diff, tm_in)
        y_r, wkv = _wkv(p, l, mu_p, zero_shift, *wkv_args, zero_state, 1, seq, wkv_chunk,
                        WKV_CHUNKS_PER_STEP if seq % (WKV_CHUNKS_PER_STEP * wkv_chunk) == 0 else 1)
        y_d = _attn_prompt(q, kv_p[0], kv_p[1], l, bias_p, lqk, subln_v, lam_init, TB)
        x1 = _outproj(xp, y_r, y_d, gt1, w_out_b, l, tm_p)
        xp, conv = _ffn(x1, sh2, sc2, gt2, g_ffn_v, up_b, ffn_conv_w, conv_b_v, zero_conv, down_b, gf, l,
                        1, tm_p, last)
        outs_p["wkv"].append(wkv)
        outs_p["shift"].append(_ungroup_rwkv_cols(p[seq - 1:seq], d_rwkv).reshape(bp, 1, n_rwkv_cols))
        outs_p["conv"].append(conv.reshape(bp, CONV_W - 1, d_ff))

        ms = mod[l, bp:bp + bs]
        rows_bm = jnp.repeat(ms, dseq, axis=0)
        rows_tm = jnp.tile(ms, (dseq, 1, 1))
        p, q, *kv_s = _inproj(xs, rows_bm[:, 0], rows_bm[:, 1], g_mix_v, w_in_b, l, kv_s, n_rwkv_pad, d_diff, m_s)
        y_r, wkv = _wkv(p, l, mu_p, shift_s_in[l], *wkv_args, state_wkv, bs, dseq, dseq, 1)
        kn4, vn4 = (a.reshape(depth, m_s, n_dheads, 2 * DIFF_HEAD) for a in kv_s)
        y_d = _attn_sample(q, kn4, vn4, l, cache_k, cache_v, bias_t, lqk, subln_v, lam_init, bs, dseq,
                           MAX_DISTANCE)
        x1 = _outproj(xs, y_r, y_d, rows_bm[:, 2], w_out_b, l, m_s)
        conv0 = state_conv[l].swapaxes(0, 1).reshape((CONV_W - 1) * bs, d_ff)
        x2, conv = _ffn(time_major(x1), rows_tm[:, 3], rows_tm[:, 4], rows_tm[:, 5], g_ffn_v, up_b, ffn_conv_w,
                        conv_b_v, conv0, down_b, gf, l, bs, m_s, last)
        xs = batch_major(x2)
        outs_s["wkv"].append(wkv)
        p_last = p.reshape(bs, dseq, n_rwkv_pad)[:, dseq - 1:dseq]
        outs_s["shift"].append(_ungroup_rwkv_cols(p_last, d_rwkv))
        outs_s["conv"].append(conv.reshape(CONV_W - 1, bs, d_ff).swapaxes(0, 1))

    st = lambda xs_: jnp.stack(xs_)
    head_shape = (n_dheads, 2 * DIFF_HEAD)
    return (xp.reshape(bp, seq, d_model), xs.reshape(bs, dseq, d_model),
            kv_p[0].reshape(depth, bp, seq, *head_shape), kv_p[1].reshape(depth, bp, seq, *head_shape),
            st(outs_p["wkv"]), st(outs_p["shift"]), st(outs_p["conv"]),
            kv_s[0].reshape(depth, bs, dseq, *head_shape), kv_s[1].reshape(depth, bs, dseq, *head_shape),
            st(outs_s["wkv"]), st(outs_s["shift"]), st(outs_s["conv"]))
```
